```python
import math
import jax, jax.numpy as jnp
from jax import lax
import numpy as np

D_MODEL = 2048
BATCH = 2
SEQ = 8192
DEPTH = 2

D_FF = 5632
NORM_EPS = 1e-6

W_LRU = 1024
LRU_HEADS = 8
LRU_BLOCK = W_LRU // LRU_HEADS
CONV_W = 4
CONV_PAD_L = 2
LRU_C = 8.0

W_MLSTM = 1024
MLSTM_HEADS = 4
MLSTM_DH = W_MLSTM // MLSTM_HEADS
MLSTM_CHUNK = 64

AB_SPLITS = (W_LRU, W_LRU, W_MLSTM, W_MLSTM, W_MLSTM, W_MLSTM, 4 * MLSTM_HEADS)
AB_IN = 2 * W_LRU + 4 * W_MLSTM + 4 * MLSTM_HEADS

GLA_HEADS = 4
GLA_DK = 128
GLA_DV = 256
GLA_QK = GLA_HEADS * GLA_DK
GLA_V = GLA_HEADS * GLA_DV
GLA_RANK = 16
GLA_TAU = 16.0
GLA_CHUNK = 64

S5_W = 1024
S5_GROUP = 16
S5_GROUPS = S5_W // S5_GROUP
S5_P = 64
S5_DT_MIN = 0.001
S5_DT_MAX = 0.1

CD_SPLITS = (GLA_QK, GLA_QK, GLA_V, GLA_V, 2 * GLA_RANK, S5_W)
CD_IN = 2 * GLA_QK + 2 * GLA_V + 2 * GLA_RANK + S5_W

MIX_OUT = W_LRU + W_MLSTM

kernel_name = 'hybrid_bidir_rglru_mlstm_gla_s5_macaron'


def _split(t, sizes):
    bounds = [int(b) for b in np.cumsum(sizes)[:-1]]
    return jnp.split(t, bounds, axis=-1)


def rms_norm(x, g):
    xf = x.astype(jnp.float32)
    y = xf * lax.rsqrt(jnp.mean(xf * xf, axis=-1, keepdims=True) + NORM_EPS)
    return (y * g.astype(jnp.float32)).astype(x.dtype)


def headwise_rms_norm(t, g):
    y = t * lax.rsqrt(jnp.mean(t * t, axis=-1, keepdims=True) + NORM_EPS)
    return y.reshape(t.shape[:2] + (-1,)) * g


def swiglu(h, w_gu, w_down):
    g, u = jnp.split(h @ w_gu, 2, axis=-1)
    return (jax.nn.silu(g) * u) @ w_down


def _heads(t, n):
    return t.reshape(t.shape[:2] + (n, -1)).transpose(0, 2, 1, 3)


def _flip_seq(t):
    return jnp.flip(t, axis=2)


def _to_chunks(t, chunk):
    b, h, s = t.shape[:3]
    t = t.reshape((b, h, s // chunk, chunk) + t.shape[3:])
    return jnp.moveaxis(t, 2, 0)


def _from_chunks(t):
    nc, b, h, l = t.shape[:4]
    return jnp.moveaxis(t, 0, 2).reshape((b, h, nc * l) + t.shape[4:])


def _affine_combine(e1, e2):
    a1, b1 = e1
    a2, b2 = e2
    return a1 * a2, a2 * b1 + b2


def _complex_affine_combine(e1, e2):
    a1r, a1i, b1r, b1i = e1
    a2r, a2i, b2r, b2i = e2
    ar = a1r * a2r - a1i * a2i
    ai = a1r * a2i + a1i * a2r
    br = a2r * b1r - a2i * b1i + b2r
    bi = a2r * b1i + a2i * b1r + b2i
    return ar, ai, br, bi


def centred_depthwise_conv(x, w, b):
    c = x.shape[-1]
    y = lax.conv_general_dilated(
        x, w[:, None, :].astype(x.dtype), window_strides=(1,),
        padding=[(CONV_PAD_L, CONV_W - 1 - CONV_PAD_L)],
        dimension_numbers=('NWC', 'WIO', 'NWC'), feature_group_count=c)
    return y + b


def block_diag_linear(x, w, b):
    xh = x.reshape(x.shape[:2] + (LRU_HEADS, LRU_BLOCK))
    return jnp.einsum('bshi,hij->bshj', xh, w).reshape(x.shape) + b


def rg_lru_direction(x, w_r, b_r, w_i, b_i, lam, reverse):
    r = jax.nn.sigmoid(block_diag_linear(x, w_r, b_r))
    i = jax.nn.sigmoid(block_diag_linear(x, w_i, b_i))
    log_a = -LRU_C * r * jax.nn.softplus(-lam)
    a = jnp.exp(log_a)
    b = jnp.sqrt(-jnp.expm1(2.0 * log_a)) * (i * x)
    _, h = lax.associative_scan(_affine_combine, (a, b), axis=1, reverse=reverse)
    return h


def mlstm_chunkwise(q, k, v, ig, lf):
    bsz, nh, _, dh = q.shape
    L = MLSTM_CHUNK
    mask = jnp.tril(jnp.ones((L, L), dtype=bool))
    xs = tuple(_to_chunks(t, L) for t in (q, k, v, ig, lf))

    def step(carry, inp):
        c_st, n_st, m_st = carry
        qt, kt, vt, it, ft = inp
        cum = jnp.cumsum(ft, axis=-1)
        dmat = jnp.where(mask, cum[..., :, None] - cum[..., None, :] + it[..., None, :], -jnp.inf)
        m_inter = cum + m_st[..., None]
        m_t = jnp.maximum(jnp.max(dmat, axis=-1), m_inter)
        w_intra = jnp.exp(dmat - m_t[..., None])
        w_inter = jnp.exp(m_inter - m_t)
        s = jnp.einsum('bhtd,bhsd->bhts', qt, kt) * w_intra
        num = jnp.einsum('bhts,bhse->bhte', s, vt) + w_inter[..., None] * jnp.einsum('bhtd,bhde->bhte', qt, c_st)
        den = jnp.sum(s, axis=-1) + w_inter * jnp.einsum('bhtd,bhd->bht', qt, n_st)
        h = num / jnp.maximum(jnp.abs(den), jnp.exp(-m_t))[..., None]
        tot = cum[..., -1]
        dec_s = tot[..., None] - cum + it
        m_new = jnp.maximum(tot + m_st, jnp.max(dec_s, axis=-1))
        ws = jnp.exp(dec_s - m_new[..., None])
        wc = jnp.exp(tot + m_st - m_new)
        c_new = wc[..., None, None] * c_st + jnp.einsum('bhs,bhsd,bhse->bhde', ws, kt, vt)
        n_new = wc[..., None] * n_st + jnp.einsum('bhs,bhsd->bhd', ws, kt)
        return (c_new, n_new, m_new), h

    init = (jnp.zeros((bsz, nh, dh, dh), q.dtype), jnp.zeros((bsz, nh, dh), q.dtype),
            jnp.zeros((bsz, nh), q.dtype))
    _, hs = lax.scan(step, init, xs)
    return _from_chunks(hs)


def mixer_ab(h, w_in, conv_w, conv_b, lru_gate_w, lru_gate_b, lru_lambda, mlstm_gate_b, mlstm_norm, w_out):
    bsz, s, _ = h.shape
    proj = (h @ w_in).astype(jnp.float32)
    xr, gr, q, k, v, o, gates = _split(proj, AB_SPLITS)
    xr = centred_depthwise_conv(xr, conv_w, conv_b)
    h_lru = (rg_lru_direction(xr, lru_gate_w[0, 0], lru_gate_b[0, 0], lru_gate_w[0, 1], lru_gate_b[0, 1],
                              lru_lambda[0], False)
             + rg_lru_direction(xr, lru_gate_w[1, 0], lru_gate_b[1, 0], lru_gate_w[1, 1], lru_gate_b[1, 1],
                                lru_lambda[1], True))
    y_a = jax.nn.gelu(gr) * h_lru
    q = _heads(q, MLSTM_HEADS) * (MLSTM_DH ** -0.5)
    k = _heads(k, MLSTM_HEADS)
    v = _heads(v, MLSTM_HEADS)
    g = (gates.reshape(bsz, s, 2, 2, MLSTM_HEADS) + mlstm_gate_b).transpose(2, 3, 0, 4, 1)
    h_f = mlstm_chunkwise(q, k, v, g[0, 0], jax.nn.log_sigmoid(g[0, 1]))
    h_b = _flip_seq(mlstm_chunkwise(_flip_seq(q), _flip_seq(k), _flip_seq(v), _flip_seq(g[1, 0]),
                                    _flip_seq(jax.nn.log_sigmoid(g[1, 1]))))
    hm = headwise_rms_norm((h_f + h_b).transpose(0, 2, 1, 3), mlstm_norm)
    y_b = jax.nn.sigmoid(o) * hm
    y = jnp.concatenate([y_a, y_b], axis=-1)
    return y.astype(h.dtype) @ w_out


def gla_chunked(q, k, v, la):
    bsz, nh, _, dk = q.shape
    dv = v.shape[-1]
    L = GLA_CHUNK
    mask = jnp.tril(jnp.ones((L, L), dtype=bool))[..., None]
    xs = tuple(_to_chunks(t, L) for t in (q, k, v, la))

    def step(s_st, inp):
        qt, kt, vt, at = inp
        cum = jnp.cumsum(at, axis=2)
        rel = jnp.where(mask, cum[:, :, :, None, :] - cum[:, :, None, :, :], -jnp.inf)
        attn = jnp.einsum('bhtd,bhsd,bhtsd->bhts', qt, kt, jnp.exp(rel))
        o = jnp.einsum('bhts,bhse->bhte', attn, vt) + jnp.einsum('bhtd,bhde->bhte', qt * jnp.exp(cum), s_st)
        tot = cum[:, :, -1]
        s_new = jnp.exp(tot)[..., None] * s_st + jnp.einsum('bhsd,bhse->bhde', kt * jnp.exp(tot[:, :, None] - cum), vt)
        return s_new, o

    _, os_ = lax.scan(step, jnp.zeros((bsz, nh, dk, dv), q.dtype), xs)
    return _from_chunks(os_)


def s5_discretise(a_re, a_im, log_dt, b_re, b_im):
    dt = jnp.exp(log_dt)[:, None]
    mag = jnp.exp(dt * a_re)
    lr = mag * jnp.cos(dt * a_im)
    li = mag * jnp.sin(dt * a_im)
    den = a_re * a_re + a_im * a_im
    nr = lr - 1.0
    cr = (nr * a_re + li * a_im) / den
    ci = (li * a_re - nr * a_im) / den
    bbr = cr[..., None] * b_re - ci[..., None] * b_im
    bbi = cr[..., None] * b_im + ci[..., None] * b_re
    return lr, li, bbr, bbi


def s5_scan(ug, lr, li, bbr, bbi, reverse):
    bu_r = jnp.einsum('bsgc,gpc->bsgp', ug, bbr)
    bu_i = jnp.einsum('bsgc,gpc->bsgp', ug, bbi)
    ar = jnp.broadcast_to(lr, bu_r.shape)
    ai = jnp.broadcast_to(li, bu_r.shape)
    _, _, xr, xi = lax.associative_scan(_complex_affine_combine, (ar, ai, bu_r, bu_i), axis=1, reverse=reverse)
    return xr, xi


def s5_mixer(u, a_re, a_im, log_dt, b_re, b_im, c_re, c_im, d, w_glu):
    bsz, s, _ = u.shape
    ug = u.reshape(bsz, s, S5_GROUPS, S5_GROUP)
    fr, fi = s5_scan(ug, *s5_discretise(a_re[0], a_im[0], log_dt[0], b_re, b_im), False)
    br_, bi_ = s5_scan(ug, *s5_discretise(a_re[1], a_im[1], log_dt[1], b_re, b_im), True)
    xr = fr + br_
    xi = fi + bi_
    y = (jnp.einsum('bsgp,gcp->bsgc', xr, c_re) - jnp.einsum('bsgp,gcp->bsgc', xi, c_im)
         + d.reshape(S5_GROUPS, S5_GROUP) * ug)
    y = jax.nn.gelu(y.reshape(bsz, s, S5_W))
    return y * jax.nn.sigmoid(y @ w_glu)


def mixer_cd(h, w_in, gla_w_gate2, gla_gate_b, gla_norm, s5_a_re, s5_a_im, s5_log_dt,
             s5_b_re, s5_b_im, s5_c_re, s5_c_im, s5_d, s5_w_glu, w_out):
    bsz, s, _ = h.shape
    proj = (h @ w_in).astype(jnp.float32)
    q, k, v, r, glr, u = _split(proj, CD_SPLITS)
    q = _heads(q, GLA_HEADS) * (GLA_DK ** -0.5)
    k = _heads(k, GLA_HEADS)
    v = _heads(v, GLA_HEADS)
    low = glr.reshape(bsz, s, 2, GLA_RANK)
    gate_pre = jnp.einsum('bsdr,drk->dbsk', low, gla_w_gate2) + gla_gate_b[:, None, None, :]
    la = (jax.nn.log_sigmoid(gate_pre) / GLA_TAU).reshape(2, bsz, s, GLA_HEADS, GLA_DK).transpose(0, 1, 3, 2, 4)
    o_f = gla_chunked(q, k, v, la[0])
    o_b = _flip_seq(gla_chunked(_flip_seq(q), _flip_seq(k), _flip_seq(v), _flip_seq(la[1])))
    y_c = headwise_rms_norm((o_f + o_b).transpose(0, 2, 1, 3), gla_norm) * jax.nn.silu(r)
    y_d = s5_mixer(u, s5_a_re, s5_a_im, s5_log_dt, s5_b_re, s5_b_im, s5_c_re, s5_c_im, s5_d, s5_w_glu)
    y = jnp.concatenate([y_c, y_d], axis=-1)
    return y.astype(h.dtype) @ w_out


def setup_inputs(seed: int = 0) -> dict:
    key = jax.random.key(seed)
    keys = iter(jax.random.split(key, 48))
    n_even = (DEPTH + 1) // 2
    n_odd = DEPTH // 2
    f32 = jnp.float32

    def normal(shape, scale):
        return scale * jax.random.normal(next(keys), shape, f32)

    def gain(shape):
        return 1.0 + 0.02 * jax.random.normal(next(keys), shape, f32)

    def uniform(shape, lo, hi):
        return jax.random.uniform(next(keys), shape, f32, lo, hi)

    x = normal((BATCH, SEQ, D_MODEL), 1.0)
    norm_ffn1 = gain((DEPTH, D_MODEL))
    ffn1_w_gu = normal((DEPTH, D_MODEL, 2 * D_FF), D_MODEL ** -0.5)
    ffn1_w_down = normal((DEPTH, D_FF, D_MODEL), D_FF ** -0.5)
    norm_mix = gain((DEPTH, D_MODEL))
    norm_ffn2 = gain((DEPTH, D_MODEL))
    ffn2_w_gu = normal((DEPTH, D_MODEL, 2 * D_FF), D_MODEL ** -0.5)
    ffn2_w_down = normal((DEPTH, D_FF, D_MODEL), D_FF ** -0.5)

    ab_w_in = normal((n_even, D_MODEL, AB_IN), D_MODEL ** -0.5)
    lru_conv_w = normal((n_even, CONV_W, W_LRU), CONV_W ** -0.5)
    lru_conv_b = normal((n_even, W_LRU), 0.02)
    lru_gate_w = normal((n_even, 2, 2, LRU_HEADS, LRU_BLOCK, LRU_BLOCK), LRU_BLOCK ** -0.5)
    lru_gate_b = normal((n_even, 2, 2, W_LRU), 0.02)
    a_c = uniform((n_even, 2, W_LRU), 0.9, 0.999)
    a0 = a_c ** (1.0 / LRU_C)
    lru_lambda = jnp.log(a0) - jnp.log1p(-a0)
    i_bias = normal((n_even, 2, MLSTM_HEADS), 0.1)
    f_bias = jnp.linspace(3.0, 6.0, MLSTM_HEADS, dtype=f32) + normal((n_even, 2, MLSTM_HEADS), 0.1)
    mlstm_gate_b = jnp.stack([i_bias, f_bias], axis=2)
    mlstm_norm = gain((n_even, W_MLSTM))
    ab_w_out = normal((n_even, MIX_OUT, D_MODEL), MIX_OUT ** -0.5)

    cd_w_in = normal((n_odd, D_MODEL, CD_IN), D_MODEL ** -0.5)
    gla_w_gate2 = normal((n_odd, 2, GLA_RANK, GLA_QK), GLA_RANK ** -0.5)
    gla_gate_b = normal((n_odd, 2, GLA_QK), 0.1)
    gla_norm = gain((n_odd, GLA_V))
    s5_a_re = -0.5 + normal((n_odd, 2, S5_GROUPS, S5_P), 0.01)
    s5_a_im = math.pi * jnp.arange(S5_P, dtype=f32) + normal((n_odd, 2, S5_GROUPS, S5_P), 0.01)
    s5_log_dt = uniform((n_odd, 2, S5_GROUPS), math.log(S5_DT_MIN), math.log(S5_DT_MAX))
    s5_b_re = normal((n_odd, S5_GROUPS, S5_P, S5_GROUP), (2 * S5_GROUP) ** -0.5)
    s5_b_im = normal((n_odd, S5_GROUPS, S5_P, S5_GROUP), (2 * S5_GROUP) ** -0.5)
    s5_c_re = normal((n_odd, S5_GROUPS, S5_GROUP, S5_P), S5_P ** -0.5)
    s5_c_im = normal((n_odd, S5_GROUPS, S5_GROUP, S5_P), S5_P ** -0.5)
    s5_d = normal((n_odd, S5_W), 1.0)
    s5_w_glu = normal((n_odd, S5_W, S5_W), S5_W ** -0.5)
    cd_w_out = normal((n_odd, MIX_OUT, D_MODEL), MIX_OUT ** -0.5)
    final_norm = gain((D_MODEL,))
    return {
        'x': x,
        'norm_ffn1': norm_ffn1, 'ffn1_w_gu': ffn1_w_gu, 'ffn1_w_down': ffn1_w_down,
        'norm_mix': norm_mix,
        'norm_ffn2': norm_ffn2, 'ffn2_w_gu': ffn2_w_gu, 'ffn2_w_down': ffn2_w_down,
        'ab_w_in': ab_w_in, 'lru_conv_w': lru_conv_w, 'lru_conv_b': lru_conv_b,
        'lru_gate_w': lru_gate_w, 'lru_gate_b': lru_gate_b, 'lru_lambda': lru_lambda,
        'mlstm_gate_b': mlstm_gate_b, 'mlstm_norm': mlstm_norm, 'ab_w_out': ab_w_out,
        'cd_w_in': cd_w_in, 'gla_w_gate2': gla_w_gate2, 'gla_gate_b': gla_gate_b, 'gla_norm': gla_norm,
        's5_a_re': s5_a_re, 's5_a_im': s5_a_im, 's5_log_dt': s5_log_dt,
        's5_b_re': s5_b_re, 's5_b_im': s5_b_im, 's5_c_re': s5_c_re, 's5_c_im': s5_c_im,
        's5_d': s5_d, 's5_w_glu': s5_w_glu, 'cd_w_out': cd_w_out,
        'final_norm': final_norm,
    }


def reference(x, norm_ffn1, ffn1_w_gu, ffn1_w_down, norm_mix, norm_ffn2, ffn2_w_gu, ffn2_w_down,
              ab_w_in, lru_conv_w, lru_conv_b, lru_gate_w, lru_gate_b, lru_lambda, mlstm_gate_b,
              mlstm_norm, ab_w_out, cd_w_in, gla_w_gate2, gla_gate_b, gla_norm, s5_a_re, s5_a_im,
              s5_log_dt, s5_b_re, s5_b_im, s5_c_re, s5_c_im, s5_d, s5_w_glu, cd_w_out, final_norm):
    for l in range(DEPTH):
        x = x + 0.5 * swiglu(rms_norm(x, norm_ffn1[l]), ffn1_w_gu[l], ffn1_w_down[l])
        hn = rms_norm(x, norm_mix[l])
        j = l // 2
        if l % 2 == 0:
            x = x + mixer_ab(hn, ab_w_in[j], lru_conv_w[j], lru_conv_b[j], lru_gate_w[j], lru_gate_b[j],
                             lru_lambda[j], mlstm_gate_b[j], mlstm_norm[j], ab_w_out[j])
        else:
            x = x + mixer_cd(hn, cd_w_in[j], gla_w_gate2[j], gla_gate_b[j], gla_norm[j], s5_a_re[j],
                             s5_a_im[j], s5_log_dt[j], s5_b_re[j], s5_b_im[j], s5_c_re[j], s5_c_im[j],
                             s5_d[j], s5_w_glu[j], cd_w_out[j])
        x = x + 0.5 * swiglu(rms_norm(x, norm_ffn2[l]), ffn2_w_gu[l], ffn2_w_down[l])
    return rms_norm(x, final_norm)
```

```python
import functools
import math

import numpy as np
import jax
import jax.numpy as jnp
from jax import lax
from jax.experimental import pallas as pl
from jax.experimental.pallas import tpu as pltpu

D_MODEL = 2048
D_FF = 5632
NORM_EPS = 1e-6

W_LRU = 1024
LRU_HEADS = 8
LRU_BLOCK = W_LRU // LRU_HEADS
CONV_W = 4
CONV_PAD_L = 2
LRU_C = 8.0

W_MLSTM = 1024
MLSTM_HEADS = 4
MLSTM_DH = W_MLSTM // MLSTM_HEADS
MLSTM_CHUNK = 64
AB_SPLITS = (W_LRU, W_LRU, W_MLSTM, W_MLSTM, W_MLSTM, W_MLSTM, 4 * MLSTM_HEADS)

GLA_HEADS = 4
GLA_DK = 128
GLA_DV = 256
GLA_QK = GLA_HEADS * GLA_DK
GLA_V = GLA_HEADS * GLA_DV
GLA_RANK = 16
GLA_TAU = 16.0
GLA_CHUNK = 64

S5_W = 1024
S5_GROUP = 16
S5_GROUPS = S5_W // S5_GROUP
S5_P = 64
CD_SPLITS = (GLA_QK, GLA_QK, GLA_V, GLA_V, 2 * GLA_RANK, S5_W)

LANE = 128
VMEM_LIMIT = 52 * 1024 * 1024

BF16 = jnp.bfloat16
F32 = jnp.float32


def _rms(x, g):
    ms = jnp.mean(x * x, axis=-1, keepdims=True)
    return x * lax.rsqrt(ms + NORM_EPS) * g


def _ffn_kernel(x_ref, g_ref, wg_ref, wu_ref, wd_ref, *rest, final):
    if final:
        fg_ref, o_ref, h_ref = rest
    else:
        o_ref, h_ref = rest
    j = pl.program_id(1)

    @pl.when(j == 0)
    def _():
        x = x_ref[...]
        h_ref[...] = _rms(x, g_ref[...]).astype(BF16)
        o_ref[...] = x

    h = h_ref[...]
    g = jnp.dot(h, wg_ref[...], preferred_element_type=F32)
    u = jnp.dot(h, wu_ref[...], preferred_element_type=F32)
    a = (0.5 * g * jax.nn.sigmoid(g) * u).astype(BF16)
    o_ref[...] += jnp.dot(a, wd_ref[...], preferred_element_type=F32)

    if final:
        @pl.when(j == pl.num_programs(1) - 1)
        def _():
            o_ref[...] = _rms(o_ref[...], fg_ref[...])


def _ffn(x, g, w_gu, w_down, final_g=None, tm=512, tf=512):
    n, d = x.shape
    f = w_down.shape[0]
    nj = f // tf
    final = final_g is not None
    in_specs = [
        pl.BlockSpec((tm, d), lambda i, j: (i, 0)),
        pl.BlockSpec((1, d), lambda i, j: (0, 0)),
        pl.BlockSpec((d, tf), lambda i, j: (0, j)),
        pl.BlockSpec((d, tf), lambda i, j: (0, j + nj)),
        pl.BlockSpec((tf, d), lambda i, j: (j, 0)),
    ]
    args = [x, g.reshape(1, d), w_gu, w_gu, w_down]
    if final:
        in_specs.append(pl.BlockSpec((1, d), lambda i, j: (0, 0)))
        args.append(final_g.reshape(1, d))
    return pl.pallas_call(
        functools.partial(_ffn_kernel, final=final),
        grid=(n // tm, nj),
        in_specs=in_specs,
        out_specs=pl.BlockSpec((tm, d), lambda i, j: (i, 0)),
        out_shape=jax.ShapeDtypeStruct((n, d), F32),
        scratch_shapes=[pltpu.VMEM((tm, d), BF16)],
        compiler_params=pltpu.CompilerParams(
            dimension_semantics=("arbitrary", "arbitrary"),
            vmem_limit_bytes=VMEM_LIMIT),
        name="ffn_final" if final else "ffn",
    )(*args)


def _norm_proj_kernel(x_ref, g_ref, w_ref, o_ref, h_ref):
    @pl.when(pl.program_id(1) == 0)
    def _():
        h_ref[...] = _rms(x_ref[...], g_ref[...]).astype(BF16)

    o_ref[...] = jnp.dot(h_ref[...], w_ref[...], preferred_element_type=F32)


def _norm_proj(x, g, w, tm=512, tn=512):
    n, d = x.shape
    m = w.shape[1]
    return pl.pallas_call(
        _norm_proj_kernel,
        grid=(n // tm, m // tn),
        in_specs=[
            pl.BlockSpec((tm, d), lambda i, j: (i, 0)),
            pl.BlockSpec((1, d), lambda i, j: (0, 0)),
            pl.BlockSpec((d, tn), lambda i, j: (0, j)),
        ],
        out_specs=pl.BlockSpec((tm, tn), lambda i, j: (i, j)),
        out_shape=jax.ShapeDtypeStruct((n, m), F32),
        scratch_shapes=[pltpu.VMEM((tm, d), BF16)],
        compiler_params=pltpu.CompilerParams(
            dimension_semantics=("arbitrary", "arbitrary"),
            vmem_limit_bytes=VMEM_LIMIT),
        name="norm_proj",
    )(x, g.reshape(1, d), w)


def _out_proj_kernel(x_ref, y_ref, w_ref, o_ref):
    o_ref[...] = x_ref[...] + jnp.dot(y_ref[...].astype(BF16), w_ref[...],
                                      preferred_element_type=F32)


def _out_proj(x, y, w, tm=512):
    n, d = x.shape
    k = y.shape[1]
    return pl.pallas_call(
        _out_proj_kernel,
        grid=(n // tm,),
        in_specs=[
            pl.BlockSpec((tm, d), lambda i: (i, 0)),
            pl.BlockSpec((tm, k), lambda i: (i, 0)),
            pl.BlockSpec((k, d), lambda i: (0, 0)),
        ],
        out_specs=pl.BlockSpec((tm, d), lambda i: (i, 0)),
        out_shape=jax.ShapeDtypeStruct((n, d), F32),
        compiler_params=pltpu.CompilerParams(
            dimension_semantics=("arbitrary",),
            vmem_limit_bytes=VMEM_LIMIT),
        name="out_proj",
    )(x, y, w)


def _split(t, sizes):
    bounds = [int(b) for b in np.cumsum(sizes)[:-1]]
    return jnp.split(t, bounds, axis=-1)


def _headwise_rms_norm(t, g):
    y = t * lax.rsqrt(jnp.mean(t * t, axis=-1, keepdims=True) + NORM_EPS)
    return y.reshape(t.shape[:2] + (-1,)) * g


def _heads(t, n):
    return t.reshape(t.shape[:2] + (n, -1)).transpose(0, 2, 1, 3)


def _flip_seq(t):
    return jnp.flip(t, axis=2)


def _to_chunks(t, chunk):
    b, h, s = t.shape[:3]
    t = t.reshape((b, h, s // chunk, chunk) + t.shape[3:])
    return jnp.moveaxis(t, 2, 0)


def _from_chunks(t):
    nc, b, h, l = t.shape[:4]
    return jnp.moveaxis(t, 0, 2).reshape((b, h, nc * l) + t.shape[4:])


def _affine_combine(e1, e2):
    a1, b1 = e1
    a2, b2 = e2
    return a1 * a2, a2 * b1 + b2


def _complex_affine_combine(e1, e2):
    a1r, a1i, b1r, b1i = e1
    a2r, a2i, b2r, b2i = e2
    ar = a1r * a2r - a1i * a2i
    ai = a1r * a2i + a1i * a2r
    br = a2r * b1r - a2i * b1i + b2r
    bi = a2r * b1i + a2i * b1r + b2i
    return ar, ai, br, bi


def _centred_depthwise_conv(x, w, b):
    c = x.shape[-1]
    y = lax.conv_general_dilated(
        x, w[:, None, :].astype(x.dtype), window_strides=(1,),
        padding=[(CONV_PAD_L, CONV_W - 1 - CONV_PAD_L)],
        dimension_numbers=('NWC', 'WIO', 'NWC'), feature_group_count=c)
    return y + b


def _block_diag_linear(x, w, b):
    xh = x.reshape(x.shape[:2] + (LRU_HEADS, LRU_BLOCK))
    return jnp.einsum('bshi,hij->bshj', xh, w).reshape(x.shape) + b


def _rg_lru_direction(x, w_r, b_r, w_i, b_i, lam, reverse):
    r = jax.nn.sigmoid(_block_diag_linear(x, w_r, b_r))
    i = jax.nn.sigmoid(_block_diag_linear(x, w_i, b_i))
    log_a = -LRU_C * r * jax.nn.softplus(-lam)
    a = jnp.exp(log_a)
    b = jnp.sqrt(-jnp.expm1(2.0 * log_a)) * (i * x)
    _, h = lax.associative_scan(_affine_combine, (a, b), axis=1, reverse=reverse)
    return h


def _mlstm_chunkwise(q, k, v, ig, lf):
    bsz, nh, _, dh = q.shape
    L = MLSTM_CHUNK
    mask = jnp.tril(jnp.ones((L, L), dtype=bool))
    xs = tuple(_to_chunks(t, L) for t in (q, k, v, ig, lf))

    def step(carry, inp):
        c_st, n_st, m_st = carry
        qt, kt, vt, it, ft = inp
        cum = jnp.cumsum(ft, axis=-1)
        dmat = jnp.where(mask, cum[..., :, None] - cum[..., None, :] + it[..., None, :], -jnp.inf)
        m_inter = cum + m_st[..., None]
        m_t = jnp.maximum(jnp.max(dmat, axis=-1), m_inter)
        w_intra = jnp.exp(dmat - m_t[..., None])
        w_inter = jnp.exp(m_inter - m_t)
        s = jnp.einsum('bhtd,bhsd->bhts', qt, kt) * w_intra
        num = jnp.einsum('bhts,bhse->bhte', s, vt) + w_inter[..., None] * jnp.einsum('bhtd,bhde->bhte', qt, c_st)
        den = jnp.sum(s, axis=-1) + w_inter * jnp.einsum('bhtd,bhd->bht', qt, n_st)
        h = num / jnp.maximum(jnp.abs(den), jnp.exp(-m_t))[..., None]
        tot = cum[..., -1]
        dec_s = tot[..., None] - cum + it
        m_new = jnp.maximum(tot + m_st, jnp.max(dec_s, axis=-1))
        ws = jnp.exp(dec_s - m_new[..., None])
        wc = jnp.exp(tot + m_st - m_new)
        c_new = wc[..., None, None] * c_st + jnp.einsum('bhs,bhsd,bhse->bhde', ws, kt, vt)
        n_new = wc[..., None] * n_st + jnp.einsum('bhs,bhsd->bhd', ws, kt)
        return (c_new, n_new, m_new), h

    init = (jnp.zeros((bsz, nh, dh, dh), q.dtype), jnp.zeros((bsz, nh, dh), q.dtype),
            jnp.zeros((bsz, nh), q.dtype))
    _, hs = lax.scan(step, init, xs)
    return _from_chunks(hs)


def _mixer_ab_core(proj, conv_w, conv_b, lru_gate_w, lru_gate_b, lru_lambda, mlstm_gate_b, mlstm_norm):
    bsz, s, _ = proj.shape
    xr, gr, q, k, v, o, gates = _split(proj, AB_SPLITS)
    xr = _centred_depthwise_conv(xr, conv_w, conv_b)
    h_lru = (_rg_lru_direction(xr, lru_gate_w[0, 0], lru_gate_b[0, 0], lru_gate_w[0, 1], lru_gate_b[0, 1],
                               lru_lambda[0], False)
             + _rg_lru_direction(xr, lru_gate_w[1, 0], lru_gate_b[1, 0], lru_gate_w[1, 1], lru_gate_b[1, 1],
                                 lru_lambda[1], True))
    y_a = jax.nn.gelu(gr) * h_lru
    q = _heads(q, MLSTM_HEADS) * (MLSTM_DH ** -0.5)
    k = _heads(k, MLSTM_HEADS)
    v = _heads(v, MLSTM_HEADS)
    g = (gates.reshape(bsz, s, 2, 2, MLSTM_HEADS) + mlstm_gate_b).transpose(2, 3, 0, 4, 1)
    h_f = _mlstm_chunkwise(q, k, v, g[0, 0], jax.nn.log_sigmoid(g[0, 1]))
    h_b = _flip_seq(_mlstm_chunkwise(_flip_seq(q), _flip_seq(k), _flip_seq(v), _flip_seq(g[1, 0]),
                                     _flip_seq(jax.nn.log_sigmoid(g[1, 1]))))
    hm = _headwise_rms_norm((h_f + h_b).transpose(0, 2, 1, 3), mlstm_norm)
    y_b = jax.nn.sigmoid(o) * hm
    return jnp.concatenate([y_a, y_b], axis=-1)


def _gla_chunked(q, k, v, la):
    bsz, nh, _, dk = q.shape
    dv = v.shape[-1]
    L = GLA_CHUNK
    mask = jnp.tril(jnp.ones((L, L), dtype=bool))[..., None]
    xs = tuple(_to_chunks(t, L) for t in (q, k, v, la))

    def step(s_st, inp):
        qt, kt, vt, at = inp
        cum = jnp.cumsum(at, axis=2)
        rel = jnp.where(mask, cum[:, :, :, None, :] - cum[:, :, None, :, :], -jnp.inf)
        attn = jnp.einsum('bhtd,bhsd,bhtsd->bhts', qt, kt, jnp.exp(rel))
        o = jnp.einsum('bhts,bhse->bhte', attn, vt) + jnp.einsum('bhtd,bhde->bhte', qt * jnp.exp(cum), s_st)
        tot = cum[:, :, -1]
        s_new = jnp.exp(tot)[..., None] * s_st + jnp.einsum('bhsd,bhse->bhde', kt * jnp.exp(tot[:, :, None] - cum), vt)
        return s_new, o

    _, os_ = lax.scan(step, jnp.zeros((bsz, nh, dk, dv), q.dtype), xs)
    return _from_chunks(os_)


def _s5_discretise(a_re, a_im, log_dt, b_re, b_im):
    dt = jnp.exp(log_dt)[:, None]
    mag = jnp.exp(dt * a_re)
    lr = mag * jnp.cos(dt * a_im)
    li = mag * jnp.sin(dt * a_im)
    den = a_re * a_re + a_im * a_im
    nr = lr - 1.0
    cr = (nr * a_re + li * a_im) / den
    ci = (li * a_re - nr * a_im) / den
    bbr = cr[..., None] * b_re - ci[..., None] * b_im
    bbi = cr[..., None] * b_im + ci[..., None] * b_re
    return lr, li, bbr, bbi


def _s5_scan(ug, lr, li, bbr, bbi, reverse):
    bu_r = jnp.einsum('bsgc,gpc->bsgp', ug, bbr)
    bu_i = jnp.einsum('bsgc,gpc->bsgp', ug, bbi)
    ar = jnp.broadcast_to(lr, bu_r.shape)
    ai = jnp.broadcast_to(li, bu_r.shape)
    _, _, xr, xi = lax.associative_scan(_complex_affine_combine, (ar, ai, bu_r, bu_i), axis=1, reverse=reverse)
    return xr, xi


def _s5_mixer(u, a_re, a_im, log_dt, b_re, b_im, c_re, c_im, d, w_glu):
    bsz, s, _ = u.shape
    ug = u.reshape(bsz, s, S5_GROUPS, S5_GROUP)
    fr, fi = _s5_scan(ug, *_s5_discretise(a_re[0], a_im[0], log_dt[0], b_re, b_im), False)
    br_, bi_ = _s5_scan(ug, *_s5_discretise(a_re[1], a_im[1], log_dt[1], b_re, b_im), True)
    xr = fr + br_
    xi = fi + bi_
    y = (jnp.einsum('bsgp,gcp->bsgc', xr, c_re) - jnp.einsum('bsgp,gcp->bsgc', xi, c_im)
         + d.reshape(S5_GROUPS, S5_GROUP) * ug)
    y = jax.nn.gelu(y.reshape(bsz, s, S5_W))
    return y * jax.nn.sigmoid(y @ w_glu)


def _mixer_cd_core(proj, gla_w_gate2, gla_gate_b, gla_norm, s5_a_re, s5_a_im, s5_log_dt,
                   s5_b_re, s5_b_im, s5_c_re, s5_c_im, s5_d, s5_w_glu):
    bsz, s, _ = proj.shape
    q, k, v, r, glr, u = _split(proj, CD_SPLITS)
    q = _heads(q, GLA_HEADS) * (GLA_DK ** -0.5)
    k = _heads(k, GLA_HEADS)
    v = _heads(v, GLA_HEADS)
    low = glr.reshape(bsz, s, 2, GLA_RANK)
    gate_pre = jnp.einsum('bsdr,drk->dbsk', low, gla_w_gate2) + gla_gate_b[:, None, None, :]
    la = (jax.nn.log_sigmoid(gate_pre) / GLA_TAU).reshape(2, bsz, s, GLA_HEADS, GLA_DK).transpose(0, 1, 3, 2, 4)
    o_f = _gla_chunked(q, k, v, la[0])
    o_b = _flip_seq(_gla_chunked(_flip_seq(q), _flip_seq(k), _flip_seq(v), _flip_seq(la[1])))
    y_c = _headwise_rms_norm((o_f + o_b).transpose(0, 2, 1, 3), gla_norm) * jax.nn.silu(r)
    y_d = _s5_mixer(u, s5_a_re, s5_a_im, s5_log_dt, s5_b_re, s5_b_im, s5_c_re, s5_c_im, s5_d, s5_w_glu)
    return jnp.concatenate([y_c, y_d], axis=-1)


def _pad_cols(w, mult):
    pad = (-w.shape[1]) % mult
    return jnp.pad(w, ((0, 0), (0, pad))) if pad else w


def kernel(x, norm_ffn1, ffn1_w_gu, ffn1_w_down, norm_mix, norm_ffn2, ffn2_w_gu, ffn2_w_down,
           ab_w_in, lru_conv_w, lru_conv_b, lru_gate_w, lru_gate_b, lru_lambda, mlstm_gate_b,
           mlstm_norm, ab_w_out, cd_w_in, gla_w_gate2, gla_gate_b, gla_norm, s5_a_re, s5_a_im,
           s5_log_dt, s5_b_re, s5_b_im, s5_c_re, s5_c_im, s5_d, s5_w_glu, cd_w_out, final_norm):
    bsz, seq, d = x.shape
    n = bsz * seq
    depth = norm_ffn1.shape[0]
    xf = x.reshape(n, d)
    for l in range(depth):
        xf = _ffn(xf, norm_ffn1[l], ffn1_w_gu[l].astype(BF16), ffn1_w_down[l].astype(BF16))
        j = l // 2
        if l % 2 == 0:
            w_in = _pad_cols(ab_w_in[j], 512).astype(BF16)
            proj = _norm_proj(xf, norm_mix[l], w_in)[:, :ab_w_in.shape[2]]
            y = _mixer_ab_core(proj.reshape(bsz, seq, -1), lru_conv_w[j], lru_conv_b[j], lru_gate_w[j],
                               lru_gate_b[j], lru_lambda[j], mlstm_gate_b[j], mlstm_norm[j])
            w_out = ab_w_out[j]
        else:
            w_in = _pad_cols(cd_w_in[j], 512).astype(BF16)
            proj = _norm_proj(xf, norm_mix[l], w_in)[:, :cd_w_in.shape[2]]
            y = _mixer_cd_core(proj.reshape(bsz, seq, -1), gla_w_gate2[j], gla_gate_b[j], gla_norm[j],
                               s5_a_re[j], s5_a_im[j], s5_log_dt[j], s5_b_re[j], s5_b_im[j],
                               s5_c_re[j], s5_c_im[j], s5_d[j], s5_w_glu[j])
            w_out = cd_w_out[j]
        xf = _out_proj(xf, y.reshape(n, -1), w_out.astype(BF16))
        xf = _ffn(xf, norm_ffn2[l], ffn2_w_gu[l].astype(BF16), ffn2_w_down[l].astype(BF16),
                  final_g=final_norm if l == depth - 1 else None)
    return xf.reshape(bsz, seq, d)
```

```python
import functools

import numpy as np
import jax
import jax.numpy as jnp
from jax import lax
from jax.experimental import pallas as pl
from jax.experimental.pallas import tpu as pltpu

D_MODEL = 2048
D_FF = 5632
NORM_EPS = 1e-6

W_LRU = 1024
LRU_HEADS = 8
LRU_BLOCK = W_LRU // LRU_HEADS
CONV_W = 4
CONV_PAD_L = 2
LRU_C = 8.0

W_MLSTM = 1024
MLSTM_HEADS = 4
MLSTM_DH = W_MLSTM // MLSTM_HEADS
MLSTM_CHUNK = 64

GLA_HEADS = 4
GLA_DK = 128
GLA_DV = 256
GLA_QK = GLA_HEADS * GLA_DK
GLA_V = GLA_HEADS * GLA_DV
GLA_RANK = 16
GLA_TAU = 16.0
GLA_CHUNK = 64

S5_W = 1024
S5_GROUP = 16
S5_GROUPS = S5_W // S5_GROUP
S5_P = 64
S5_CHUNK = 16

LANE = 128
SUBLANE = 8
VMEM_LIMIT = 52 * 1024 * 1024

BF16 = jnp.bfloat16
F32 = jnp.float32
HI = lax.Precision.HIGHEST


def _rms(x, g):
    ms = jnp.mean(x * x, axis=-1, keepdims=True)
    return x * lax.rsqrt(ms + NORM_EPS) * g


def _softplus(z):
    return jnp.maximum(z, 0.0) + jnp.log1p(jnp.exp(-jnp.abs(z)))


def _params(*sem):
    return pltpu.CompilerParams(dimension_semantics=sem, vmem_limit_bytes=VMEM_LIMIT)


def _ffn_kernel(x_ref, g_ref, wg_ref, wu_ref, wd_ref, *rest, final):
    if final:
        fg_ref, o_ref, h_ref = rest
    else:
        o_ref, h_ref = rest
    j = pl.program_id(1)

    @pl.when(j == 0)
    def _():
        x = x_ref[...]
        h_ref[...] = _rms(x, g_ref[...]).astype(BF16)
        o_ref[...] = x

    h = h_ref[...]
    g = jnp.dot(h, wg_ref[...], preferred_element_type=F32)
    u = jnp.dot(h, wu_ref[...], preferred_element_type=F32)
    a = (0.5 * g * jax.nn.sigmoid(g) * u).astype(BF16)
    o_ref[...] += jnp.dot(a, wd_ref[...], preferred_element_type=F32)

    if final:
        @pl.when(j == pl.num_programs(1) - 1)
        def _():
            o_ref[...] = _rms(o_ref[...], fg_ref[...])


def _ffn(x, g, w_gu, w_down, final_g=None, tm=512, tf=512):
    n, d = x.shape
    f = w_down.shape[0]
    tm = min(tm, n)
    nj = f // tf
    final = final_g is not None
    in_specs = [
        pl.BlockSpec((tm, d), lambda i, j: (i, 0)),
        pl.BlockSpec((1, d), lambda i, j: (0, 0)),
        pl.BlockSpec((d, tf), lambda i, j: (0, j)),
        pl.BlockSpec((d, tf), lambda i, j: (0, j + nj)),
        pl.BlockSpec((tf, d), lambda i, j: (j, 0)),
    ]
    args = [x, g.reshape(1, d), w_gu, w_gu, w_down]
    if final:
        in_specs.append(pl.BlockSpec((1, d), lambda i, j: (0, 0)))
        args.append(final_g.reshape(1, d))
    return pl.pallas_call(
        functools.partial(_ffn_kernel, final=final),
        grid=(n // tm, nj),
        in_specs=in_specs,
        out_specs=pl.BlockSpec((tm, d), lambda i, j: (i, 0)),
        out_shape=jax.ShapeDtypeStruct((n, d), F32),
        scratch_shapes=[pltpu.VMEM((tm, d), BF16)],
        compiler_params=_params("arbitrary", "arbitrary"),
        name="ffn_final" if final else "ffn",
    )(*args)


def _norm_proj_kernel(x_ref, g_ref, w_ref, ws_ref, o_ref, os_ref, h_ref):
    @pl.when(pl.program_id(1) == 0)
    def _():
        h = _rms(x_ref[...], g_ref[...]).astype(BF16)
        h_ref[...] = h
        os_ref[...] = jnp.dot(h, ws_ref[...], preferred_element_type=F32)

    o_ref[...] = jnp.dot(h_ref[...], w_ref[...], preferred_element_type=F32)


def _norm_proj(x, g, w, w_small, tm=512, tn=512):
    n, d = x.shape
    m = w.shape[1]
    ms = w_small.shape[1]
    tm = min(tm, n)
    return pl.pallas_call(
        _norm_proj_kernel,
        grid=(n // tm, m // tn),
        in_specs=[
            pl.BlockSpec((tm, d), lambda i, j: (i, 0)),
            pl.BlockSpec((1, d), lambda i, j: (0, 0)),
            pl.BlockSpec((d, tn), lambda i, j: (0, j)),
            pl.BlockSpec((d, ms), lambda i, j: (0, 0)),
        ],
        out_specs=[pl.BlockSpec((tm, tn), lambda i, j: (i, j)),
                   pl.BlockSpec((tm, ms), lambda i, j: (i, 0))],
        out_shape=[jax.ShapeDtypeStruct((n, m), F32), jax.ShapeDtypeStruct((n, ms), F32)],
        scratch_shapes=[pltpu.VMEM((tm, d), BF16)],
        compiler_params=_params("arbitrary", "arbitrary"),
        name="norm_proj",
    )(x, g.reshape(1, d), w, w_small)


def _lru_kernel(xp_ref, x_ref, xn_ref, cw_ref, cb_ref, gw_ref, gb_ref, lam_ref, o_ref,
                xs_ref, a_ref, b_ref, h_ref, *, tt):
    d = pl.program_id(0)
    t = pl.program_id(2)
    nt = pl.num_programs(2)
    tb = jnp.where(d == 0, t, nt - 1 - t)
    w = x_ref.shape[1]

    xs_ref[0:SUBLANE, :] = jnp.where(tb > 0, xp_ref[...], 0.0)
    xs_ref[SUBLANE:SUBLANE + tt, :] = x_ref[...]
    xs_ref[SUBLANE + tt:2 * SUBLANE + tt, :] = jnp.where(tb < nt - 1, xn_ref[...], 0.0)
    cw = cw_ref[...]
    xc = cb_ref[...]
    for k in range(CONV_W):
        off = SUBLANE + k - CONV_PAD_L
        xc = xc + cw[k:k + 1, :] * xs_ref[off:off + tt, :]

    xcb = xc.astype(BF16)
    gb = gb_ref[0]
    sp = _softplus(-lam_ref[0])
    for hd in range(w // LRU_BLOCK):
        sl = slice(hd * LRU_BLOCK, (hd + 1) * LRU_BLOCK)
        z = jnp.dot(xcb[:, sl], gw_ref[0, hd], preferred_element_type=F32)
        r = jax.nn.sigmoid(z[:, :LRU_BLOCK] + gb[0:1, sl])
        i = jax.nn.sigmoid(z[:, LRU_BLOCK:] + gb[1:2, sl])
        log_a = (-LRU_C * sp[:, sl]) * r
        a = jnp.exp(log_a)
        a_ref[:, sl] = a
        b_ref[:, sl] = jnp.sqrt(-jnp.tanh(log_a) * (a * a + 1.0)) * (i * xc[:, sl])

    @pl.when(t == 0)
    def _():
        h_ref[...] = jnp.zeros_like(h_ref)

    sub = lax.broadcasted_iota(jnp.int32, (SUBLANE, w), 0)
    ntile = tt // SUBLANE

    def scan(rev):
        def body(jt, h):
            r0 = pl.multiple_of((ntile - 1 - jt if rev else jt) * SUBLANE, SUBLANE)
            tile = jnp.zeros((SUBLANE, w), F32)
            at = a_ref[pl.ds(r0, SUBLANE), :]
            bt = b_ref[pl.ds(r0, SUBLANE), :]
            for r in (range(SUBLANE - 1, -1, -1) if rev else range(SUBLANE)):
                h = at[r:r + 1, :] * h + bt[r:r + 1, :]
                tile = jnp.where(sub == r, h, tile)
            o_ref[0, pl.ds(r0, SUBLANE), :] = tile
            return h

        h_ref[...] = lax.fori_loop(0, ntile, body, h_ref[...])

    pl.when(d == 0)(functools.partial(scan, False))
    pl.when(d == 1)(functools.partial(scan, True))


def _lru(proj, bsz, conv_w, conv_b, gate_w, gate_b, lam, tt=256):
    n = proj.shape[0]
    seq = n // bsz
    tt = min(tt, seq)
    nt = seq // tt
    w = W_LRU
    r8 = tt // SUBLANE
    gw = jnp.concatenate([gate_w[:, 0], gate_w[:, 1]], axis=-1).astype(BF16)

    def tb(d, t):
        return t + d * (nt - 1 - 2 * t)

    return pl.pallas_call(
        functools.partial(_lru_kernel, tt=tt),
        grid=(2, bsz, nt),
        in_specs=[
            pl.BlockSpec((SUBLANE, w), lambda d, b, t: (jnp.maximum((b * nt + tb(d, t)) * r8 - 1, 0), 0)),
            pl.BlockSpec((tt, w), lambda d, b, t: (b * nt + tb(d, t), 0)),
            pl.BlockSpec((SUBLANE, w), lambda d, b, t: (jnp.minimum((b * nt + tb(d, t) + 1) * r8, n // SUBLANE - 1), 0)),
            pl.BlockSpec((CONV_W, w), lambda d, b, t: (0, 0)),
            pl.BlockSpec((1, w), lambda d, b, t: (0, 0)),
            pl.BlockSpec((1, LRU_HEADS, LRU_BLOCK, 2 * LRU_BLOCK), lambda d, b, t: (d, 0, 0, 0)),
            pl.BlockSpec((1, 2, w), lambda d, b, t: (d, 0, 0)),
            pl.BlockSpec((1, 1, w), lambda d, b, t: (d, 0, 0)),
        ],
        out_specs=pl.BlockSpec((1, tt, w), lambda d, b, t: (d, b * nt + tb(d, t), 0)),
        out_shape=jax.ShapeDtypeStruct((2, n, w), F32),
        scratch_shapes=[pltpu.VMEM((tt + 2 * SUBLANE, w), F32), pltpu.VMEM((tt, w), F32),
                        pltpu.VMEM((tt, w), F32), pltpu.VMEM((1, w), F32)],
        compiler_params=_params("arbitrary", "arbitrary", "arbitrary"),
        name="rg_lru",
    )(proj, proj, proj, conv_w, conv_b.reshape(1, w), gw, gate_b, lam.reshape(2, 1, w))


def _s5_discretise(a_re, a_im, log_dt, b_re, b_im):
    dt = jnp.exp(log_dt)[:, None]
    mag = jnp.exp(dt * a_re)
    lr = mag * jnp.cos(dt * a_im)
    li = mag * jnp.sin(dt * a_im)
    den = a_re * a_re + a_im * a_im
    nr = lr - 1.0
    cr = (nr * a_re + li * a_im) / den
    ci = (li * a_re - nr * a_im) / den
    bbr = cr[..., None] * b_re - ci[..., None] * b_im
    bbi = cr[..., None] * b_im + ci[..., None] * b_re
    return lr, li, bbr, bbi


def _s5_operators(a_re, a_im, log_dt, b_re, b_im, c_re, c_im):
    L = S5_CHUNK
    G, P, C = S5_GROUPS, S5_P, S5_GROUP
    kfs, mins, mouts, lams = [], [], [], []
    for d in range(2):
        lr, li, bbr, bbi = _s5_discretise(a_re[d], a_im[d], log_dt[d], b_re, b_im)
        pr, pi = [jnp.ones_like(lr)], [jnp.zeros_like(li)]
        for _ in range(L):
            pr, pi = pr + [pr[-1] * lr - pi[-1] * li], pi + [pr[-1] * li + pi[-1] * lr]
        pr, pi = jnp.stack(pr), jnp.stack(pi)
        clr = c_re[None] * pr[:, :, None, :] - c_im[None] * pi[:, :, None, :]
        cli = c_re[None] * pi[:, :, None, :] + c_im[None] * pr[:, :, None, :]
        kfs.append(jnp.einsum('jgcp,gpk->jgck', clr[:L], bbr, precision=HI)
                   - jnp.einsum('jgcp,gpk->jgck', cli[:L], bbi, precision=HI))
        blr = pr[:, :, :, None] * bbr[None] - pi[:, :, :, None] * bbi[None]
        bli = pr[:, :, :, None] * bbi[None] + pi[:, :, :, None] * bbr[None]
        if d == 0:
            e_in = jnp.arange(L - 1, -1, -1)
            e_out = jnp.arange(1, L + 1)
        else:
            e_in = jnp.arange(L)
            e_out = jnp.arange(L, 0, -1)
        m_in = jnp.stack([blr[e_in], bli[e_in]], axis=0)
        mins.append(m_in.transpose(2, 1, 4, 0, 3))
        m_out = jnp.stack([clr[e_out], -cli[e_out]], axis=0)
        mouts.append(m_out.transpose(2, 0, 4, 1, 3))
        lams.append(jnp.stack([pr[L], pi[L]], axis=1))
    kf, kb = kfs
    kfull = jnp.concatenate([kb[:0:-1], (kf[0] + kb[0])[None], kf[1:]], axis=0)
    idx = jnp.arange(L)[None, :] - jnp.arange(L)[:, None] + (L - 1)
    tmat = kfull[idx].transpose(2, 0, 4, 1, 3).reshape(G, L * C, L * C)
    m_in = jnp.concatenate(mins, axis=3).reshape(G, L * C, 4, P)
    m_out = jnp.concatenate(mouts, axis=1).reshape(G, 4, P, L * C)
    lam_l = jnp.concatenate(lams, axis=1)
    eye2 = jnp.eye(2, dtype=F32)
    np_ = G // 2
    w = L * C
    t_pair = jnp.einsum('pgrc,gh->pgrhc', tmat.reshape(np_, 2, w, w), eye2).reshape(np_, 2 * w, 2 * w)
    min_pair = jnp.einsum('pgrqj,gh->pgrqhj', m_in.reshape(np_, 2, w, 4, P), eye2).reshape(np_, 2 * w, 8 * P)
    mout_pair = jnp.einsum('pgqjc,gh->pqgjhc', m_out.reshape(np_, 2, 4, P, w), eye2).reshape(np_, 8 * P, 2 * w)
    lam_pair = lam_l.reshape(np_, 2, 4, P).transpose(0, 2, 1, 3).reshape(np_, 1, 8 * P)
    return t_pair.astype(BF16), min_pair.astype(BF16), mout_pair.astype(BF16), lam_pair


def _s5_kernel(u_ref, min_ref, t_ref, mout_ref, lam_ref, y_ref, hloc_ref, hin_ref, *, nb):
    rows = u_ref.shape[0]
    rb = rows // nb
    hw = LANE
    u = u_ref[...]
    hloc_ref[...] = jnp.dot(u, min_ref[0], preferred_element_type=F32)
    lam = lam_ref[0]
    lrf, lif = lam[:, 0:hw], lam[:, hw:2 * hw]
    lrb, lib = lam[:, 2 * hw:3 * hw], lam[:, 3 * hw:4 * hw]

    sub = lax.broadcasted_iota(jnp.int32, (SUBLANE, hw), 0)

    def body(it, carry):
        new = []
        for b in range(nb):
            fr, fi, br, bi = carry[4 * b:4 * b + 4]
            f0 = pl.multiple_of(b * rb + it * SUBLANE, SUBLANE)
            b0 = pl.multiple_of(b * rb + rb - SUBLANE - it * SUBLANE, SUBLANE)
            tiles = [jnp.zeros((SUBLANE, hw), F32)] * 4
            lf = hloc_ref[pl.ds(f0, SUBLANE), 0:2 * hw]
            lb = hloc_ref[pl.ds(b0, SUBLANE), 2 * hw:4 * hw]
            for r in range(SUBLANE):
                q = SUBLANE - 1 - r
                tiles = [jnp.where(sub == r, fr, tiles[0]), jnp.where(sub == r, fi, tiles[1]),
                         jnp.where(sub == q, br, tiles[2]), jnp.where(sub == q, bi, tiles[3])]
                fr, fi, br, bi = (lrf * fr - lif * fi + lf[r:r + 1, 0:hw],
                                  lrf * fi + lif * fr + lf[r:r + 1, hw:2 * hw],
                                  lrb * br - lib * bi + lb[q:q + 1, 0:hw],
                                  lrb * bi + lib * br + lb[q:q + 1, hw:2 * hw])
            hin_ref[pl.ds(f0, SUBLANE), 0:hw] = tiles[0]
            hin_ref[pl.ds(f0, SUBLANE), hw:2 * hw] = tiles[1]
            hin_ref[pl.ds(b0, SUBLANE), 2 * hw:3 * hw] = tiles[2]
            hin_ref[pl.ds(b0, SUBLANE), 3 * hw:4 * hw] = tiles[3]
            new += [fr, fi, br, bi]
        return tuple(new)

    zero = jnp.zeros((1, hw), F32)
    lax.fori_loop(0, rb // SUBLANE, body, (zero,) * (4 * nb))
    y_ref[...] = (jnp.dot(u, t_ref[0], preferred_element_type=F32)
                  + jnp.dot(hin_ref[...].astype(BF16), mout_ref[0], preferred_element_type=F32))


def _s5(u, bsz, ops):
    t_pair, min_pair, mout_pair, lam_pair = ops
    n = u.shape[0]
    L, G, C = S5_CHUNK, S5_GROUPS, S5_GROUP
    rows = n // L
    pw = 2 * L * C
    u2 = u.reshape(rows, L, G, C).transpose(0, 2, 1, 3).reshape(rows, G * L * C).astype(BF16)
    y2 = pl.pallas_call(
        functools.partial(_s5_kernel, nb=bsz),
        grid=(G // 2,),
        in_specs=[
            pl.BlockSpec((rows, pw), lambda p: (0, p)),
            pl.BlockSpec((1, pw, 4 * LANE), lambda p: (p, 0, 0)),
            pl.BlockSpec((1, pw, pw), lambda p: (p, 0, 0)),
            pl.BlockSpec((1, 4 * LANE, pw), lambda p: (p, 0, 0)),
            pl.BlockSpec((1, 1, 4 * LANE), lambda p: (p, 0, 0)),
        ],
        out_specs=pl.BlockSpec((rows, pw), lambda p: (0, p)),
        out_shape=jax.ShapeDtypeStruct((rows, G * L * C), F32),
        scratch_shapes=[pltpu.VMEM((rows, 4 * LANE), F32), pltpu.VMEM((rows, 4 * LANE), F32)],
        compiler_params=_params("arbitrary"),
        name="s5",
    )(u2, min_pair, t_pair, mout_pair, lam_pair)
    return y2.reshape(rows, G, L, C).transpose(0, 2, 1, 3).reshape(n, G * C)


def _ab_out_kernel(x_ref, gr_ref, hl_ref, yb_ref, w_ref, o_ref):
    k = gr_ref.shape[1]
    ya = jax.nn.gelu(gr_ref[...]) * (hl_ref[0] + hl_ref[1])
    o_ref[...] = (x_ref[...]
                  + jnp.dot(ya.astype(BF16), w_ref[0:k, :], preferred_element_type=F32)
                  + jnp.dot(yb_ref[...].astype(BF16), w_ref[k:, :], preferred_element_type=F32))


def _ab_out(x, proj, hl, yb, w, tm=512):
    n, d = x.shape
    k = W_LRU
    tm = min(tm, n)
    return pl.pallas_call(
        _ab_out_kernel,
        grid=(n // tm,),
        in_specs=[
            pl.BlockSpec((tm, d), lambda i: (i, 0)),
            pl.BlockSpec((tm, k), lambda i: (i, 1)),
            pl.BlockSpec((2, tm, k), lambda i: (0, i, 0)),
            pl.BlockSpec((tm, k), lambda i: (i, 0)),
            pl.BlockSpec((2 * k, d), lambda i: (0, 0)),
        ],
        out_specs=pl.BlockSpec((tm, d), lambda i: (i, 0)),
        out_shape=jax.ShapeDtypeStruct((n, d), F32),
        compiler_params=_params("arbitrary"),
        name="ab_out",
    )(x, proj, hl, yb, w)


def _cd_out_kernel(x_ref, yc_ref, ys_ref, u_ref, d_ref, wg_ref, w_ref, o_ref):
    k = yc_ref.shape[1]
    y = jax.nn.gelu(ys_ref[...] + d_ref[...] * u_ref[...])
    yd = y * jax.nn.sigmoid(jnp.dot(y.astype(BF16), wg_ref[...], preferred_element_type=F32))
    o_ref[...] = (x_ref[...]
                  + jnp.dot(yc_ref[...].astype(BF16), w_ref[0:k, :], preferred_element_type=F32)
                  + jnp.dot(yd.astype(BF16), w_ref[k:, :], preferred_element_type=F32))


def _cd_out(x, yc, ys, proj, u_col_block, s5_d, w_glu, w, tm=512):
    n, d = x.shape
    k = S5_W
    tm = min(tm, n)
    return pl.pallas_call(
        _cd_out_kernel,
        grid=(n // tm,),
        in_specs=[
            pl.BlockSpec((tm, d), lambda i: (i, 0)),
            pl.BlockSpec((tm, k), lambda i: (i, 0)),
            pl.BlockSpec((tm, k), lambda i: (i, 0)),
            pl.BlockSpec((tm, k), lambda i: (i, u_col_block)),
            pl.BlockSpec((1, k), lambda i: (0, 0)),
            pl.BlockSpec((k, k), lambda i: (0, 0)),
            pl.BlockSpec((2 * k, d), lambda i: (0, 0)),
        ],
        out_specs=pl.BlockSpec((tm, d), lambda i: (i, 0)),
        out_shape=jax.ShapeDtypeStruct((n, d), F32),
        compiler_params=_params("arbitrary"),
        name="cd_out",
    )(x, yc, ys, proj, s5_d.reshape(1, k), w_glu, w)


def _headwise_rms_norm(t, g):
    y = t * lax.rsqrt(jnp.mean(t * t, axis=-1, keepdims=True) + NORM_EPS)
    return y.reshape(t.shape[:2] + (-1,)) * g


def _heads(t, n):
    return t.reshape(t.shape[:2] + (n, -1)).transpose(0, 2, 1, 3)


def _flip_seq(t):
    return jnp.flip(t, axis=2)


def _to_chunks(t, chunk):
    b, h, s = t.shape[:3]
    t = t.reshape((b, h, s // chunk, chunk) + t.shape[3:])
    return jnp.moveaxis(t, 2, 0)


def _from_chunks(t):
    nc, b, h, l = t.shape[:4]
    return jnp.moveaxis(t, 0, 2).reshape((b, h, nc * l) + t.shape[4:])


def _mlstm_chunkwise(q, k, v, ig, lf):
    bsz, nh, _, dh = q.shape
    L = MLSTM_CHUNK
    mask = jnp.tril(jnp.ones((L, L), dtype=bool))
    xs = tuple(_to_chunks(t, L) for t in (q, k, v, ig, lf))

    def step(carry, inp):
        c_st, n_st, m_st = carry
        qt, kt, vt, it, ft = inp
        cum = jnp.cumsum(ft, axis=-1)
        dmat = jnp.where(mask, cum[..., :, None] - cum[..., None, :] + it[..., None, :], -jnp.inf)
        m_inter = cum + m_st[..., None]
        m_t = jnp.maximum(jnp.max(dmat, axis=-1), m_inter)
        w_intra = jnp.exp(dmat - m_t[..., None])
        w_inter = jnp.exp(m_inter - m_t)
        s = jnp.einsum('bhtd,bhsd->bhts', qt, kt) * w_intra
        num = jnp.einsum('bhts,bhse->bhte', s, vt) + w_inter[..., None] * jnp.einsum('bhtd,bhde->bhte', qt, c_st)
        den = jnp.sum(s, axis=-1) + w_inter * jnp.einsum('bhtd,bhd->bht', qt, n_st)
        h = num / jnp.maximum(jnp.abs(den), jnp.exp(-m_t))[..., None]
        tot = cum[..., -1]
        dec_s = tot[..., None] - cum + it
        m_new = jnp.maximum(tot + m_st, jnp.max(dec_s, axis=-1))
        ws = jnp.exp(dec_s - m_new[..., None])
        wc = jnp.exp(tot + m_st - m_new)
        c_new = wc[..., None, None] * c_st + jnp.einsum('bhs,bhsd,bhse->bhde', ws, kt, vt)
        n_new = wc[..., None] * n_st + jnp.einsum('bhs,bhsd->bhd', ws, kt)
        return (c_new, n_new, m_new), h

    init = (jnp.zeros((bsz, nh, dh, dh), q.dtype), jnp.zeros((bsz, nh, dh), q.dtype),
            jnp.zeros((bsz, nh), q.dtype))
    _, hs = lax.scan(step, init, xs)
    return _from_chunks(hs)


def _mlstm_core(q, k, v, o, gates, mlstm_gate_b, mlstm_norm):
    bsz, s, _ = q.shape
    q = _heads(q, MLSTM_HEADS) * (MLSTM_DH ** -0.5)
    k = _heads(k, MLSTM_HEADS)
    v = _heads(v, MLSTM_HEADS)
    g = (gates.reshape(bsz, s, 2, 2, MLSTM_HEADS) + mlstm_gate_b).transpose(2, 3, 0, 4, 1)
    h_f = _mlstm_chunkwise(q, k, v, g[0, 0], jax.nn.log_sigmoid(g[0, 1]))
    h_b = _flip_seq(_mlstm_chunkwise(_flip_seq(q), _flip_seq(k), _flip_seq(v), _flip_seq(g[1, 0]),
                                     _flip_seq(jax.nn.log_sigmoid(g[1, 1]))))
    hm = _headwise_rms_norm((h_f + h_b).transpose(0, 2, 1, 3), mlstm_norm)
    return jax.nn.sigmoid(o) * hm


def _gla_chunked(q, k, v, la):
    bsz, nh, _, dk = q.shape
    dv = v.shape[-1]
    L = GLA_CHUNK
    mask = jnp.tril(jnp.ones((L, L), dtype=bool))[..., None]
    xs = tuple(_to_chunks(t, L) for t in (q, k, v, la))

    def step(s_st, inp):
        qt, kt, vt, at = inp
        cum = jnp.cumsum(at, axis=2)
        rel = jnp.where(mask, cum[:, :, :, None, :] - cum[:, :, None, :, :], -jnp.inf)
        attn = jnp.einsum('bhtd,bhsd,bhtsd->bhts', qt, kt, jnp.exp(rel))
        o = jnp.einsum('bhts,bhse->bhte', attn, vt) + jnp.einsum('bhtd,bhde->bhte', qt * jnp.exp(cum), s_st)
        tot = cum[:, :, -1]
        s_new = jnp.exp(tot)[..., None] * s_st + jnp.einsum('bhsd,bhse->bhde', kt * jnp.exp(tot[:, :, None] - cum), vt)
        return s_new, o

    _, os_ = lax.scan(step, jnp.zeros((bsz, nh, dk, dv), q.dtype), xs)
    return _from_chunks(os_)


def _gla_core(q, k, v, r, glr, gla_w_gate2, gla_gate_b, gla_norm):
    bsz, s, _ = q.shape
    q = _heads(q, GLA_HEADS) * (GLA_DK ** -0.5)
    k = _heads(k, GLA_HEADS)
    v = _heads(v, GLA_HEADS)
    low = glr.reshape(bsz, s, 2, GLA_RANK)
    gate_pre = jnp.einsum('bsdr,drk->dbsk', low, gla_w_gate2) + gla_gate_b[:, None, None, :]
    la = (jax.nn.log_sigmoid(gate_pre) / GLA_TAU).reshape(2, bsz, s, GLA_HEADS, GLA_DK).transpose(0, 1, 3, 2, 4)
    o_f = _gla_chunked(q, k, v, la[0])
    o_b = _flip_seq(_gla_chunked(_flip_seq(q), _flip_seq(k), _flip_seq(v), _flip_seq(la[1])))
    return _headwise_rms_norm((o_f + o_b).transpose(0, 2, 1, 3), gla_norm) * jax.nn.silu(r)


def _split_small(w, n_main):
    tail = w[:, n_main:]
    tail = jnp.pad(tail, ((0, 0), (0, LANE - tail.shape[1])))
    return w[:, :n_main].astype(BF16), tail.astype(BF16)


def kernel(x, norm_ffn1, ffn1_w_gu, ffn1_w_down, norm_mix, norm_ffn2, ffn2_w_gu, ffn2_w_down,
           ab_w_in, lru_conv_w, lru_conv_b, lru_gate_w, lru_gate_b, lru_lambda, mlstm_gate_b,
           mlstm_norm, ab_w_out, cd_w_in, gla_w_gate2, gla_gate_b, gla_norm, s5_a_re, s5_a_im,
           s5_log_dt, s5_b_re, s5_b_im, s5_c_re, s5_c_im, s5_d, s5_w_glu, cd_w_out, final_norm):
    bsz, seq, d = x.shape
    n = bsz * seq
    depth = norm_ffn1.shape[0]
    xf = x.reshape(n, d)
    for l in range(depth):
        xf = _ffn(xf, norm_ffn1[l], ffn1_w_gu[l].astype(BF16), ffn1_w_down[l].astype(BF16))
        j = l // 2
        if l % 2 == 0:
            w_main, w_tail = _split_small(ab_w_in[j], 2 * W_LRU + 4 * W_MLSTM)
            proj, tail = _norm_proj(xf, norm_mix[l], w_main, w_tail)
            hl = _lru(proj, bsz, lru_conv_w[j], lru_conv_b[j], lru_gate_w[j], lru_gate_b[j], lru_lambda[j])
            p3 = proj.reshape(bsz, seq, -1)
            q, k, v, o = (p3[..., 2 * W_LRU + i * W_MLSTM:2 * W_LRU + (i + 1) * W_MLSTM] for i in range(4))
            gates = tail[:, :4 * MLSTM_HEADS].reshape(bsz, seq, -1)
            yb = _mlstm_core(q, k, v, o, gates, mlstm_gate_b[j], mlstm_norm[j])
            xf = _ab_out(xf, proj, hl, yb.reshape(n, -1), ab_w_out[j].astype(BF16))
        else:
            w = cd_w_in[j]
            n_gla = 2 * GLA_QK + 2 * GLA_V
            w = jnp.concatenate([w[:, :n_gla], w[:, n_gla + 2 * GLA_RANK:], w[:, n_gla:n_gla + 2 * GLA_RANK]], axis=1)
            w_main, w_tail = _split_small(w, n_gla + S5_W)
            proj, tail = _norm_proj(xf, norm_mix[l], w_main, w_tail)
            p3 = proj.reshape(bsz, seq, -1)
            q, k = p3[..., :GLA_QK], p3[..., GLA_QK:2 * GLA_QK]
            v, r = p3[..., 2 * GLA_QK:2 * GLA_QK + GLA_V], p3[..., 2 * GLA_QK + GLA_V:n_gla]
            glr = tail[:, :2 * GLA_RANK].reshape(bsz, seq, -1)
            yc = _gla_core(q, k, v, r, glr, gla_w_gate2[j], gla_gate_b[j], gla_norm[j])
            ops = _s5_operators(s5_a_re[j], s5_a_im[j], s5_log_dt[j], s5_b_re[j], s5_b_im[j],
                                s5_c_re[j], s5_c_im[j])
            ys = _s5(proj[:, n_gla:], bsz, ops)
            xf = _cd_out(xf, yc.reshape(n, -1), ys, proj, n_gla // S5_W, s5_d[j],
                         s5_w_glu[j].astype(BF16), cd_w_out[j].astype(BF16))
        xf = _ffn(xf, norm_ffn2[l], ffn2_w_gu[l].astype(BF16), ffn2_w_down[l].astype(BF16),
                  final_g=final_norm if l == depth - 1 else None)
    return xf.reshape(bsz, seq, d)
```

```python
import functools

import numpy as np
import jax
import jax.numpy as jnp
from jax import lax
from jax.experimental import pallas as pl
from jax.experimental.pallas import tpu as pltpu

D_MODEL = 2048
D_FF = 5632
NORM_EPS = 1e-6

W_LRU = 1024
LRU_HEADS = 8
LRU_BLOCK = W_LRU // LRU_HEADS
CONV_W = 4
CONV_PAD_L = 2
LRU_C = 8.0

W_MLSTM = 1024
MLSTM_HEADS = 4
MLSTM_DH = W_MLSTM // MLSTM_HEADS
MLSTM_CHUNK = 64

GLA_HEADS = 4
GLA_DK = 128
GLA_DV = 256
GLA_QK = GLA_HEADS * GLA_DK
GLA_V = GLA_HEADS * GLA_DV
GLA_RANK = 16
GLA_TAU = 16.0
GLA_CHUNK = 64

S5_W = 1024
S5_GROUP = 16
S5_GROUPS = S5_W // S5_GROUP
S5_P = 64
S5_CHUNK = 16

LANE = 128
SUBLANE = 8
VMEM_LIMIT = 52 * 1024 * 1024

BF16 = jnp.bfloat16
F32 = jnp.float32
HI = lax.Precision.HIGHEST


def _rms(x, g):
    ms = jnp.mean(x * x, axis=-1, keepdims=True)
    return x * lax.rsqrt(ms + NORM_EPS) * g


def _softplus(z):
    return jnp.maximum(z, 0.0) + jnp.log1p(jnp.exp(-jnp.abs(z)))


def _params(*sem):
    return pltpu.CompilerParams(dimension_semantics=sem, vmem_limit_bytes=VMEM_LIMIT)


def _ffn_kernel(x_ref, g_ref, wg_ref, wu_ref, wd_ref, *rest, final):
    if final:
        fg_ref, o_ref, h_ref = rest
    else:
        o_ref, h_ref = rest
    j = pl.program_id(1)

    @pl.when(j == 0)
    def _():
        x = x_ref[...]
        h_ref[...] = _rms(x, g_ref[...]).astype(BF16)
        o_ref[...] = x

    h = h_ref[...]
    g = jnp.dot(h, wg_ref[...], preferred_element_type=F32)
    u = jnp.dot(h, wu_ref[...], preferred_element_type=F32)
    a = (0.5 * g * jax.nn.sigmoid(g) * u).astype(BF16)
    o_ref[...] += jnp.dot(a, wd_ref[...], preferred_element_type=F32)

    if final:
        @pl.when(j == pl.num_programs(1) - 1)
        def _():
            o_ref[...] = _rms(o_ref[...], fg_ref[...])


def _ffn(x, g, w_gu, w_down, final_g=None, tm=512, tf=512):
    n, d = x.shape
    f = w_down.shape[0]
    tm = min(tm, n)
    nj = f // tf
    final = final_g is not None
    in_specs = [
        pl.BlockSpec((tm, d), lambda i, j: (i, 0)),
        pl.BlockSpec((1, d), lambda i, j: (0, 0)),
        pl.BlockSpec((d, tf), lambda i, j: (0, j)),
        pl.BlockSpec((d, tf), lambda i, j: (0, j + nj)),
        pl.BlockSpec((tf, d), lambda i, j: (j, 0)),
    ]
    args = [x, g.reshape(1, d), w_gu, w_gu, w_down]
    if final:
        in_specs.append(pl.BlockSpec((1, d), lambda i, j: (0, 0)))
        args.append(final_g.reshape(1, d))
    return pl.pallas_call(
        functools.partial(_ffn_kernel, final=final),
        grid=(n // tm, nj),
        in_specs=in_specs,
        out_specs=pl.BlockSpec((tm, d), lambda i, j: (i, 0)),
        out_shape=jax.ShapeDtypeStruct((n, d), F32),
        scratch_shapes=[pltpu.VMEM((tm, d), BF16)],
        compiler_params=_params("arbitrary", "arbitrary"),
        name="ffn_final" if final else "ffn",
    )(*args)


def _norm_proj_kernel(x_ref, g_ref, w_ref, ws_ref, o_ref, os_ref, h_ref):
    @pl.when(pl.program_id(1) == 0)
    def _():
        h = _rms(x_ref[...], g_ref[...]).astype(BF16)
        h_ref[...] = h
        os_ref[...] = jnp.dot(h, ws_ref[...], preferred_element_type=F32)

    o_ref[...] = jnp.dot(h_ref[...], w_ref[...], preferred_element_type=F32)


def _norm_proj(x, g, w, w_small, tm=512, tn=512):
    n, d = x.shape
    m = w.shape[1]
    ms = w_small.shape[1]
    tm = min(tm, n)
    return pl.pallas_call(
        _norm_proj_kernel,
        grid=(n // tm, m // tn),
        in_specs=[
            pl.BlockSpec((tm, d), lambda i, j: (i, 0)),
            pl.BlockSpec((1, d), lambda i, j: (0, 0)),
            pl.BlockSpec((d, tn), lambda i, j: (0, j)),
            pl.BlockSpec((d, ms), lambda i, j: (0, 0)),
        ],
        out_specs=[pl.BlockSpec((tm, tn), lambda i, j: (i, j)),
                   pl.BlockSpec((tm, ms), lambda i, j: (i, 0))],
        out_shape=[jax.ShapeDtypeStruct((n, m), F32), jax.ShapeDtypeStruct((n, ms), F32)],
        scratch_shapes=[pltpu.VMEM((tm, d), BF16)],
        compiler_params=_params("arbitrary", "arbitrary"),
        name="norm_proj",
    )(x, g.reshape(1, d), w, w_small)


def _lru_kernel(xp_ref, x_ref, xn_ref, cw_ref, cb_ref, gw_ref, gb_ref, lam_ref, o_ref,
                xs_ref, a_ref, b_ref, h_ref, *, tt):
    d = pl.program_id(0)
    t = pl.program_id(2)
    nt = pl.num_programs(2)
    tb = jnp.where(d == 0, t, nt - 1 - t)
    w = x_ref.shape[1]

    xs_ref[0:SUBLANE, :] = jnp.where(tb > 0, xp_ref[...], 0.0)
    xs_ref[SUBLANE:SUBLANE + tt, :] = x_ref[...]
    xs_ref[SUBLANE + tt:2 * SUBLANE + tt, :] = jnp.where(tb < nt - 1, xn_ref[...], 0.0)
    cw = cw_ref[...]
    xc = cb_ref[...]
    for k in range(CONV_W):
        off = SUBLANE + k - CONV_PAD_L
        xc = xc + cw[k:k + 1, :] * xs_ref[off:off + tt, :]

    xcb = xc.astype(BF16)
    gb = gb_ref[0]
    sp = _softplus(-lam_ref[0])
    for hd in range(w // LRU_BLOCK):
        sl = slice(hd * LRU_BLOCK, (hd + 1) * LRU_BLOCK)
        z = jnp.dot(xcb[:, sl], gw_ref[0, hd], preferred_element_type=F32)
        r = jax.nn.sigmoid(z[:, :LRU_BLOCK] + gb[0:1, sl])
        i = jax.nn.sigmoid(z[:, LRU_BLOCK:] + gb[1:2, sl])
        log_a = (-LRU_C * sp[:, sl]) * r
        a = jnp.exp(log_a)
        a_ref[:, sl] = a
        b_ref[:, sl] = jnp.sqrt(-jnp.tanh(log_a) * (a * a + 1.0)) * (i * xc[:, sl])

    @pl.when(t == 0)
    def _():
        h_ref[...] = jnp.zeros_like(h_ref)

    sub = lax.broadcasted_iota(jnp.int32, (SUBLANE, w), 0)
    ntile = tt // SUBLANE

    def scan(rev):
        def body(jt, h):
            r0 = pl.multiple_of((ntile - 1 - jt if rev else jt) * SUBLANE, SUBLANE)
            tile = jnp.zeros((SUBLANE, w), F32)
            at = a_ref[pl.ds(r0, SUBLANE), :]
            bt = b_ref[pl.ds(r0, SUBLANE), :]
            for r in (range(SUBLANE - 1, -1, -1) if rev else range(SUBLANE)):
                h = at[r:r + 1, :] * h + bt[r:r + 1, :]
                tile = jnp.where(sub == r, h, tile)
            o_ref[0, pl.ds(r0, SUBLANE), :] = tile
            return h

        h_ref[...] = lax.fori_loop(0, ntile, body, h_ref[...])

    pl.when(d == 0)(functools.partial(scan, False))
    pl.when(d == 1)(functools.partial(scan, True))


def _lru(proj, bsz, conv_w, conv_b, gate_w, gate_b, lam, tt=256):
    n = proj.shape[0]
    seq = n // bsz
    tt = min(tt, seq)
    nt = seq // tt
    w = W_LRU
    r8 = tt // SUBLANE
    gw = jnp.concatenate([gate_w[:, 0], gate_w[:, 1]], axis=-1).astype(BF16)

    def tb(d, t):
        return t + d * (nt - 1 - 2 * t)

    return pl.pallas_call(
        functools.partial(_lru_kernel, tt=tt),
        grid=(2, bsz, nt),
        in_specs=[
            pl.BlockSpec((SUBLANE, w), lambda d, b, t: (jnp.maximum((b * nt + tb(d, t)) * r8 - 1, 0), 0)),
            pl.BlockSpec((tt, w), lambda d, b, t: (b * nt + tb(d, t), 0)),
            pl.BlockSpec((SUBLANE, w), lambda d, b, t: (jnp.minimum((b * nt + tb(d, t) + 1) * r8, n // SUBLANE - 1), 0)),
            pl.BlockSpec((CONV_W, w), lambda d, b, t: (0, 0)),
            pl.BlockSpec((1, w), lambda d, b, t: (0, 0)),
            pl.BlockSpec((1, LRU_HEADS, LRU_BLOCK, 2 * LRU_BLOCK), lambda d, b, t: (d, 0, 0, 0)),
            pl.BlockSpec((1, 2, w), lambda d, b, t: (d, 0, 0)),
            pl.BlockSpec((1, 1, w), lambda d, b, t: (d, 0, 0)),
        ],
        out_specs=pl.BlockSpec((1, tt, w), lambda d, b, t: (d, b * nt + tb(d, t), 0)),
        out_shape=jax.ShapeDtypeStruct((2, n, w), F32),
        scratch_shapes=[pltpu.VMEM((tt + 2 * SUBLANE, w), F32), pltpu.VMEM((tt, w), F32),
                        pltpu.VMEM((tt, w), F32), pltpu.VMEM((1, w), F32)],
        compiler_params=_params("arbitrary", "arbitrary", "arbitrary"),
        name="rg_lru",
    )(proj, proj, proj, conv_w, conv_b.reshape(1, w), gw, gate_b, lam.reshape(2, 1, w))


def _s5_discretise(a_re, a_im, log_dt, b_re, b_im):
    dt = jnp.exp(log_dt)[:, None]
    mag = jnp.exp(dt * a_re)
    lr = mag * jnp.cos(dt * a_im)
    li = mag * jnp.sin(dt * a_im)
    den = a_re * a_re + a_im * a_im
    nr = lr - 1.0
    cr = (nr * a_re + li * a_im) / den
    ci = (li * a_re - nr * a_im) / den
    bbr = cr[..., None] * b_re - ci[..., None] * b_im
    bbi = cr[..., None] * b_im + ci[..., None] * b_re
    return lr, li, bbr, bbi


def _s5_operators(a_re, a_im, log_dt, b_re, b_im, c_re, c_im):
    L = S5_CHUNK
    G, P, C = S5_GROUPS, S5_P, S5_GROUP
    kfs, mins, mouts, lams = [], [], [], []
    for d in range(2):
        lr, li, bbr, bbi = _s5_discretise(a_re[d], a_im[d], log_dt[d], b_re, b_im)
        pr, pi = [jnp.ones_like(lr)], [jnp.zeros_like(li)]
        for _ in range(L):
            pr, pi = pr + [pr[-1] * lr - pi[-1] * li], pi + [pr[-1] * li + pi[-1] * lr]
        pr, pi = jnp.stack(pr), jnp.stack(pi)
        clr = c_re[None] * pr[:, :, None, :] - c_im[None] * pi[:, :, None, :]
        cli = c_re[None] * pi[:, :, None, :] + c_im[None] * pr[:, :, None, :]
        kfs.append(jnp.einsum('jgcp,gpk->jgck', clr[:L], bbr, precision=HI)
                   - jnp.einsum('jgcp,gpk->jgck', cli[:L], bbi, precision=HI))
        blr = pr[:, :, :, None] * bbr[None] - pi[:, :, :, None] * bbi[None]
        bli = pr[:, :, :, None] * bbi[None] + pi[:, :, :, None] * bbr[None]
        if d == 0:
            e_in = jnp.arange(L - 1, -1, -1)
            e_out = jnp.arange(1, L + 1)
        else:
            e_in = jnp.arange(L)
            e_out = jnp.arange(L, 0, -1)
        m_in = jnp.stack([blr[e_in], bli[e_in]], axis=0)
        mins.append(m_in.transpose(2, 1, 4, 0, 3))
        m_out = jnp.stack([clr[e_out], -cli[e_out]], axis=0)
        mouts.append(m_out.transpose(2, 0, 4, 1, 3))
        lams.append(jnp.stack([pr[L], pi[L]], axis=1))
    kf, kb = kfs
    kfull = jnp.concatenate([kb[:0:-1], (kf[0] + kb[0])[None], kf[1:]], axis=0)
    idx = jnp.arange(L)[None, :] - jnp.arange(L)[:, None] + (L - 1)
    tmat = kfull[idx].transpose(2, 0, 4, 1, 3).reshape(G, L * C, L * C)
    m_in = jnp.concatenate(mins, axis=3).reshape(G, L * C, 4, P)
    m_out = jnp.concatenate(mouts, axis=1).reshape(G, 4, P, L * C)
    lam_l = jnp.concatenate(lams, axis=1)
    eye2 = jnp.eye(2, dtype=F32)
    np_ = G // 2
    w = L * C
    t_pair = jnp.einsum('pgrc,gh->pgrhc', tmat.reshape(np_, 2, w, w), eye2).reshape(np_, 2 * w, 2 * w)
    min_pair = jnp.einsum('pgrqj,gh->pgrqhj', m_in.reshape(np_, 2, w, 4, P), eye2).reshape(np_, 2 * w, 8 * P)
    mout_pair = jnp.einsum('pgqjc,gh->pqgjhc', m_out.reshape(np_, 2, 4, P, w), eye2).reshape(np_, 8 * P, 2 * w)
    lam_pair = lam_l.reshape(np_, 2, 4, P).transpose(0, 2, 1, 3).reshape(np_, 1, 8 * P)
    return t_pair.astype(BF16), min_pair.astype(BF16), mout_pair.astype(BF16), lam_pair


def _s5_kernel(u_ref, min_ref, t_ref, mout_ref, lam_ref, y_ref, hloc_ref, hin_ref, *, nb):
    rows = u_ref.shape[0]
    rb = rows // nb
    hw = LANE
    u = u_ref[...]
    hloc_ref[...] = jnp.dot(u, min_ref[0], preferred_element_type=F32)
    lam = lam_ref[0]
    lrf, lif = lam[:, 0:hw], lam[:, hw:2 * hw]
    lrb, lib = lam[:, 2 * hw:3 * hw], lam[:, 3 * hw:4 * hw]

    sub = lax.broadcasted_iota(jnp.int32, (SUBLANE, hw), 0)

    def body(it, carry):
        new = []
        for b in range(nb):
            fr, fi, br, bi = carry[4 * b:4 * b + 4]
            f0 = pl.multiple_of(b * rb + it * SUBLANE, SUBLANE)
            b0 = pl.multiple_of(b * rb + rb - SUBLANE - it * SUBLANE, SUBLANE)
            tiles = [jnp.zeros((SUBLANE, hw), F32)] * 4
            lf = hloc_ref[pl.ds(f0, SUBLANE), 0:2 * hw]
            lb = hloc_ref[pl.ds(b0, SUBLANE), 2 * hw:4 * hw]
            for r in range(SUBLANE):
                q = SUBLANE - 1 - r
                tiles = [jnp.where(sub == r, fr, tiles[0]), jnp.where(sub == r, fi, tiles[1]),
                         jnp.where(sub == q, br, tiles[2]), jnp.where(sub == q, bi, tiles[3])]
                fr, fi, br, bi = (lrf * fr - lif * fi + lf[r:r + 1, 0:hw],
                                  lrf * fi + lif * fr + lf[r:r + 1, hw:2 * hw],
                                  lrb * br - lib * bi + lb[q:q + 1, 0:hw],
                                  lrb * bi + lib * br + lb[q:q + 1, hw:2 * hw])
            hin_ref[pl.ds(f0, SUBLANE), 0:hw] = tiles[0]
            hin_ref[pl.ds(f0, SUBLANE), hw:2 * hw] = tiles[1]
            hin_ref[pl.ds(b0, SUBLANE), 2 * hw:3 * hw] = tiles[2]
            hin_ref[pl.ds(b0, SUBLANE), 3 * hw:4 * hw] = tiles[3]
            new += [fr, fi, br, bi]
        return tuple(new)

    zero = jnp.zeros((1, hw), F32)
    lax.fori_loop(0, rb // SUBLANE, body, (zero,) * (4 * nb))
    y_ref[...] = (jnp.dot(u, t_ref[0], preferred_element_type=F32)
                  + jnp.dot(hin_ref[...].astype(BF16), mout_ref[0], preferred_element_type=F32))


def _s5(u, bsz, ops):
    t_pair, min_pair, mout_pair, lam_pair = ops
    n = u.shape[0]
    L, G, C = S5_CHUNK, S5_GROUPS, S5_GROUP
    rows = n // L
    pw = 2 * L * C
    u2 = u.reshape(rows, L, G, C).transpose(0, 2, 1, 3).reshape(rows, G * L * C).astype(BF16)
    y2 = pl.pallas_call(
        functools.partial(_s5_kernel, nb=bsz),
        grid=(G // 2,),
        in_specs=[
            pl.BlockSpec((rows, pw), lambda p: (0, p)),
            pl.BlockSpec((1, pw, 4 * LANE), lambda p: (p, 0, 0)),
            pl.BlockSpec((1, pw, pw), lambda p: (p, 0, 0)),
            pl.BlockSpec((1, 4 * LANE, pw), lambda p: (p, 0, 0)),
            pl.BlockSpec((1, 1, 4 * LANE), lambda p: (p, 0, 0)),
        ],
        out_specs=pl.BlockSpec((rows, pw), lambda p: (0, p)),
        out_shape=jax.ShapeDtypeStruct((rows, G * L * C), F32),
        scratch_shapes=[pltpu.VMEM((rows, 4 * LANE), F32), pltpu.VMEM((rows, 4 * LANE), F32)],
        compiler_params=_params("arbitrary"),
        name="s5",
    )(u2, min_pair, t_pair, mout_pair, lam_pair)
    return y2.reshape(rows, G, L, C).transpose(0, 2, 1, 3).reshape(n, G * C)


def _log_sigmoid(z):
    return -_softplus(-z)


def _tri(length, d):
    ti = lax.broadcasted_iota(jnp.int32, (length, length), 0)
    si = lax.broadcasted_iota(jnp.int32, (length, length), 1)
    return (ti - si) * (1 - 2 * d) >= 0


def _mlstm_kernel(q_ref, k_ref, v_ref, g_ref, gb_ref, o_ref, c_ref, n_ref, m_ref):
    d = pl.program_id(0)
    length = q_ref.shape[0]
    nh, dh = MLSTM_HEADS, MLSTM_DH

    @pl.when(pl.program_id(2) == 0)
    def _():
        c_ref[...] = jnp.zeros_like(c_ref)
        n_ref[...] = jnp.zeros_like(n_ref)
        m_ref[...] = jnp.zeros_like(m_ref)

    causal = _tri(length, d)
    gp = g_ref[...] + gb_ref[...]
    gp = jnp.where(d == 0, gp, pltpu.roll(gp, LANE - 2 * nh, 1))
    lf = _log_sigmoid(gp)
    cum = jnp.dot(causal.astype(F32), lf, precision=HI, preferred_element_type=F32)
    tot = jnp.sum(lf, axis=0, keepdims=True)
    gp_t = gp.T
    cum_t = cum.T
    nt = (((1,), (1,)), ((), ()))
    tn = (((0,), (0,)), ((), ()))

    for h in range(nh):
        sl = slice(h * dh, (h + 1) * dh)
        qh = (q_ref[:, sl] * (dh ** -0.5)).astype(BF16)
        kf = k_ref[:, sl]
        kh = kf.astype(BF16)
        vh = v_ref[:, sl].astype(BF16)
        ig_c, cum_c = gp[:, h:h + 1], cum[:, nh + h:nh + h + 1]
        ig_r, cum_r = gp_t[h:h + 1, :], cum_t[nh + h:nh + h + 1, :]
        tot_h = tot[:, nh + h:nh + h + 1]
        m_st = m_ref[h:h + 1, 0:1]
        c_st = c_ref[h]
        dmat = jnp.where(causal, cum_c - cum_r + ig_r, -jnp.inf)
        m_inter = cum_c + m_st
        m_t = jnp.maximum(jnp.max(dmat, axis=1, keepdims=True), m_inter)
        w_inter = jnp.exp(m_inter - m_t)
        s = lax.dot_general(qh, kh, nt, preferred_element_type=F32) * jnp.exp(dmat - m_t)
        num = (jnp.dot(s.astype(BF16), vh, preferred_element_type=F32)
               + w_inter * jnp.dot(qh, c_st.astype(BF16), preferred_element_type=F32))
        den = (jnp.sum(s, axis=1, keepdims=True)
               + w_inter * jnp.sum(qh.astype(F32) * n_ref[h:h + 1, :], axis=1, keepdims=True))
        o_ref[0, :, sl] = num / jnp.maximum(jnp.abs(den), jnp.exp(-m_t))
        dec = tot_h - cum_c + ig_c
        m_new = jnp.maximum(tot_h + m_st, jnp.max(dec, axis=0, keepdims=True))
        wc = jnp.exp(tot_h + m_st - m_new)
        kw = jnp.exp(dec - m_new) * kf
        c_ref[h] = wc * c_st + lax.dot_general(kw.astype(BF16), vh, tn, preferred_element_type=F32)
        n_ref[h:h + 1, :] = wc * n_ref[h:h + 1, :] + jnp.sum(kw, axis=0, keepdims=True)
        m_ref[h:h + 1, :] = jnp.broadcast_to(m_new, (1, LANE))


def _mlstm(proj, tail, bsz, gate_b, length=256):
    n = proj.shape[0]
    seq = n // bsz
    length = min(length, seq)
    nc = seq // length
    w = W_MLSTM
    q_blk = 2 * W_LRU // w
    gb = jnp.pad(gate_b.reshape(1, -1), ((0, 0), (0, LANE - gate_b.size)))

    def row(d, b, c):
        return b * nc + c + d * (nc - 1 - 2 * c)

    return pl.pallas_call(
        _mlstm_kernel,
        grid=(2, bsz, nc),
        in_specs=[
            pl.BlockSpec((length, w), lambda d, b, c: (row(d, b, c), q_blk)),
            pl.BlockSpec((length, w), lambda d, b, c: (row(d, b, c), q_blk + 1)),
            pl.BlockSpec((length, w), lambda d, b, c: (row(d, b, c), q_blk + 2)),
            pl.BlockSpec((length, LANE), lambda d, b, c: (row(d, b, c), 0)),
            pl.BlockSpec((1, LANE), lambda d, b, c: (0, 0)),
        ],
        out_specs=pl.BlockSpec((1, length, w), lambda d, b, c: (d, row(d, b, c), 0)),
        out_shape=jax.ShapeDtypeStruct((2, n, w), F32),
        scratch_shapes=[pltpu.VMEM((MLSTM_HEADS, MLSTM_DH, MLSTM_DH), F32),
                        pltpu.VMEM((SUBLANE, MLSTM_DH), F32), pltpu.VMEM((SUBLANE, LANE), F32)],
        compiler_params=_params("arbitrary", "arbitrary", "arbitrary"),
        name="mlstm",
    )(proj, proj, proj, tail, gb)


GLA_SUB = 16


def _gla_chunk(q_ref, k_ref, v_ref, cum, o_ref, s_ref, rev):
    length = q_ref.shape[0]
    c = GLA_SUB
    nt = (((1,), (1,)), ((), ()))
    tn = (((0,), (0,)), ((), ()))

    def rows(a, b):
        return slice(length - b, length - a) if rev else slice(a, b)

    def row(a):
        i = length - 1 - a if rev else a
        return slice(i, i + 1)

    ti = lax.broadcasted_iota(jnp.int32, (c, 1), 0)
    for h in range(GLA_HEADS):
        kl = slice(h * GLA_DK, (h + 1) * GLA_DK)
        vl = slice(h * GLA_DV, (h + 1) * GLA_DV)
        ch = cum[:, kl]
        qs = q_ref[:, kl] * (GLA_DK ** -0.5)
        kk = k_ref[:, kl]
        vf = v_ref[:, vl]
        vv = vf.astype(BF16)
        st = s_ref[h]
        tot = ch[row(length - 1)]
        o = lax.dot_general((qs * jnp.exp(ch)).astype(BF16), st.astype(BF16), nt, preferred_element_type=F32)
        ob = [o[rows(p * c, (p + 1) * c)] for p in range(length // c)]
        m = length // 2
        while m >= c:
            for start in range(0, length, 2 * m):
                fst, sec = rows(start, start + m), rows(start + m, start + 2 * m)
                r = ch[row(start + m)]
                qh = (qs[sec] * jnp.exp(ch[sec] - r)).astype(BF16)
                kh = (kk[fst] * jnp.exp(r - ch[fst])).astype(BF16)
                att = lax.dot_general(qh, kh, nt, preferred_element_type=F32)
                contrib = jnp.dot(att.astype(BF16), vv[fst], preferred_element_type=F32)
                for p in range((start + m) // c, (start + 2 * m) // c):
                    lo = rows(p * c, (p + 1) * c).start - sec.start
                    ob[p] = ob[p] + contrib[lo:lo + c]
            m //= 2
        for p in range(length // c):
            blk = rows(p * c, (p + 1) * c)
            cb, qb, kb, vb = ch[blk], qs[blk], kk[blk], vf[blk]
            acc = ob[p]
            for s in range(c):
                sees = (ti <= s) if rev else (ti >= s)
                e = jnp.exp(jnp.where(sees, cb - cb[s:s + 1], -jnp.inf))
                a = jnp.sum(qb * kb[s:s + 1] * e, axis=1, keepdims=True)
                acc = acc + a * vb[s:s + 1]
            o_ref[0, blk, vl] = acc
        kd = (kk * jnp.exp(tot - ch)).astype(BF16)
        s_ref[h] = jnp.exp(tot) * st + lax.dot_general(vv, kd, tn, preferred_element_type=F32)


def _gla_kernel(q_ref, k_ref, v_ref, low_ref, wg_ref, bg_ref, o_ref, s_ref):
    d = pl.program_id(0)
    length = q_ref.shape[0]

    @pl.when(pl.program_id(2) == 0)
    def _():
        s_ref[...] = jnp.zeros_like(s_ref)

    gate_pre = jnp.dot(low_ref[...], wg_ref[0], precision=HI, preferred_element_type=F32) + bg_ref[0]
    la = _log_sigmoid(gate_pre) * (1.0 / GLA_TAU)
    cum = jnp.dot(_tri(length, d).astype(F32), la, precision=HI, preferred_element_type=F32)
    pl.when(d == 0)(lambda: _gla_chunk(q_ref, k_ref, v_ref, cum, o_ref, s_ref, False))
    pl.when(d == 1)(lambda: _gla_chunk(q_ref, k_ref, v_ref, cum, o_ref, s_ref, True))


def _gla(proj, tail, bsz, w_gate2, gate_b, length=64):
    n = proj.shape[0]
    seq = n // bsz
    length = min(length, seq)
    nc = seq // length
    wg = jnp.zeros((2, LANE, GLA_QK), F32)
    for d in range(2):
        wg = wg.at[d, d * GLA_RANK:(d + 1) * GLA_RANK].set(w_gate2[d])

    def row(d, b, c):
        return b * nc + c + d * (nc - 1 - 2 * c)

    return pl.pallas_call(
        _gla_kernel,
        grid=(2, bsz, nc),
        in_specs=[
            pl.BlockSpec((length, GLA_QK), lambda d, b, c: (row(d, b, c), 0)),
            pl.BlockSpec((length, GLA_QK), lambda d, b, c: (row(d, b, c), 1)),
            pl.BlockSpec((length, GLA_V), lambda d, b, c: (row(d, b, c), 2 * GLA_QK // GLA_V)),
            pl.BlockSpec((length, LANE), lambda d, b, c: (row(d, b, c), 0)),
            pl.BlockSpec((1, LANE, GLA_QK), lambda d, b, c: (d, 0, 0)),
            pl.BlockSpec((1, 1, GLA_QK), lambda d, b, c: (d, 0, 0)),
        ],
        out_specs=pl.BlockSpec((1, length, GLA_V), lambda d, b, c: (d, row(d, b, c), 0)),
        out_shape=jax.ShapeDtypeStruct((2, n, GLA_V), F32),
        scratch_shapes=[pltpu.VMEM((GLA_HEADS, GLA_DV, GLA_DK), F32)],
        compiler_params=_params("arbitrary", "arbitrary", "arbitrary"),
        name="gla",
    )(proj, proj, proj, tail, wg, gate_b.reshape(2, 1, GLA_QK))


def _head_norm(t, g, heads):
    dh = t.shape[1] // heads
    outs = []
    for h in range(heads):
        th = t[:, h * dh:(h + 1) * dh]
        outs.append(th * lax.rsqrt(jnp.mean(th * th, axis=1, keepdims=True) + NORM_EPS))
    return jnp.concatenate(outs, axis=1) * g


def _ab_out_kernel(x_ref, gr_ref, og_ref, hl_ref, hm_ref, g_ref, w_ref, o_ref):
    k = gr_ref.shape[1]
    ya = jax.nn.gelu(gr_ref[...]) * (hl_ref[0] + hl_ref[1])
    yb = jax.nn.sigmoid(og_ref[...]) * _head_norm(hm_ref[0] + hm_ref[1], g_ref[...], MLSTM_HEADS)
    o_ref[...] = (x_ref[...]
                  + jnp.dot(ya.astype(BF16), w_ref[0:k, :], preferred_element_type=F32)
                  + jnp.dot(yb.astype(BF16), w_ref[k:, :], preferred_element_type=F32))


def _ab_out(x, proj, hl, hm, norm_g, w, tm=512):
    n, d = x.shape
    k = W_LRU
    tm = min(tm, n)
    return pl.pallas_call(
        _ab_out_kernel,
        grid=(n // tm,),
        in_specs=[
            pl.BlockSpec((tm, d), lambda i: (i, 0)),
            pl.BlockSpec((tm, k), lambda i: (i, 1)),
            pl.BlockSpec((tm, k), lambda i: (i, 5)),
            pl.BlockSpec((2, tm, k), lambda i: (0, i, 0)),
            pl.BlockSpec((2, tm, k), lambda i: (0, i, 0)),
            pl.BlockSpec((1, k), lambda i: (0, 0)),
            pl.BlockSpec((2 * k, d), lambda i: (0, 0)),
        ],
        out_specs=pl.BlockSpec((tm, d), lambda i: (i, 0)),
        out_shape=jax.ShapeDtypeStruct((n, d), F32),
        compiler_params=_params("arbitrary"),
        name="ab_out",
    )(x, proj, proj, hl, hm, norm_g.reshape(1, k), w)


def _cd_out_kernel(x_ref, r_ref, og_ref, g_ref, ys_ref, u_ref, d_ref, wg_ref, w_ref, o_ref):
    k = r_ref.shape[1]
    r = r_ref[...]
    yc = _head_norm(og_ref[0] + og_ref[1], g_ref[...], GLA_HEADS) * (r * jax.nn.sigmoid(r))
    y = jax.nn.gelu(ys_ref[...] + d_ref[...] * u_ref[...])
    yd = y * jax.nn.sigmoid(jnp.dot(y.astype(BF16), wg_ref[...], preferred_element_type=F32))
    o_ref[...] = (x_ref[...]
                  + jnp.dot(yc.astype(BF16), w_ref[0:k, :], preferred_element_type=F32)
                  + jnp.dot(yd.astype(BF16), w_ref[k:, :], preferred_element_type=F32))


def _cd_out(x, proj, og, norm_g, ys, s5_d, w_glu, w, tm=512):
    n, d = x.shape
    k = S5_W
    tm = min(tm, n)
    return pl.pallas_call(
        _cd_out_kernel,
        grid=(n // tm,),
        in_specs=[
            pl.BlockSpec((tm, d), lambda i: (i, 0)),
            pl.BlockSpec((tm, k), lambda i: (i, 2)),
            pl.BlockSpec((2, tm, k), lambda i: (0, i, 0)),
            pl.BlockSpec((1, k), lambda i: (0, 0)),
            pl.BlockSpec((tm, k), lambda i: (i, 0)),
            pl.BlockSpec((tm, k), lambda i: (i, 3)),
            pl.BlockSpec((1, k), lambda i: (0, 0)),
            pl.BlockSpec((k, k), lambda i: (0, 0)),
            pl.BlockSpec((2 * k, d), lambda i: (0, 0)),
        ],
        out_specs=pl.BlockSpec((tm, d), lambda i: (i, 0)),
        out_shape=jax.ShapeDtypeStruct((n, d), F32),
        compiler_params=_params("arbitrary"),
        name="cd_out",
    )(x, proj, og, norm_g.reshape(1, k), ys, proj, s5_d.reshape(1, k), w_glu, w)


def _headwise_rms_norm(t, g):
    y = t * lax.rsqrt(jnp.mean(t * t, axis=-1, keepdims=True) + NORM_EPS)
    return y.reshape(t.shape[:2] + (-1,)) * g


def _heads(t, n):
    return t.reshape(t.shape[:2] + (n, -1)).transpose(0, 2, 1, 3)


def _flip_seq(t):
    return jnp.flip(t, axis=2)


def _to_chunks(t, chunk):
    b, h, s = t.shape[:3]
    t = t.reshape((b, h, s // chunk, chunk) + t.shape[3:])
    return jnp.moveaxis(t, 2, 0)


def _from_chunks(t):
    nc, b, h, l = t.shape[:4]
    return jnp.moveaxis(t, 0, 2).reshape((b, h, nc * l) + t.shape[4:])


def _mlstm_chunkwise(q, k, v, ig, lf):
    bsz, nh, _, dh = q.shape
    L = MLSTM_CHUNK
    mask = jnp.tril(jnp.ones((L, L), dtype=bool))
    xs = tuple(_to_chunks(t, L) for t in (q, k, v, ig, lf))

    def step(carry, inp):
        c_st, n_st, m_st = carry
        qt, kt, vt, it, ft = inp
        cum = jnp.cumsum(ft, axis=-1)
        dmat = jnp.where(mask, cum[..., :, None] - cum[..., None, :] + it[..., None, :], -jnp.inf)
        m_inter = cum + m_st[..., None]
        m_t = jnp.maximum(jnp.max(dmat, axis=-1), m_inter)
        w_intra = jnp.exp(dmat - m_t[..., None])
        w_inter = jnp.exp(m_inter - m_t)
        s = jnp.einsum('bhtd,bhsd->bhts', qt, kt) * w_intra
        num = jnp.einsum('bhts,bhse->bhte', s, vt) + w_inter[..., None] * jnp.einsum('bhtd,bhde->bhte', qt, c_st)
        den = jnp.sum(s, axis=-1) + w_inter * jnp.einsum('bhtd,bhd->bht', qt, n_st)
        h = num / jnp.maximum(jnp.abs(den), jnp.exp(-m_t))[..., None]
        tot = cum[..., -1]
        dec_s = tot[..., None] - cum + it
        m_new = jnp.maximum(tot + m_st, jnp.max(dec_s, axis=-1))
        ws = jnp.exp(dec_s - m_new[..., None])
        wc = jnp.exp(tot + m_st - m_new)
        c_new = wc[..., None, None] * c_st + jnp.einsum('bhs,bhsd,bhse->bhde', ws, kt, vt)
        n_new = wc[..., None] * n_st + jnp.einsum('bhs,bhsd->bhd', ws, kt)
        return (c_new, n_new, m_new), h

    init = (jnp.zeros((bsz, nh, dh, dh), q.dtype), jnp.zeros((bsz, nh, dh), q.dtype),
            jnp.zeros((bsz, nh), q.dtype))
    _, hs = lax.scan(step, init, xs)
    return _from_chunks(hs)


def _mlstm_core(q, k, v, o, gates, mlstm_gate_b, mlstm_norm):
    bsz, s, _ = q.shape
    q = _heads(q, MLSTM_HEADS) * (MLSTM_DH ** -0.5)
    k = _heads(k, MLSTM_HEADS)
    v = _heads(v, MLSTM_HEADS)
    g = (gates.reshape(bsz, s, 2, 2, MLSTM_HEADS) + mlstm_gate_b).transpose(2, 3, 0, 4, 1)
    h_f = _mlstm_chunkwise(q, k, v, g[0, 0], jax.nn.log_sigmoid(g[0, 1]))
    h_b = _flip_seq(_mlstm_chunkwise(_flip_seq(q), _flip_seq(k), _flip_seq(v), _flip_seq(g[1, 0]),
                                     _flip_seq(jax.nn.log_sigmoid(g[1, 1]))))
    hm = _headwise_rms_norm((h_f + h_b).transpose(0, 2, 1, 3), mlstm_norm)
    return jax.nn.sigmoid(o) * hm


def _gla_chunked(q, k, v, la):
    bsz, nh, _, dk = q.shape
    dv = v.shape[-1]
    L = GLA_CHUNK
    mask = jnp.tril(jnp.ones((L, L), dtype=bool))[..., None]
    xs = tuple(_to_chunks(t, L) for t in (q, k, v, la))

    def step(s_st, inp):
        qt, kt, vt, at = inp
        cum = jnp.cumsum(at, axis=2)
        rel = jnp.where(mask, cum[:, :, :, None, :] - cum[:, :, None, :, :], -jnp.inf)
        attn = jnp.einsum('bhtd,bhsd,bhtsd->bhts', qt, kt, jnp.exp(rel))
        o = jnp.einsum('bhts,bhse->bhte', attn, vt) + jnp.einsum('bhtd,bhde->bhte', qt * jnp.exp(cum), s_st)
        tot = cum[:, :, -1]
        s_new = jnp.exp(tot)[..., None] * s_st + jnp.einsum('bhsd,bhse->bhde', kt * jnp.exp(tot[:, :, None] - cum), vt)
        return s_new, o

    _, os_ = lax.scan(step, jnp.zeros((bsz, nh, dk, dv), q.dtype), xs)
    return _from_chunks(os_)


def _gla_core(q, k, v, r, glr, gla_w_gate2, gla_gate_b, gla_norm):
    bsz, s, _ = q.shape
    q = _heads(q, GLA_HEADS) * (GLA_DK ** -0.5)
    k = _heads(k, GLA_HEADS)
    v = _heads(v, GLA_HEADS)
    low = glr.reshape(bsz, s, 2, GLA_RANK)
    gate_pre = jnp.einsum('bsdr,drk->dbsk', low, gla_w_gate2) + gla_gate_b[:, None, None, :]
    la = (jax.nn.log_sigmoid(gate_pre) / GLA_TAU).reshape(2, bsz, s, GLA_HEADS, GLA_DK).transpose(0, 1, 3, 2, 4)
    o_f = _gla_chunked(q, k, v, la[0])
    o_b = _flip_seq(_gla_chunked(_flip_seq(q), _flip_seq(k), _flip_seq(v), _flip_seq(la[1])))
    return _headwise_rms_norm((o_f + o_b).transpose(0, 2, 1, 3), gla_norm) * jax.nn.silu(r)


def _split_small(w, n_main):
    tail = w[:, n_main:]
    tail = jnp.pad(tail, ((0, 0), (0, LANE - tail.shape[1])))
    return w[:, :n_main].astype(BF16), tail.astype(BF16)


def kernel(x, norm_ffn1, ffn1_w_gu, ffn1_w_down, norm_mix, norm_ffn2, ffn2_w_gu, ffn2_w_down,
           ab_w_in, lru_conv_w, lru_conv_b, lru_gate_w, lru_gate_b, lru_lambda, mlstm_gate_b,
           mlstm_norm, ab_w_out, cd_w_in, gla_w_gate2, gla_gate_b, gla_norm, s5_a_re, s5_a_im,
           s5_log_dt, s5_b_re, s5_b_im, s5_c_re, s5_c_im, s5_d, s5_w_glu, cd_w_out, final_norm):
    bsz, seq, d = x.shape
    n = bsz * seq
    depth = norm_ffn1.shape[0]
    xf = x.reshape(n, d)
    for l in range(depth):
        xf = _ffn(xf, norm_ffn1[l], ffn1_w_gu[l].astype(BF16), ffn1_w_down[l].astype(BF16))
        j = l // 2
        if l % 2 == 0:
            w_main, w_tail = _split_small(ab_w_in[j], 2 * W_LRU + 4 * W_MLSTM)
            proj, tail = _norm_proj(xf, norm_mix[l], w_main, w_tail)
            hl = _lru(proj, bsz, lru_conv_w[j], lru_conv_b[j], lru_gate_w[j], lru_gate_b[j], lru_lambda[j])
            hm = _mlstm(proj, tail, bsz, mlstm_gate_b[j])
            xf = _ab_out(xf, proj, hl, hm, mlstm_norm[j], ab_w_out[j].astype(BF16))
        else:
            w = cd_w_in[j]
            n_gla = 2 * GLA_QK + 2 * GLA_V
            w = jnp.concatenate([w[:, :n_gla], w[:, n_gla + 2 * GLA_RANK:], w[:, n_gla:n_gla + 2 * GLA_RANK]], axis=1)
            w_main, w_tail = _split_small(w, n_gla + S5_W)
            proj, tail = _norm_proj(xf, norm_mix[l], w_main, w_tail)
            og = _gla(proj, tail, bsz, gla_w_gate2[j], gla_gate_b[j])
            ops = _s5_operators(s5_a_re[j], s5_a_im[j], s5_log_dt[j], s5_b_re[j], s5_b_im[j],
                                s5_c_re[j], s5_c_im[j])
            ys = _s5(proj[:, n_gla:], bsz, ops)
            xf = _cd_out(xf, proj, og, gla_norm[j], ys, s5_d[j],
                         s5_w_glu[j].astype(BF16), cd_w_out[j].astype(BF16))
        xf = _ffn(xf, norm_ffn2[l], ffn2_w_gu[l].astype(BF16), ffn2_w_down[l].astype(BF16),
                  final_g=final_norm if l == depth - 1 else None)
    return xf.reshape(bsz, seq, d)
```

```python
import functools

import numpy as np
import jax
import jax.numpy as jnp
from jax import lax
from jax.experimental import pallas as pl
from jax.experimental.pallas import tpu as pltpu

D_MODEL = 2048
D_FF = 5632
NORM_EPS = 1e-6

W_LRU = 1024
LRU_HEADS = 8
LRU_BLOCK = W_LRU // LRU_HEADS
CONV_W = 4
CONV_PAD_L = 2
LRU_C = 8.0

W_MLSTM = 1024
MLSTM_HEADS = 4
MLSTM_DH = W_MLSTM // MLSTM_HEADS
MLSTM_CHUNK = 64

GLA_HEADS = 4
GLA_DK = 128
GLA_DV = 256
GLA_QK = GLA_HEADS * GLA_DK
GLA_V = GLA_HEADS * GLA_DV
GLA_RANK = 16
GLA_TAU = 16.0
GLA_CHUNK = 64

S5_W = 1024
S5_GROUP = 16
S5_GROUPS = S5_W // S5_GROUP
S5_P = 64
S5_CHUNK = 16
S5_GB = 8
S5_ROW_TILE = 256

LANE = 128
SUBLANE = 8
VMEM_LIMIT = 52 * 1024 * 1024

BF16 = jnp.bfloat16
F32 = jnp.float32
HI = lax.Precision.HIGHEST


def _rms(x, g):
    ms = jnp.mean(x * x, axis=-1, keepdims=True)
    return x * lax.rsqrt(ms + NORM_EPS) * g


def _softplus(z):
    return jnp.maximum(z, 0.0) + jnp.log1p(jnp.exp(-jnp.abs(z)))


def _params(*sem):
    return pltpu.CompilerParams(dimension_semantics=sem, vmem_limit_bytes=VMEM_LIMIT)


def _ffn_kernel(x_ref, g_ref, wg_ref, wu_ref, wd_ref, *rest, final):
    if final:
        fg_ref, o_ref, h_ref = rest
    else:
        o_ref, h_ref = rest
    j = pl.program_id(1)

    @pl.when(j == 0)
    def _():
        x = x_ref[...]
        h_ref[...] = _rms(x, g_ref[...]).astype(BF16)
        o_ref[...] = x

    h = h_ref[...]
    g = jnp.dot(h, wg_ref[...], preferred_element_type=F32)
    u = jnp.dot(h, wu_ref[...], preferred_element_type=F32)
    a = (0.5 * g * jax.nn.sigmoid(g) * u).astype(BF16)
    o_ref[...] += jnp.dot(a, wd_ref[...], preferred_element_type=F32)

    if final:
        @pl.when(j == pl.num_programs(1) - 1)
        def _():
            o_ref[...] = _rms(o_ref[...], fg_ref[...])


def _ffn(x, g, w_gu, w_down, final_g=None, tm=512, tf=512):
    n, d = x.shape
    f = w_down.shape[0]
    tm = min(tm, n)
    nj = f // tf
    final = final_g is not None
    in_specs = [
        pl.BlockSpec((tm, d), lambda i, j: (i, 0)),
        pl.BlockSpec((1, d), lambda i, j: (0, 0)),
        pl.BlockSpec((d, tf), lambda i, j: (0, j)),
        pl.BlockSpec((d, tf), lambda i, j: (0, j + nj)),
        pl.BlockSpec((tf, d), lambda i, j: (j, 0)),
    ]
    args = [x, g.reshape(1, d), w_gu, w_gu, w_down]
    if final:
        in_specs.append(pl.BlockSpec((1, d), lambda i, j: (0, 0)))
        args.append(final_g.reshape(1, d))
    return pl.pallas_call(
        functools.partial(_ffn_kernel, final=final),
        grid=(n // tm, nj),
        in_specs=in_specs,
        out_specs=pl.BlockSpec((tm, d), lambda i, j: (i, 0)),
        out_shape=jax.ShapeDtypeStruct((n, d), F32),
        scratch_shapes=[pltpu.VMEM((tm, d), BF16)],
        compiler_params=_params("arbitrary", "arbitrary"),
        name="ffn_final" if final else "ffn",
    )(*args)


def _norm_proj_kernel(x_ref, g_ref, w_ref, ws_ref, o_ref, os_ref, h_ref):
    @pl.when(pl.program_id(1) == 0)
    def _():
        h = _rms(x_ref[...], g_ref[...]).astype(BF16)
        h_ref[...] = h
        os_ref[...] = jnp.dot(h, ws_ref[...], preferred_element_type=F32)

    o_ref[...] = jnp.dot(h_ref[...], w_ref[...], preferred_element_type=F32)


def _norm_proj(x, g, w, w_small, tm=1024, tn=1024):
    n, d = x.shape
    m = w.shape[1]
    ms = w_small.shape[1]
    tm = min(tm, n)
    return pl.pallas_call(
        _norm_proj_kernel,
        grid=(n // tm, m // tn),
        in_specs=[
            pl.BlockSpec((tm, d), lambda i, j: (i, 0)),
            pl.BlockSpec((1, d), lambda i, j: (0, 0)),
            pl.BlockSpec((d, tn), lambda i, j: (0, j)),
            pl.BlockSpec((d, ms), lambda i, j: (0, 0)),
        ],
        out_specs=[pl.BlockSpec((tm, tn), lambda i, j: (i, j)),
                   pl.BlockSpec((tm, ms), lambda i, j: (i, 0))],
        out_shape=[jax.ShapeDtypeStruct((n, m), F32), jax.ShapeDtypeStruct((n, ms), F32)],
        scratch_shapes=[pltpu.VMEM((tm, d), BF16)],
        compiler_params=_params("arbitrary", "arbitrary"),
        name="norm_proj",
    )(x, g.reshape(1, d), w, w_small)


def _lru_kernel(xp_ref, x_ref, xn_ref, cw_ref, cb_ref, gw_ref, gb_ref, lam_ref, o_ref,
                xs_ref, a_ref, b_ref, h_ref, *, tt):
    d = pl.program_id(0)
    t = pl.program_id(2)
    nt = pl.num_programs(2)
    tb = jnp.where(d == 0, t, nt - 1 - t)
    w = x_ref.shape[1]

    xs_ref[0:SUBLANE, :] = jnp.where(tb > 0, xp_ref[...], 0.0)
    xs_ref[SUBLANE:SUBLANE + tt, :] = x_ref[...]
    xs_ref[SUBLANE + tt:2 * SUBLANE + tt, :] = jnp.where(tb < nt - 1, xn_ref[...], 0.0)
    cw = cw_ref[...]
    xc = cb_ref[...]
    for k in range(CONV_W):
        off = SUBLANE + k - CONV_PAD_L
        xc = xc + cw[k:k + 1, :] * xs_ref[off:off + tt, :]

    xcb = xc.astype(BF16)
    gb = gb_ref[0]
    sp = _softplus(-lam_ref[0])
    for hd in range(w // LRU_BLOCK):
        sl = slice(hd * LRU_BLOCK, (hd + 1) * LRU_BLOCK)
        z = jnp.dot(xcb[:, sl], gw_ref[0, hd], preferred_element_type=F32)
        r = jax.nn.sigmoid(z[:, :LRU_BLOCK] + gb[0:1, sl])
        i = jax.nn.sigmoid(z[:, LRU_BLOCK:] + gb[1:2, sl])
        log_a = (-LRU_C * sp[:, sl]) * r
        a = jnp.exp(log_a)
        a_ref[:, sl] = a
        b_ref[:, sl] = jnp.sqrt(-jnp.tanh(log_a) * (a * a + 1.0)) * (i * xc[:, sl])

    @pl.when(t == 0)
    def _():
        h_ref[...] = jnp.zeros_like(h_ref)

    sub = lax.broadcasted_iota(jnp.int32, (SUBLANE, w), 0)
    ntile = tt // SUBLANE

    def scan(rev):
        def body(jt, h):
            r0 = pl.multiple_of((ntile - 1 - jt if rev else jt) * SUBLANE, SUBLANE)
            tile = jnp.zeros((SUBLANE, w), F32)
            at = a_ref[pl.ds(r0, SUBLANE), :]
            bt = b_ref[pl.ds(r0, SUBLANE), :]
            for r in (range(SUBLANE - 1, -1, -1) if rev else range(SUBLANE)):
                h = at[r:r + 1, :] * h + bt[r:r + 1, :]
                tile = jnp.where(sub == r, h, tile)
            o_ref[0, pl.ds(r0, SUBLANE), :] = tile
            return h

        h_ref[...] = lax.fori_loop(0, ntile, body, h_ref[...])

    pl.when(d == 0)(functools.partial(scan, False))
    pl.when(d == 1)(functools.partial(scan, True))


def _lru(proj, bsz, conv_w, conv_b, gate_w, gate_b, lam, tt=256):
    n = proj.shape[0]
    seq = n // bsz
    tt = min(tt, seq)
    nt = seq // tt
    w = W_LRU
    r8 = tt // SUBLANE
    gw = jnp.concatenate([gate_w[:, 0], gate_w[:, 1]], axis=-1).astype(BF16)

    def tb(d, t):
        return t + d * (nt - 1 - 2 * t)

    return pl.pallas_call(
        functools.partial(_lru_kernel, tt=tt),
        grid=(2, bsz, nt),
        in_specs=[
            pl.BlockSpec((SUBLANE, w), lambda d, b, t: (jnp.maximum((b * nt + tb(d, t)) * r8 - 1, 0), 0)),
            pl.BlockSpec((tt, w), lambda d, b, t: (b * nt + tb(d, t), 0)),
            pl.BlockSpec((SUBLANE, w), lambda d, b, t: (jnp.minimum((b * nt + tb(d, t) + 1) * r8, n // SUBLANE - 1), 0)),
            pl.BlockSpec((CONV_W, w), lambda d, b, t: (0, 0)),
            pl.BlockSpec((1, w), lambda d, b, t: (0, 0)),
            pl.BlockSpec((1, LRU_HEADS, LRU_BLOCK, 2 * LRU_BLOCK), lambda d, b, t: (d, 0, 0, 0)),
            pl.BlockSpec((1, 2, w), lambda d, b, t: (d, 0, 0)),
            pl.BlockSpec((1, 1, w), lambda d, b, t: (d, 0, 0)),
        ],
        out_specs=pl.BlockSpec((1, tt, w), lambda d, b, t: (d, b * nt + tb(d, t), 0)),
        out_shape=jax.ShapeDtypeStruct((2, n, w), F32),
        scratch_shapes=[pltpu.VMEM((tt + 2 * SUBLANE, w), F32), pltpu.VMEM((tt, w), F32),
                        pltpu.VMEM((tt, w), F32), pltpu.VMEM((1, w), F32)],
        compiler_params=_params("arbitrary", "arbitrary", "arbitrary"),
        name="rg_lru",
    )(proj, proj, proj, conv_w, conv_b.reshape(1, w), gw, gate_b, lam.reshape(2, 1, w))


def _s5_discretise(a_re, a_im, log_dt, b_re, b_im):
    dt = jnp.exp(log_dt)[:, None]
    mag = jnp.exp(dt * a_re)
    lr = mag * jnp.cos(dt * a_im)
    li = mag * jnp.sin(dt * a_im)
    den = a_re * a_re + a_im * a_im
    nr = lr - 1.0
    cr = (nr * a_re + li * a_im) / den
    ci = (li * a_re - nr * a_im) / den
    bbr = cr[..., None] * b_re - ci[..., None] * b_im
    bbi = cr[..., None] * b_im + ci[..., None] * b_re
    return lr, li, bbr, bbi


def _s5_operators(a_re, a_im, log_dt, b_re, b_im, c_re, c_im):
    L = S5_CHUNK
    G, P, C = S5_GROUPS, S5_P, S5_GROUP
    kfs, mins, mouts, lams = [], [], [], []
    for d in range(2):
        lr, li, bbr, bbi = _s5_discretise(a_re[d], a_im[d], log_dt[d], b_re, b_im)
        pr, pi = [jnp.ones_like(lr)], [jnp.zeros_like(li)]
        for _ in range(L):
            pr, pi = pr + [pr[-1] * lr - pi[-1] * li], pi + [pr[-1] * li + pi[-1] * lr]
        pr, pi = jnp.stack(pr), jnp.stack(pi)
        clr = c_re[None] * pr[:, :, None, :] - c_im[None] * pi[:, :, None, :]
        cli = c_re[None] * pi[:, :, None, :] + c_im[None] * pr[:, :, None, :]
        kfs.append(jnp.einsum('jgcp,gpk->jgck', clr[:L], bbr, precision=HI)
                   - jnp.einsum('jgcp,gpk->jgck', cli[:L], bbi, precision=HI))
        blr = pr[:, :, :, None] * bbr[None] - pi[:, :, :, None] * bbi[None]
        bli = pr[:, :, :, None] * bbi[None] + pi[:, :, :, None] * bbr[None]
        if d == 0:
            e_in = jnp.arange(L - 1, -1, -1)
            e_out = jnp.arange(1, L + 1)
        else:
            e_in = jnp.arange(L)
            e_out = jnp.arange(L, 0, -1)
        m_in = jnp.stack([blr[e_in], bli[e_in]], axis=0)
        mins.append(m_in.transpose(2, 1, 4, 0, 3))
        m_out = jnp.stack([clr[e_out], -cli[e_out]], axis=0)
        mouts.append(m_out.transpose(2, 0, 4, 1, 3))
        lams.append(jnp.stack([pr[L], pi[L]], axis=1))
    kf, kb = kfs
    kfull = jnp.concatenate([kb[:0:-1], (kf[0] + kb[0])[None], kf[1:]], axis=0)
    idx = jnp.arange(L)[None, :] - jnp.arange(L)[:, None] + (L - 1)
    tmat = kfull[idx].transpose(2, 0, 4, 1, 3).reshape(G, L * C, L * C)
    m_in = jnp.concatenate(mins, axis=3).reshape(G, L * C, 4, P)
    m_out = jnp.concatenate(mouts, axis=1).reshape(G, 4, P, L * C)
    lam_l = jnp.concatenate(lams, axis=1)
    gb = S5_GB
    nb_ = G // gb
    eye = jnp.eye(gb, dtype=F32)
    k6 = tmat.reshape(nb_, gb, L, C, L, C).transpose(0, 2, 1, 3, 4, 5)
    t_blk = (k6[:, :, :, :, :, None, :] * eye[None, None, :, None, None, :, None]).reshape(nb_, L * gb * C, L * gb * C)
    mi = m_in.reshape(nb_, gb, L, C, 4, P).transpose(0, 2, 1, 3, 4, 5)
    min_blk = (mi[:, :, :, :, :, None, :] * eye[None, None, :, None, None, :, None]).reshape(nb_, L * gb * C, 4 * gb * P)
    mo = m_out.reshape(nb_, gb, 4, P, L, C).transpose(0, 2, 1, 3, 4, 5)
    mout_blk = (mo[:, :, :, :, :, None, :] * eye[None, None, :, None, None, :, None]).reshape(nb_, 4 * gb * P, L * gb * C)
    lam_blk = lam_l.reshape(nb_, gb, 4, P).transpose(0, 2, 1, 3).reshape(nb_, 1, 4 * gb * P)
    return t_blk.astype(BF16), min_blk.astype(BF16), mout_blk.astype(BF16), lam_blk


def _s5_state_kernel(u_ref, min_ref, lam_ref, hin_ref, h_ref, *, nb):
    nl, rows, _ = u_ref.shape
    rb = rows // nb
    hw = h_ref.shape[1] // 4
    rt = min(rows, S5_ROW_TILE)
    for r0 in range(0, rows, rt):
        u = jnp.concatenate([u_ref[s, r0:r0 + rt, :] for s in range(nl)], axis=1)
        h_ref[r0:r0 + rt, :] = jnp.dot(u, min_ref[0], preferred_element_type=F32)
    lam = lam_ref[0]
    lrf, lif = lam[:, 0:hw], lam[:, hw:2 * hw]
    lrb, lib = lam[:, 2 * hw:3 * hw], lam[:, 3 * hw:4 * hw]
    sub = lax.broadcasted_iota(jnp.int32, (SUBLANE, hw), 0)

    def body(it, carry):
        new = []
        for b in range(nb):
            fr, fi, br, bi = carry[4 * b:4 * b + 4]
            f0 = pl.multiple_of(b * rb + it * SUBLANE, SUBLANE)
            b0 = pl.multiple_of(b * rb + rb - SUBLANE - it * SUBLANE, SUBLANE)
            tiles = [jnp.zeros((SUBLANE, hw), F32)] * 4
            lf = h_ref[pl.ds(f0, SUBLANE), 0:2 * hw]
            lb = h_ref[pl.ds(b0, SUBLANE), 2 * hw:4 * hw]
            for r in range(SUBLANE):
                q = SUBLANE - 1 - r
                tiles = [jnp.where(sub == r, fr, tiles[0]), jnp.where(sub == r, fi, tiles[1]),
                         jnp.where(sub == q, br, tiles[2]), jnp.where(sub == q, bi, tiles[3])]
                fr, fi, br, bi = (lrf * fr - lif * fi + lf[r:r + 1, 0:hw],
                                  lrf * fi + lif * fr + lf[r:r + 1, hw:2 * hw],
                                  lrb * br - lib * bi + lb[q:q + 1, 0:hw],
                                  lrb * bi + lib * br + lb[q:q + 1, hw:2 * hw])
            h_ref[pl.ds(f0, SUBLANE), 0:hw] = tiles[0]
            h_ref[pl.ds(f0, SUBLANE), hw:2 * hw] = tiles[1]
            h_ref[pl.ds(b0, SUBLANE), 2 * hw:3 * hw] = tiles[2]
            h_ref[pl.ds(b0, SUBLANE), 3 * hw:4 * hw] = tiles[3]
            new += [fr, fi, br, bi]
        return tuple(new)

    zero = jnp.zeros((1, hw), F32)
    lax.fori_loop(0, rb // SUBLANE, body, (zero,) * (4 * nb))
    hin_ref[0] = h_ref[...].astype(BF16)


def _s5_out_kernel(u_ref, hin_ref, t_ref, mout_ref, y_ref):
    nl = u_ref.shape[0]
    u = jnp.concatenate([u_ref[s] for s in range(nl)], axis=1)
    y = (jnp.dot(u, t_ref[0], preferred_element_type=F32)
         + jnp.dot(hin_ref[0], mout_ref[0], preferred_element_type=F32))
    for t in range(nl):
        y_ref[t] = y[:, t * LANE:(t + 1) * LANE]


def _s5(u, bsz, ops, tr=256):
    t_blk, min_blk, mout_blk, lam_blk = ops
    n, w = u.shape
    L = S5_CHUNK
    rows = n // L
    nblk = w // LANE
    kw = L * LANE
    sw = lam_blk.shape[2]
    tr = min(tr, rows)
    u3 = u.reshape(rows, L, w).transpose(1, 0, 2).astype(BF16)
    hin = pl.pallas_call(
        functools.partial(_s5_state_kernel, nb=bsz),
        grid=(nblk,),
        in_specs=[
            pl.BlockSpec((L, rows, LANE), lambda b: (0, 0, b)),
            pl.BlockSpec((1, kw, sw), lambda b: (b, 0, 0)),
            pl.BlockSpec((1, 1, sw), lambda b: (b, 0, 0)),
        ],
        out_specs=pl.BlockSpec((1, rows, sw), lambda b: (b, 0, 0)),
        out_shape=jax.ShapeDtypeStruct((nblk, rows, sw), BF16),
        scratch_shapes=[pltpu.VMEM((rows, sw), F32)],
        compiler_params=_params("arbitrary"),
        name="s5_state",
    )(u3, min_blk, lam_blk)
    y3 = pl.pallas_call(
        _s5_out_kernel,
        grid=(nblk, rows // tr),
        in_specs=[
            pl.BlockSpec((L, tr, LANE), lambda b, i: (0, i, b)),
            pl.BlockSpec((1, tr, sw), lambda b, i: (b, i, 0)),
            pl.BlockSpec((1, kw, kw), lambda b, i: (b, 0, 0)),
            pl.BlockSpec((1, sw, kw), lambda b, i: (b, 0, 0)),
        ],
        out_specs=pl.BlockSpec((L, tr, LANE), lambda b, i: (0, i, b)),
        out_shape=jax.ShapeDtypeStruct((L, rows, w), F32),
        compiler_params=_params("arbitrary", "arbitrary"),
        name="s5_out",
    )(u3, hin, t_blk, mout_blk)
    return y3.transpose(1, 0, 2).reshape(n, w)


def _log_sigmoid(z):
    return -_softplus(-z)


def _tri(length, d):
    ti = lax.broadcasted_iota(jnp.int32, (length, length), 0)
    si = lax.broadcasted_iota(jnp.int32, (length, length), 1)
    return (ti - si) * (1 - 2 * d) >= 0


def _mlstm_kernel(q_ref, k_ref, v_ref, g_ref, gb_ref, o_ref, c_ref, n_ref, m_ref):
    d = pl.program_id(0)
    length = q_ref.shape[0]
    nh, dh = MLSTM_HEADS, MLSTM_DH

    @pl.when(pl.program_id(2) == 0)
    def _():
        c_ref[...] = jnp.zeros_like(c_ref)
        n_ref[...] = jnp.zeros_like(n_ref)
        m_ref[...] = jnp.zeros_like(m_ref)

    causal = _tri(length, d)
    gp = g_ref[...] + gb_ref[...]
    gp = jnp.where(d == 0, gp, pltpu.roll(gp, LANE - 2 * nh, 1))
    lf = _log_sigmoid(gp)
    cum = jnp.dot(causal.astype(F32), lf, precision=HI, preferred_element_type=F32)
    tot = jnp.sum(lf, axis=0, keepdims=True)
    gp_t = gp.T
    cum_t = cum.T
    nt = (((1,), (1,)), ((), ()))
    tn = (((0,), (0,)), ((), ()))

    for h in range(nh):
        sl = slice(h * dh, (h + 1) * dh)
        qh = (q_ref[:, sl] * (dh ** -0.5)).astype(BF16)
        kf = k_ref[:, sl]
        kh = kf.astype(BF16)
        vh = v_ref[:, sl].astype(BF16)
        ig_c, cum_c = gp[:, h:h + 1], cum[:, nh + h:nh + h + 1]
        ig_r, cum_r = gp_t[h:h + 1, :], cum_t[nh + h:nh + h + 1, :]
        tot_h = tot[:, nh + h:nh + h + 1]
        m_st = m_ref[h:h + 1, 0:1]
        c_st = c_ref[h]
        dmat = jnp.where(causal, cum_c - cum_r + ig_r, -jnp.inf)
        m_inter = cum_c + m_st
        m_t = jnp.maximum(jnp.max(dmat, axis=1, keepdims=True), m_inter)
        w_inter = jnp.exp(m_inter - m_t)
        s = lax.dot_general(qh, kh, nt, preferred_element_type=F32) * jnp.exp(dmat - m_t)
        num = (jnp.dot(s.astype(BF16), vh, preferred_element_type=F32)
               + w_inter * jnp.dot(qh, c_st.astype(BF16), preferred_element_type=F32))
        den = (jnp.sum(s, axis=1, keepdims=True)
               + w_inter * jnp.sum(qh.astype(F32) * n_ref[h:h + 1, :], axis=1, keepdims=True))
        o_ref[0, :, sl] = num / jnp.maximum(jnp.abs(den), jnp.exp(-m_t))
        dec = tot_h - cum_c + ig_c
        m_new = jnp.maximum(tot_h + m_st, jnp.max(dec, axis=0, keepdims=True))
        wc = jnp.exp(tot_h + m_st - m_new)
        kw = jnp.exp(dec - m_new) * kf
        c_ref[h] = wc * c_st + lax.dot_general(kw.astype(BF16), vh, tn, preferred_element_type=F32)
        n_ref[h:h + 1, :] = wc * n_ref[h:h + 1, :] + jnp.sum(kw, axis=0, keepdims=True)
        m_ref[h:h + 1, :] = jnp.broadcast_to(m_new, (1, LANE))


def _mlstm(proj, tail, bsz, gate_b, length=256):
    n = proj.shape[0]
    seq = n // bsz
    length = min(length, seq)
    nc = seq // length
    w = W_MLSTM
    q_blk = 2 * W_LRU // w
    gb = jnp.pad(gate_b.reshape(1, -1), ((0, 0), (0, LANE - gate_b.size)))

    def row(d, b, c):
        return b * nc + c + d * (nc - 1 - 2 * c)

    return pl.pallas_call(
        _mlstm_kernel,
        grid=(2, bsz, nc),
        in_specs=[
            pl.BlockSpec((length, w), lambda d, b, c: (row(d, b, c), q_blk)),
            pl.BlockSpec((length, w), lambda d, b, c: (row(d, b, c), q_blk + 1)),
            pl.BlockSpec((length, w), lambda d, b, c: (row(d, b, c), q_blk + 2)),
            pl.BlockSpec((length, LANE), lambda d, b, c: (row(d, b, c), 0)),
            pl.BlockSpec((1, LANE), lambda d, b, c: (0, 0)),
        ],
        out_specs=pl.BlockSpec((1, length, w), lambda d, b, c: (d, row(d, b, c), 0)),
        out_shape=jax.ShapeDtypeStruct((2, n, w), F32),
        scratch_shapes=[pltpu.VMEM((MLSTM_HEADS, MLSTM_DH, MLSTM_DH), F32),
                        pltpu.VMEM((SUBLANE, MLSTM_DH), F32), pltpu.VMEM((SUBLANE, LANE), F32)],
        compiler_params=_params("arbitrary", "arbitrary", "arbitrary"),
        name="mlstm",
    )(proj, proj, proj, tail, gb)


GLA_SUB = 16


def _gla_chunk(q_ref, k_ref, v_ref, cum, o_ref, s_ref, rev):
    length = q_ref.shape[0]
    c = GLA_SUB
    nt = (((1,), (1,)), ((), ()))
    tn = (((0,), (0,)), ((), ()))

    def rows(a, b):
        return slice(length - b, length - a) if rev else slice(a, b)

    def row(a):
        i = length - 1 - a if rev else a
        return slice(i, i + 1)

    ti = lax.broadcasted_iota(jnp.int32, (c, 1), 0)
    for h in range(GLA_HEADS):
        kl = slice(h * GLA_DK, (h + 1) * GLA_DK)
        vl = slice(h * GLA_DV, (h + 1) * GLA_DV)
        ch = cum[:, kl]
        qs = q_ref[:, kl] * (GLA_DK ** -0.5)
        kk = k_ref[:, kl]
        vf = v_ref[:, vl]
        vv = vf.astype(BF16)
        st = s_ref[h]
        tot = ch[row(length - 1)]
        o = lax.dot_general((qs * jnp.exp(ch)).astype(BF16), st.astype(BF16), nt, preferred_element_type=F32)
        ob = [o[rows(p * c, (p + 1) * c)] for p in range(length // c)]
        m = length // 2
        while m >= c:
            for start in range(0, length, 2 * m):
                fst, sec = rows(start, start + m), rows(start + m, start + 2 * m)
                r = ch[row(start + m)]
                qh = (qs[sec] * jnp.exp(ch[sec] - r)).astype(BF16)
                kh = (kk[fst] * jnp.exp(r - ch[fst])).astype(BF16)
                att = lax.dot_general(qh, kh, nt, preferred_element_type=F32)
                contrib = jnp.dot(att.astype(BF16), vv[fst], preferred_element_type=F32)
                for p in range((start + m) // c, (start + 2 * m) // c):
                    lo = rows(p * c, (p + 1) * c).start - sec.start
                    ob[p] = ob[p] + contrib[lo:lo + c]
            m //= 2
        for p in range(length // c):
            blk = rows(p * c, (p + 1) * c)
            cb, qb, kb, vb = ch[blk], qs[blk], kk[blk], vf[blk]
            acc = ob[p]
            for s in range(c):
                sees = (ti <= s) if rev else (ti >= s)
                e = jnp.exp(jnp.where(sees, cb - cb[s:s + 1], -jnp.inf))
                a = jnp.sum(qb * kb[s:s + 1] * e, axis=1, keepdims=True)
                acc = acc + a * vb[s:s + 1]
            o_ref[0, blk, vl] = acc
        kd = (kk * jnp.exp(tot - ch)).astype(BF16)
        s_ref[h] = jnp.exp(tot) * st + lax.dot_general(vv, kd, tn, preferred_element_type=F32)


def _gla_kernel(q_ref, k_ref, v_ref, low_ref, wg_ref, bg_ref, o_ref, s_ref):
    d = pl.program_id(0)
    length = q_ref.shape[0]

    @pl.when(pl.program_id(2) == 0)
    def _():
        s_ref[...] = jnp.zeros_like(s_ref)

    gate_pre = jnp.dot(low_ref[...], wg_ref[0], precision=HI, preferred_element_type=F32) + bg_ref[0]
    la = _log_sigmoid(gate_pre) * (1.0 / GLA_TAU)
    cum = jnp.dot(_tri(length, d).astype(F32), la, precision=HI, preferred_element_type=F32)
    pl.when(d == 0)(lambda: _gla_chunk(q_ref, k_ref, v_ref, cum, o_ref, s_ref, False))
    pl.when(d == 1)(lambda: _gla_chunk(q_ref, k_ref, v_ref, cum, o_ref, s_ref, True))


def _gla(proj, tail, bsz, w_gate2, gate_b, length=128):
    n = proj.shape[0]
    seq = n // bsz
    length = min(length, seq)
    nc = seq // length
    wg = jnp.zeros((2, LANE, GLA_QK), F32)
    for d in range(2):
        wg = wg.at[d, d * GLA_RANK:(d + 1) * GLA_RANK].set(w_gate2[d])

    def row(d, b, c):
        return b * nc + c + d * (nc - 1 - 2 * c)

    return pl.pallas_call(
        _gla_kernel,
        grid=(2, bsz, nc),
        in_specs=[
            pl.BlockSpec((length, GLA_QK), lambda d, b, c: (row(d, b, c), 0)),
            pl.BlockSpec((length, GLA_QK), lambda d, b, c: (row(d, b, c), 1)),
            pl.BlockSpec((length, GLA_V), lambda d, b, c: (row(d, b, c), 2 * GLA_QK // GLA_V)),
            pl.BlockSpec((length, LANE), lambda d, b, c: (row(d, b, c), 0)),
            pl.BlockSpec((1, LANE, GLA_QK), lambda d, b, c: (d, 0, 0)),
            pl.BlockSpec((1, 1, GLA_QK), lambda d, b, c: (d, 0, 0)),
        ],
        out_specs=pl.BlockSpec((1, length, GLA_V), lambda d, b, c: (d, row(d, b, c), 0)),
        out_shape=jax.ShapeDtypeStruct((2, n, GLA_V), F32),
        scratch_shapes=[pltpu.VMEM((GLA_HEADS, GLA_DV, GLA_DK), F32)],
        compiler_params=_params("arbitrary", "arbitrary", "arbitrary"),
        name="gla",
    )(proj, proj, proj, tail, wg, gate_b.reshape(2, 1, GLA_QK))


def _head_norm(t, g, heads):
    dh = t.shape[1] // heads
    outs = []
    for h in range(heads):
        th = t[:, h * dh:(h + 1) * dh]
        outs.append(th * lax.rsqrt(jnp.mean(th * th, axis=1, keepdims=True) + NORM_EPS))
    return jnp.concatenate(outs, axis=1) * g


def _ab_out_kernel(x_ref, gr_ref, og_ref, hl_ref, hm_ref, g_ref, w_ref, o_ref):
    k = gr_ref.shape[1]
    ya = jax.nn.gelu(gr_ref[...]) * (hl_ref[0] + hl_ref[1])
    yb = jax.nn.sigmoid(og_ref[...]) * _head_norm(hm_ref[0] + hm_ref[1], g_ref[...], MLSTM_HEADS)
    o_ref[...] = (x_ref[...]
                  + jnp.dot(ya.astype(BF16), w_ref[0:k, :], preferred_element_type=F32)
                  + jnp.dot(yb.astype(BF16), w_ref[k:, :], preferred_element_type=F32))


def _ab_out(x, proj, hl, hm, norm_g, w, tm=512):
    n, d = x.shape
    k = W_LRU
    tm = min(tm, n)
    return pl.pallas_call(
        _ab_out_kernel,
        grid=(n // tm,),
        in_specs=[
            pl.BlockSpec((tm, d), lambda i: (i, 0)),
            pl.BlockSpec((tm, k), lambda i: (i, 1)),
            pl.BlockSpec((tm, k), lambda i: (i, 5)),
            pl.BlockSpec((2, tm, k), lambda i: (0, i, 0)),
            pl.BlockSpec((2, tm, k), lambda i: (0, i, 0)),
            pl.BlockSpec((1, k), lambda i: (0, 0)),
            pl.BlockSpec((2 * k, d), lambda i: (0, 0)),
        ],
        out_specs=pl.BlockSpec((tm, d), lambda i: (i, 0)),
        out_shape=jax.ShapeDtypeStruct((n, d), F32),
        compiler_params=_params("arbitrary"),
        name="ab_out",
    )(x, proj, proj, hl, hm, norm_g.reshape(1, k), w)


def _cd_out_kernel(x_ref, r_ref, og_ref, g_ref, ys_ref, u_ref, d_ref, wg_ref, w_ref, o_ref):
    k = r_ref.shape[1]
    r = r_ref[...]
    yc = _head_norm(og_ref[0] + og_ref[1], g_ref[...], GLA_HEADS) * (r * jax.nn.sigmoid(r))
    y = jax.nn.gelu(ys_ref[...] + d_ref[...] * u_ref[...])
    yd = y * jax.nn.sigmoid(jnp.dot(y.astype(BF16), wg_ref[...], preferred_element_type=F32))
    o_ref[...] = (x_ref[...]
                  + jnp.dot(yc.astype(BF16), w_ref[0:k, :], preferred_element_type=F32)
                  + jnp.dot(yd.astype(BF16), w_ref[k:, :], preferred_element_type=F32))


def _cd_out(x, proj, og, norm_g, ys, s5_d, w_glu, w, tm=512):
    n, d = x.shape
    k = S5_W
    tm = min(tm, n)
    return pl.pallas_call(
        _cd_out_kernel,
        grid=(n // tm,),
        in_specs=[
            pl.BlockSpec((tm, d), lambda i: (i, 0)),
            pl.BlockSpec((tm, k), lambda i: (i, 2)),
            pl.BlockSpec((2, tm, k), lambda i: (0, i, 0)),
            pl.BlockSpec((1, k), lambda i: (0, 0)),
            pl.BlockSpec((tm, k), lambda i: (i, 0)),
            pl.BlockSpec((tm, k), lambda i: (i, 3)),
            pl.BlockSpec((1, k), lambda i: (0, 0)),
            pl.BlockSpec((k, k), lambda i: (0, 0)),
            pl.BlockSpec((2 * k, d), lambda i: (0, 0)),
        ],
        out_specs=pl.BlockSpec((tm, d), lambda i: (i, 0)),
        out_shape=jax.ShapeDtypeStruct((n, d), F32),
        compiler_params=_params("arbitrary"),
        name="cd_out",
    )(x, proj, og, norm_g.reshape(1, k), ys, proj, s5_d.reshape(1, k), w_glu, w)


def _headwise_rms_norm(t, g):
    y = t * lax.rsqrt(jnp.mean(t * t, axis=-1, keepdims=True) + NORM_EPS)
    return y.reshape(t.shape[:2] + (-1,)) * g


def _heads(t, n):
    return t.reshape(t.shape[:2] + (n, -1)).transpose(0, 2, 1, 3)


def _flip_seq(t):
    return jnp.flip(t, axis=2)


def _to_chunks(t, chunk):
    b, h, s = t.shape[:3]
    t = t.reshape((b, h, s // chunk, chunk) + t.shape[3:])
    return jnp.moveaxis(t, 2, 0)


def _from_chunks(t):
    nc, b, h, l = t.shape[:4]
    return jnp.moveaxis(t, 0, 2).reshape((b, h, nc * l) + t.shape[4:])


def _mlstm_chunkwise(q, k, v, ig, lf):
    bsz, nh, _, dh = q.shape
    L = MLSTM_CHUNK
    mask = jnp.tril(jnp.ones((L, L), dtype=bool))
    xs = tuple(_to_chunks(t, L) for t in (q, k, v, ig, lf))

    def step(carry, inp):
        c_st, n_st, m_st = carry
        qt, kt, vt, it, ft = inp
        cum = jnp.cumsum(ft, axis=-1)
        dmat = jnp.where(mask, cum[..., :, None] - cum[..., None, :] + it[..., None, :], -jnp.inf)
        m_inter = cum + m_st[..., None]
        m_t = jnp.maximum(jnp.max(dmat, axis=-1), m_inter)
        w_intra = jnp.exp(dmat - m_t[..., None])
        w_inter = jnp.exp(m_inter - m_t)
        s = jnp.einsum('bhtd,bhsd->bhts', qt, kt) * w_intra
        num = jnp.einsum('bhts,bhse->bhte', s, vt) + w_inter[..., None] * jnp.einsum('bhtd,bhde->bhte', qt, c_st)
        den = jnp.sum(s, axis=-1) + w_inter * jnp.einsum('bhtd,bhd->bht', qt, n_st)
        h = num / jnp.maximum(jnp.abs(den), jnp.exp(-m_t))[..., None]
        tot = cum[..., -1]
        dec_s = tot[..., None] - cum + it
        m_new = jnp.maximum(tot + m_st, jnp.max(dec_s, axis=-1))
        ws = jnp.exp(dec_s - m_new[..., None])
        wc = jnp.exp(tot + m_st - m_new)
        c_new = wc[..., None, None] * c_st + jnp.einsum('bhs,bhsd,bhse->bhde', ws, kt, vt)
        n_new = wc[..., None] * n_st + jnp.einsum('bhs,bhsd->bhd', ws, kt)
        return (c_new, n_new, m_new), h

    init = (jnp.zeros((bsz, nh, dh, dh), q.dtype), jnp.zeros((bsz, nh, dh), q.dtype),
            jnp.zeros((bsz, nh), q.dtype))
    _, hs = lax.scan(step, init, xs)
    return _from_chunks(hs)


def _mlstm_core(q, k, v, o, gates, mlstm_gate_b, mlstm_norm):
    bsz, s, _ = q.shape
    q = _heads(q, MLSTM_HEADS) * (MLSTM_DH ** -0.5)
    k = _heads(k, MLSTM_HEADS)
    v = _heads(v, MLSTM_HEADS)
    g = (gates.reshape(bsz, s, 2, 2, MLSTM_HEADS) + mlstm_gate_b).transpose(2, 3, 0, 4, 1)
    h_f = _mlstm_chunkwise(q, k, v, g[0, 0], jax.nn.log_sigmoid(g[0, 1]))
    h_b = _flip_seq(_mlstm_chunkwise(_flip_seq(q), _flip_seq(k), _flip_seq(v), _flip_seq(g[1, 0]),
                                     _flip_seq(jax.nn.log_sigmoid(g[1, 1]))))
    hm = _headwise_rms_norm((h_f + h_b).transpose(0, 2, 1, 3), mlstm_norm)
    return jax.nn.sigmoid(o) * hm


def _gla_chunked(q, k, v, la):
    bsz, nh, _, dk = q.shape
    dv = v.shape[-1]
    L = GLA_CHUNK
    mask = jnp.tril(jnp.ones((L, L), dtype=bool))[..., None]
    xs = tuple(_to_chunks(t, L) for t in (q, k, v, la))

    def step(s_st, inp):
        qt, kt, vt, at = inp
        cum = jnp.cumsum(at, axis=2)
        rel = jnp.where(mask, cum[:, :, :, None, :] - cum[:, :, None, :, :], -jnp.inf)
        attn = jnp.einsum('bhtd,bhsd,bhtsd->bhts', qt, kt, jnp.exp(rel))
        o = jnp.einsum('bhts,bhse->bhte', attn, vt) + jnp.einsum('bhtd,bhde->bhte', qt * jnp.exp(cum), s_st)
        tot = cum[:, :, -1]
        s_new = jnp.exp(tot)[..., None] * s_st + jnp.einsum('bhsd,bhse->bhde', kt * jnp.exp(tot[:, :, None] - cum), vt)
        return s_new, o

    _, os_ = lax.scan(step, jnp.zeros((bsz, nh, dk, dv), q.dtype), xs)
    return _from_chunks(os_)


def _gla_core(q, k, v, r, glr, gla_w_gate2, gla_gate_b, gla_norm):
    bsz, s, _ = q.shape
    q = _heads(q, GLA_HEADS) * (GLA_DK ** -0.5)
    k = _heads(k, GLA_HEADS)
    v = _heads(v, GLA_HEADS)
    low = glr.reshape(bsz, s, 2, GLA_RANK)
    gate_pre = jnp.einsum('bsdr,drk->dbsk', low, gla_w_gate2) + gla_gate_b[:, None, None, :]
    la = (jax.nn.log_sigmoid(gate_pre) / GLA_TAU).reshape(2, bsz, s, GLA_HEADS, GLA_DK).transpose(0, 1, 3, 2, 4)
    o_f = _gla_chunked(q, k, v, la[0])
    o_b = _flip_seq(_gla_chunked(_flip_seq(q), _flip_seq(k), _flip_seq(v), _flip_seq(la[1])))
    return _headwise_rms_norm((o_f + o_b).transpose(0, 2, 1, 3), gla_norm) * jax.nn.silu(r)


def _split_small(w, n_main):
    tail = w[:, n_main:]
    tail = jnp.pad(tail, ((0, 0), (0, LANE - tail.shape[1])))
    return w[:, :n_main].astype(BF16), tail.astype(BF16)


def kernel(x, norm_ffn1, ffn1_w_gu, ffn1_w_down, norm_mix, norm_ffn2, ffn2_w_gu, ffn2_w_down,
           ab_w_in, lru_conv_w, lru_conv_b, lru_gate_w, lru_gate_b, lru_lambda, mlstm_gate_b,
           mlstm_norm, ab_w_out, cd_w_in, gla_w_gate2, gla_gate_b, gla_norm, s5_a_re, s5_a_im,
           s5_log_dt, s5_b_re, s5_b_im, s5_c_re, s5_c_im, s5_d, s5_w_glu, cd_w_out, final_norm):
    bsz, seq, d = x.shape
    n = bsz * seq
    depth = norm_ffn1.shape[0]
    xf = x.reshape(n, d)
    for l in range(depth):
        xf = _ffn(xf, norm_ffn1[l], ffn1_w_gu[l].astype(BF16), ffn1_w_down[l].astype(BF16))
        j = l // 2
        if l % 2 == 0:
            w_main, w_tail = _split_small(ab_w_in[j], 2 * W_LRU + 4 * W_MLSTM)
            proj, tail = _norm_proj(xf, norm_mix[l], w_main, w_tail)
            hl = _lru(proj, bsz, lru_conv_w[j], lru_conv_b[j], lru_gate_w[j], lru_gate_b[j], lru_lambda[j])
            hm = _mlstm(proj, tail, bsz, mlstm_gate_b[j])
            xf = _ab_out(xf, proj, hl, hm, mlstm_norm[j], ab_w_out[j].astype(BF16))
        else:
            w = cd_w_in[j]
            n_gla = 2 * GLA_QK + 2 * GLA_V
            w = jnp.concatenate([w[:, :n_gla], w[:, n_gla + 2 * GLA_RANK:], w[:, n_gla:n_gla + 2 * GLA_RANK]], axis=1)
            w_main, w_tail = _split_small(w, n_gla + S5_W)
            proj, tail = _norm_proj(xf, norm_mix[l], w_main, w_tail)
            og = _gla(proj, tail, bsz, gla_w_gate2[j], gla_gate_b[j])
            ops = _s5_operators(s5_a_re[j], s5_a_im[j], s5_log_dt[j], s5_b_re[j], s5_b_im[j],
                                s5_c_re[j], s5_c_im[j])
            ys = _s5(proj[:, n_gla:], bsz, ops)
            xf = _cd_out(xf, proj, og, gla_norm[j], ys, s5_d[j],
                         s5_w_glu[j].astype(BF16), cd_w_out[j].astype(BF16))
        xf = _ffn(xf, norm_ffn2[l], ffn2_w_gu[l].astype(BF16), ffn2_w_down[l].astype(BF16),
                  final_g=final_norm if l == depth - 1 else None)
    return xf.reshape(bsz, seq, d)
```

```python
import functools

import numpy as np
import jax
import jax.numpy as jnp
from jax import lax
from jax.experimental import pallas as pl
from jax.experimental.pallas import tpu as pltpu

D_MODEL = 2048
D_FF = 5632
NORM_EPS = 1e-6

W_LRU = 1024
LRU_HEADS = 8
LRU_BLOCK = W_LRU // LRU_HEADS
CONV_W = 4
CONV_PAD_L = 2
LRU_C = 8.0

W_MLSTM = 1024
MLSTM_HEADS = 4
MLSTM_DH = W_MLSTM // MLSTM_HEADS
MLSTM_CHUNK = 64

GLA_HEADS = 4
GLA_DK = 128
GLA_DV = 256
GLA_QK = GLA_HEADS * GLA_DK
GLA_V = GLA_HEADS * GLA_DV
GLA_RANK = 16
GLA_TAU = 16.0
GLA_CHUNK = 64

S5_W = 1024
S5_GROUP = 16
S5_GROUPS = S5_W // S5_GROUP
S5_P = 64
S5_CHUNK = 16
S5_GB = 8
S5_ROW_TILE = 256

LANE = 128
SUBLANE = 8
VMEM_LIMIT = 52 * 1024 * 1024

BF16 = jnp.bfloat16
F32 = jnp.float32
HI = lax.Precision.HIGHEST


def _rms(x, g):
    ms = jnp.mean(x * x, axis=-1, keepdims=True)
    return x * lax.rsqrt(ms + NORM_EPS) * g


def _softplus(z):
    return jnp.maximum(z, 0.0) + jnp.log1p(jnp.exp(-jnp.abs(z)))


def _params(*sem):
    return pltpu.CompilerParams(dimension_semantics=sem, vmem_limit_bytes=VMEM_LIMIT)


def _ffn_kernel(x_ref, g_ref, wg_ref, wu_ref, wd_ref, *rest, final):
    if final:
        fg_ref, o_ref, h_ref = rest
    else:
        o_ref, h_ref = rest
    j = pl.program_id(1)

    @pl.when(j == 0)
    def _():
        x = x_ref[...]
        h_ref[...] = _rms(x, g_ref[...]).astype(BF16)
        o_ref[...] = x

    h = h_ref[...]
    g = jnp.dot(h, wg_ref[...], preferred_element_type=F32)
    u = jnp.dot(h, wu_ref[...], preferred_element_type=F32)
    a = (0.5 * g * jax.nn.sigmoid(g) * u).astype(BF16)
    o_ref[...] += jnp.dot(a, wd_ref[...], preferred_element_type=F32)

    if final:
        @pl.when(j == pl.num_programs(1) - 1)
        def _():
            o_ref[...] = _rms(o_ref[...], fg_ref[...])


def _ffn(x, g, w_gu, w_down, final_g=None, tm=512, tf=512):
    n, d = x.shape
    f = w_down.shape[0]
    tm = min(tm, n)
    nj = f // tf
    final = final_g is not None
    in_specs = [
        pl.BlockSpec((tm, d), lambda i, j: (i, 0)),
        pl.BlockSpec((1, d), lambda i, j: (0, 0)),
        pl.BlockSpec((d, tf), lambda i, j: (0, j)),
        pl.BlockSpec((d, tf), lambda i, j: (0, j + nj)),
        pl.BlockSpec((tf, d), lambda i, j: (j, 0)),
    ]
    args = [x, g.reshape(1, d), w_gu, w_gu, w_down]
    if final:
        in_specs.append(pl.BlockSpec((1, d), lambda i, j: (0, 0)))
        args.append(final_g.reshape(1, d))
    return pl.pallas_call(
        functools.partial(_ffn_kernel, final=final),
        grid=(n // tm, nj),
        in_specs=in_specs,
        out_specs=pl.BlockSpec((tm, d), lambda i, j: (i, 0)),
        out_shape=jax.ShapeDtypeStruct((n, d), F32),
        scratch_shapes=[pltpu.VMEM((tm, d), BF16)],
        compiler_params=_params("arbitrary", "arbitrary"),
        name="ffn_final" if final else "ffn",
    )(*args)


def _norm_proj_kernel(x_ref, g_ref, w_ref, ws_ref, o_ref, os_ref, h_ref):
    @pl.when(pl.program_id(1) == 0)
    def _():
        h = _rms(x_ref[...], g_ref[...]).astype(BF16)
        h_ref[...] = h
        os_ref[...] = jnp.dot(h, ws_ref[...], preferred_element_type=F32)

    o_ref[...] = jnp.dot(h_ref[...], w_ref[...], preferred_element_type=F32)


def _norm_proj(x, g, w, w_small, tm=1024, tn=1024):
    n, d = x.shape
    m = w.shape[1]
    ms = w_small.shape[1]
    tm = min(tm, n)
    return pl.pallas_call(
        _norm_proj_kernel,
        grid=(n // tm, m // tn),
        in_specs=[
            pl.BlockSpec((tm, d), lambda i, j: (i, 0)),
            pl.BlockSpec((1, d), lambda i, j: (0, 0)),
            pl.BlockSpec((d, tn), lambda i, j: (0, j)),
            pl.BlockSpec((d, ms), lambda i, j: (0, 0)),
        ],
        out_specs=[pl.BlockSpec((tm, tn), lambda i, j: (i, j)),
                   pl.BlockSpec((tm, ms), lambda i, j: (i, 0))],
        out_shape=[jax.ShapeDtypeStruct((n, m), F32), jax.ShapeDtypeStruct((n, ms), F32)],
        scratch_shapes=[pltpu.VMEM((tm, d), BF16)],
        compiler_params=_params("arbitrary", "arbitrary"),
        name="norm_proj",
    )(x, g.reshape(1, d), w, w_small)


def _lru_kernel(xp_ref, x_ref, xn_ref, cw_ref, cb_ref, gw_ref, gb_ref, lam_ref, o_ref,
                xs_ref, a_ref, b_ref, h_ref, *, tt):
    d = pl.program_id(0)
    t = pl.program_id(2)
    nt = pl.num_programs(2)
    tb = jnp.where(d == 0, t, nt - 1 - t)
    w = x_ref.shape[1]

    xs_ref[0:SUBLANE, :] = jnp.where(tb > 0, xp_ref[...], 0.0)
    xs_ref[SUBLANE:SUBLANE + tt, :] = x_ref[...]
    xs_ref[SUBLANE + tt:2 * SUBLANE + tt, :] = jnp.where(tb < nt - 1, xn_ref[...], 0.0)
    cw = cw_ref[...]
    xc = cb_ref[...]
    for k in range(CONV_W):
        off = SUBLANE + k - CONV_PAD_L
        xc = xc + cw[k:k + 1, :] * xs_ref[off:off + tt, :]

    xcb = xc.astype(BF16)
    gb = gb_ref[0]
    sp = _softplus(-lam_ref[0])
    for hd in range(w // LRU_BLOCK):
        sl = slice(hd * LRU_BLOCK, (hd + 1) * LRU_BLOCK)
        z = jnp.dot(xcb[:, sl], gw_ref[0, hd], preferred_element_type=F32)
        r = jax.nn.sigmoid(z[:, :LRU_BLOCK] + gb[0:1, sl])
        i = jax.nn.sigmoid(z[:, LRU_BLOCK:] + gb[1:2, sl])
        log_a = (-LRU_C * sp[:, sl]) * r
        a = jnp.exp(log_a)
        a_ref[:, sl] = a
        b_ref[:, sl] = jnp.sqrt(-jnp.tanh(log_a) * (a * a + 1.0)) * (i * xc[:, sl])

    @pl.when(t == 0)
    def _():
        h_ref[...] = jnp.zeros_like(h_ref)

    sub = lax.broadcasted_iota(jnp.int32, (SUBLANE, w), 0)
    ntile = tt // SUBLANE

    def scan(rev):
        def body(jt, h):
            r0 = pl.multiple_of((ntile - 1 - jt if rev else jt) * SUBLANE, SUBLANE)
            tile = jnp.zeros((SUBLANE, w), F32)
            at = a_ref[pl.ds(r0, SUBLANE), :]
            bt = b_ref[pl.ds(r0, SUBLANE), :]
            for r in (range(SUBLANE - 1, -1, -1) if rev else range(SUBLANE)):
                h = at[r:r + 1, :] * h + bt[r:r + 1, :]
                tile = jnp.where(sub == r, h, tile)
            o_ref[0, pl.ds(r0, SUBLANE), :] = tile
            return h

        h_ref[...] = lax.fori_loop(0, ntile, body, h_ref[...])

    pl.when(d == 0)(functools.partial(scan, False))
    pl.when(d == 1)(functools.partial(scan, True))


def _lru(proj, bsz, conv_w, conv_b, gate_w, gate_b, lam, tt=256):
    n = proj.shape[0]
    seq = n // bsz
    tt = min(tt, seq)
    nt = seq // tt
    w = W_LRU
    r8 = tt // SUBLANE
    gw = jnp.concatenate([gate_w[:, 0], gate_w[:, 1]], axis=-1).astype(BF16)

    def tb(d, t):
        return t + d * (nt - 1 - 2 * t)

    return pl.pallas_call(
        functools.partial(_lru_kernel, tt=tt),
        grid=(2, bsz, nt),
        in_specs=[
            pl.BlockSpec((SUBLANE, w), lambda d, b, t: (jnp.maximum((b * nt + tb(d, t)) * r8 - 1, 0), 0)),
            pl.BlockSpec((tt, w), lambda d, b, t: (b * nt + tb(d, t), 0)),
            pl.BlockSpec((SUBLANE, w), lambda d, b, t: (jnp.minimum((b * nt + tb(d, t) + 1) * r8, n // SUBLANE - 1), 0)),
            pl.BlockSpec((CONV_W, w), lambda d, b, t: (0, 0)),
            pl.BlockSpec((1, w), lambda d, b, t: (0, 0)),
            pl.BlockSpec((1, LRU_HEADS, LRU_BLOCK, 2 * LRU_BLOCK), lambda d, b, t: (d, 0, 0, 0)),
            pl.BlockSpec((1, 2, w), lambda d, b, t: (d, 0, 0)),
            pl.BlockSpec((1, 1, w), lambda d, b, t: (d, 0, 0)),
        ],
        out_specs=pl.BlockSpec((1, tt, w), lambda d, b, t: (d, b * nt + tb(d, t), 0)),
        out_shape=jax.ShapeDtypeStruct((2, n, w), F32),
        scratch_shapes=[pltpu.VMEM((tt + 2 * SUBLANE, w), F32), pltpu.VMEM((tt, w), F32),
                        pltpu.VMEM((tt, w), F32), pltpu.VMEM((1, w), F32)],
        compiler_params=_params("arbitrary", "arbitrary", "arbitrary"),
        name="rg_lru",
    )(proj, proj, proj, conv_w, conv_b.reshape(1, w), gw, gate_b, lam.reshape(2, 1, w))


def _s5_discretise(a_re, a_im, log_dt, b_re, b_im):
    dt = jnp.exp(log_dt)[:, None]
    mag = jnp.exp(dt * a_re)
    lr = mag * jnp.cos(dt * a_im)
    li = mag * jnp.sin(dt * a_im)
    den = a_re * a_re + a_im * a_im
    nr = lr - 1.0
    cr = (nr * a_re + li * a_im) / den
    ci = (li * a_re - nr * a_im) / den
    bbr = cr[..., None] * b_re - ci[..., None] * b_im
    bbi = cr[..., None] * b_im + ci[..., None] * b_re
    return lr, li, bbr, bbi


def _s5_operators(a_re, a_im, log_dt, b_re, b_im, c_re, c_im):
    L = S5_CHUNK
    G, P, C = S5_GROUPS, S5_P, S5_GROUP
    kfs, mins, mouts, lams = [], [], [], []
    for d in range(2):
        lr, li, bbr, bbi = _s5_discretise(a_re[d], a_im[d], log_dt[d], b_re, b_im)
        pr, pi = [jnp.ones_like(lr)], [jnp.zeros_like(li)]
        for _ in range(L):
            pr, pi = pr + [pr[-1] * lr - pi[-1] * li], pi + [pr[-1] * li + pi[-1] * lr]
        pr, pi = jnp.stack(pr), jnp.stack(pi)
        clr = c_re[None] * pr[:, :, None, :] - c_im[None] * pi[:, :, None, :]
        cli = c_re[None] * pi[:, :, None, :] + c_im[None] * pr[:, :, None, :]
        kfs.append(jnp.einsum('jgcp,gpk->jgck', clr[:L], bbr, precision=HI)
                   - jnp.einsum('jgcp,gpk->jgck', cli[:L], bbi, precision=HI))
        blr = pr[:, :, :, None] * bbr[None] - pi[:, :, :, None] * bbi[None]
        bli = pr[:, :, :, None] * bbi[None] + pi[:, :, :, None] * bbr[None]
        if d == 0:
            e_in = jnp.arange(L - 1, -1, -1)
            e_out = jnp.arange(1, L + 1)
        else:
            e_in = jnp.arange(L)
            e_out = jnp.arange(L, 0, -1)
        m_in = jnp.stack([blr[e_in], bli[e_in]], axis=0)
        mins.append(m_in.transpose(2, 1, 4, 0, 3))
        m_out = jnp.stack([clr[e_out], -cli[e_out]], axis=0)
        mouts.append(m_out.transpose(2, 1, 3, 0, 4))
        lams.append(jnp.stack([pr[L], pi[L]], axis=1))
    kf, kb = kfs
    kfull = jnp.concatenate([kb[:0:-1], (kf[0] + kb[0])[None], kf[1:]], axis=0)
    gb = S5_GB
    nb_ = G // gb
    k2 = jnp.swapaxes(kfull.reshape(2 * L - 1, nb_, gb, C, C), 3, 4)
    k2 = k2.reshape(2 * L - 1, nb_, gb * C, C).transpose(1, 0, 2, 3)

    def rows_of_block(parts):
        a = jnp.stack(parts, axis=3).reshape(nb_, gb, L, C, 4 * P)
        return a.transpose(0, 2, 1, 3, 4).reshape(nb_, L * gb * C, 4 * P)

    lam_l = jnp.concatenate(lams, axis=1)
    lam_blk = lam_l.reshape(nb_, gb, 4, P).transpose(0, 2, 1, 3).reshape(nb_, 1, 4 * gb * P)
    return k2, rows_of_block(mins), rows_of_block(mouts), lam_blk


def _s5_state_kernel(u_ref, min_ref, lam_ref, hin_ref, h_ref, *, nb):
    nl, rows, _ = u_ref.shape
    rb = rows // nb
    hw = h_ref.shape[1] // 4
    rt = min(rows, S5_ROW_TILE)
    lo = lax.broadcasted_iota(jnp.int32, (rt, LANE), 1) < S5_P
    ng = hw // S5_P

    def pairs():
        for d in range(2):
            for k in range(ng // 2):
                yield ((d * ng + 2 * k) * LANE, (d * ng + 2 * k + 1) * LANE,
                       2 * d * hw + k * LANE, (2 * d + 1) * hw + k * LANE)

    for r0 in range(0, rows, rt):
        u = jnp.concatenate([u_ref[s, r0:r0 + rt, :] for s in range(nl)], axis=1)
        hl = jnp.dot(u, min_ref[0], preferred_element_type=F32)
        for ca, cb, cre, cim in pairs():
            a, b = hl[:, ca:ca + LANE], hl[:, cb:cb + LANE]
            h_ref[r0:r0 + rt, cre:cre + LANE] = jnp.where(lo, a, pltpu.roll(b, S5_P, 1))
            h_ref[r0:r0 + rt, cim:cim + LANE] = jnp.where(lo, pltpu.roll(a, S5_P, 1), b)
    lam = lam_ref[0]
    lrf, lif = lam[:, 0:hw], lam[:, hw:2 * hw]
    lrb, lib = lam[:, 2 * hw:3 * hw], lam[:, 3 * hw:4 * hw]
    sub = lax.broadcasted_iota(jnp.int32, (SUBLANE, hw), 0)

    def body(it, carry):
        new = []
        for b in range(nb):
            fr, fi, br, bi = carry[4 * b:4 * b + 4]
            f0 = pl.multiple_of(b * rb + it * SUBLANE, SUBLANE)
            b0 = pl.multiple_of(b * rb + rb - SUBLANE - it * SUBLANE, SUBLANE)
            tiles = [jnp.zeros((SUBLANE, hw), F32)] * 4
            lf = h_ref[pl.ds(f0, SUBLANE), 0:2 * hw]
            lb = h_ref[pl.ds(b0, SUBLANE), 2 * hw:4 * hw]
            for r in range(SUBLANE):
                q = SUBLANE - 1 - r
                tiles = [jnp.where(sub == r, fr, tiles[0]), jnp.where(sub == r, fi, tiles[1]),
                         jnp.where(sub == q, br, tiles[2]), jnp.where(sub == q, bi, tiles[3])]
                fr, fi, br, bi = (lrf * fr - lif * fi + lf[r:r + 1, 0:hw],
                                  lrf * fi + lif * fr + lf[r:r + 1, hw:2 * hw],
                                  lrb * br - lib * bi + lb[q:q + 1, 0:hw],
                                  lrb * bi + lib * br + lb[q:q + 1, hw:2 * hw])
            h_ref[pl.ds(f0, SUBLANE), 0:hw] = tiles[0]
            h_ref[pl.ds(f0, SUBLANE), hw:2 * hw] = tiles[1]
            h_ref[pl.ds(b0, SUBLANE), 2 * hw:3 * hw] = tiles[2]
            h_ref[pl.ds(b0, SUBLANE), 3 * hw:4 * hw] = tiles[3]
            new += [fr, fi, br, bi]
        return tuple(new)

    zero = jnp.zeros((1, hw), F32)
    lax.fori_loop(0, rb // SUBLANE, body, (zero,) * (4 * nb))
    for r0 in range(0, rows, rt):
        for ca, cb, cre, cim in pairs():
            re, im = h_ref[r0:r0 + rt, cre:cre + LANE], h_ref[r0:r0 + rt, cim:cim + LANE]
            hin_ref[0, r0:r0 + rt, ca:ca + LANE] = jnp.where(lo, re, pltpu.roll(im, S5_P, 1)).astype(BF16)
            hin_ref[0, r0:r0 + rt, cb:cb + LANE] = jnp.where(lo, pltpu.roll(re, S5_P, 1), im).astype(BF16)


def _s5_expand_kernel(a_ref, o_ref, *, transpose):
    a = a_ref[0]
    rows = a.shape[0]
    row_g = (lax.broadcasted_iota(jnp.int32, (rows, LANE), 0) // S5_GROUP) % S5_GB
    for d in range(2):
        ad = a[:, d * LANE:(d + 1) * LANE]
        for g in range(S5_GB):
            tile = jnp.where(row_g == g, ad, 0.0)
            col = (d * S5_GB + g) * LANE
            if transpose:
                o_ref[0, col:col + LANE, :] = tile.T.astype(BF16)
            else:
                o_ref[0, :, col:col + LANE] = tile.astype(BF16)


def _s5_expand(a, transpose):
    nblk, rows, _ = a.shape
    sw = 2 * S5_GB * LANE
    shape = (nblk, sw, rows) if transpose else (nblk, rows, sw)
    return pl.pallas_call(
        functools.partial(_s5_expand_kernel, transpose=transpose),
        grid=(nblk,),
        in_specs=[pl.BlockSpec((1,) + a.shape[1:], lambda b: (b, 0, 0))],
        out_specs=pl.BlockSpec((1,) + shape[1:], lambda b: (b, 0, 0)),
        out_shape=jax.ShapeDtypeStruct(shape, BF16),
        compiler_params=_params("arbitrary"),
        name="s5_expand",
    )(a)


def _s5_out_kernel(u_ref, hin_ref, k2_ref, mout_ref, y_ref, t_ref):
    nl = u_ref.shape[0]

    @pl.when(pl.program_id(1) == 0)
    def _():
        row_g = lax.broadcasted_iota(jnp.int32, (LANE, LANE), 0) // S5_GROUP
        col_g = lax.broadcasted_iota(jnp.int32, (LANE, LANE), 1) // S5_GROUP
        spread = (lax.broadcasted_iota(jnp.int32, (S5_GROUP, LANE), 1) % S5_GROUP
                  == lax.broadcasted_iota(jnp.int32, (S5_GROUP, LANE), 0)).astype(BF16)
        for j in range(2 * nl - 1):
            rep = jnp.dot(k2_ref[0, j].astype(BF16), spread, preferred_element_type=F32)
            tile = jnp.where(row_g == col_g, rep, 0.0).astype(BF16)
            for s in range(nl):
                t = s + j - (nl - 1)
                if 0 <= t < nl:
                    t_ref[s * LANE:(s + 1) * LANE, t * LANE:(t + 1) * LANE] = tile

    u = jnp.concatenate([u_ref[s] for s in range(nl)], axis=1)
    y = (jnp.dot(u, t_ref[...], preferred_element_type=F32)
         + jnp.dot(hin_ref[0], mout_ref[0], preferred_element_type=F32))
    for t in range(nl):
        y_ref[t] = y[:, t * LANE:(t + 1) * LANE]


def _s5(u, bsz, ops, tr=256):
    k2, a_min, a_mout, lam_blk = ops
    min_blk = _s5_expand(a_min, False)
    mout_blk = _s5_expand(a_mout, True)
    n, w = u.shape
    L = S5_CHUNK
    rows = n // L
    nblk = w // LANE
    kw = L * LANE
    sw = lam_blk.shape[2]
    tr = min(tr, rows)
    u3 = u.reshape(rows, L, w).transpose(1, 0, 2).astype(BF16)
    hin = pl.pallas_call(
        functools.partial(_s5_state_kernel, nb=bsz),
        grid=(nblk,),
        in_specs=[
            pl.BlockSpec((L, rows, LANE), lambda b: (0, 0, b)),
            pl.BlockSpec((1, kw, sw), lambda b: (b, 0, 0)),
            pl.BlockSpec((1, 1, sw), lambda b: (b, 0, 0)),
        ],
        out_specs=pl.BlockSpec((1, rows, sw), lambda b: (b, 0, 0)),
        out_shape=jax.ShapeDtypeStruct((nblk, rows, sw), BF16),
        scratch_shapes=[pltpu.VMEM((rows, sw), F32)],
        compiler_params=_params("arbitrary"),
        name="s5_state",
    )(u3, min_blk, lam_blk)
    y3 = pl.pallas_call(
        _s5_out_kernel,
        grid=(nblk, rows // tr),
        in_specs=[
            pl.BlockSpec((L, tr, LANE), lambda b, i: (0, i, b)),
            pl.BlockSpec((1, tr, sw), lambda b, i: (b, i, 0)),
            pl.BlockSpec((1,) + k2.shape[1:], lambda b, i: (b, 0, 0, 0)),
            pl.BlockSpec((1, sw, kw), lambda b, i: (b, 0, 0)),
        ],
        out_specs=pl.BlockSpec((L, tr, LANE), lambda b, i: (0, i, b)),
        out_shape=jax.ShapeDtypeStruct((L, rows, w), F32),
        scratch_shapes=[pltpu.VMEM((kw, kw), BF16)],
        compiler_params=_params("arbitrary", "arbitrary"),
        name="s5_out",
    )(u3, hin, k2, mout_blk)
    return y3.transpose(1, 0, 2).reshape(n, w)


def _log_sigmoid(z):
    return -_softplus(-z)


def _tri(length, d):
    ti = lax.broadcasted_iota(jnp.int32, (length, length), 0)
    si = lax.broadcasted_iota(jnp.int32, (length, length), 1)
    return (ti - si) * (1 - 2 * d) >= 0


def _mlstm_kernel(q_ref, k_ref, v_ref, g_ref, gb_ref, o_ref, c_ref, n_ref, m_ref):
    d = pl.program_id(0)
    length = q_ref.shape[0]
    nh, dh = MLSTM_HEADS, MLSTM_DH

    @pl.when(pl.program_id(2) == 0)
    def _():
        c_ref[...] = jnp.zeros_like(c_ref)
        n_ref[...] = jnp.zeros_like(n_ref)
        m_ref[...] = jnp.zeros_like(m_ref)

    causal = _tri(length, d)
    gp = g_ref[...] + gb_ref[...]
    gp = jnp.where(d == 0, gp, pltpu.roll(gp, LANE - 2 * nh, 1))
    lf = _log_sigmoid(gp)
    cum = jnp.dot(causal.astype(F32), lf, precision=HI, preferred_element_type=F32)
    tot = jnp.sum(lf, axis=0, keepdims=True)
    gp_t = gp.T
    cum_t = cum.T
    nt = (((1,), (1,)), ((), ()))
    tn = (((0,), (0,)), ((), ()))

    for h in range(nh):
        sl = slice(h * dh, (h + 1) * dh)
        qh = (q_ref[:, sl] * (dh ** -0.5)).astype(BF16)
        kf = k_ref[:, sl]
        kh = kf.astype(BF16)
        vh = v_ref[:, sl].astype(BF16)
        ig_c, cum_c = gp[:, h:h + 1], cum[:, nh + h:nh + h + 1]
        ig_r, cum_r = gp_t[h:h + 1, :], cum_t[nh + h:nh + h + 1, :]
        tot_h = tot[:, nh + h:nh + h + 1]
        m_st = m_ref[h:h + 1, 0:1]
        c_st = c_ref[h]
        dmat = jnp.where(causal, cum_c - cum_r + ig_r, -jnp.inf)
        m_inter = cum_c + m_st
        m_t = jnp.maximum(jnp.max(dmat, axis=1, keepdims=True), m_inter)
        w_inter = jnp.exp(m_inter - m_t)
        s = lax.dot_general(qh, kh, nt, preferred_element_type=F32) * jnp.exp(dmat - m_t)
        num = (jnp.dot(s.astype(BF16), vh, preferred_element_type=F32)
               + w_inter * jnp.dot(qh, c_st.astype(BF16), preferred_element_type=F32))
        den = (jnp.sum(s, axis=1, keepdims=True)
               + w_inter * jnp.sum(qh.astype(F32) * n_ref[h:h + 1, :], axis=1, keepdims=True))
        o_ref[0, :, sl] = num / jnp.maximum(jnp.abs(den), jnp.exp(-m_t))
        dec = tot_h - cum_c + ig_c
        m_new = jnp.maximum(tot_h + m_st, jnp.max(dec, axis=0, keepdims=True))
        wc = jnp.exp(tot_h + m_st - m_new)
        kw = jnp.exp(dec - m_new) * kf
        c_ref[h] = wc * c_st + lax.dot_general(kw.astype(BF16), vh, tn, preferred_element_type=F32)
        n_ref[h:h + 1, :] = wc * n_ref[h:h + 1, :] + jnp.sum(kw, axis=0, keepdims=True)
        m_ref[h:h + 1, :] = jnp.broadcast_to(m_new, (1, LANE))


def _mlstm(proj, tail, bsz, gate_b, length=256):
    n = proj.shape[0]
    seq = n // bsz
    length = min(length, seq)
    nc = seq // length
    w = W_MLSTM
    q_blk = 2 * W_LRU // w
    gb = jnp.pad(gate_b.reshape(1, -1), ((0, 0), (0, LANE - gate_b.size)))

    def row(d, b, c):
        return b * nc + c + d * (nc - 1 - 2 * c)

    return pl.pallas_call(
        _mlstm_kernel,
        grid=(2, bsz, nc),
        in_specs=[
            pl.BlockSpec((length, w), lambda d, b, c: (row(d, b, c), q_blk)),
            pl.BlockSpec((length, w), lambda d, b, c: (row(d, b, c), q_blk + 1)),
            pl.BlockSpec((length, w), lambda d, b, c: (row(d, b, c), q_blk + 2)),
            pl.BlockSpec((length, LANE), lambda d, b, c: (row(d, b, c), 0)),
            pl.BlockSpec((1, LANE), lambda d, b, c: (0, 0)),
        ],
        out_specs=pl.BlockSpec((1, length, w), lambda d, b, c: (d, row(d, b, c), 0)),
        out_shape=jax.ShapeDtypeStruct((2, n, w), F32),
        scratch_shapes=[pltpu.VMEM((MLSTM_HEADS, MLSTM_DH, MLSTM_DH), F32),
                        pltpu.VMEM((SUBLANE, MLSTM_DH), F32), pltpu.VMEM((SUBLANE, LANE), F32)],
        compiler_params=_params("arbitrary", "arbitrary", "arbitrary"),
        name="mlstm",
    )(proj, proj, proj, tail, gb)


GLA_SUB = 16


def _gla_chunk(q_ref, k_ref, v_ref, cum, o_ref, s_ref, rev):
    length = q_ref.shape[0]
    c = GLA_SUB
    nt = (((1,), (1,)), ((), ()))
    tn = (((0,), (0,)), ((), ()))

    def rows(a, b):
        return slice(length - b, length - a) if rev else slice(a, b)

    def row(a):
        i = length - 1 - a if rev else a
        return slice(i, i + 1)

    ti = lax.broadcasted_iota(jnp.int32, (c, 1), 0)
    for h in range(GLA_HEADS):
        kl = slice(h * GLA_DK, (h + 1) * GLA_DK)
        vl = slice(h * GLA_DV, (h + 1) * GLA_DV)
        ch = cum[:, kl]
        qs = q_ref[:, kl] * (GLA_DK ** -0.5)
        kk = k_ref[:, kl]
        vf = v_ref[:, vl]
        vv = vf.astype(BF16)
        st = s_ref[h]
        tot = ch[row(length - 1)]
        o = lax.dot_general((qs * jnp.exp(ch)).astype(BF16), st.astype(BF16), nt, preferred_element_type=F32)
        ob = [o[rows(p * c, (p + 1) * c)] for p in range(length // c)]
        m = length // 2
        while m >= c:
            for start in range(0, length, 2 * m):
                fst, sec = rows(start, start + m), rows(start + m, start + 2 * m)
                r = ch[row(start + m)]
                qh = (qs[sec] * jnp.exp(ch[sec] - r)).astype(BF16)
                kh = (kk[fst] * jnp.exp(r - ch[fst])).astype(BF16)
                att = lax.dot_general(qh, kh, nt, preferred_element_type=F32)
                contrib = jnp.dot(att.astype(BF16), vv[fst], preferred_element_type=F32)
                for p in range((start + m) // c, (start + 2 * m) // c):
                    lo = rows(p * c, (p + 1) * c).start - sec.start
                    ob[p] = ob[p] + contrib[lo:lo + c]
            m //= 2
        for p in range(length // c):
            blk = rows(p * c, (p + 1) * c)
            cb, qb, kb, vb = ch[blk], qs[blk], kk[blk], vf[blk]
            acc = ob[p]
            for s in range(c):
                sees = (ti <= s) if rev else (ti >= s)
                e = jnp.exp(jnp.where(sees, cb - cb[s:s + 1], -jnp.inf))
                a = jnp.sum(qb * kb[s:s + 1] * e, axis=1, keepdims=True)
                acc = acc + a * vb[s:s + 1]
            o_ref[0, blk, vl] = acc
        kd = (kk * jnp.exp(tot - ch)).astype(BF16)
        s_ref[h] = jnp.exp(tot) * st + lax.dot_general(vv, kd, tn, preferred_element_type=F32)


def _gla_kernel(q_ref, k_ref, v_ref, low_ref, wg_ref, bg_ref, o_ref, s_ref):
    d = pl.program_id(0)
    length = q_ref.shape[0]

    @pl.when(pl.program_id(2) == 0)
    def _():
        s_ref[...] = jnp.zeros_like(s_ref)

    gate_pre = jnp.dot(low_ref[...], wg_ref[0], precision=HI, preferred_element_type=F32) + bg_ref[0]
    la = _log_sigmoid(gate_pre) * (1.0 / GLA_TAU)
    cum = jnp.dot(_tri(length, d).astype(F32), la, precision=HI, preferred_element_type=F32)
    pl.when(d == 0)(lambda: _gla_chunk(q_ref, k_ref, v_ref, cum, o_ref, s_ref, False))
    pl.when(d == 1)(lambda: _gla_chunk(q_ref, k_ref, v_ref, cum, o_ref, s_ref, True))


def _gla(proj, tail, bsz, w_gate2, gate_b, length=128):
    n = proj.shape[0]
    seq = n // bsz
    length = min(length, seq)
    nc = seq // length
    wg = jnp.zeros((2, LANE, GLA_QK), F32)
    for d in range(2):
        wg = wg.at[d, d * GLA_RANK:(d + 1) * GLA_RANK].set(w_gate2[d])

    def row(d, b, c):
        return b * nc + c + d * (nc - 1 - 2 * c)

    return pl.pallas_call(
        _gla_kernel,
        grid=(2, bsz, nc),
        in_specs=[
            pl.BlockSpec((length, GLA_QK), lambda d, b, c: (row(d, b, c), 0)),
            pl.BlockSpec((length, GLA_QK), lambda d, b, c: (row(d, b, c), 1)),
            pl.BlockSpec((length, GLA_V), lambda d, b, c: (row(d, b, c), 2 * GLA_QK // GLA_V)),
            pl.BlockSpec((length, LANE), lambda d, b, c: (row(d, b, c), 0)),
            pl.BlockSpec((1, LANE, GLA_QK), lambda d, b, c: (d, 0, 0)),
            pl.BlockSpec((1, 1, GLA_QK), lambda d, b, c: (d, 0, 0)),
        ],
        out_specs=pl.BlockSpec((1, length, GLA_V), lambda d, b, c: (d, row(d, b, c), 0)),
        out_shape=jax.ShapeDtypeStruct((2, n, GLA_V), F32),
        scratch_shapes=[pltpu.VMEM((GLA_HEADS, GLA_DV, GLA_DK), F32)],
        compiler_params=_params("arbitrary", "arbitrary", "arbitrary"),
        name="gla",
    )(proj, proj, proj, tail, wg, gate_b.reshape(2, 1, GLA_QK))


def _head_norm(t, g, heads):
    dh = t.shape[1] // heads
    outs = []
    for h in range(heads):
        th = t[:, h * dh:(h + 1) * dh]
        outs.append(th * lax.rsqrt(jnp.mean(th * th, axis=1, keepdims=True) + NORM_EPS))
    return jnp.concatenate(outs, axis=1) * g


def _ab_out_kernel(x_ref, gr_ref, og_ref, hl_ref, hm_ref, g_ref, w_ref, o_ref):
    k = gr_ref.shape[1]
    ya = jax.nn.gelu(gr_ref[...]) * (hl_ref[0] + hl_ref[1])
    yb = jax.nn.sigmoid(og_ref[...]) * _head_norm(hm_ref[0] + hm_ref[1], g_ref[...], MLSTM_HEADS)
    o_ref[...] = (x_ref[...]
                  + jnp.dot(ya.astype(BF16), w_ref[0:k, :], preferred_element_type=F32)
                  + jnp.dot(yb.astype(BF16), w_ref[k:, :], preferred_element_type=F32))


def _ab_out(x, proj, hl, hm, norm_g, w, tm=512):
    n, d = x.shape
    k = W_LRU
    tm = min(tm, n)
    return pl.pallas_call(
        _ab_out_kernel,
        grid=(n // tm,),
        in_specs=[
            pl.BlockSpec((tm, d), lambda i: (i, 0)),
            pl.BlockSpec((tm, k), lambda i: (i, 1)),
            pl.BlockSpec((tm, k), lambda i: (i, 5)),
            pl.BlockSpec((2, tm, k), lambda i: (0, i, 0)),
            pl.BlockSpec((2, tm, k), lambda i: (0, i, 0)),
            pl.BlockSpec((1, k), lambda i: (0, 0)),
            pl.BlockSpec((2 * k, d), lambda i: (0, 0)),
        ],
        out_specs=pl.BlockSpec((tm, d), lambda i: (i, 0)),
        out_shape=jax.ShapeDtypeStruct((n, d), F32),
        compiler_params=_params("arbitrary"),
        name="ab_out",
    )(x, proj, proj, hl, hm, norm_g.reshape(1, k), w)


def _cd_out_kernel(x_ref, r_ref, og_ref, g_ref, ys_ref, u_ref, d_ref, wg_ref, w_ref, o_ref):
    k = r_ref.shape[1]
    r = r_ref[...]
    yc = _head_norm(og_ref[0] + og_ref[1], g_ref[...], GLA_HEADS) * (r * jax.nn.sigmoid(r))
    y = jax.nn.gelu(ys_ref[...] + d_ref[...] * u_ref[...])
    yd = y * jax.nn.sigmoid(jnp.dot(y.astype(BF16), wg_ref[...], preferred_element_type=F32))
    o_ref[...] = (x_ref[...]
                  + jnp.dot(yc.astype(BF16), w_ref[0:k, :], preferred_element_type=F32)
                  + jnp.dot(yd.astype(BF16), w_ref[k:, :], preferred_element_type=F32))


def _cd_out(x, proj, og, norm_g, ys, s5_d, w_glu, w, tm=512):
    n, d = x.shape
    k = S5_W
    tm = min(tm, n)
    return pl.pallas_call(
        _cd_out_kernel,
        grid=(n // tm,),
        in_specs=[
            pl.BlockSpec((tm, d), lambda i: (i, 0)),
            pl.BlockSpec((tm, k), lambda i: (i, 2)),
            pl.BlockSpec((2, tm, k), lambda i: (0, i, 0)),
            pl.BlockSpec((1, k), lambda i: (0, 0)),
            pl.BlockSpec((tm, k), lambda i: (i, 0)),
            pl.BlockSpec((tm, k), lambda i: (i, 3)),
            pl.BlockSpec((1, k), lambda i: (0, 0)),
            pl.BlockSpec((k, k), lambda i: (0, 0)),
            pl.BlockSpec((2 * k, d), lambda i: (0, 0)),
        ],
        out_specs=pl.BlockSpec((tm, d), lambda i: (i, 0)),
        out_shape=jax.ShapeDtypeStruct((n, d), F32),
        compiler_params=_params("arbitrary"),
        name="cd_out",
    )(x, proj, og, norm_g.reshape(1, k), ys, proj, s5_d.reshape(1, k), w_glu, w)


def _headwise_rms_norm(t, g):
    y = t * lax.rsqrt(jnp.mean(t * t, axis=-1, keepdims=True) + NORM_EPS)
    return y.reshape(t.shape[:2] + (-1,)) * g


def _heads(t, n):
    return t.reshape(t.shape[:2] + (n, -1)).transpose(0, 2, 1, 3)


def _flip_seq(t):
    return jnp.flip(t, axis=2)


def _to_chunks(t, chunk):
    b, h, s = t.shape[:3]
    t = t.reshape((b, h, s // chunk, chunk) + t.shape[3:])
    return jnp.moveaxis(t, 2, 0)


def _from_chunks(t):
    nc, b, h, l = t.shape[:4]
    return jnp.moveaxis(t, 0, 2).reshape((b, h, nc * l) + t.shape[4:])


def _mlstm_chunkwise(q, k, v, ig, lf):
    bsz, nh, _, dh = q.shape
    L = MLSTM_CHUNK
    mask = jnp.tril(jnp.ones((L, L), dtype=bool))
    xs = tuple(_to_chunks(t, L) for t in (q, k, v, ig, lf))

    def step(carry, inp):
        c_st, n_st, m_st = carry
        qt, kt, vt, it, ft = inp
        cum = jnp.cumsum(ft, axis=-1)
        dmat = jnp.where(mask, cum[..., :, None] - cum[..., None, :] + it[..., None, :], -jnp.inf)
        m_inter = cum + m_st[..., None]
        m_t = jnp.maximum(jnp.max(dmat, axis=-1), m_inter)
        w_intra = jnp.exp(dmat - m_t[..., None])
        w_inter = jnp.exp(m_inter - m_t)
        s = jnp.einsum('bhtd,bhsd->bhts', qt, kt) * w_intra
        num = jnp.einsum('bhts,bhse->bhte', s, vt) + w_inter[..., None] * jnp.einsum('bhtd,bhde->bhte', qt, c_st)
        den = jnp.sum(s, axis=-1) + w_inter * jnp.einsum('bhtd,bhd->bht', qt, n_st)
        h = num / jnp.maximum(jnp.abs(den), jnp.exp(-m_t))[..., None]
        tot = cum[..., -1]
        dec_s = tot[..., None] - cum + it
        m_new = jnp.maximum(tot + m_st, jnp.max(dec_s, axis=-1))
        ws = jnp.exp(dec_s - m_new[..., None])
        wc = jnp.exp(tot + m_st - m_new)
        c_new = wc[..., None, None] * c_st + jnp.einsum('bhs,bhsd,bhse->bhde', ws, kt, vt)
        n_new = wc[..., None] * n_st + jnp.einsum('bhs,bhsd->bhd', ws, kt)
        return (c_new, n_new, m_new), h

    init = (jnp.zeros((bsz, nh, dh, dh), q.dtype), jnp.zeros((bsz, nh, dh), q.dtype),
            jnp.zeros((bsz, nh), q.dtype))
    _, hs = lax.scan(step, init, xs)
    return _from_chunks(hs)


def _mlstm_core(q, k, v, o, gates, mlstm_gate_b, mlstm_norm):
    bsz, s, _ = q.shape
    q = _heads(q, MLSTM_HEADS) * (MLSTM_DH ** -0.5)
    k = _heads(k, MLSTM_HEADS)
    v = _heads(v, MLSTM_HEADS)
    g = (gates.reshape(bsz, s, 2, 2, MLSTM_HEADS) + mlstm_gate_b).transpose(2, 3, 0, 4, 1)
    h_f = _mlstm_chunkwise(q, k, v, g[0, 0], jax.nn.log_sigmoid(g[0, 1]))
    h_b = _flip_seq(_mlstm_chunkwise(_flip_seq(q), _flip_seq(k), _flip_seq(v), _flip_seq(g[1, 0]),
                                     _flip_seq(jax.nn.log_sigmoid(g[1, 1]))))
    hm = _headwise_rms_norm((h_f + h_b).transpose(0, 2, 1, 3), mlstm_norm)
    return jax.nn.sigmoid(o) * hm


def _gla_chunked(q, k, v, la):
    bsz, nh, _, dk = q.shape
    dv = v.shape[-1]
    L = GLA_CHUNK
    mask = jnp.tril(jnp.ones((L, L), dtype=bool))[..., None]
    xs = tuple(_to_chunks(t, L) for t in (q, k, v, la))

    def step(s_st, inp):
        qt, kt, vt, at = inp
        cum = jnp.cumsum(at, axis=2)
        rel = jnp.where(mask, cum[:, :, :, None, :] - cum[:, :, None, :, :], -jnp.inf)
        attn = jnp.einsum('bhtd,bhsd,bhtsd->bhts', qt, kt, jnp.exp(rel))
        o = jnp.einsum('bhts,bhse->bhte', attn, vt) + jnp.einsum('bhtd,bhde->bhte', qt * jnp.exp(cum), s_st)
        tot = cum[:, :, -1]
        s_new = jnp.exp(tot)[..., None] * s_st + jnp.einsum('bhsd,bhse->bhde', kt * jnp.exp(tot[:, :, None] - cum), vt)
        return s_new, o

    _, os_ = lax.scan(step, jnp.zeros((bsz, nh, dk, dv), q.dtype), xs)
    return _from_chunks(os_)


def _gla_core(q, k, v, r, glr, gla_w_gate2, gla_gate_b, gla_norm):
    bsz, s, _ = q.shape
    q = _heads(q, GLA_HEADS) * (GLA_DK ** -0.5)
    k = _heads(k, GLA_HEADS)
    v = _heads(v, GLA_HEADS)
    low = glr.reshape(bsz, s, 2, GLA_RANK)
    gate_pre = jnp.einsum('bsdr,drk->dbsk', low, gla_w_gate2) + gla_gate_b[:, None, None, :]
    la = (jax.nn.log_sigmoid(gate_pre) / GLA_TAU).reshape(2, bsz, s, GLA_HEADS, GLA_DK).transpose(0, 1, 3, 2, 4)
    o_f = _gla_chunked(q, k, v, la[0])
    o_b = _flip_seq(_gla_chunked(_flip_seq(q), _flip_seq(k), _flip_seq(v), _flip_seq(la[1])))
    return _headwise_rms_norm((o_f + o_b).transpose(0, 2, 1, 3), gla_norm) * jax.nn.silu(r)


def _split_small(w, n_main):
    tail = w[:, n_main:]
    tail = jnp.pad(tail, ((0, 0), (0, LANE - tail.shape[1])))
    return w[:, :n_main].astype(BF16), tail.astype(BF16)


def kernel(x, norm_ffn1, ffn1_w_gu, ffn1_w_down, norm_mix, norm_ffn2, ffn2_w_gu, ffn2_w_down,
           ab_w_in, lru_conv_w, lru_conv_b, lru_gate_w, lru_gate_b, lru_lambda, mlstm_gate_b,
           mlstm_norm, ab_w_out, cd_w_in, gla_w_gate2, gla_gate_b, gla_norm, s5_a_re, s5_a_im,
           s5_log_dt, s5_b_re, s5_b_im, s5_c_re, s5_c_im, s5_d, s5_w_glu, cd_w_out, final_norm):
    bsz, seq, d = x.shape
    n = bsz * seq
    depth = norm_ffn1.shape[0]
    xf = x.reshape(n, d)
    for l in range(depth):
        xf = _ffn(xf, norm_ffn1[l], ffn1_w_gu[l].astype(BF16), ffn1_w_down[l].astype(BF16))
        j = l // 2
        if l % 2 == 0:
            w_main, w_tail = _split_small(ab_w_in[j], 2 * W_LRU + 4 * W_MLSTM)
            proj, tail = _norm_proj(xf, norm_mix[l], w_main, w_tail)
            hl = _lru(proj, bsz, lru_conv_w[j], lru_conv_b[j], lru_gate_w[j], lru_gate_b[j], lru_lambda[j])
            hm = _mlstm(proj, tail, bsz, mlstm_gate_b[j])
            xf = _ab_out(xf, proj, hl, hm, mlstm_norm[j], ab_w_out[j].astype(BF16))
        else:
            w = cd_w_in[j]
            n_gla = 2 * GLA_QK + 2 * GLA_V
            w = jnp.concatenate([w[:, :n_gla], w[:, n_gla + 2 * GLA_RANK:], w[:, n_gla:n_gla + 2 * GLA_RANK]], axis=1)
            w_main, w_tail = _split_small(w, n_gla + S5_W)
            proj, tail = _norm_proj(xf, norm_mix[l], w_main, w_tail)
            og = _gla(proj, tail, bsz, gla_w_gate2[j], gla_gate_b[j])
            ops = _s5_operators(s5_a_re[j], s5_a_im[j], s5_log_dt[j], s5_b_re[j], s5_b_im[j],
                                s5_c_re[j], s5_c_im[j])
            ys = _s5(proj[:, n_gla:], bsz, ops)
            xf = _cd_out(xf, proj, og, gla_norm[j], ys, s5_d[j],
                         s5_w_glu[j].astype(BF16), cd_w_out[j].astype(BF16))
        xf = _ffn(xf, norm_ffn2[l], ffn2_w_gu[l].astype(BF16), ffn2_w_down[l].astype(BF16),
                  final_g=final_norm if l == depth - 1 else None)
    return xf.reshape(bsz, seq, d)
```

```python
import functools

import numpy as np
import jax
import jax.numpy as jnp
from jax import lax
from jax.experimental import pallas as pl
from jax.experimental.pallas import tpu as pltpu

D_MODEL = 2048
D_FF = 5632
NORM_EPS = 1e-6

W_LRU = 1024
LRU_HEADS = 8
LRU_BLOCK = W_LRU // LRU_HEADS
CONV_W = 4
CONV_PAD_L = 2
LRU_C = 8.0

W_MLSTM = 1024
MLSTM_HEADS = 4
MLSTM_DH = W_MLSTM // MLSTM_HEADS
MLSTM_CHUNK = 64

GLA_HEADS = 4
GLA_DK = 128
GLA_DV = 256
GLA_QK = GLA_HEADS * GLA_DK
GLA_V = GLA_HEADS * GLA_DV
GLA_RANK = 16
GLA_TAU = 16.0
GLA_CHUNK = 64

S5_W = 1024
S5_GROUP = 16
S5_GROUPS = S5_W // S5_GROUP
S5_P = 64
S5_CHUNK = 16
S5_GB = 8
S5_ROW_TILE = 256

LANE = 128
SUBLANE = 8
VMEM_LIMIT = 52 * 1024 * 1024

BF16 = jnp.bfloat16
F32 = jnp.float32
HI = lax.Precision.HIGHEST


def _rms(x, g):
    ms = jnp.mean(x * x, axis=-1, keepdims=True)
    return x * lax.rsqrt(ms + NORM_EPS) * g


def _softplus(z):
    return jnp.maximum(z, 0.0) + jnp.log1p(jnp.exp(-jnp.abs(z)))


def _params(*sem):
    return pltpu.CompilerParams(dimension_semantics=sem, vmem_limit_bytes=VMEM_LIMIT)


def _ffn_kernel(x_ref, g_ref, wg_ref, wu_ref, wd_ref, *rest, final):
    if final:
        fg_ref, o_ref, h_ref = rest
    else:
        o_ref, h_ref = rest
    j = pl.program_id(1)

    @pl.when(j == 0)
    def _():
        x = x_ref[...]
        h_ref[...] = _rms(x, g_ref[...]).astype(BF16)
        o_ref[...] = x

    h = h_ref[...]
    g = jnp.dot(h, wg_ref[...], preferred_element_type=F32)
    u = jnp.dot(h, wu_ref[...], preferred_element_type=F32)
    a = (0.5 * g * jax.nn.sigmoid(g) * u).astype(BF16)
    o_ref[...] += jnp.dot(a, wd_ref[...], preferred_element_type=F32)

    if final:
        @pl.when(j == pl.num_programs(1) - 1)
        def _():
            o_ref[...] = _rms(o_ref[...], fg_ref[...])


def _ffn(x, g, w_gu, w_down, final_g=None, tm=512, tf=512):
    n, d = x.shape
    f = w_down.shape[0]
    tm = min(tm, n)
    nj = f // tf
    final = final_g is not None
    in_specs = [
        pl.BlockSpec((tm, d), lambda i, j: (i, 0)),
        pl.BlockSpec((1, d), lambda i, j: (0, 0)),
        pl.BlockSpec((d, tf), lambda i, j: (0, j)),
        pl.BlockSpec((d, tf), lambda i, j: (0, j + nj)),
        pl.BlockSpec((tf, d), lambda i, j: (j, 0)),
    ]
    args = [x, g.reshape(1, d), w_gu, w_gu, w_down]
    if final:
        in_specs.append(pl.BlockSpec((1, d), lambda i, j: (0, 0)))
        args.append(final_g.reshape(1, d))
    return pl.pallas_call(
        functools.partial(_ffn_kernel, final=final),
        grid=(n // tm, nj),
        in_specs=in_specs,
        out_specs=pl.BlockSpec((tm, d), lambda i, j: (i, 0)),
        out_shape=jax.ShapeDtypeStruct((n, d), F32),
        scratch_shapes=[pltpu.VMEM((tm, d), BF16)],
        compiler_params=_params("arbitrary", "arbitrary"),
        name="ffn_final" if final else "ffn",
    )(*args)


def _norm_proj_kernel(x_ref, g_ref, w_ref, ws_ref, o_ref, os_ref, h_ref):
    @pl.when(pl.program_id(1) == 0)
    def _():
        h = _rms(x_ref[...], g_ref[...]).astype(BF16)
        h_ref[...] = h
        os_ref[...] = jnp.dot(h, ws_ref[...], preferred_element_type=F32)

    o_ref[...] = jnp.dot(h_ref[...], w_ref[...], preferred_element_type=F32)


def _norm_proj(x, g, w, w_small, tm=1024, tn=1024):
    n, d = x.shape
    m = w.shape[1]
    ms = w_small.shape[1]
    tm = min(tm, n)
    return pl.pallas_call(
        _norm_proj_kernel,
        grid=(n // tm, m // tn),
        in_specs=[
            pl.BlockSpec((tm, d), lambda i, j: (i, 0)),
            pl.BlockSpec((1, d), lambda i, j: (0, 0)),
            pl.BlockSpec((d, tn), lambda i, j: (0, j)),
            pl.BlockSpec((d, ms), lambda i, j: (0, 0)),
        ],
        out_specs=[pl.BlockSpec((tm, tn), lambda i, j: (i, j)),
                   pl.BlockSpec((tm, ms), lambda i, j: (i, 0))],
        out_shape=[jax.ShapeDtypeStruct((n, m), F32), jax.ShapeDtypeStruct((n, ms), F32)],
        scratch_shapes=[pltpu.VMEM((tm, d), BF16)],
        compiler_params=_params("arbitrary", "arbitrary"),
        name="norm_proj",
    )(x, g.reshape(1, d), w, w_small)


def _lru_kernel(xp_ref, x_ref, xn_ref, cw_ref, cb_ref, gw_ref, gb_ref, lam_ref, o_ref,
                xs_ref, a_ref, b_ref, h_ref, *, tt):
    d = pl.program_id(0)
    t = pl.program_id(2)
    nt = pl.num_programs(2)
    tb = jnp.where(d == 0, t, nt - 1 - t)
    w = x_ref.shape[1]

    xs_ref[0:SUBLANE, :] = jnp.where(tb > 0, xp_ref[...], 0.0)
    xs_ref[SUBLANE:SUBLANE + tt, :] = x_ref[...]
    xs_ref[SUBLANE + tt:2 * SUBLANE + tt, :] = jnp.where(tb < nt - 1, xn_ref[...], 0.0)
    cw = cw_ref[...]
    xc = cb_ref[...]
    for k in range(CONV_W):
        off = SUBLANE + k - CONV_PAD_L
        xc = xc + cw[k:k + 1, :] * xs_ref[off:off + tt, :]

    xcb = xc.astype(BF16)
    gb = gb_ref[0]
    sp = _softplus(-lam_ref[0])
    for hd in range(w // LRU_BLOCK):
        sl = slice(hd * LRU_BLOCK, (hd + 1) * LRU_BLOCK)
        z = jnp.dot(xcb[:, sl], gw_ref[0, hd], preferred_element_type=F32)
        r = jax.nn.sigmoid(z[:, :LRU_BLOCK] + gb[0:1, sl])
        i = jax.nn.sigmoid(z[:, LRU_BLOCK:] + gb[1:2, sl])
        log_a = (-LRU_C * sp[:, sl]) * r
        a = jnp.exp(log_a)
        a_ref[:, sl] = a
        b_ref[:, sl] = jnp.sqrt(-jnp.tanh(log_a) * (a * a + 1.0)) * (i * xc[:, sl])

    @pl.when(t == 0)
    def _():
        h_ref[...] = jnp.zeros_like(h_ref)

    sub = lax.broadcasted_iota(jnp.int32, (SUBLANE, w), 0)
    ntile = tt // SUBLANE

    def scan(rev):
        def earlier(x, k, fill):
            if rev:
                return jnp.where(sub < SUBLANE - k, pltpu.roll(x, SUBLANE - k, 0), fill)
            return jnp.where(sub >= k, pltpu.roll(x, k, 0), fill)

        def body(jt, h):
            r0 = pl.multiple_of((ntile - 1 - jt if rev else jt) * SUBLANE, SUBLANE)
            a = a_ref[pl.ds(r0, SUBLANE), :]
            b = b_ref[pl.ds(r0, SUBLANE), :]
            k = 1
            while k < SUBLANE:
                a, b = a * earlier(a, k, 1.0), b + a * earlier(b, k, 0.0)
                k *= 2
            tile = a * h + b
            o_ref[0, pl.ds(r0, SUBLANE), :] = tile
            last = 0 if rev else SUBLANE - 1
            return tile[last:last + 1, :]

        h_ref[...] = lax.fori_loop(0, ntile, body, h_ref[...])

    pl.when(d == 0)(functools.partial(scan, False))
    pl.when(d == 1)(functools.partial(scan, True))


def _lru(proj, bsz, conv_w, conv_b, gate_w, gate_b, lam, tt=256):
    n = proj.shape[0]
    seq = n // bsz
    tt = min(tt, seq)
    nt = seq // tt
    w = W_LRU
    r8 = tt // SUBLANE
    gw = jnp.concatenate([gate_w[:, 0], gate_w[:, 1]], axis=-1).astype(BF16)

    def tb(d, t):
        return t + d * (nt - 1 - 2 * t)

    return pl.pallas_call(
        functools.partial(_lru_kernel, tt=tt),
        grid=(2, bsz, nt),
        in_specs=[
            pl.BlockSpec((SUBLANE, w), lambda d, b, t: (jnp.maximum((b * nt + tb(d, t)) * r8 - 1, 0), 0)),
            pl.BlockSpec((tt, w), lambda d, b, t: (b * nt + tb(d, t), 0)),
            pl.BlockSpec((SUBLANE, w), lambda d, b, t: (jnp.minimum((b * nt + tb(d, t) + 1) * r8, n // SUBLANE - 1), 0)),
            pl.BlockSpec((CONV_W, w), lambda d, b, t: (0, 0)),
            pl.BlockSpec((1, w), lambda d, b, t: (0, 0)),
            pl.BlockSpec((1, LRU_HEADS, LRU_BLOCK, 2 * LRU_BLOCK), lambda d, b, t: (d, 0, 0, 0)),
            pl.BlockSpec((1, 2, w), lambda d, b, t: (d, 0, 0)),
            pl.BlockSpec((1, 1, w), lambda d, b, t: (d, 0, 0)),
        ],
        out_specs=pl.BlockSpec((1, tt, w), lambda d, b, t: (d, b * nt + tb(d, t), 0)),
        out_shape=jax.ShapeDtypeStruct((2, n, w), F32),
        scratch_shapes=[pltpu.VMEM((tt + 2 * SUBLANE, w), F32), pltpu.VMEM((tt, w), F32),
                        pltpu.VMEM((tt, w), F32), pltpu.VMEM((1, w), F32)],
        compiler_params=_params("arbitrary", "arbitrary", "arbitrary"),
        name="rg_lru",
    )(proj, proj, proj, conv_w, conv_b.reshape(1, w), gw, gate_b, lam.reshape(2, 1, w))


def _s5_discretise(a_re, a_im, log_dt, b_re, b_im):
    dt = jnp.exp(log_dt)[:, None]
    mag = jnp.exp(dt * a_re)
    lr = mag * jnp.cos(dt * a_im)
    li = mag * jnp.sin(dt * a_im)
    den = a_re * a_re + a_im * a_im
    nr = lr - 1.0
    cr = (nr * a_re + li * a_im) / den
    ci = (li * a_re - nr * a_im) / den
    bbr = cr[..., None] * b_re - ci[..., None] * b_im
    bbi = cr[..., None] * b_im + ci[..., None] * b_re
    return lr, li, bbr, bbi


def _s5_operators(a_re, a_im, log_dt, b_re, b_im, c_re, c_im):
    L = S5_CHUNK
    G, P, C = S5_GROUPS, S5_P, S5_GROUP
    kfs, mins, mouts, lams = [], [], [], []
    for d in range(2):
        lr, li, bbr, bbi = _s5_discretise(a_re[d], a_im[d], log_dt[d], b_re, b_im)
        bbr, bbi = bbr.transpose(0, 2, 1), bbi.transpose(0, 2, 1)
        pr, pi = [jnp.ones_like(lr)], [jnp.zeros_like(li)]
        for _ in range(L):
            pr, pi = pr + [pr[-1] * lr - pi[-1] * li], pi + [pr[-1] * li + pi[-1] * lr]
        pr, pi = jnp.stack(pr)[:, :, None, :], jnp.stack(pi)[:, :, None, :]
        cl = jnp.concatenate([c_re[None] * pr - c_im[None] * pi, -(c_re[None] * pi + c_im[None] * pr)], axis=-1)
        bl = jnp.concatenate([pr * bbr[None] - pi * bbi[None], pr * bbi[None] + pi * bbr[None]], axis=-1)
        kfs.append(jnp.einsum('jgcx,gkx->jgkc', cl[:L], bl[0], precision=HI))
        if d == 0:
            e_in = jnp.arange(L - 1, -1, -1)
            e_out = jnp.arange(1, L + 1)
        else:
            e_in = jnp.arange(L)
            e_out = jnp.arange(L, 0, -1)
        mins.append(bl[e_in])
        mouts.append(cl[e_out])
        lams.append(jnp.stack([pr[L, :, 0], pi[L, :, 0]], axis=1))
    kf, kb = kfs
    kfull = jnp.concatenate([kb[:0:-1], (kf[0] + kb[0])[None], kf[1:]], axis=0)
    gb = S5_GB
    nb_ = G // gb
    k2 = kfull.reshape(2 * L - 1, nb_, gb * C, C).transpose(1, 0, 2, 3)

    def rows_of_block(parts):
        a = jnp.concatenate(parts, axis=-1).reshape(L, nb_, gb * C, 4 * P)
        return a.transpose(1, 0, 2, 3).reshape(nb_, L * gb * C, 4 * P)

    lam_l = jnp.concatenate(lams, axis=1)
    lam_blk = lam_l.reshape(nb_, gb, 4, P).transpose(0, 2, 1, 3).reshape(nb_, 1, 4 * gb * P)
    return k2, rows_of_block(mins), rows_of_block(mouts), lam_blk


def _s5_state_kernel(u_ref, min_ref, lam_ref, hin_ref, h_ref, *, nb):
    nl, rows, _ = u_ref.shape
    rb = rows // nb
    hw = h_ref.shape[1] // 4
    rt = min(rows, S5_ROW_TILE)
    lo = lax.broadcasted_iota(jnp.int32, (rt, LANE), 1) < S5_P
    ng = hw // S5_P

    def pairs():
        for d in range(2):
            for k in range(ng // 2):
                yield ((d * ng + 2 * k) * LANE, (d * ng + 2 * k + 1) * LANE,
                       2 * d * hw + k * LANE, (2 * d + 1) * hw + k * LANE)

    for r0 in range(0, rows, rt):
        u = jnp.concatenate([u_ref[s, r0:r0 + rt, :] for s in range(nl)], axis=1)
        hl = jnp.dot(u, min_ref[0], preferred_element_type=F32)
        for ca, cb, cre, cim in pairs():
            a, b = hl[:, ca:ca + LANE], hl[:, cb:cb + LANE]
            h_ref[r0:r0 + rt, cre:cre + LANE] = jnp.where(lo, a, pltpu.roll(b, S5_P, 1))
            h_ref[r0:r0 + rt, cim:cim + LANE] = jnp.where(lo, pltpu.roll(a, S5_P, 1), b)
    lam = lam_ref[0]
    lrf, lif = lam[:, 0:hw], lam[:, hw:2 * hw]
    lrb, lib = lam[:, 2 * hw:3 * hw], lam[:, 3 * hw:4 * hw]
    sub = lax.broadcasted_iota(jnp.int32, (SUBLANE, hw), 0)

    def body(it, carry):
        new = []
        for b in range(nb):
            fr, fi, br, bi = carry[4 * b:4 * b + 4]
            f0 = pl.multiple_of(b * rb + it * SUBLANE, SUBLANE)
            b0 = pl.multiple_of(b * rb + rb - SUBLANE - it * SUBLANE, SUBLANE)
            tiles = [jnp.zeros((SUBLANE, hw), F32)] * 4
            lf = h_ref[pl.ds(f0, SUBLANE), 0:2 * hw]
            lb = h_ref[pl.ds(b0, SUBLANE), 2 * hw:4 * hw]
            for r in range(SUBLANE):
                q = SUBLANE - 1 - r
                tiles = [jnp.where(sub == r, fr, tiles[0]), jnp.where(sub == r, fi, tiles[1]),
                         jnp.where(sub == q, br, tiles[2]), jnp.where(sub == q, bi, tiles[3])]
                fr, fi, br, bi = (lrf * fr - lif * fi + lf[r:r + 1, 0:hw],
                                  lrf * fi + lif * fr + lf[r:r + 1, hw:2 * hw],
                                  lrb * br - lib * bi + lb[q:q + 1, 0:hw],
                                  lrb * bi + lib * br + lb[q:q + 1, hw:2 * hw])
            h_ref[pl.ds(f0, SUBLANE), 0:hw] = tiles[0]
            h_ref[pl.ds(f0, SUBLANE), hw:2 * hw] = tiles[1]
            h_ref[pl.ds(b0, SUBLANE), 2 * hw:3 * hw] = tiles[2]
            h_ref[pl.ds(b0, SUBLANE), 3 * hw:4 * hw] = tiles[3]
            new += [fr, fi, br, bi]
        return tuple(new)

    zero = jnp.zeros((1, hw), F32)
    lax.fori_loop(0, rb // SUBLANE, body, (zero,) * (4 * nb))
    for r0 in range(0, rows, rt):
        for ca, cb, cre, cim in pairs():
            re, im = h_ref[r0:r0 + rt, cre:cre + LANE], h_ref[r0:r0 + rt, cim:cim + LANE]
            hin_ref[0, r0:r0 + rt, ca:ca + LANE] = jnp.where(lo, re, pltpu.roll(im, S5_P, 1)).astype(BF16)
            hin_ref[0, r0:r0 + rt, cb:cb + LANE] = jnp.where(lo, pltpu.roll(re, S5_P, 1), im).astype(BF16)


def _s5_expand_kernel(a_ref, o_ref, *, transpose):
    a = a_ref[0]
    rows = a.shape[0]
    row_g = (lax.broadcasted_iota(jnp.int32, (rows, LANE), 0) // S5_GROUP) % S5_GB
    for d in range(2):
        ad = a[:, d * LANE:(d + 1) * LANE]
        for g in range(S5_GB):
            tile = jnp.where(row_g == g, ad, 0.0)
            col = (d * S5_GB + g) * LANE
            if transpose:
                o_ref[0, col:col + LANE, :] = tile.T.astype(BF16)
            else:
                o_ref[0, :, col:col + LANE] = tile.astype(BF16)


def _s5_expand(a, transpose):
    nblk, rows, _ = a.shape
    sw = 2 * S5_GB * LANE
    shape = (nblk, sw, rows) if transpose else (nblk, rows, sw)
    return pl.pallas_call(
        functools.partial(_s5_expand_kernel, transpose=transpose),
        grid=(nblk,),
        in_specs=[pl.BlockSpec((1,) + a.shape[1:], lambda b: (b, 0, 0))],
        out_specs=pl.BlockSpec((1,) + shape[1:], lambda b: (b, 0, 0)),
        out_shape=jax.ShapeDtypeStruct(shape, BF16),
        compiler_params=_params("arbitrary"),
        name="s5_expand",
    )(a)


def _s5_out_kernel(u_ref, hin_ref, k2_ref, mout_ref, y_ref, t_ref):
    nl = u_ref.shape[0]

    @pl.when(pl.program_id(1) == 0)
    def _():
        row_g = lax.broadcasted_iota(jnp.int32, (LANE, LANE), 0) // S5_GROUP
        col_g = lax.broadcasted_iota(jnp.int32, (LANE, LANE), 1) // S5_GROUP
        spread = (lax.broadcasted_iota(jnp.int32, (S5_GROUP, LANE), 1) % S5_GROUP
                  == lax.broadcasted_iota(jnp.int32, (S5_GROUP, LANE), 0)).astype(BF16)
        for j in range(2 * nl - 1):
            rep = jnp.dot(k2_ref[0, j].astype(BF16), spread, preferred_element_type=F32)
            tile = jnp.where(row_g == col_g, rep, 0.0).astype(BF16)
            for s in range(nl):
                t = s + j - (nl - 1)
                if 0 <= t < nl:
                    t_ref[s * LANE:(s + 1) * LANE, t * LANE:(t + 1) * LANE] = tile

    u = jnp.concatenate([u_ref[s] for s in range(nl)], axis=1)
    y = (jnp.dot(u, t_ref[...], preferred_element_type=F32)
         + jnp.dot(hin_ref[0], mout_ref[0], preferred_element_type=F32))
    for t in range(nl):
        y_ref[t] = y[:, t * LANE:(t + 1) * LANE]


def _s5(u, bsz, ops, tr=256):
    k2, a_min, a_mout, lam_blk = ops
    min_blk = _s5_expand(a_min, False)
    mout_blk = _s5_expand(a_mout, True)
    n, w = u.shape
    L = S5_CHUNK
    rows = n // L
    nblk = w // LANE
    kw = L * LANE
    sw = lam_blk.shape[2]
    tr = min(tr, rows)
    u3 = u.reshape(rows, L, w).transpose(1, 0, 2).astype(BF16)
    hin = pl.pallas_call(
        functools.partial(_s5_state_kernel, nb=bsz),
        grid=(nblk,),
        in_specs=[
            pl.BlockSpec((L, rows, LANE), lambda b: (0, 0, b)),
            pl.BlockSpec((1, kw, sw), lambda b: (b, 0, 0)),
            pl.BlockSpec((1, 1, sw), lambda b: (b, 0, 0)),
        ],
        out_specs=pl.BlockSpec((1, rows, sw), lambda b: (b, 0, 0)),
        out_shape=jax.ShapeDtypeStruct((nblk, rows, sw), BF16),
        scratch_shapes=[pltpu.VMEM((rows, sw), F32)],
        compiler_params=_params("arbitrary"),
        name="s5_state",
    )(u3, min_blk, lam_blk)
    y3 = pl.pallas_call(
        _s5_out_kernel,
        grid=(nblk, rows // tr),
        in_specs=[
            pl.BlockSpec((L, tr, LANE), lambda b, i: (0, i, b)),
            pl.BlockSpec((1, tr, sw), lambda b, i: (b, i, 0)),
            pl.BlockSpec((1,) + k2.shape[1:], lambda b, i: (b, 0, 0, 0)),
            pl.BlockSpec((1, sw, kw), lambda b, i: (b, 0, 0)),
        ],
        out_specs=pl.BlockSpec((L, tr, LANE), lambda b, i: (0, i, b)),
        out_shape=jax.ShapeDtypeStruct((L, rows, w), F32),
        scratch_shapes=[pltpu.VMEM((kw, kw), BF16)],
        compiler_params=_params("arbitrary", "arbitrary"),
        name="s5_out",
    )(u3, hin, k2, mout_blk)
    return y3.transpose(1, 0, 2).reshape(n, w)


def _log_sigmoid(z):
    return -_softplus(-z)


def _tri(length, d):
    ti = lax.broadcasted_iota(jnp.int32, (length, length), 0)
    si = lax.broadcasted_iota(jnp.int32, (length, length), 1)
    return (ti - si) * (1 - 2 * d) >= 0


def _mlstm_kernel(q_ref, k_ref, v_ref, g_ref, gb_ref, o_ref, c_ref, n_ref, m_ref):
    d = pl.program_id(0)
    length = q_ref.shape[0]
    nh, dh = MLSTM_HEADS, MLSTM_DH

    @pl.when(pl.program_id(2) == 0)
    def _():
        c_ref[...] = jnp.zeros_like(c_ref)
        n_ref[...] = jnp.zeros_like(n_ref)
        m_ref[...] = jnp.zeros_like(m_ref)

    causal = _tri(length, d)
    gp = g_ref[...] + gb_ref[...]
    gp = jnp.where(d == 0, gp, pltpu.roll(gp, LANE - 2 * nh, 1))
    lf = _log_sigmoid(gp)
    cum = jnp.dot(causal.astype(F32), lf, precision=HI, preferred_element_type=F32)
    tot = jnp.sum(lf, axis=0, keepdims=True)
    gp_t = gp.T
    cum_t = cum.T
    nt = (((1,), (1,)), ((), ()))
    tn = (((0,), (0,)), ((), ()))

    for h in range(nh):
        sl = slice(h * dh, (h + 1) * dh)
        qh = (q_ref[:, sl] * (dh ** -0.5)).astype(BF16)
        kf = k_ref[:, sl]
        kh = kf.astype(BF16)
        vh = v_ref[:, sl].astype(BF16)
        ig_c, cum_c = gp[:, h:h + 1], cum[:, nh + h:nh + h + 1]
        ig_r, cum_r = gp_t[h:h + 1, :], cum_t[nh + h:nh + h + 1, :]
        tot_h = tot[:, nh + h:nh + h + 1]
        m_st = m_ref[h:h + 1, 0:1]
        c_st = c_ref[h]
        dmat = jnp.where(causal, cum_c - cum_r + ig_r, -jnp.inf)
        m_inter = cum_c + m_st
        m_t = jnp.maximum(jnp.max(dmat, axis=1, keepdims=True), m_inter)
        w_inter = jnp.exp(m_inter - m_t)
        s = lax.dot_general(qh, kh, nt, preferred_element_type=F32) * jnp.exp(dmat - m_t)
        num = (jnp.dot(s.astype(BF16), vh, preferred_element_type=F32)
               + w_inter * jnp.dot(qh, c_st.astype(BF16), preferred_element_type=F32))
        den = (jnp.sum(s, axis=1, keepdims=True)
               + w_inter * jnp.sum(qh.astype(F32) * n_ref[h:h + 1, :], axis=1, keepdims=True))
        o_ref[0, :, sl] = num / jnp.maximum(jnp.abs(den), jnp.exp(-m_t))
        dec = tot_h - cum_c + ig_c
        m_new = jnp.maximum(tot_h + m_st, jnp.max(dec, axis=0, keepdims=True))
        wc = jnp.exp(tot_h + m_st - m_new)
        kw = jnp.exp(dec - m_new) * kf
        c_ref[h] = wc * c_st + lax.dot_general(kw.astype(BF16), vh, tn, preferred_element_type=F32)
        n_ref[h:h + 1, :] = wc * n_ref[h:h + 1, :] + jnp.sum(kw, axis=0, keepdims=True)
        m_ref[h:h + 1, :] = jnp.broadcast_to(m_new, (1, LANE))


def _mlstm(proj, tail, bsz, gate_b, length=256):
    n = proj.shape[0]
    seq = n // bsz
    length = min(length, seq)
    nc = seq // length
    w = W_MLSTM
    q_blk = 2 * W_LRU // w
    gb = jnp.pad(gate_b.reshape(1, -1), ((0, 0), (0, LANE - gate_b.size)))

    def row(d, b, c):
        return b * nc + c + d * (nc - 1 - 2 * c)

    return pl.pallas_call(
        _mlstm_kernel,
        grid=(2, bsz, nc),
        in_specs=[
            pl.BlockSpec((length, w), lambda d, b, c: (row(d, b, c), q_blk)),
            pl.BlockSpec((length, w), lambda d, b, c: (row(d, b, c), q_blk + 1)),
            pl.BlockSpec((length, w), lambda d, b, c: (row(d, b, c), q_blk + 2)),
            pl.BlockSpec((length, LANE), lambda d, b, c: (row(d, b, c), 0)),
            pl.BlockSpec((1, LANE), lambda d, b, c: (0, 0)),
        ],
        out_specs=pl.BlockSpec((1, length, w), lambda d, b, c: (d, row(d, b, c), 0)),
        out_shape=jax.ShapeDtypeStruct((2, n, w), F32),
        scratch_shapes=[pltpu.VMEM((MLSTM_HEADS, MLSTM_DH, MLSTM_DH), F32),
                        pltpu.VMEM((SUBLANE, MLSTM_DH), F32), pltpu.VMEM((SUBLANE, LANE), F32)],
        compiler_params=_params("arbitrary", "arbitrary", "arbitrary"),
        name="mlstm",
    )(proj, proj, proj, tail, gb)


GLA_SUB = 16


def _gla_chunk(q_ref, k_ref, v_ref, cum, o_ref, s_ref, rev):
    length = q_ref.shape[0]
    c = GLA_SUB
    nt = (((1,), (1,)), ((), ()))
    tn = (((0,), (0,)), ((), ()))

    def rows(a, b):
        return slice(length - b, length - a) if rev else slice(a, b)

    def row(a):
        i = length - 1 - a if rev else a
        return slice(i, i + 1)

    ti = lax.broadcasted_iota(jnp.int32, (c, 1), 0)
    for h in range(GLA_HEADS):
        kl = slice(h * GLA_DK, (h + 1) * GLA_DK)
        vl = slice(h * GLA_DV, (h + 1) * GLA_DV)
        ch = cum[:, kl]
        qs = q_ref[:, kl] * (GLA_DK ** -0.5)
        kk = k_ref[:, kl]
        vf = v_ref[:, vl]
        vv = vf.astype(BF16)
        st = s_ref[h]
        tot = ch[row(length - 1)]
        o = lax.dot_general((qs * jnp.exp(ch)).astype(BF16), st.astype(BF16), nt, preferred_element_type=F32)
        ob = [o[rows(p * c, (p + 1) * c)] for p in range(length // c)]
        m = length // 2
        while m >= c:
            for start in range(0, length, 2 * m):
                fst, sec = rows(start, start + m), rows(start + m, start + 2 * m)
                r = ch[row(start + m)]
                qh = (qs[sec] * jnp.exp(ch[sec] - r)).astype(BF16)
                kh = (kk[fst] * jnp.exp(r - ch[fst])).astype(BF16)
                att = lax.dot_general(qh, kh, nt, preferred_element_type=F32)
                contrib = jnp.dot(att.astype(BF16), vv[fst], preferred_element_type=F32)
                for p in range((start + m) // c, (start + 2 * m) // c):
                    lo = rows(p * c, (p + 1) * c).start - sec.start
                    ob[p] = ob[p] + contrib[lo:lo + c]
            m //= 2
        for p in range(length // c):
            blk = rows(p * c, (p + 1) * c)
            cb, qb, kb, vb = ch[blk], qs[blk], kk[blk], vf[blk]
            acc = ob[p]
            for s in range(c):
                sees = (ti <= s) if rev else (ti >= s)
                e = jnp.exp(jnp.where(sees, cb - cb[s:s + 1], -jnp.inf))
                a = jnp.sum(qb * kb[s:s + 1] * e, axis=1, keepdims=True)
                acc = acc + a * vb[s:s + 1]
            o_ref[0, blk, vl] = acc
        kd = (kk * jnp.exp(tot - ch)).astype(BF16)
        s_ref[h] = jnp.exp(tot) * st + lax.dot_general(vv, kd, tn, preferred_element_type=F32)


def _gla_kernel(q_ref, k_ref, v_ref, low_ref, wg_ref, bg_ref, o_ref, s_ref):
    d = pl.program_id(0)
    length = q_ref.shape[0]

    @pl.when(pl.program_id(2) == 0)
    def _():
        s_ref[...] = jnp.zeros_like(s_ref)

    gate_pre = jnp.dot(low_ref[...], wg_ref[0], precision=HI, preferred_element_type=F32) + bg_ref[0]
    la = _log_sigmoid(gate_pre) * (1.0 / GLA_TAU)
    cum = jnp.dot(_tri(length, d).astype(F32), la, precision=HI, preferred_element_type=F32)
    pl.when(d == 0)(lambda: _gla_chunk(q_ref, k_ref, v_ref, cum, o_ref, s_ref, False))
    pl.when(d == 1)(lambda: _gla_chunk(q_ref, k_ref, v_ref, cum, o_ref, s_ref, True))


def _gla(proj, tail, bsz, w_gate2, gate_b, length=128):
    n = proj.shape[0]
    seq = n // bsz
    length = min(length, seq)
    nc = seq // length
    wg = jnp.zeros((2, LANE, GLA_QK), F32)
    for d in range(2):
        wg = wg.at[d, d * GLA_RANK:(d + 1) * GLA_RANK].set(w_gate2[d])

    def row(d, b, c):
        return b * nc + c + d * (nc - 1 - 2 * c)

    return pl.pallas_call(
        _gla_kernel,
        grid=(2, bsz, nc),
        in_specs=[
            pl.BlockSpec((length, GLA_QK), lambda d, b, c: (row(d, b, c), 0)),
            pl.BlockSpec((length, GLA_QK), lambda d, b, c: (row(d, b, c), 1)),
            pl.BlockSpec((length, GLA_V), lambda d, b, c: (row(d, b, c), 2 * GLA_QK // GLA_V)),
            pl.BlockSpec((length, LANE), lambda d, b, c: (row(d, b, c), 0)),
            pl.BlockSpec((1, LANE, GLA_QK), lambda d, b, c: (d, 0, 0)),
            pl.BlockSpec((1, 1, GLA_QK), lambda d, b, c: (d, 0, 0)),
        ],
        out_specs=pl.BlockSpec((1, length, GLA_V), lambda d, b, c: (d, row(d, b, c), 0)),
        out_shape=jax.ShapeDtypeStruct((2, n, GLA_V), F32),
        scratch_shapes=[pltpu.VMEM((GLA_HEADS, GLA_DV, GLA_DK), F32)],
        compiler_params=_params("arbitrary", "arbitrary", "arbitrary"),
        name="gla",
    )(proj, proj, proj, tail, wg, gate_b.reshape(2, 1, GLA_QK))


def _head_norm(t, g, heads):
    dh = t.shape[1] // heads
    outs = []
    for h in range(heads):
        th = t[:, h * dh:(h + 1) * dh]
        outs.append(th * lax.rsqrt(jnp.mean(th * th, axis=1, keepdims=True) + NORM_EPS))
    return jnp.concatenate(outs, axis=1) * g


def _ab_out_kernel(x_ref, gr_ref, og_ref, hl_ref, hm_ref, g_ref, w_ref, o_ref):
    k = gr_ref.shape[1]
    ya = jax.nn.gelu(gr_ref[...]) * (hl_ref[0] + hl_ref[1])
    yb = jax.nn.sigmoid(og_ref[...]) * _head_norm(hm_ref[0] + hm_ref[1], g_ref[...], MLSTM_HEADS)
    o_ref[...] = (x_ref[...]
                  + jnp.dot(ya.astype(BF16), w_ref[0:k, :], preferred_element_type=F32)
                  + jnp.dot(yb.astype(BF16), w_ref[k:, :], preferred_element_type=F32))


def _ab_out(x, proj, hl, hm, norm_g, w, tm=512):
    n, d = x.shape
    k = W_LRU
    tm = min(tm, n)
    return pl.pallas_call(
        _ab_out_kernel,
        grid=(n // tm,),
        in_specs=[
            pl.BlockSpec((tm, d), lambda i: (i, 0)),
            pl.BlockSpec((tm, k), lambda i: (i, 1)),
            pl.BlockSpec((tm, k), lambda i: (i, 5)),
            pl.BlockSpec((2, tm, k), lambda i: (0, i, 0)),
            pl.BlockSpec((2, tm, k), lambda i: (0, i, 0)),
            pl.BlockSpec((1, k), lambda i: (0, 0)),
            pl.BlockSpec((2 * k, d), lambda i: (0, 0)),
        ],
        out_specs=pl.BlockSpec((tm, d), lambda i: (i, 0)),
        out_shape=jax.ShapeDtypeStruct((n, d), F32),
        compiler_params=_params("arbitrary"),
        name="ab_out",
    )(x, proj, proj, hl, hm, norm_g.reshape(1, k), w)


def _cd_out_kernel(x_ref, r_ref, og_ref, g_ref, ys_ref, u_ref, d_ref, wg_ref, w_ref, o_ref):
    k = r_ref.shape[1]
    r = r_ref[...]
    yc = _head_norm(og_ref[0] + og_ref[1], g_ref[...], GLA_HEADS) * (r * jax.nn.sigmoid(r))
    y = jax.nn.gelu(ys_ref[...] + d_ref[...] * u_ref[...])
    yd = y * jax.nn.sigmoid(jnp.dot(y.astype(BF16), wg_ref[...], preferred_element_type=F32))
    o_ref[...] = (x_ref[...]
                  + jnp.dot(yc.astype(BF16), w_ref[0:k, :], preferred_element_type=F32)
                  + jnp.dot(yd.astype(BF16), w_ref[k:, :], preferred_element_type=F32))


def _cd_out(x, proj, og, norm_g, ys, s5_d, w_glu, w, tm=512):
    n, d = x.shape
    k = S5_W
    tm = min(tm, n)
    return pl.pallas_call(
        _cd_out_kernel,
        grid=(n // tm,),
        in_specs=[
            pl.BlockSpec((tm, d), lambda i: (i, 0)),
            pl.BlockSpec((tm, k), lambda i: (i, 2)),
            pl.BlockSpec((2, tm, k), lambda i: (0, i, 0)),
            pl.BlockSpec((1, k), lambda i: (0, 0)),
            pl.BlockSpec((tm, k), lambda i: (i, 0)),
            pl.BlockSpec((tm, k), lambda i: (i, 3)),
            pl.BlockSpec((1, k), lambda i: (0, 0)),
            pl.BlockSpec((k, k), lambda i: (0, 0)),
            pl.BlockSpec((2 * k, d), lambda i: (0, 0)),
        ],
        out_specs=pl.BlockSpec((tm, d), lambda i: (i, 0)),
        out_shape=jax.ShapeDtypeStruct((n, d), F32),
        compiler_params=_params("arbitrary"),
        name="cd_out",
    )(x, proj, og, norm_g.reshape(1, k), ys, proj, s5_d.reshape(1, k), w_glu, w)


def _headwise_rms_norm(t, g):
    y = t * lax.rsqrt(jnp.mean(t * t, axis=-1, keepdims=True) + NORM_EPS)
    return y.reshape(t.shape[:2] + (-1,)) * g


def _heads(t, n):
    return t.reshape(t.shape[:2] + (n, -1)).transpose(0, 2, 1, 3)


def _flip_seq(t):
    return jnp.flip(t, axis=2)


def _to_chunks(t, chunk):
    b, h, s = t.shape[:3]
    t = t.reshape((b, h, s // chunk, chunk) + t.shape[3:])
    return jnp.moveaxis(t, 2, 0)


def _from_chunks(t):
    nc, b, h, l = t.shape[:4]
    return jnp.moveaxis(t, 0, 2).reshape((b, h, nc * l) + t.shape[4:])


def _mlstm_chunkwise(q, k, v, ig, lf):
    bsz, nh, _, dh = q.shape
    L = MLSTM_CHUNK
    mask = jnp.tril(jnp.ones((L, L), dtype=bool))
    xs = tuple(_to_chunks(t, L) for t in (q, k, v, ig, lf))

    def step(carry, inp):
        c_st, n_st, m_st = carry
        qt, kt, vt, it, ft = inp
        cum = jnp.cumsum(ft, axis=-1)
        dmat = jnp.where(mask, cum[..., :, None] - cum[..., None, :] + it[..., None, :], -jnp.inf)
        m_inter = cum + m_st[..., None]
        m_t = jnp.maximum(jnp.max(dmat, axis=-1), m_inter)
        w_intra = jnp.exp(dmat - m_t[..., None])
        w_inter = jnp.exp(m_inter - m_t)
        s = jnp.einsum('bhtd,bhsd->bhts', qt, kt) * w_intra
        num = jnp.einsum('bhts,bhse->bhte', s, vt) + w_inter[..., None] * jnp.einsum('bhtd,bhde->bhte', qt, c_st)
        den = jnp.sum(s, axis=-1) + w_inter * jnp.einsum('bhtd,bhd->bht', qt, n_st)
        h = num / jnp.maximum(jnp.abs(den), jnp.exp(-m_t))[..., None]
        tot = cum[..., -1]
        dec_s = tot[..., None] - cum + it
        m_new = jnp.maximum(tot + m_st, jnp.max(dec_s, axis=-1))
        ws = jnp.exp(dec_s - m_new[..., None])
        wc = jnp.exp(tot + m_st - m_new)
        c_new = wc[..., None, None] * c_st + jnp.einsum('bhs,bhsd,bhse->bhde', ws, kt, vt)
        n_new = wc[..., None] * n_st + jnp.einsum('bhs,bhsd->bhd', ws, kt)
        return (c_new, n_new, m_new), h

    init = (jnp.zeros((bsz, nh, dh, dh), q.dtype), jnp.zeros((bsz, nh, dh), q.dtype),
            jnp.zeros((bsz, nh), q.dtype))
    _, hs = lax.scan(step, init, xs)
    return _from_chunks(hs)


def _mlstm_core(q, k, v, o, gates, mlstm_gate_b, mlstm_norm):
    bsz, s, _ = q.shape
    q = _heads(q, MLSTM_HEADS) * (MLSTM_DH ** -0.5)
    k = _heads(k, MLSTM_HEADS)
    v = _heads(v, MLSTM_HEADS)
    g = (gates.reshape(bsz, s, 2, 2, MLSTM_HEADS) + mlstm_gate_b).transpose(2, 3, 0, 4, 1)
    h_f = _mlstm_chunkwise(q, k, v, g[0, 0], jax.nn.log_sigmoid(g[0, 1]))
    h_b = _flip_seq(_mlstm_chunkwise(_flip_seq(q), _flip_seq(k), _flip_seq(v), _flip_seq(g[1, 0]),
                                     _flip_seq(jax.nn.log_sigmoid(g[1, 1]))))
    hm = _headwise_rms_norm((h_f + h_b).transpose(0, 2, 1, 3), mlstm_norm)
    return jax.nn.sigmoid(o) * hm


def _gla_chunked(q, k, v, la):
    bsz, nh, _, dk = q.shape
    dv = v.shape[-1]
    L = GLA_CHUNK
    mask = jnp.tril(jnp.ones((L, L), dtype=bool))[..., None]
    xs = tuple(_to_chunks(t, L) for t in (q, k, v, la))

    def step(s_st, inp):
        qt, kt, vt, at = inp
        cum = jnp.cumsum(at, axis=2)
        rel = jnp.where(mask, cum[:, :, :, None, :] - cum[:, :, None, :, :], -jnp.inf)
        attn = jnp.einsum('bhtd,bhsd,bhtsd->bhts', qt, kt, jnp.exp(rel))
        o = jnp.einsum('bhts,bhse->bhte', attn, vt) + jnp.einsum('bhtd,bhde->bhte', qt * jnp.exp(cum), s_st)
        tot = cum[:, :, -1]
        s_new = jnp.exp(tot)[..., None] * s_st + jnp.einsum('bhsd,bhse->bhde', kt * jnp.exp(tot[:, :, None] - cum), vt)
        return s_new, o

    _, os_ = lax.scan(step, jnp.zeros((bsz, nh, dk, dv), q.dtype), xs)
    return _from_chunks(os_)


def _gla_core(q, k, v, r, glr, gla_w_gate2, gla_gate_b, gla_norm):
    bsz, s, _ = q.shape
    q = _heads(q, GLA_HEADS) * (GLA_DK ** -0.5)
    k = _heads(k, GLA_HEADS)
    v = _heads(v, GLA_HEADS)
    low = glr.reshape(bsz, s, 2, GLA_RANK)
    gate_pre = jnp.einsum('bsdr,drk->dbsk', low, gla_w_gate2) + gla_gate_b[:, None, None, :]
    la = (jax.nn.log_sigmoid(gate_pre) / GLA_TAU).reshape(2, bsz, s, GLA_HEADS, GLA_DK).transpose(0, 1, 3, 2, 4)
    o_f = _gla_chunked(q, k, v, la[0])
    o_b = _flip_seq(_gla_chunked(_flip_seq(q), _flip_seq(k), _flip_seq(v), _flip_seq(la[1])))
    return _headwise_rms_norm((o_f + o_b).transpose(0, 2, 1, 3), gla_norm) * jax.nn.silu(r)


def _split_small(w, n_main):
    tail = w[:, n_main:]
    tail = jnp.pad(tail, ((0, 0), (0, LANE - tail.shape[1])))
    return w[:, :n_main].astype(BF16), tail.astype(BF16)


def kernel(x, norm_ffn1, ffn1_w_gu, ffn1_w_down, norm_mix, norm_ffn2, ffn2_w_gu, ffn2_w_down,
           ab_w_in, lru_conv_w, lru_conv_b, lru_gate_w, lru_gate_b, lru_lambda, mlstm_gate_b,
           mlstm_norm, ab_w_out, cd_w_in, gla_w_gate2, gla_gate_b, gla_norm, s5_a_re, s5_a_im,
           s5_log_dt, s5_b_re, s5_b_im, s5_c_re, s5_c_im, s5_d, s5_w_glu, cd_w_out, final_norm):
    bsz, seq, d = x.shape
    n = bsz * seq
    depth = norm_ffn1.shape[0]
    xf = x.reshape(n, d)
    for l in range(depth):
        xf = _ffn(xf, norm_ffn1[l], ffn1_w_gu[l].astype(BF16), ffn1_w_down[l].astype(BF16))
        j = l // 2
        if l % 2 == 0:
            w_main, w_tail = _split_small(ab_w_in[j], 2 * W_LRU + 4 * W_MLSTM)
            proj, tail = _norm_proj(xf, norm_mix[l], w_main, w_tail)
            hl = _lru(proj, bsz, lru_conv_w[j], lru_conv_b[j], lru_gate_w[j], lru_gate_b[j], lru_lambda[j])
            hm = _mlstm(proj, tail, bsz, mlstm_gate_b[j])
            xf = _ab_out(xf, proj, hl, hm, mlstm_norm[j], ab_w_out[j].astype(BF16))
        else:
            w = cd_w_in[j]
            n_gla = 2 * GLA_QK + 2 * GLA_V
            w = jnp.concatenate([w[:, :n_gla], w[:, n_gla + 2 * GLA_RANK:], w[:, n_gla:n_gla + 2 * GLA_RANK]], axis=1)
            w_main, w_tail = _split_small(w, n_gla + S5_W)
            proj, tail = _norm_proj(xf, norm_mix[l], w_main, w_tail)
            og = _gla(proj, tail, bsz, gla_w_gate2[j], gla_gate_b[j])
            ops = _s5_operators(s5_a_re[j], s5_a_im[j], s5_log_dt[j], s5_b_re[j], s5_b_im[j],
                                s5_c_re[j], s5_c_im[j])
            ys = _s5(proj[:, n_gla:], bsz, ops)
            xf = _cd_out(xf, proj, og, gla_norm[j], ys, s5_d[j],
                         s5_w_glu[j].astype(BF16), cd_w_out[j].astype(BF16))
        xf = _ffn(xf, norm_ffn2[l], ffn2_w_gu[l].astype(BF16), ffn2_w_down[l].astype(BF16),
                  final_g=final_norm if l == depth - 1 else None)
    return xf.reshape(bsz, seq, d)
```

```python
import functools

import jax
import jax.numpy as jnp
from jax import lax
from jax.experimental import pallas as pl
from jax.experimental.pallas import tpu as pltpu

NORM_EPS = 1e-6

W_LRU = 1024
LRU_HEADS = 8
LRU_BLOCK = W_LRU // LRU_HEADS
CONV_W = 4
CONV_PAD_L = 2
LRU_C = 8.0

W_MLSTM = 1024
MLSTM_HEADS = 4
MLSTM_DH = W_MLSTM // MLSTM_HEADS

GLA_HEADS = 4
GLA_DK = 128
GLA_DV = 256
GLA_QK = GLA_HEADS * GLA_DK
GLA_V = GLA_HEADS * GLA_DV
GLA_RANK = 16
GLA_TAU = 16.0

S5_W = 1024
S5_GROUP = 16
S5_GROUPS = S5_W // S5_GROUP
S5_P = 64
S5_CHUNK = 16
S5_GB = 8
S5_ROW_TILE = 256

LANE = 128
SUBLANE = 8
VMEM_LIMIT = 52 * 1024 * 1024

BF16 = jnp.bfloat16
F32 = jnp.float32
HI = lax.Precision.HIGHEST


def _rms(x, g):
    ms = jnp.mean(x * x, axis=-1, keepdims=True)
    return x * lax.rsqrt(ms + NORM_EPS) * g


def _softplus(z):
    return jnp.maximum(z, 0.0) + jnp.log1p(jnp.exp(-jnp.abs(z)))


def _params(*sem):
    return pltpu.CompilerParams(dimension_semantics=sem, vmem_limit_bytes=VMEM_LIMIT)


def _ffn_kernel(x_ref, g_ref, wg_ref, wu_ref, wd_ref, *rest, final):
    if final:
        fg_ref, o_ref, h_ref = rest
    else:
        o_ref, h_ref = rest
    j = pl.program_id(1)

    @pl.when(j == 0)
    def _():
        x = x_ref[...]
        h_ref[...] = _rms(x, g_ref[...]).astype(BF16)
        o_ref[...] = x

    h = h_ref[...]
    g = jnp.dot(h, wg_ref[...], preferred_element_type=F32)
    u = jnp.dot(h, wu_ref[...], preferred_element_type=F32)
    a = (0.5 * g * jax.nn.sigmoid(g) * u).astype(BF16)
    o_ref[...] += jnp.dot(a, wd_ref[...], preferred_element_type=F32)

    if final:
        @pl.when(j == pl.num_programs(1) - 1)
        def _():
            o_ref[...] = _rms(o_ref[...], fg_ref[...])


def _ffn(x, g, w_gu, w_down, layer, final_g=None, tm=512, tf=512):
    n, d = x.shape
    f = w_down.shape[1]
    tm = min(tm, n)
    nj = f // tf
    final = final_g is not None
    in_specs = [
        pl.BlockSpec((tm, d), lambda i, j: (i, 0)),
        pl.BlockSpec((1, d), lambda i, j: (0, 0)),
        pl.BlockSpec((None, d, tf), lambda i, j: (layer, 0, j)),
        pl.BlockSpec((None, d, tf), lambda i, j: (layer, 0, j + nj)),
        pl.BlockSpec((None, tf, d), lambda i, j: (layer, j, 0)),
    ]
    args = [x, g.reshape(1, d), w_gu, w_gu, w_down]
    if final:
        in_specs.append(pl.BlockSpec((1, d), lambda i, j: (0, 0)))
        args.append(final_g.reshape(1, d))
    return pl.pallas_call(
        functools.partial(_ffn_kernel, final=final),
        grid=(n // tm, nj),
        in_specs=in_specs,
        out_specs=pl.BlockSpec((tm, d), lambda i, j: (i, 0)),
        out_shape=jax.ShapeDtypeStruct((n, d), F32),
        scratch_shapes=[pltpu.VMEM((tm, d), BF16)],
        compiler_params=_params("arbitrary", "arbitrary"),
        name="ffn_final" if final else "ffn",
    )(*args)


def _norm_proj_kernel(x_ref, g_ref, w_ref, ws_ref, o_ref, os_ref, h_ref):
    @pl.when(pl.program_id(1) == 0)
    def _():
        h = _rms(x_ref[...], g_ref[...]).astype(BF16)
        h_ref[...] = h
        os_ref[...] = jnp.dot(h, ws_ref[...], preferred_element_type=F32)

    o_ref[...] = jnp.dot(h_ref[...], w_ref[...], preferred_element_type=F32)


def _norm_proj(x, g, w, w_small, tm=1024, tn=1024):
    n, d = x.shape
    m = w.shape[1]
    ms = w_small.shape[1]
    tm = min(tm, n)
    return pl.pallas_call(
        _norm_proj_kernel,
        grid=(n // tm, m // tn),
        in_specs=[
            pl.BlockSpec((tm, d), lambda i, j: (i, 0)),
            pl.BlockSpec((1, d), lambda i, j: (0, 0)),
            pl.BlockSpec((d, tn), lambda i, j: (0, j)),
            pl.BlockSpec((d, ms), lambda i, j: (0, 0)),
        ],
        out_specs=[pl.BlockSpec((tm, tn), lambda i, j: (i, j)),
                   pl.BlockSpec((tm, ms), lambda i, j: (i, 0))],
        out_shape=[jax.ShapeDtypeStruct((n, m), F32), jax.ShapeDtypeStruct((n, ms), F32)],
        scratch_shapes=[pltpu.VMEM((tm, d), BF16)],
        compiler_params=_params("arbitrary", "arbitrary"),
        name="norm_proj",
    )(x, g.reshape(1, d), w, w_small)


def _lru_kernel(xp_ref, x_ref, xn_ref, cw_ref, cb_ref, gw_ref, gb_ref, lam_ref, o_ref,
                xs_ref, a_ref, b_ref, h_ref, *, tt):
    d = pl.program_id(0)
    t = pl.program_id(2)
    nt = pl.num_programs(2)
    tb = jnp.where(d == 0, t, nt - 1 - t)
    w = x_ref.shape[1]

    xs_ref[0:SUBLANE, :] = jnp.where(tb > 0, xp_ref[...], 0.0)
    xs_ref[SUBLANE:SUBLANE + tt, :] = x_ref[...]
    xs_ref[SUBLANE + tt:2 * SUBLANE + tt, :] = jnp.where(tb < nt - 1, xn_ref[...], 0.0)
    cw = cw_ref[...]
    xc = cb_ref[...]
    for k in range(CONV_W):
        off = SUBLANE + k - CONV_PAD_L
        xc = xc + cw[k:k + 1, :] * xs_ref[off:off + tt, :]

    xcb = xc.astype(BF16)
    gb = gb_ref[0]
    sp = _softplus(-lam_ref[0])
    for hd in range(w // LRU_BLOCK):
        sl = slice(hd * LRU_BLOCK, (hd + 1) * LRU_BLOCK)
        z = jnp.dot(xcb[:, sl], gw_ref[0, hd], preferred_element_type=F32)
        r = jax.nn.sigmoid(z[:, :LRU_BLOCK] + gb[0:1, sl])
        i = jax.nn.sigmoid(z[:, LRU_BLOCK:] + gb[1:2, sl])
        log_a = (-LRU_C * sp[:, sl]) * r
        a = jnp.exp(log_a)
        a_ref[:, sl] = a
        b_ref[:, sl] = jnp.sqrt(-jnp.tanh(log_a) * (a * a + 1.0)) * (i * xc[:, sl])

    @pl.when(t == 0)
    def _():
        h_ref[...] = jnp.zeros_like(h_ref)

    sub = lax.broadcasted_iota(jnp.int32, (SUBLANE, w), 0)
    ntile = tt // SUBLANE

    def scan(rev):
        def earlier(x, k, fill):
            if rev:
                return jnp.where(sub < SUBLANE - k, pltpu.roll(x, SUBLANE - k, 0), fill)
            return jnp.where(sub >= k, pltpu.roll(x, k, 0), fill)

        def body(jt, h):
            r0 = pl.multiple_of((ntile - 1 - jt if rev else jt) * SUBLANE, SUBLANE)
            a = a_ref[pl.ds(r0, SUBLANE), :]
            b = b_ref[pl.ds(r0, SUBLANE), :]
            k = 1
            while k < SUBLANE:
                a, b = a * earlier(a, k, 1.0), b + a * earlier(b, k, 0.0)
                k *= 2
            tile = a * h + b
            o_ref[0, pl.ds(r0, SUBLANE), :] = tile
            last = 0 if rev else SUBLANE - 1
            return tile[last:last + 1, :]

        h_ref[...] = lax.fori_loop(0, ntile, body, h_ref[...])

    pl.when(d == 0)(functools.partial(scan, False))
    pl.when(d == 1)(functools.partial(scan, True))


def _lru(proj, bsz, conv_w, conv_b, gate_w, gate_b, lam, tt=256):
    n = proj.shape[0]
    seq = n // bsz
    tt = min(tt, seq)
    nt = seq // tt
    w = W_LRU
    r8 = tt // SUBLANE
    gw = jnp.concatenate([gate_w[:, 0], gate_w[:, 1]], axis=-1).astype(BF16)

    def tb(d, t):
        return t + d * (nt - 1 - 2 * t)

    return pl.pallas_call(
        functools.partial(_lru_kernel, tt=tt),
        grid=(2, bsz, nt),
        in_specs=[
            pl.BlockSpec((SUBLANE, w), lambda d, b, t: (jnp.maximum((b * nt + tb(d, t)) * r8 - 1, 0), 0)),
            pl.BlockSpec((tt, w), lambda d, b, t: (b * nt + tb(d, t), 0)),
            pl.BlockSpec((SUBLANE, w), lambda d, b, t: (jnp.minimum((b * nt + tb(d, t) + 1) * r8, n // SUBLANE - 1), 0)),
            pl.BlockSpec((CONV_W, w), lambda d, b, t: (0, 0)),
            pl.BlockSpec((1, w), lambda d, b, t: (0, 0)),
            pl.BlockSpec((1, LRU_HEADS, LRU_BLOCK, 2 * LRU_BLOCK), lambda d, b, t: (d, 0, 0, 0)),
            pl.BlockSpec((1, 2, w), lambda d, b, t: (d, 0, 0)),
            pl.BlockSpec((1, 1, w), lambda d, b, t: (d, 0, 0)),
        ],
        out_specs=pl.BlockSpec((1, tt, w), lambda d, b, t: (d, b * nt + tb(d, t), 0)),
        out_shape=jax.ShapeDtypeStruct((2, n, w), F32),
        scratch_shapes=[pltpu.VMEM((tt + 2 * SUBLANE, w), F32), pltpu.VMEM((tt, w), F32),
                        pltpu.VMEM((tt, w), F32), pltpu.VMEM((1, w), F32)],
        compiler_params=_params("arbitrary", "arbitrary", "arbitrary"),
        name="rg_lru",
    )(proj, proj, proj, conv_w, conv_b.reshape(1, w), gw, gate_b, lam.reshape(2, 1, w))


def _s5_discretise(a_re, a_im, log_dt, b_re, b_im):
    dt = jnp.exp(log_dt)[:, None]
    mag = jnp.exp(dt * a_re)
    lr = mag * jnp.cos(dt * a_im)
    li = mag * jnp.sin(dt * a_im)
    den = a_re * a_re + a_im * a_im
    nr = lr - 1.0
    cr = (nr * a_re + li * a_im) / den
    ci = (li * a_re - nr * a_im) / den
    bbr = cr[..., None] * b_re - ci[..., None] * b_im
    bbi = cr[..., None] * b_im + ci[..., None] * b_re
    return lr, li, bbr, bbi


def _s5_operators(a_re, a_im, log_dt, b_re, b_im, c_re, c_im):
    L = S5_CHUNK
    G, P, C = S5_GROUPS, S5_P, S5_GROUP
    kfs, mins, mouts, lams = [], [], [], []
    for d in range(2):
        lr, li, bbr, bbi = _s5_discretise(a_re[d], a_im[d], log_dt[d], b_re, b_im)
        bbr, bbi = bbr.transpose(0, 2, 1), bbi.transpose(0, 2, 1)
        pr, pi = [jnp.ones_like(lr)], [jnp.zeros_like(li)]
        for _ in range(L):
            pr, pi = pr + [pr[-1] * lr - pi[-1] * li], pi + [pr[-1] * li + pi[-1] * lr]
        pr, pi = jnp.stack(pr)[:, :, None, :], jnp.stack(pi)[:, :, None, :]
        cl = jnp.concatenate([c_re[None] * pr - c_im[None] * pi, -(c_re[None] * pi + c_im[None] * pr)], axis=-1)
        bl = jnp.concatenate([pr * bbr[None] - pi * bbi[None], pr * bbi[None] + pi * bbr[None]], axis=-1)
        kfs.append(jnp.einsum('jgcx,gkx->jgkc', cl[:L], bl[0], precision=HI))
        if d == 0:
            e_in = jnp.arange(L - 1, -1, -1)
            e_out = jnp.arange(1, L + 1)
        else:
            e_in = jnp.arange(L)
            e_out = jnp.arange(L, 0, -1)
        mins.append(bl[e_in])
        mouts.append(cl[e_out])
        lams.append(jnp.stack([pr[L, :, 0], pi[L, :, 0]], axis=1))
    kf, kb = kfs
    kfull = jnp.concatenate([kb[:0:-1], (kf[0] + kb[0])[None], kf[1:]], axis=0)
    gb = S5_GB
    nb_ = G // gb
    k2 = kfull.reshape(2 * L - 1, nb_, gb * C, C).transpose(1, 0, 2, 3)

    def rows_of_block(parts):
        a = jnp.concatenate(parts, axis=-1).reshape(L, nb_, gb * C, 4 * P)
        return a.transpose(1, 0, 2, 3).reshape(nb_, L * gb * C, 4 * P)

    lam_l = jnp.concatenate(lams, axis=1)
    lam_blk = lam_l.reshape(nb_, gb, 4, P).transpose(0, 2, 1, 3).reshape(nb_, 1, 4 * gb * P)
    return k2, rows_of_block(mins), rows_of_block(mouts), lam_blk


def _s5_state_kernel(u_ref, min_ref, lam_ref, hin_ref, h_ref, *, nb):
    nl, rows, _ = u_ref.shape
    rb = rows // nb
    hw = h_ref.shape[1] // 4
    rt = min(rows, S5_ROW_TILE)
    lo = lax.broadcasted_iota(jnp.int32, (rt, LANE), 1) < S5_P
    ng = hw // S5_P

    def pairs():
        for d in range(2):
            for k in range(ng // 2):
                yield ((d * ng + 2 * k) * LANE, (d * ng + 2 * k + 1) * LANE,
                       2 * d * hw + k * LANE, (2 * d + 1) * hw + k * LANE)

    for r0 in range(0, rows, rt):
        u = jnp.concatenate([u_ref[s, r0:r0 + rt, :] for s in range(nl)], axis=1)
        hl = jnp.dot(u, min_ref[0], preferred_element_type=F32)
        for ca, cb, cre, cim in pairs():
            a, b = hl[:, ca:ca + LANE], hl[:, cb:cb + LANE]
            h_ref[r0:r0 + rt, cre:cre + LANE] = jnp.where(lo, a, pltpu.roll(b, S5_P, 1))
            h_ref[r0:r0 + rt, cim:cim + LANE] = jnp.where(lo, pltpu.roll(a, S5_P, 1), b)
    lam = lam_ref[0]
    lrf, lif = lam[:, 0:hw], lam[:, hw:2 * hw]
    lrb, lib = lam[:, 2 * hw:3 * hw], lam[:, 3 * hw:4 * hw]
    sub = lax.broadcasted_iota(jnp.int32, (SUBLANE, hw), 0)

    def body(it, carry):
        new = []
        for b in range(nb):
            fr, fi, br, bi = carry[4 * b:4 * b + 4]
            f0 = pl.multiple_of(b * rb + it * SUBLANE, SUBLANE)
            b0 = pl.multiple_of(b * rb + rb - SUBLANE - it * SUBLANE, SUBLANE)
            tiles = [jnp.zeros((SUBLANE, hw), F32)] * 4
            lf = h_ref[pl.ds(f0, SUBLANE), 0:2 * hw]
            lb = h_ref[pl.ds(b0, SUBLANE), 2 * hw:4 * hw]
            for r in range(SUBLANE):
                q = SUBLANE - 1 - r
                tiles = [jnp.where(sub == r, fr, tiles[0]), jnp.where(sub == r, fi, tiles[1]),
                         jnp.where(sub == q, br, tiles[2]), jnp.where(sub == q, bi, tiles[3])]
                fr, fi, br, bi = (lrf * fr - lif * fi + lf[r:r + 1, 0:hw],
                                  lrf * fi + lif * fr + lf[r:r + 1, hw:2 * hw],
                                  lrb * br - lib * bi + lb[q:q + 1, 0:hw],
                                  lrb * bi + lib * br + lb[q:q + 1, hw:2 * hw])
            h_ref[pl.ds(f0, SUBLANE), 0:hw] = tiles[0]
            h_ref[pl.ds(f0, SUBLANE), hw:2 * hw] = tiles[1]
            h_ref[pl.ds(b0, SUBLANE), 2 * hw:3 * hw] = tiles[2]
            h_ref[pl.ds(b0, SUBLANE), 3 * hw:4 * hw] = tiles[3]
            new += [fr, fi, br, bi]
        return tuple(new)

    zero = jnp.zeros((1, hw), F32)
    lax.fori_loop(0, rb // SUBLANE, body, (zero,) * (4 * nb))
    for r0 in range(0, rows, rt):
        for ca, cb, cre, cim in pairs():
            re, im = h_ref[r0:r0 + rt, cre:cre + LANE], h_ref[r0:r0 + rt, cim:cim + LANE]
            hin_ref[0, r0:r0 + rt, ca:ca + LANE] = jnp.where(lo, re, pltpu.roll(im, S5_P, 1)).astype(BF16)
            hin_ref[0, r0:r0 + rt, cb:cb + LANE] = jnp.where(lo, pltpu.roll(re, S5_P, 1), im).astype(BF16)


def _s5_expand_kernel(a_ref, o_ref, *, transpose):
    a = a_ref[0]
    rows = a.shape[0]
    row_g = (lax.broadcasted_iota(jnp.int32, (rows, LANE), 0) // S5_GROUP) % S5_GB
    for d in range(2):
        ad = a[:, d * LANE:(d + 1) * LANE]
        for g in range(S5_GB):
            tile = jnp.where(row_g == g, ad, 0.0)
            col = (d * S5_GB + g) * LANE
            if transpose:
                o_ref[0, col:col + LANE, :] = tile.T.astype(BF16)
            else:
                o_ref[0, :, col:col + LANE] = tile.astype(BF16)


def _s5_expand(a, transpose):
    nblk, rows, _ = a.shape
    sw = 2 * S5_GB * LANE
    shape = (nblk, sw, rows) if transpose else (nblk, rows, sw)
    return pl.pallas_call(
        functools.partial(_s5_expand_kernel, transpose=transpose),
        grid=(nblk,),
        in_specs=[pl.BlockSpec((1,) + a.shape[1:], lambda b: (b, 0, 0))],
        out_specs=pl.BlockSpec((1,) + shape[1:], lambda b: (b, 0, 0)),
        out_shape=jax.ShapeDtypeStruct(shape, BF16),
        compiler_params=_params("arbitrary"),
        name="s5_expand",
    )(a)


def _s5_out_kernel(u_ref, hin_ref, k2_ref, mout_ref, y_ref, t_ref):
    nl = u_ref.shape[0]

    @pl.when(pl.program_id(1) == 0)
    def _():
        row_g = lax.broadcasted_iota(jnp.int32, (LANE, LANE), 0) // S5_GROUP
        col_g = lax.broadcasted_iota(jnp.int32, (LANE, LANE), 1) // S5_GROUP
        spread = (lax.broadcasted_iota(jnp.int32, (S5_GROUP, LANE), 1) % S5_GROUP
                  == lax.broadcasted_iota(jnp.int32, (S5_GROUP, LANE), 0)).astype(BF16)
        for j in range(2 * nl - 1):
            rep = jnp.dot(k2_ref[0, j].astype(BF16), spread, preferred_element_type=F32)
            tile = jnp.where(row_g == col_g, rep, 0.0).astype(BF16)
            for s in range(nl):
                t = s + j - (nl - 1)
                if 0 <= t < nl:
                    t_ref[s * LANE:(s + 1) * LANE, t * LANE:(t + 1) * LANE] = tile

    u = jnp.concatenate([u_ref[s] for s in range(nl)], axis=1)
    y = (jnp.dot(u, t_ref[...], preferred_element_type=F32)
         + jnp.dot(hin_ref[0], mout_ref[0], preferred_element_type=F32))
    for t in range(nl):
        y_ref[t] = y[:, t * LANE:(t + 1) * LANE]


def _s5(u, bsz, ops, tr=256):
    k2, a_min, a_mout, lam_blk = ops
    min_blk = _s5_expand(a_min, False)
    mout_blk = _s5_expand(a_mout, True)
    n, w = u.shape
    L = S5_CHUNK
    rows = n // L
    nblk = w // LANE
    kw = L * LANE
    sw = lam_blk.shape[2]
    tr = min(tr, rows)
    u3 = u.reshape(rows, L, w).transpose(1, 0, 2).astype(BF16)
    hin = pl.pallas_call(
        functools.partial(_s5_state_kernel, nb=bsz),
        grid=(nblk,),
        in_specs=[
            pl.BlockSpec((L, rows, LANE), lambda b: (0, 0, b)),
            pl.BlockSpec((1, kw, sw), lambda b: (b, 0, 0)),
            pl.BlockSpec((1, 1, sw), lambda b: (b, 0, 0)),
        ],
        out_specs=pl.BlockSpec((1, rows, sw), lambda b: (b, 0, 0)),
        out_shape=jax.ShapeDtypeStruct((nblk, rows, sw), BF16),
        scratch_shapes=[pltpu.VMEM((rows, sw), F32)],
        compiler_params=_params("arbitrary"),
        name="s5_state",
    )(u3, min_blk, lam_blk)
    y3 = pl.pallas_call(
        _s5_out_kernel,
        grid=(nblk, rows // tr),
        in_specs=[
            pl.BlockSpec((L, tr, LANE), lambda b, i: (0, i, b)),
            pl.BlockSpec((1, tr, sw), lambda b, i: (b, i, 0)),
            pl.BlockSpec((1,) + k2.shape[1:], lambda b, i: (b, 0, 0, 0)),
            pl.BlockSpec((1, sw, kw), lambda b, i: (b, 0, 0)),
        ],
        out_specs=pl.BlockSpec((L, tr, LANE), lambda b, i: (0, i, b)),
        out_shape=jax.ShapeDtypeStruct((L, rows, w), F32),
        scratch_shapes=[pltpu.VMEM((kw, kw), BF16)],
        compiler_params=_params("arbitrary", "arbitrary"),
        name="s5_out",
    )(u3, hin, k2, mout_blk)
    return y3.transpose(1, 0, 2).reshape(n, w)


def _log_sigmoid(z):
    return -_softplus(-z)


def _tri(length, d):
    ti = lax.broadcasted_iota(jnp.int32, (length, length), 0)
    si = lax.broadcasted_iota(jnp.int32, (length, length), 1)
    return (ti - si) * (1 - 2 * d) >= 0


def _mlstm_kernel(q_ref, k_ref, v_ref, g_ref, gb_ref, o_ref, c_ref, n_ref, m_ref):
    d = pl.program_id(0)
    length = q_ref.shape[0]
    nh, dh = MLSTM_HEADS, MLSTM_DH

    @pl.when(pl.program_id(2) == 0)
    def _():
        c_ref[...] = jnp.zeros_like(c_ref)
        n_ref[...] = jnp.zeros_like(n_ref)
        m_ref[...] = jnp.zeros_like(m_ref)

    causal = _tri(length, d)
    gp = g_ref[...] + gb_ref[...]
    gp = jnp.where(d == 0, gp, pltpu.roll(gp, LANE - 2 * nh, 1))
    lf = _log_sigmoid(gp)
    cum = jnp.dot(causal.astype(F32), lf, precision=HI, preferred_element_type=F32)
    tot = jnp.sum(lf, axis=0, keepdims=True)
    gp_t = gp.T
    cum_t = cum.T
    nt = (((1,), (1,)), ((), ()))
    tn = (((0,), (0,)), ((), ()))

    for h in range(nh):
        sl = slice(h * dh, (h + 1) * dh)
        qh = (q_ref[:, sl] * (dh ** -0.5)).astype(BF16)
        kf = k_ref[:, sl]
        kh = kf.astype(BF16)
        vh = v_ref[:, sl].astype(BF16)
        ig_c, cum_c = gp[:, h:h + 1], cum[:, nh + h:nh + h + 1]
        ig_r, cum_r = gp_t[h:h + 1, :], cum_t[nh + h:nh + h + 1, :]
        tot_h = tot[:, nh + h:nh + h + 1]
        m_st = m_ref[h:h + 1, 0:1]
        c_st = c_ref[h]
        dmat = jnp.where(causal, cum_c - cum_r + ig_r, -jnp.inf)
        m_inter = cum_c + m_st
        m_t = jnp.maximum(jnp.max(dmat, axis=1, keepdims=True), m_inter)
        w_inter = jnp.exp(m_inter - m_t)
        s = lax.dot_general(qh, kh, nt, preferred_element_type=F32) * jnp.exp(dmat - m_t)
        num = (jnp.dot(s.astype(BF16), vh, preferred_element_type=F32)
               + w_inter * jnp.dot(qh, c_st.astype(BF16), preferred_element_type=F32))
        den = (jnp.sum(s, axis=1, keepdims=True)
               + w_inter * jnp.sum(qh.astype(F32) * n_ref[h:h + 1, :], axis=1, keepdims=True))
        o_ref[0, :, sl] = num / jnp.maximum(jnp.abs(den), jnp.exp(-m_t))
        dec = tot_h - cum_c + ig_c
        m_new = jnp.maximum(tot_h + m_st, jnp.max(dec, axis=0, keepdims=True))
        wc = jnp.exp(tot_h + m_st - m_new)
        kw = jnp.exp(dec - m_new) * kf
        c_ref[h] = wc * c_st + lax.dot_general(kw.astype(BF16), vh, tn, preferred_element_type=F32)
        n_ref[h:h + 1, :] = wc * n_ref[h:h + 1, :] + jnp.sum(kw, axis=0, keepdims=True)
        m_ref[h:h + 1, :] = jnp.broadcast_to(m_new, (1, LANE))


def _mlstm(proj, tail, bsz, gate_b, length=256):
    n = proj.shape[0]
    seq = n // bsz
    length = min(length, seq)
    nc = seq // length
    w = W_MLSTM
    q_blk = 2 * W_LRU // w
    gb = jnp.pad(gate_b.reshape(1, -1), ((0, 0), (0, LANE - gate_b.size)))

    def row(d, b, c):
        return b * nc + c + d * (nc - 1 - 2 * c)

    return pl.pallas_call(
        _mlstm_kernel,
        grid=(2, bsz, nc),
        in_specs=[
            pl.BlockSpec((length, w), lambda d, b, c: (row(d, b, c), q_blk)),
            pl.BlockSpec((length, w), lambda d, b, c: (row(d, b, c), q_blk + 1)),
            pl.BlockSpec((length, w), lambda d, b, c: (row(d, b, c), q_blk + 2)),
            pl.BlockSpec((length, LANE), lambda d, b, c: (row(d, b, c), 0)),
            pl.BlockSpec((1, LANE), lambda d, b, c: (0, 0)),
        ],
        out_specs=pl.BlockSpec((1, length, w), lambda d, b, c: (d, row(d, b, c), 0)),
        out_shape=jax.ShapeDtypeStruct((2, n, w), F32),
        scratch_shapes=[pltpu.VMEM((MLSTM_HEADS, MLSTM_DH, MLSTM_DH), F32),
                        pltpu.VMEM((SUBLANE, MLSTM_DH), F32), pltpu.VMEM((SUBLANE, LANE), F32)],
        compiler_params=_params("arbitrary", "arbitrary", "arbitrary"),
        name="mlstm",
    )(proj, proj, proj, tail, gb)


GLA_SUB = 16


def _gla_chunk(q_ref, k_ref, v_ref, cum, o_ref, s_ref, rev):
    length = q_ref.shape[0]
    c = GLA_SUB
    nt = (((1,), (1,)), ((), ()))
    tn = (((0,), (0,)), ((), ()))

    def rows(a, b):
        return slice(length - b, length - a) if rev else slice(a, b)

    def row(a):
        i = length - 1 - a if rev else a
        return slice(i, i + 1)

    ti = lax.broadcasted_iota(jnp.int32, (c, 1), 0)
    for h in range(GLA_HEADS):
        kl = slice(h * GLA_DK, (h + 1) * GLA_DK)
        vl = slice(h * GLA_DV, (h + 1) * GLA_DV)
        ch = cum[:, kl]
        qs = q_ref[:, kl] * (GLA_DK ** -0.5)
        kk = k_ref[:, kl]
        vf = v_ref[:, vl]
        vv = vf.astype(BF16)
        st = s_ref[h]
        tot = ch[row(length - 1)]
        o = lax.dot_general((qs * jnp.exp(ch)).astype(BF16), st.astype(BF16), nt, preferred_element_type=F32)
        ob = [o[rows(p * c, (p + 1) * c)] for p in range(length // c)]
        m = length // 2
        while m >= c:
            for start in range(0, length, 2 * m):
                fst, sec = rows(start, start + m), rows(start + m, start + 2 * m)
                r = ch[row(start + m)]
                qh = (qs[sec] * jnp.exp(ch[sec] - r)).astype(BF16)
                kh = (kk[fst] * jnp.exp(r - ch[fst])).astype(BF16)
                att = lax.dot_general(qh, kh, nt, preferred_element_type=F32)
                contrib = jnp.dot(att.astype(BF16), vv[fst], preferred_element_type=F32)
                for p in range((start + m) // c, (start + 2 * m) // c):
                    lo = rows(p * c, (p + 1) * c).start - sec.start
                    ob[p] = ob[p] + contrib[lo:lo + c]
            m //= 2
        for p in range(length // c):
            blk = rows(p * c, (p + 1) * c)
            cb, qb, kb, vb = ch[blk], qs[blk], kk[blk], vf[blk]
            acc = ob[p]
            for s in range(c):
                sees = (ti <= s) if rev else (ti >= s)
                e = jnp.exp(jnp.where(sees, cb - cb[s:s + 1], -jnp.inf))
                a = jnp.sum(qb * kb[s:s + 1] * e, axis=1, keepdims=True)
                acc = acc + a * vb[s:s + 1]
            o_ref[0, blk, vl] = acc
        kd = (kk * jnp.exp(tot - ch)).astype(BF16)
        s_ref[h] = jnp.exp(tot) * st + lax.dot_general(vv, kd, tn, preferred_element_type=F32)


def _gla_kernel(q_ref, k_ref, v_ref, low_ref, wg_ref, bg_ref, o_ref, s_ref):
    d = pl.program_id(0)
    length = q_ref.shape[0]

    @pl.when(pl.program_id(2) == 0)
    def _():
        s_ref[...] = jnp.zeros_like(s_ref)

    gate_pre = jnp.dot(low_ref[...], wg_ref[0], precision=HI, preferred_element_type=F32) + bg_ref[0]
    la = _log_sigmoid(gate_pre) * (1.0 / GLA_TAU)
    cum = jnp.dot(_tri(length, d).astype(F32), la, precision=HI, preferred_element_type=F32)
    pl.when(d == 0)(lambda: _gla_chunk(q_ref, k_ref, v_ref, cum, o_ref, s_ref, False))
    pl.when(d == 1)(lambda: _gla_chunk(q_ref, k_ref, v_ref, cum, o_ref, s_ref, True))


def _gla(proj, tail, bsz, w_gate2, gate_b, length=128):
    n = proj.shape[0]
    seq = n // bsz
    length = min(length, seq)
    nc = seq // length
    wg = jnp.zeros((2, LANE, GLA_QK), F32)
    for d in range(2):
        wg = wg.at[d, d * GLA_RANK:(d + 1) * GLA_RANK].set(w_gate2[d])

    def row(d, b, c):
        return b * nc + c + d * (nc - 1 - 2 * c)

    return pl.pallas_call(
        _gla_kernel,
        grid=(2, bsz, nc),
        in_specs=[
            pl.BlockSpec((length, GLA_QK), lambda d, b, c: (row(d, b, c), 0)),
            pl.BlockSpec((length, GLA_QK), lambda d, b, c: (row(d, b, c), 1)),
            pl.BlockSpec((length, GLA_V), lambda d, b, c: (row(d, b, c), 2 * GLA_QK // GLA_V)),
            pl.BlockSpec((length, LANE), lambda d, b, c: (row(d, b, c), 0)),
            pl.BlockSpec((1, LANE, GLA_QK), lambda d, b, c: (d, 0, 0)),
            pl.BlockSpec((1, 1, GLA_QK), lambda d, b, c: (d, 0, 0)),
        ],
        out_specs=pl.BlockSpec((1, length, GLA_V), lambda d, b, c: (d, row(d, b, c), 0)),
        out_shape=jax.ShapeDtypeStruct((2, n, GLA_V), F32),
        scratch_shapes=[pltpu.VMEM((GLA_HEADS, GLA_DV, GLA_DK), F32)],
        compiler_params=_params("arbitrary", "arbitrary", "arbitrary"),
        name="gla",
    )(proj, proj, proj, tail, wg, gate_b.reshape(2, 1, GLA_QK))


def _head_norm(t, g, heads):
    dh = t.shape[1] // heads
    outs = []
    for h in range(heads):
        th = t[:, h * dh:(h + 1) * dh]
        outs.append(th * lax.rsqrt(jnp.mean(th * th, axis=1, keepdims=True) + NORM_EPS))
    return jnp.concatenate(outs, axis=1) * g


def _ab_out_kernel(x_ref, gr_ref, og_ref, hl_ref, hm_ref, g_ref, w_ref, o_ref):
    k = gr_ref.shape[1]
    ya = jax.nn.gelu(gr_ref[...]) * (hl_ref[0] + hl_ref[1])
    yb = jax.nn.sigmoid(og_ref[...]) * _head_norm(hm_ref[0] + hm_ref[1], g_ref[...], MLSTM_HEADS)
    o_ref[...] = (x_ref[...]
                  + jnp.dot(ya.astype(BF16), w_ref[0:k, :], preferred_element_type=F32)
                  + jnp.dot(yb.astype(BF16), w_ref[k:, :], preferred_element_type=F32))


def _ab_out(x, proj, hl, hm, norm_g, w, tm=512):
    n, d = x.shape
    k = W_LRU
    tm = min(tm, n)
    return pl.pallas_call(
        _ab_out_kernel,
        grid=(n // tm,),
        in_specs=[
            pl.BlockSpec((tm, d), lambda i: (i, 0)),
            pl.BlockSpec((tm, k), lambda i: (i, 1)),
            pl.BlockSpec((tm, k), lambda i: (i, 5)),
            pl.BlockSpec((2, tm, k), lambda i: (0, i, 0)),
            pl.BlockSpec((2, tm, k), lambda i: (0, i, 0)),
            pl.BlockSpec((1, k), lambda i: (0, 0)),
            pl.BlockSpec((2 * k, d), lambda i: (0, 0)),
        ],
        out_specs=pl.BlockSpec((tm, d), lambda i: (i, 0)),
        out_shape=jax.ShapeDtypeStruct((n, d), F32),
        compiler_params=_params("arbitrary"),
        name="ab_out",
    )(x, proj, proj, hl, hm, norm_g.reshape(1, k), w)


def _cd_out_kernel(x_ref, r_ref, og_ref, g_ref, ys_ref, u_ref, d_ref, wg_ref, w_ref, o_ref):
    k = r_ref.shape[1]
    r = r_ref[...]
    yc = _head_norm(og_ref[0] + og_ref[1], g_ref[...], GLA_HEADS) * (r * jax.nn.sigmoid(r))
    y = jax.nn.gelu(ys_ref[...] + d_ref[...] * u_ref[...])
    yd = y * jax.nn.sigmoid(jnp.dot(y.astype(BF16), wg_ref[...], preferred_element_type=F32))
    o_ref[...] = (x_ref[...]
                  + jnp.dot(yc.astype(BF16), w_ref[0:k, :], preferred_element_type=F32)
                  + jnp.dot(yd.astype(BF16), w_ref[k:, :], preferred_element_type=F32))


def _cd_out(x, proj, og, norm_g, ys, s5_d, w_glu, w, tm=512):
    n, d = x.shape
    k = S5_W
    tm = min(tm, n)
    return pl.pallas_call(
        _cd_out_kernel,
        grid=(n // tm,),
        in_specs=[
            pl.BlockSpec((tm, d), lambda i: (i, 0)),
            pl.BlockSpec((tm, k), lambda i: (i, 2)),
            pl.BlockSpec((2, tm, k), lambda i: (0, i, 0)),
            pl.BlockSpec((1, k), lambda i: (0, 0)),
            pl.BlockSpec((tm, k), lambda i: (i, 0)),
            pl.BlockSpec((tm, k), lambda i: (i, 3)),
            pl.BlockSpec((1, k), lambda i: (0, 0)),
            pl.BlockSpec((k, k), lambda i: (0, 0)),
            pl.BlockSpec((2 * k, d), lambda i: (0, 0)),
        ],
        out_specs=pl.BlockSpec((tm, d), lambda i: (i, 0)),
        out_shape=jax.ShapeDtypeStruct((n, d), F32),
        compiler_params=_params("arbitrary"),
        name="cd_out",
    )(x, proj, og, norm_g.reshape(1, k), ys, proj, s5_d.reshape(1, k), w_glu, w)


def _split_small(w, n_main):
    tail = w[:, n_main:]
    tail = jnp.pad(tail, ((0, 0), (0, LANE - tail.shape[1])))
    return w[:, :n_main].astype(BF16), tail.astype(BF16)


def kernel(x, norm_ffn1, ffn1_w_gu, ffn1_w_down, norm_mix, norm_ffn2, ffn2_w_gu, ffn2_w_down,
           ab_w_in, lru_conv_w, lru_conv_b, lru_gate_w, lru_gate_b, lru_lambda, mlstm_gate_b,
           mlstm_norm, ab_w_out, cd_w_in, gla_w_gate2, gla_gate_b, gla_norm, s5_a_re, s5_a_im,
           s5_log_dt, s5_b_re, s5_b_im, s5_c_re, s5_c_im, s5_d, s5_w_glu, cd_w_out, final_norm):
    bsz, seq, d = x.shape
    n = bsz * seq
    depth = norm_ffn1.shape[0]
    xf = x.reshape(n, d)
    w1_gu, w1_down = ffn1_w_gu.astype(BF16), ffn1_w_down.astype(BF16)
    w2_gu, w2_down = ffn2_w_gu.astype(BF16), ffn2_w_down.astype(BF16)
    for l in range(depth):
        xf = _ffn(xf, norm_ffn1[l], w1_gu, w1_down, l)
        j = l // 2
        if l % 2 == 0:
            w_main, w_tail = _split_small(ab_w_in[j], 2 * W_LRU + 4 * W_MLSTM)
            proj, tail = _norm_proj(xf, norm_mix[l], w_main, w_tail)
            hl = _lru(proj, bsz, lru_conv_w[j], lru_conv_b[j], lru_gate_w[j], lru_gate_b[j], lru_lambda[j])
            hm = _mlstm(proj, tail, bsz, mlstm_gate_b[j])
            xf = _ab_out(xf, proj, hl, hm, mlstm_norm[j], ab_w_out[j].astype(BF16))
        else:
            w = cd_w_in[j]
            n_gla = 2 * GLA_QK + 2 * GLA_V
            w = jnp.concatenate([w[:, :n_gla], w[:, n_gla + 2 * GLA_RANK:], w[:, n_gla:n_gla + 2 * GLA_RANK]], axis=1)
            w_main, w_tail = _split_small(w, n_gla + S5_W)
            proj, tail = _norm_proj(xf, norm_mix[l], w_main, w_tail)
            og = _gla(proj, tail, bsz, gla_w_gate2[j], gla_gate_b[j])
            ops = _s5_operators(s5_a_re[j], s5_a_im[j], s5_log_dt[j], s5_b_re[j], s5_b_im[j],
                                s5_c_re[j], s5_c_im[j])
            ys = _s5(proj[:, n_gla:], bsz, ops)
            xf = _cd_out(xf, proj, og, gla_norm[j], ys, s5_d[j],
                         s5_w_glu[j].astype(BF16), cd_w_out[j].astype(BF16))
        xf = _ffn(xf, norm_ffn2[l], w2_gu, w2_down, l, final_g=final_norm if l == depth - 1 else None)
    return xf.reshape(bsz, seq, d)
```

```python
import functools

import jax
import jax.numpy as jnp
from jax import lax
from jax.experimental import pallas as pl
from jax.experimental.pallas import tpu as pltpu

NORM_EPS = 1e-6

W_LRU = 1024
LRU_HEADS = 8
LRU_BLOCK = W_LRU // LRU_HEADS
CONV_W = 4
CONV_PAD_L = 2
LRU_C = 8.0

W_MLSTM = 1024
MLSTM_HEADS = 4
MLSTM_DH = W_MLSTM // MLSTM_HEADS

GLA_HEADS = 4
GLA_DK = 128
GLA_DV = 256
GLA_QK = GLA_HEADS * GLA_DK
GLA_V = GLA_HEADS * GLA_DV
GLA_RANK = 16
GLA_TAU = 16.0

S5_W = 1024
S5_GROUP = 16
S5_GROUPS = S5_W // S5_GROUP
S5_P = 64
S5_CHUNK = 16
S5_GB = 8
S5_ROW_TILE = 256

LANE = 128
SUBLANE = 8
VMEM_LIMIT = 52 * 1024 * 1024

BF16 = jnp.bfloat16
F32 = jnp.float32
HI = lax.Precision.HIGHEST


def _rms(x, g):
    ms = jnp.mean(x * x, axis=-1, keepdims=True)
    return x * lax.rsqrt(ms + NORM_EPS) * g


def _softplus(z):
    return jnp.maximum(z, 0.0) + jnp.log1p(jnp.exp(-jnp.abs(z)))


def _params(*sem):
    return pltpu.CompilerParams(dimension_semantics=sem, vmem_limit_bytes=VMEM_LIMIT)


def _ffn_kernel(x_ref, g_ref, wg_ref, wu_ref, wd_ref, *rest, final):
    if final:
        fg_ref, o_ref, h_ref = rest
    else:
        o_ref, h_ref = rest
    j = pl.program_id(1)

    @pl.when(j == 0)
    def _():
        x = x_ref[...]
        h_ref[...] = _rms(x, g_ref[...]).astype(BF16)
        o_ref[...] = x

    h = h_ref[...]
    g = jnp.dot(h, wg_ref[...], preferred_element_type=F32)
    u = jnp.dot(h, wu_ref[...], preferred_element_type=F32)
    a = (0.5 * g * jax.nn.sigmoid(g) * u).astype(BF16)
    o_ref[...] += jnp.dot(a, wd_ref[...], preferred_element_type=F32)

    if final:
        @pl.when(j == pl.num_programs(1) - 1)
        def _():
            o_ref[...] = _rms(o_ref[...], fg_ref[...])


def _ffn(x, g, w_gu, w_down, layer, final_g=None, tm=512, tf=512):
    n, d = x.shape
    f = w_down.shape[1]
    tm = min(tm, n)
    nj = f // tf
    final = final_g is not None
    in_specs = [
        pl.BlockSpec((tm, d), lambda i, j: (i, 0)),
        pl.BlockSpec((1, d), lambda i, j: (0, 0)),
        pl.BlockSpec((None, d, tf), lambda i, j: (layer, 0, j)),
        pl.BlockSpec((None, d, tf), lambda i, j: (layer, 0, j + nj)),
        pl.BlockSpec((None, tf, d), lambda i, j: (layer, j, 0)),
    ]
    args = [x, g.reshape(1, d), w_gu, w_gu, w_down]
    if final:
        in_specs.append(pl.BlockSpec((1, d), lambda i, j: (0, 0)))
        args.append(final_g.reshape(1, d))
    return pl.pallas_call(
        functools.partial(_ffn_kernel, final=final),
        grid=(n // tm, nj),
        in_specs=in_specs,
        out_specs=pl.BlockSpec((tm, d), lambda i, j: (i, 0)),
        out_shape=jax.ShapeDtypeStruct((n, d), F32),
        scratch_shapes=[pltpu.VMEM((tm, d), BF16)],
        compiler_params=_params("arbitrary", "arbitrary"),
        name="ffn_final" if final else "ffn",
    )(*args)


def _norm_proj_kernel(x_ref, g_ref, w_ref, ws_ref, o_ref, os_ref, h_ref):
    @pl.when(pl.program_id(1) == 0)
    def _():
        h = _rms(x_ref[...], g_ref[...]).astype(BF16)
        h_ref[...] = h
        os_ref[...] = jnp.dot(h, ws_ref[...], preferred_element_type=F32)

    o_ref[...] = jnp.dot(h_ref[...], w_ref[...], preferred_element_type=F32)


def _norm_proj(x, g, w, w_small, tm=1024, tn=1024):
    n, d = x.shape
    m = w.shape[1]
    ms = w_small.shape[1]
    tm = min(tm, n)
    return pl.pallas_call(
        _norm_proj_kernel,
        grid=(n // tm, m // tn),
        in_specs=[
            pl.BlockSpec((tm, d), lambda i, j: (i, 0)),
            pl.BlockSpec((1, d), lambda i, j: (0, 0)),
            pl.BlockSpec((d, tn), lambda i, j: (0, j)),
            pl.BlockSpec((d, ms), lambda i, j: (0, 0)),
        ],
        out_specs=[pl.BlockSpec((tm, tn), lambda i, j: (i, j)),
                   pl.BlockSpec((tm, ms), lambda i, j: (i, 0))],
        out_shape=[jax.ShapeDtypeStruct((n, m), F32), jax.ShapeDtypeStruct((n, ms), F32)],
        scratch_shapes=[pltpu.VMEM((tm, d), BF16)],
        compiler_params=_params("arbitrary", "arbitrary"),
        name="norm_proj",
    )(x, g.reshape(1, d), w, w_small)


def _lru_kernel(xp_ref, x_ref, xn_ref, cw_ref, cb_ref, gw_ref, gb_ref, lam_ref, o_ref,
                xs_ref, a_ref, b_ref, h_ref, *, tt):
    d = pl.program_id(0)
    t = pl.program_id(2)
    nt = pl.num_programs(2)
    tb = jnp.where(d == 0, t, nt - 1 - t)
    w = x_ref.shape[1]

    xs_ref[0:SUBLANE, :] = jnp.where(tb > 0, xp_ref[...], 0.0)
    xs_ref[SUBLANE:SUBLANE + tt, :] = x_ref[...]
    xs_ref[SUBLANE + tt:2 * SUBLANE + tt, :] = jnp.where(tb < nt - 1, xn_ref[...], 0.0)
    cw = cw_ref[...]
    xc = cb_ref[...]
    for k in range(CONV_W):
        off = SUBLANE + k - CONV_PAD_L
        xc = xc + cw[k:k + 1, :] * xs_ref[off:off + tt, :]

    xcb = xc.astype(BF16)
    gb = gb_ref[0]
    sp = _softplus(-lam_ref[0])
    for hd in range(w // LRU_BLOCK):
        sl = slice(hd * LRU_BLOCK, (hd + 1) * LRU_BLOCK)
        z = jnp.dot(xcb[:, sl], gw_ref[0, hd], preferred_element_type=F32)
        r = jax.nn.sigmoid(z[:, :LRU_BLOCK] + gb[0:1, sl])
        i = jax.nn.sigmoid(z[:, LRU_BLOCK:] + gb[1:2, sl])
        log_a = (-LRU_C * sp[:, sl]) * r
        a = jnp.exp(log_a)
        a_ref[:, sl] = a
        b_ref[:, sl] = jnp.sqrt(-jnp.tanh(log_a) * (a * a + 1.0)) * (i * xc[:, sl])

    @pl.when(t == 0)
    def _():
        h_ref[...] = jnp.zeros_like(h_ref)

    sub = lax.broadcasted_iota(jnp.int32, (SUBLANE, w), 0)
    ntile = tt // SUBLANE

    def scan(rev):
        def earlier(x, k, fill):
            if rev:
                return jnp.where(sub < SUBLANE - k, pltpu.roll(x, SUBLANE - k, 0), fill)
            return jnp.where(sub >= k, pltpu.roll(x, k, 0), fill)

        def body(jt, h):
            r0 = pl.multiple_of((ntile - 1 - jt if rev else jt) * SUBLANE, SUBLANE)
            a = a_ref[pl.ds(r0, SUBLANE), :]
            b = b_ref[pl.ds(r0, SUBLANE), :]
            k = 1
            while k < SUBLANE:
                a, b = a * earlier(a, k, 1.0), b + a * earlier(b, k, 0.0)
                k *= 2
            tile = a * h + b
            o_ref[0, pl.ds(r0, SUBLANE), :] = tile
            last = 0 if rev else SUBLANE - 1
            return tile[last:last + 1, :]

        h_ref[...] = lax.fori_loop(0, ntile, body, h_ref[...])

    pl.when(d == 0)(functools.partial(scan, False))
    pl.when(d == 1)(functools.partial(scan, True))


def _lru(proj, bsz, conv_w, conv_b, gate_w, gate_b, lam, tt=256):
    n = proj.shape[0]
    seq = n // bsz
    tt = min(tt, seq)
    nt = seq // tt
    w = W_LRU
    r8 = tt // SUBLANE
    gw = jnp.concatenate([gate_w[:, 0], gate_w[:, 1]], axis=-1).astype(BF16)

    def tb(d, t):
        return t + d * (nt - 1 - 2 * t)

    return pl.pallas_call(
        functools.partial(_lru_kernel, tt=tt),
        grid=(2, bsz, nt),
        in_specs=[
            pl.BlockSpec((SUBLANE, w), lambda d, b, t: (jnp.maximum((b * nt + tb(d, t)) * r8 - 1, 0), 0)),
            pl.BlockSpec((tt, w), lambda d, b, t: (b * nt + tb(d, t), 0)),
            pl.BlockSpec((SUBLANE, w), lambda d, b, t: (jnp.minimum((b * nt + tb(d, t) + 1) * r8, n // SUBLANE - 1), 0)),
            pl.BlockSpec((CONV_W, w), lambda d, b, t: (0, 0)),
            pl.BlockSpec((1, w), lambda d, b, t: (0, 0)),
            pl.BlockSpec((1, LRU_HEADS, LRU_BLOCK, 2 * LRU_BLOCK), lambda d, b, t: (d, 0, 0, 0)),
            pl.BlockSpec((1, 2, w), lambda d, b, t: (d, 0, 0)),
            pl.BlockSpec((1, 1, w), lambda d, b, t: (d, 0, 0)),
        ],
        out_specs=pl.BlockSpec((1, tt, w), lambda d, b, t: (d, b * nt + tb(d, t), 0)),
        out_shape=jax.ShapeDtypeStruct((2, n, w), F32),
        scratch_shapes=[pltpu.VMEM((tt + 2 * SUBLANE, w), F32), pltpu.VMEM((tt, w), F32),
                        pltpu.VMEM((tt, w), F32), pltpu.VMEM((1, w), F32)],
        compiler_params=_params("arbitrary", "arbitrary", "arbitrary"),
        name="rg_lru",
    )(proj, proj, proj, conv_w, conv_b.reshape(1, w), gw, gate_b, lam.reshape(2, 1, w))


def _s5_discretise(a_re, a_im, log_dt, b_re, b_im):
    dt = jnp.exp(log_dt)[:, None]
    mag = jnp.exp(dt * a_re)
    lr = mag * jnp.cos(dt * a_im)
    li = mag * jnp.sin(dt * a_im)
    den = a_re * a_re + a_im * a_im
    nr = lr - 1.0
    cr = (nr * a_re + li * a_im) / den
    ci = (li * a_re - nr * a_im) / den
    bbr = cr[..., None] * b_re - ci[..., None] * b_im
    bbi = cr[..., None] * b_im + ci[..., None] * b_re
    return lr, li, bbr, bbi


def _s5_operators(a_re, a_im, log_dt, b_re, b_im, c_re, c_im):
    L = S5_CHUNK
    G, P, C = S5_GROUPS, S5_P, S5_GROUP
    kfs, mins, mouts, lams = [], [], [], []
    for d in range(2):
        lr, li, bbr, bbi = _s5_discretise(a_re[d], a_im[d], log_dt[d], b_re, b_im)
        bbr, bbi = bbr.transpose(0, 2, 1), bbi.transpose(0, 2, 1)
        jj = jnp.arange(L + 1, dtype=F32)[:, None, None, None]
        dt = jnp.exp(log_dt[d])[None, :, None, None]
        mag = jnp.exp(jj * dt * a_re[d][None, :, None, :])
        pr = mag * jnp.cos(jj * dt * a_im[d][None, :, None, :])
        pi = mag * jnp.sin(jj * dt * a_im[d][None, :, None, :])
        cl = jnp.concatenate([c_re[None] * pr - c_im[None] * pi, -(c_re[None] * pi + c_im[None] * pr)], axis=-1)
        bl = jnp.concatenate([pr * bbr[None] - pi * bbi[None], pr * bbi[None] + pi * bbr[None]], axis=-1)
        kfs.append(jnp.sum(cl[:L, :, None, :, :] * bl[0][None, :, :, None, :], axis=-1))
        if d == 0:
            e_in = jnp.arange(L - 1, -1, -1)
            e_out = jnp.arange(1, L + 1)
        else:
            e_in = jnp.arange(L)
            e_out = jnp.arange(L, 0, -1)
        mins.append(bl[e_in])
        mouts.append(cl[e_out])
        lams.append(jnp.stack([pr[L, :, 0], pi[L, :, 0]], axis=1))
    kf, kb = kfs
    kfull = jnp.concatenate([kb[:0:-1], (kf[0] + kb[0])[None], kf[1:]], axis=0)
    gb = S5_GB
    nb_ = G // gb
    k2 = kfull.reshape(2 * L - 1, nb_, gb * C, C).transpose(1, 0, 2, 3)

    def rows_of_block(parts):
        a = jnp.concatenate(parts, axis=-1).reshape(L, nb_, gb * C, 4 * P)
        return a.transpose(1, 0, 2, 3).reshape(nb_, L * gb * C, 4 * P)

    lam_l = jnp.concatenate(lams, axis=1)
    lam_blk = lam_l.reshape(nb_, gb, 4, P).transpose(0, 2, 1, 3).reshape(nb_, 1, 4 * gb * P)
    return k2, rows_of_block(mins), rows_of_block(mouts), lam_blk


def _s5_state_kernel(u_ref, min_ref, lam_ref, hin_ref, h_ref, *, nb):
    nl, rows, _ = u_ref.shape
    rb = rows // nb
    hw = h_ref.shape[1] // 4
    rt = min(rows, S5_ROW_TILE)
    lo = lax.broadcasted_iota(jnp.int32, (rt, LANE), 1) < S5_P
    ng = hw // S5_P

    def pairs():
        for d in range(2):
            for k in range(ng // 2):
                yield ((d * ng + 2 * k) * LANE, (d * ng + 2 * k + 1) * LANE,
                       2 * d * hw + k * LANE, (2 * d + 1) * hw + k * LANE)

    for r0 in range(0, rows, rt):
        u = jnp.concatenate([u_ref[s, r0:r0 + rt, :] for s in range(nl)], axis=1)
        hl = jnp.dot(u, min_ref[0], preferred_element_type=F32)
        for ca, cb, cre, cim in pairs():
            a, b = hl[:, ca:ca + LANE], hl[:, cb:cb + LANE]
            h_ref[r0:r0 + rt, cre:cre + LANE] = jnp.where(lo, a, pltpu.roll(b, S5_P, 1))
            h_ref[r0:r0 + rt, cim:cim + LANE] = jnp.where(lo, pltpu.roll(a, S5_P, 1), b)
    lam = lam_ref[0]
    lrf, lif = lam[:, 0:hw], lam[:, hw:2 * hw]
    lrb, lib = lam[:, 2 * hw:3 * hw], lam[:, 3 * hw:4 * hw]
    sub = lax.broadcasted_iota(jnp.int32, (SUBLANE, hw), 0)

    def body(it, carry):
        new = []
        for b in range(nb):
            fr, fi, br, bi = carry[4 * b:4 * b + 4]
            f0 = pl.multiple_of(b * rb + it * SUBLANE, SUBLANE)
            b0 = pl.multiple_of(b * rb + rb - SUBLANE - it * SUBLANE, SUBLANE)
            tiles = [jnp.zeros((SUBLANE, hw), F32)] * 4
            lf = h_ref[pl.ds(f0, SUBLANE), 0:2 * hw]
            lb = h_ref[pl.ds(b0, SUBLANE), 2 * hw:4 * hw]
            for r in range(SUBLANE):
                q = SUBLANE - 1 - r
                tiles = [jnp.where(sub == r, fr, tiles[0]), jnp.where(sub == r, fi, tiles[1]),
                         jnp.where(sub == q, br, tiles[2]), jnp.where(sub == q, bi, tiles[3])]
                fr, fi, br, bi = (lrf * fr - lif * fi + lf[r:r + 1, 0:hw],
                                  lrf * fi + lif * fr + lf[r:r + 1, hw:2 * hw],
                                  lrb * br - lib * bi + lb[q:q + 1, 0:hw],
                                  lrb * bi + lib * br + lb[q:q + 1, hw:2 * hw])
            h_ref[pl.ds(f0, SUBLANE), 0:hw] = tiles[0]
            h_ref[pl.ds(f0, SUBLANE), hw:2 * hw] = tiles[1]
            h_ref[pl.ds(b0, SUBLANE), 2 * hw:3 * hw] = tiles[2]
            h_ref[pl.ds(b0, SUBLANE), 3 * hw:4 * hw] = tiles[3]
            new += [fr, fi, br, bi]
        return tuple(new)

    zero = jnp.zeros((1, hw), F32)
    lax.fori_loop(0, rb // SUBLANE, body, (zero,) * (4 * nb))
    for r0 in range(0, rows, rt):
        for ca, cb, cre, cim in pairs():
            re, im = h_ref[r0:r0 + rt, cre:cre + LANE], h_ref[r0:r0 + rt, cim:cim + LANE]
            hin_ref[0, r0:r0 + rt, ca:ca + LANE] = jnp.where(lo, re, pltpu.roll(im, S5_P, 1)).astype(BF16)
            hin_ref[0, r0:r0 + rt, cb:cb + LANE] = jnp.where(lo, pltpu.roll(re, S5_P, 1), im).astype(BF16)


def _s5_expand_kernel(a_ref, o_ref, *, transpose):
    a = a_ref[0]
    rows = a.shape[0]
    row_g = (lax.broadcasted_iota(jnp.int32, (rows, LANE), 0) // S5_GROUP) % S5_GB
    for d in range(2):
        ad = a[:, d * LANE:(d + 1) * LANE]
        for g in range(S5_GB):
            tile = jnp.where(row_g == g, ad, 0.0)
            col = (d * S5_GB + g) * LANE
            if transpose:
                o_ref[0, col:col + LANE, :] = tile.T.astype(BF16)
            else:
                o_ref[0, :, col:col + LANE] = tile.astype(BF16)


def _s5_expand(a, transpose):
    nblk, rows, _ = a.shape
    sw = 2 * S5_GB * LANE
    shape = (nblk, sw, rows) if transpose else (nblk, rows, sw)
    return pl.pallas_call(
        functools.partial(_s5_expand_kernel, transpose=transpose),
        grid=(nblk,),
        in_specs=[pl.BlockSpec((1,) + a.shape[1:], lambda b: (b, 0, 0))],
        out_specs=pl.BlockSpec((1,) + shape[1:], lambda b: (b, 0, 0)),
        out_shape=jax.ShapeDtypeStruct(shape, BF16),
        compiler_params=_params("arbitrary"),
        name="s5_expand",
    )(a)


def _s5_out_kernel(u_ref, hin_ref, k2_ref, mout_ref, y_ref, t_ref):
    nl = u_ref.shape[0]

    @pl.when(pl.program_id(1) == 0)
    def _():
        row_g = lax.broadcasted_iota(jnp.int32, (LANE, LANE), 0) // S5_GROUP
        col_g = lax.broadcasted_iota(jnp.int32, (LANE, LANE), 1) // S5_GROUP
        spread = (lax.broadcasted_iota(jnp.int32, (S5_GROUP, LANE), 1) % S5_GROUP
                  == lax.broadcasted_iota(jnp.int32, (S5_GROUP, LANE), 0)).astype(BF16)
        for j in range(2 * nl - 1):
            rep = jnp.dot(k2_ref[0, j].astype(BF16), spread, preferred_element_type=F32)
            tile = jnp.where(row_g == col_g, rep, 0.0).astype(BF16)
            for s in range(nl):
                t = s + j - (nl - 1)
                if 0 <= t < nl:
                    t_ref[s * LANE:(s + 1) * LANE, t * LANE:(t + 1) * LANE] = tile

    u = jnp.concatenate([u_ref[s] for s in range(nl)], axis=1)
    y = (jnp.dot(u, t_ref[...], preferred_element_type=F32)
         + jnp.dot(hin_ref[0], mout_ref[0], preferred_element_type=F32))
    for t in range(nl):
        y_ref[t] = y[:, t * LANE:(t + 1) * LANE]


def _s5(u, bsz, ops, tr=256):
    k2, a_min, a_mout, lam_blk = ops
    min_blk = _s5_expand(a_min, False)
    mout_blk = _s5_expand(a_mout, True)
    n, w = u.shape
    L = S5_CHUNK
    rows = n // L
    nblk = w // LANE
    kw = L * LANE
    sw = lam_blk.shape[2]
    tr = min(tr, rows)
    u3 = u.reshape(rows, L, w).transpose(1, 0, 2).astype(BF16)
    hin = pl.pallas_call(
        functools.partial(_s5_state_kernel, nb=bsz),
        grid=(nblk,),
        in_specs=[
            pl.BlockSpec((L, rows, LANE), lambda b: (0, 0, b)),
            pl.BlockSpec((1, kw, sw), lambda b: (b, 0, 0)),
            pl.BlockSpec((1, 1, sw), lambda b: (b, 0, 0)),
        ],
        out_specs=pl.BlockSpec((1, rows, sw), lambda b: (b, 0, 0)),
        out_shape=jax.ShapeDtypeStruct((nblk, rows, sw), BF16),
        scratch_shapes=[pltpu.VMEM((rows, sw), F32)],
        compiler_params=_params("arbitrary"),
        name="s5_state",
    )(u3, min_blk, lam_blk)
    y3 = pl.pallas_call(
        _s5_out_kernel,
        grid=(nblk, rows // tr),
        in_specs=[
            pl.BlockSpec((L, tr, LANE), lambda b, i: (0, i, b)),
            pl.BlockSpec((1, tr, sw), lambda b, i: (b, i, 0)),
            pl.BlockSpec((1,) + k2.shape[1:], lambda b, i: (b, 0, 0, 0)),
            pl.BlockSpec((1, sw, kw), lambda b, i: (b, 0, 0)),
        ],
        out_specs=pl.BlockSpec((L, tr, LANE), lambda b, i: (0, i, b)),
        out_shape=jax.ShapeDtypeStruct((L, rows, w), F32),
        scratch_shapes=[pltpu.VMEM((kw, kw), BF16)],
        compiler_params=_params("arbitrary", "arbitrary"),
        name="s5_out",
    )(u3, hin, k2, mout_blk)
    return y3.transpose(1, 0, 2).reshape(n, w)


def _log_sigmoid(z):
    return -_softplus(-z)


def _tri(length, d):
    ti = lax.broadcasted_iota(jnp.int32, (length, length), 0)
    si = lax.broadcasted_iota(jnp.int32, (length, length), 1)
    return (ti - si) * (1 - 2 * d) >= 0


def _mlstm_kernel(q_ref, k_ref, v_ref, g_ref, gb_ref, o_ref, c_ref, n_ref, m_ref):
    d = pl.program_id(0)
    length = q_ref.shape[0]
    nh, dh = MLSTM_HEADS, MLSTM_DH

    @pl.when(pl.program_id(2) == 0)
    def _():
        c_ref[...] = jnp.zeros_like(c_ref)
        n_ref[...] = jnp.zeros_like(n_ref)
        m_ref[...] = jnp.zeros_like(m_ref)

    causal = _tri(length, d)
    gp = g_ref[...] + gb_ref[...]
    gp = jnp.where(d == 0, gp, pltpu.roll(gp, LANE - 2 * nh, 1))
    lf = _log_sigmoid(gp)
    cum = jnp.dot(causal.astype(F32), lf, precision=HI, preferred_element_type=F32)
    tot = jnp.sum(lf, axis=0, keepdims=True)
    gp_t = gp.T
    cum_t = cum.T
    nt = (((1,), (1,)), ((), ()))
    tn = (((0,), (0,)), ((), ()))

    for h in range(nh):
        sl = slice(h * dh, (h + 1) * dh)
        qh = (q_ref[:, sl] * (dh ** -0.5)).astype(BF16)
        kf = k_ref[:, sl]
        kh = kf.astype(BF16)
        vh = v_ref[:, sl].astype(BF16)
        ig_c, cum_c = gp[:, h:h + 1], cum[:, nh + h:nh + h + 1]
        ig_r, cum_r = gp_t[h:h + 1, :], cum_t[nh + h:nh + h + 1, :]
        tot_h = tot[:, nh + h:nh + h + 1]
        m_st = m_ref[h:h + 1, 0:1]
        c_st = c_ref[h]
        dmat = jnp.where(causal, cum_c - cum_r + ig_r, -jnp.inf)
        m_inter = cum_c + m_st
        m_t = jnp.maximum(jnp.max(dmat, axis=1, keepdims=True), m_inter)
        w_inter = jnp.exp(m_inter - m_t)
        s = lax.dot_general(qh, kh, nt, preferred_element_type=F32) * jnp.exp(dmat - m_t)
        num = (jnp.dot(s.astype(BF16), vh, preferred_element_type=F32)
               + w_inter * jnp.dot(qh, c_st.astype(BF16), preferred_element_type=F32))
        den = (jnp.sum(s, axis=1, keepdims=True)
               + w_inter * jnp.sum(qh.astype(F32) * n_ref[h:h + 1, :], axis=1, keepdims=True))
        o_ref[0, :, sl] = num / jnp.maximum(jnp.abs(den), jnp.exp(-m_t))
        dec = tot_h - cum_c + ig_c
        m_new = jnp.maximum(tot_h + m_st, jnp.max(dec, axis=0, keepdims=True))
        wc = jnp.exp(tot_h + m_st - m_new)
        kw = jnp.exp(dec - m_new) * kf
        c_ref[h] = wc * c_st + lax.dot_general(kw.astype(BF16), vh, tn, preferred_element_type=F32)
        n_ref[h:h + 1, :] = wc * n_ref[h:h + 1, :] + jnp.sum(kw, axis=0, keepdims=True)
        m_ref[h:h + 1, :] = jnp.broadcast_to(m_new, (1, LANE))


def _mlstm(proj, tail, bsz, gate_b, length=256):
    n = proj.shape[0]
    seq = n // bsz
    length = min(length, seq)
    nc = seq // length
    w = W_MLSTM
    q_blk = 2 * W_LRU // w
    gb = jnp.pad(gate_b.reshape(1, -1), ((0, 0), (0, LANE - gate_b.size)))

    def row(d, b, c):
        return b * nc + c + d * (nc - 1 - 2 * c)

    return pl.pallas_call(
        _mlstm_kernel,
        grid=(2, bsz, nc),
        in_specs=[
            pl.BlockSpec((length, w), lambda d, b, c: (row(d, b, c), q_blk)),
            pl.BlockSpec((length, w), lambda d, b, c: (row(d, b, c), q_blk + 1)),
            pl.BlockSpec((length, w), lambda d, b, c: (row(d, b, c), q_blk + 2)),
            pl.BlockSpec((length, LANE), lambda d, b, c: (row(d, b, c), 0)),
            pl.BlockSpec((1, LANE), lambda d, b, c: (0, 0)),
        ],
        out_specs=pl.BlockSpec((1, length, w), lambda d, b, c: (d, row(d, b, c), 0)),
        out_shape=jax.ShapeDtypeStruct((2, n, w), F32),
        scratch_shapes=[pltpu.VMEM((MLSTM_HEADS, MLSTM_DH, MLSTM_DH), F32),
                        pltpu.VMEM((SUBLANE, MLSTM_DH), F32), pltpu.VMEM((SUBLANE, LANE), F32)],
        compiler_params=_params("arbitrary", "arbitrary", "arbitrary"),
        name="mlstm",
    )(proj, proj, proj, tail, gb)


GLA_SUB = 16


def _gla_chunk(q_ref, k_ref, v_ref, cum, o_ref, s_ref, rev):
    length = q_ref.shape[0]
    c = GLA_SUB
    nt = (((1,), (1,)), ((), ()))
    tn = (((0,), (0,)), ((), ()))

    def rows(a, b):
        return slice(length - b, length - a) if rev else slice(a, b)

    def row(a):
        i = length - 1 - a if rev else a
        return slice(i, i + 1)

    ti = lax.broadcasted_iota(jnp.int32, (c, 1), 0)
    for h in range(GLA_HEADS):
        kl = slice(h * GLA_DK, (h + 1) * GLA_DK)
        vl = slice(h * GLA_DV, (h + 1) * GLA_DV)
        ch = cum[:, kl]
        qs = q_ref[:, kl] * (GLA_DK ** -0.5)
        kk = k_ref[:, kl]
        vf = v_ref[:, vl]
        vv = vf.astype(BF16)
        st = s_ref[h]
        tot = ch[row(length - 1)]
        o = lax.dot_general((qs * jnp.exp(ch)).astype(BF16), st.astype(BF16), nt, preferred_element_type=F32)
        ob = [o[rows(p * c, (p + 1) * c)] for p in range(length // c)]
        m = length // 2
        while m >= c:
            for start in range(0, length, 2 * m):
                fst, sec = rows(start, start + m), rows(start + m, start + 2 * m)
                r = ch[row(start + m)]
                qh = (qs[sec] * jnp.exp(ch[sec] - r)).astype(BF16)
                kh = (kk[fst] * jnp.exp(r - ch[fst])).astype(BF16)
                att = lax.dot_general(qh, kh, nt, preferred_element_type=F32)
                contrib = jnp.dot(att.astype(BF16), vv[fst], preferred_element_type=F32)
                for p in range((start + m) // c, (start + 2 * m) // c):
                    lo = rows(p * c, (p + 1) * c).start - sec.start
                    ob[p] = ob[p] + contrib[lo:lo + c]
            m //= 2
        for p in range(length // c):
            blk = rows(p * c, (p + 1) * c)
            cb, qb, kb, vb = ch[blk], qs[blk], kk[blk], vf[blk]
            acc = ob[p]
            for s in range(c):
                sees = (ti <= s) if rev else (ti >= s)
                e = jnp.exp(jnp.where(sees, cb - cb[s:s + 1], -jnp.inf))
                a = jnp.sum(qb * kb[s:s + 1] * e, axis=1, keepdims=True)
                acc = acc + a * vb[s:s + 1]
            o_ref[0, blk, vl] = acc
        kd = (kk * jnp.exp(tot - ch)).astype(BF16)
        s_ref[h] = jnp.exp(tot) * st + lax.dot_general(vv, kd, tn, preferred_element_type=F32)


def _gla_kernel(q_ref, k_ref, v_ref, low_ref, wg_ref, bg_ref, o_ref, s_ref):
    d = pl.program_id(0)
    length = q_ref.shape[0]

    @pl.when(pl.program_id(2) == 0)
    def _():
        s_ref[...] = jnp.zeros_like(s_ref)

    gate_pre = jnp.dot(low_ref[...], wg_ref[0], precision=HI, preferred_element_type=F32) + bg_ref[0]
    la = _log_sigmoid(gate_pre) * (1.0 / GLA_TAU)
    cum = jnp.dot(_tri(length, d).astype(F32), la, precision=HI, preferred_element_type=F32)
    pl.when(d == 0)(lambda: _gla_chunk(q_ref, k_ref, v_ref, cum, o_ref, s_ref, False))
    pl.when(d == 1)(lambda: _gla_chunk(q_ref, k_ref, v_ref, cum, o_ref, s_ref, True))


def _gla(proj, tail, bsz, w_gate2, gate_b, length=128):
    n = proj.shape[0]
    seq = n // bsz
    length = min(length, seq)
    nc = seq // length
    wg = jnp.zeros((2, LANE, GLA_QK), F32)
    for d in range(2):
        wg = wg.at[d, d * GLA_RANK:(d + 1) * GLA_RANK].set(w_gate2[d])

    def row(d, b, c):
        return b * nc + c + d * (nc - 1 - 2 * c)

    return pl.pallas_call(
        _gla_kernel,
        grid=(2, bsz, nc),
        in_specs=[
            pl.BlockSpec((length, GLA_QK), lambda d, b, c: (row(d, b, c), 0)),
            pl.BlockSpec((length, GLA_QK), lambda d, b, c: (row(d, b, c), 1)),
            pl.BlockSpec((length, GLA_V), lambda d, b, c: (row(d, b, c), 2 * GLA_QK // GLA_V)),
            pl.BlockSpec((length, LANE), lambda d, b, c: (row(d, b, c), 0)),
            pl.BlockSpec((1, LANE, GLA_QK), lambda d, b, c: (d, 0, 0)),
            pl.BlockSpec((1, 1, GLA_QK), lambda d, b, c: (d, 0, 0)),
        ],
        out_specs=pl.BlockSpec((1, length, GLA_V), lambda d, b, c: (d, row(d, b, c), 0)),
        out_shape=jax.ShapeDtypeStruct((2, n, GLA_V), F32),
        scratch_shapes=[pltpu.VMEM((GLA_HEADS, GLA_DV, GLA_DK), F32)],
        compiler_params=_params("arbitrary", "arbitrary", "arbitrary"),
        name="gla",
    )(proj, proj, proj, tail, wg, gate_b.reshape(2, 1, GLA_QK))


def _head_norm(t, g, heads):
    dh = t.shape[1] // heads
    outs = []
    for h in range(heads):
        th = t[:, h * dh:(h + 1) * dh]
        outs.append(th * lax.rsqrt(jnp.mean(th * th, axis=1, keepdims=True) + NORM_EPS))
    return jnp.concatenate(outs, axis=1) * g


def _ab_out_kernel(x_ref, gr_ref, og_ref, hl_ref, hm_ref, g_ref, w_ref, o_ref):
    k = gr_ref.shape[1]
    ya = jax.nn.gelu(gr_ref[...]) * (hl_ref[0] + hl_ref[1])
    yb = jax.nn.sigmoid(og_ref[...]) * _head_norm(hm_ref[0] + hm_ref[1], g_ref[...], MLSTM_HEADS)
    o_ref[...] = (x_ref[...]
                  + jnp.dot(ya.astype(BF16), w_ref[0:k, :], preferred_element_type=F32)
                  + jnp.dot(yb.astype(BF16), w_ref[k:, :], preferred_element_type=F32))


def _ab_out(x, proj, hl, hm, norm_g, w, tm=512):
    n, d = x.shape
    k = W_LRU
    tm = min(tm, n)
    return pl.pallas_call(
        _ab_out_kernel,
        grid=(n // tm,),
        in_specs=[
            pl.BlockSpec((tm, d), lambda i: (i, 0)),
            pl.BlockSpec((tm, k), lambda i: (i, 1)),
            pl.BlockSpec((tm, k), lambda i: (i, 5)),
            pl.BlockSpec((2, tm, k), lambda i: (0, i, 0)),
            pl.BlockSpec((2, tm, k), lambda i: (0, i, 0)),
            pl.BlockSpec((1, k), lambda i: (0, 0)),
            pl.BlockSpec((2 * k, d), lambda i: (0, 0)),
        ],
        out_specs=pl.BlockSpec((tm, d), lambda i: (i, 0)),
        out_shape=jax.ShapeDtypeStruct((n, d), F32),
        compiler_params=_params("arbitrary"),
        name="ab_out",
    )(x, proj, proj, hl, hm, norm_g.reshape(1, k), w)


def _cd_out_kernel(x_ref, r_ref, og_ref, g_ref, ys_ref, u_ref, d_ref, wg_ref, w_ref, o_ref):
    k = r_ref.shape[1]
    r = r_ref[...]
    yc = _head_norm(og_ref[0] + og_ref[1], g_ref[...], GLA_HEADS) * (r * jax.nn.sigmoid(r))
    y = jax.nn.gelu(ys_ref[...] + d_ref[...] * u_ref[...])
    yd = y * jax.nn.sigmoid(jnp.dot(y.astype(BF16), wg_ref[...], preferred_element_type=F32))
    o_ref[...] = (x_ref[...]
                  + jnp.dot(yc.astype(BF16), w_ref[0:k, :], preferred_element_type=F32)
                  + jnp.dot(yd.astype(BF16), w_ref[k:, :], preferred_element_type=F32))


def _cd_out(x, proj, og, norm_g, ys, s5_d, w_glu, w, tm=512):
    n, d = x.shape
    k = S5_W
    tm = min(tm, n)
    return pl.pallas_call(
        _cd_out_kernel,
        grid=(n // tm,),
        in_specs=[
            pl.BlockSpec((tm, d), lambda i: (i, 0)),
            pl.BlockSpec((tm, k), lambda i: (i, 2)),
            pl.BlockSpec((2, tm, k), lambda i: (0, i, 0)),
            pl.BlockSpec((1, k), lambda i: (0, 0)),
            pl.BlockSpec((tm, k), lambda i: (i, 0)),
            pl.BlockSpec((tm, k), lambda i: (i, 3)),
            pl.BlockSpec((1, k), lambda i: (0, 0)),
            pl.BlockSpec((k, k), lambda i: (0, 0)),
            pl.BlockSpec((2 * k, d), lambda i: (0, 0)),
        ],
        out_specs=pl.BlockSpec((tm, d), lambda i: (i, 0)),
        out_shape=jax.ShapeDtypeStruct((n, d), F32),
        compiler_params=_params("arbitrary"),
        name="cd_out",
    )(x, proj, og, norm_g.reshape(1, k), ys, proj, s5_d.reshape(1, k), w_glu, w)


def _split_small(w, n_main):
    tail = w[:, n_main:]
    tail = jnp.pad(tail, ((0, 0), (0, LANE - tail.shape[1])))
    return w[:, :n_main].astype(BF16), tail.astype(BF16)


def kernel(x, norm_ffn1, ffn1_w_gu, ffn1_w_down, norm_mix, norm_ffn2, ffn2_w_gu, ffn2_w_down,
           ab_w_in, lru_conv_w, lru_conv_b, lru_gate_w, lru_gate_b, lru_lambda, mlstm_gate_b,
           mlstm_norm, ab_w_out, cd_w_in, gla_w_gate2, gla_gate_b, gla_norm, s5_a_re, s5_a_im,
           s5_log_dt, s5_b_re, s5_b_im, s5_c_re, s5_c_im, s5_d, s5_w_glu, cd_w_out, final_norm):
    bsz, seq, d = x.shape
    n = bsz * seq
    depth = norm_ffn1.shape[0]
    xf = x.reshape(n, d)
    w1_gu, w1_down = ffn1_w_gu.astype(BF16), ffn1_w_down.astype(BF16)
    w2_gu, w2_down = ffn2_w_gu.astype(BF16), ffn2_w_down.astype(BF16)
    for l in range(depth):
        xf = _ffn(xf, norm_ffn1[l], w1_gu, w1_down, l)
        j = l // 2
        if l % 2 == 0:
            w_main, w_tail = _split_small(ab_w_in[j], 2 * W_LRU + 4 * W_MLSTM)
            proj, tail = _norm_proj(xf, norm_mix[l], w_main, w_tail)
            hl = _lru(proj, bsz, lru_conv_w[j], lru_conv_b[j], lru_gate_w[j], lru_gate_b[j], lru_lambda[j])
            hm = _mlstm(proj, tail, bsz, mlstm_gate_b[j])
            xf = _ab_out(xf, proj, hl, hm, mlstm_norm[j], ab_w_out[j].astype(BF16))
        else:
            w = cd_w_in[j]
            n_gla = 2 * GLA_QK + 2 * GLA_V
            w = jnp.concatenate([w[:, :n_gla], w[:, n_gla + 2 * GLA_RANK:], w[:, n_gla:n_gla + 2 * GLA_RANK]], axis=1)
            w_main, w_tail = _split_small(w, n_gla + S5_W)
            proj, tail = _norm_proj(xf, norm_mix[l], w_main, w_tail)
            og = _gla(proj, tail, bsz, gla_w_gate2[j], gla_gate_b[j])
            ops = _s5_operators(s5_a_re[j], s5_a_im[j], s5_log_dt[j], s5_b_re[j], s5_b_im[j],
                                s5_c_re[j], s5_c_im[j])
            ys = _s5(proj[:, n_gla:], bsz, ops)
            xf = _cd_out(xf, proj, og, gla_norm[j], ys, s5_d[j],
                         s5_w_glu[j].astype(BF16), cd_w_out[j].astype(BF16))
        xf = _ffn(xf, norm_ffn2[l], w2_gu, w2_down, l, final_g=final_norm if l == depth - 1 else None)
    return xf.reshape(bsz, seq, d)
```

```python
import functools

import jax
import jax.numpy as jnp
from jax import lax
from jax.experimental import pallas as pl
from jax.experimental.pallas import tpu as pltpu

NORM_EPS = 1e-6

W_LRU = 1024
LRU_HEADS = 8
LRU_BLOCK = W_LRU // LRU_HEADS
CONV_W = 4
CONV_PAD_L = 2
LRU_C = 8.0

W_MLSTM = 1024
MLSTM_HEADS = 4
MLSTM_DH = W_MLSTM // MLSTM_HEADS

GLA_HEADS = 4
GLA_DK = 128
GLA_DV = 256
GLA_QK = GLA_HEADS * GLA_DK
GLA_V = GLA_HEADS * GLA_DV
GLA_RANK = 16
GLA_TAU = 16.0

S5_W = 1024
S5_GROUP = 16
S5_GROUPS = S5_W // S5_GROUP
S5_P = 64
S5_CHUNK = 16
S5_GB = 8
S5_ROW_TILE = 256

LANE = 128
SUBLANE = 8
VMEM_LIMIT = 52 * 1024 * 1024

BF16 = jnp.bfloat16
F32 = jnp.float32
HI = lax.Precision.HIGHEST


def _rms(x, g):
    ms = jnp.mean(x * x, axis=-1, keepdims=True)
    return x * lax.rsqrt(ms + NORM_EPS) * g


def _softplus(z):
    return jnp.maximum(z, 0.0) + jnp.log1p(jnp.exp(-jnp.abs(z)))


def _params(*sem):
    return pltpu.CompilerParams(dimension_semantics=sem, vmem_limit_bytes=VMEM_LIMIT)


def _ffn_kernel(x_ref, g_ref, wg_ref, wu_ref, wd_ref, *rest, final):
    if final:
        fg_ref, o_ref, h_ref = rest
    else:
        o_ref, h_ref = rest
    j = pl.program_id(1)

    @pl.when(j == 0)
    def _():
        x = x_ref[...]
        h_ref[...] = _rms(x, g_ref[...]).astype(BF16)
        o_ref[...] = x

    h = h_ref[...]
    g = jnp.dot(h, wg_ref[...], preferred_element_type=F32)
    u = jnp.dot(h, wu_ref[...], preferred_element_type=F32)
    a = (0.5 * g * jax.nn.sigmoid(g) * u).astype(BF16)
    o_ref[...] += jnp.dot(a, wd_ref[...], preferred_element_type=F32)

    if final:
        @pl.when(j == pl.num_programs(1) - 1)
        def _():
            o_ref[...] = _rms(o_ref[...], fg_ref[...])


def _ffn(x, g, w_gu, w_down, layer, final_g=None, tm=512, tf=512):
    n, d = x.shape
    f = w_down.shape[1]
    tm = min(tm, n)
    nj = f // tf
    final = final_g is not None
    in_specs = [
        pl.BlockSpec((tm, d), lambda i, j: (i, 0)),
        pl.BlockSpec((1, d), lambda i, j: (0, 0)),
        pl.BlockSpec((None, d, tf), lambda i, j: (layer, 0, j)),
        pl.BlockSpec((None, d, tf), lambda i, j: (layer, 0, j + nj)),
        pl.BlockSpec((None, tf, d), lambda i, j: (layer, j, 0)),
    ]
    args = [x, g.reshape(1, d), w_gu, w_gu, w_down]
    if final:
        in_specs.append(pl.BlockSpec((1, d), lambda i, j: (0, 0)))
        args.append(final_g.reshape(1, d))
    return pl.pallas_call(
        functools.partial(_ffn_kernel, final=final),
        grid=(n // tm, nj),
        in_specs=in_specs,
        out_specs=pl.BlockSpec((tm, d), lambda i, j: (i, 0)),
        out_shape=jax.ShapeDtypeStruct((n, d), F32),
        scratch_shapes=[pltpu.VMEM((tm, d), BF16)],
        compiler_params=_params("arbitrary", "arbitrary"),
        name="ffn_final" if final else "ffn",
    )(*args)


def _norm_proj_kernel(x_ref, g_ref, w_ref, ws_ref, o_ref, os_ref, h_ref):
    @pl.when(pl.program_id(1) == 0)
    def _():
        h = _rms(x_ref[...], g_ref[...]).astype(BF16)
        h_ref[...] = h
        os_ref[...] = jnp.dot(h, ws_ref[...], preferred_element_type=F32)

    o_ref[...] = jnp.dot(h_ref[...], w_ref[...], preferred_element_type=F32)


def _norm_proj(x, g, w, w_small, tm=1024, tn=1024):
    n, d = x.shape
    m = w.shape[1]
    ms = w_small.shape[1]
    tm = min(tm, n)
    return pl.pallas_call(
        _norm_proj_kernel,
        grid=(n // tm, m // tn),
        in_specs=[
            pl.BlockSpec((tm, d), lambda i, j: (i, 0)),
            pl.BlockSpec((1, d), lambda i, j: (0, 0)),
            pl.BlockSpec((d, tn), lambda i, j: (0, j)),
            pl.BlockSpec((d, ms), lambda i, j: (0, 0)),
        ],
        out_specs=[pl.BlockSpec((tm, tn), lambda i, j: (i, j)),
                   pl.BlockSpec((tm, ms), lambda i, j: (i, 0))],
        out_shape=[jax.ShapeDtypeStruct((n, m), F32), jax.ShapeDtypeStruct((n, ms), F32)],
        scratch_shapes=[pltpu.VMEM((tm, d), BF16)],
        compiler_params=_params("arbitrary", "arbitrary"),
        name="norm_proj",
    )(x, g.reshape(1, d), w, w_small)


def _lru_kernel(xp_ref, x_ref, xn_ref, cw_ref, cb_ref, gw_ref, gb_ref, lam_ref, o_ref,
                xs_ref, a_ref, b_ref, h_ref, *, tt):
    d = pl.program_id(0)
    t = pl.program_id(2)
    nt = pl.num_programs(2)
    tb = jnp.where(d == 0, t, nt - 1 - t)
    w = x_ref.shape[1]

    xs_ref[0:SUBLANE, :] = jnp.where(tb > 0, xp_ref[...], 0.0)
    xs_ref[SUBLANE:SUBLANE + tt, :] = x_ref[...]
    xs_ref[SUBLANE + tt:2 * SUBLANE + tt, :] = jnp.where(tb < nt - 1, xn_ref[...], 0.0)
    cw = cw_ref[...]
    xc = cb_ref[...]
    for k in range(CONV_W):
        off = SUBLANE + k - CONV_PAD_L
        xc = xc + cw[k:k + 1, :] * xs_ref[off:off + tt, :]

    xcb = xc.astype(BF16)
    gb = gb_ref[0]
    sp = _softplus(-lam_ref[0])
    for hd in range(w // LRU_BLOCK):
        sl = slice(hd * LRU_BLOCK, (hd + 1) * LRU_BLOCK)
        z = jnp.dot(xcb[:, sl], gw_ref[0, hd], preferred_element_type=F32)
        r = jax.nn.sigmoid(z[:, :LRU_BLOCK] + gb[0:1, sl])
        i = jax.nn.sigmoid(z[:, LRU_BLOCK:] + gb[1:2, sl])
        log_a = (-LRU_C * sp[:, sl]) * r
        a = jnp.exp(log_a)
        a_ref[:, sl] = a
        b_ref[:, sl] = jnp.sqrt(-jnp.tanh(log_a) * (a * a + 1.0)) * (i * xc[:, sl])

    @pl.when(t == 0)
    def _():
        h_ref[...] = jnp.zeros_like(h_ref)

    sub = lax.broadcasted_iota(jnp.int32, (SUBLANE, w), 0)
    ntile = tt // SUBLANE

    def scan(rev):
        def earlier(x, k, fill):
            if rev:
                return jnp.where(sub < SUBLANE - k, pltpu.roll(x, SUBLANE - k, 0), fill)
            return jnp.where(sub >= k, pltpu.roll(x, k, 0), fill)

        def body(jt, h):
            r0 = pl.multiple_of((ntile - 1 - jt if rev else jt) * SUBLANE, SUBLANE)
            a = a_ref[pl.ds(r0, SUBLANE), :]
            b = b_ref[pl.ds(r0, SUBLANE), :]
            k = 1
            while k < SUBLANE:
                a, b = a * earlier(a, k, 1.0), b + a * earlier(b, k, 0.0)
                k *= 2
            tile = a * h + b
            o_ref[0, pl.ds(r0, SUBLANE), :] = tile
            last = 0 if rev else SUBLANE - 1
            return tile[last:last + 1, :]

        h_ref[...] = lax.fori_loop(0, ntile, body, h_ref[...])

    pl.when(d == 0)(functools.partial(scan, False))
    pl.when(d == 1)(functools.partial(scan, True))


def _lru(proj, bsz, conv_w, conv_b, gate_w, gate_b, lam, tt=256):
    n = proj.shape[0]
    seq = n // bsz
    tt = min(tt, seq)
    nt = seq // tt
    w = W_LRU
    r8 = tt // SUBLANE
    gw = jnp.concatenate([gate_w[:, 0], gate_w[:, 1]], axis=-1).astype(BF16)

    def tb(d, t):
        return t + d * (nt - 1 - 2 * t)

    return pl.pallas_call(
        functools.partial(_lru_kernel, tt=tt),
        grid=(2, bsz, nt),
        in_specs=[
            pl.BlockSpec((SUBLANE, w), lambda d, b, t: (jnp.maximum((b * nt + tb(d, t)) * r8 - 1, 0), 0)),
            pl.BlockSpec((tt, w), lambda d, b, t: (b * nt + tb(d, t), 0)),
            pl.BlockSpec((SUBLANE, w), lambda d, b, t: (jnp.minimum((b * nt + tb(d, t) + 1) * r8, n // SUBLANE - 1), 0)),
            pl.BlockSpec((CONV_W, w), lambda d, b, t: (0, 0)),
            pl.BlockSpec((1, w), lambda d, b, t: (0, 0)),
            pl.BlockSpec((1, LRU_HEADS, LRU_BLOCK, 2 * LRU_BLOCK), lambda d, b, t: (d, 0, 0, 0)),
            pl.BlockSpec((1, 2, w), lambda d, b, t: (d, 0, 0)),
            pl.BlockSpec((1, 1, w), lambda d, b, t: (d, 0, 0)),
        ],
        out_specs=pl.BlockSpec((1, tt, w), lambda d, b, t: (d, b * nt + tb(d, t), 0)),
        out_shape=jax.ShapeDtypeStruct((2, n, w), F32),
        scratch_shapes=[pltpu.VMEM((tt + 2 * SUBLANE, w), F32), pltpu.VMEM((tt, w), F32),
                        pltpu.VMEM((tt, w), F32), pltpu.VMEM((1, w), F32)],
        compiler_params=_params("arbitrary", "arbitrary", "arbitrary"),
        name="rg_lru",
    )(proj, proj, proj, conv_w, conv_b.reshape(1, w), gw, gate_b, lam.reshape(2, 1, w))


def _s5_discretise(a_re, a_im, log_dt, b_re, b_im):
    dt = jnp.exp(log_dt)[:, None]
    mag = jnp.exp(dt * a_re)
    lr = mag * jnp.cos(dt * a_im)
    li = mag * jnp.sin(dt * a_im)
    den = a_re * a_re + a_im * a_im
    nr = lr - 1.0
    cr = (nr * a_re + li * a_im) / den
    ci = (li * a_re - nr * a_im) / den
    bbr = cr[..., None] * b_re - ci[..., None] * b_im
    bbi = cr[..., None] * b_im + ci[..., None] * b_re
    return lr, li, bbr, bbi


def _s5_operators(a_re, a_im, log_dt, b_re, b_im, c_re, c_im):
    L = S5_CHUNK
    G, P, C = S5_GROUPS, S5_P, S5_GROUP
    kfs, mins, mouts, lams = [], [], [], []
    for d in range(2):
        lr, li, bbr, bbi = _s5_discretise(a_re[d], a_im[d], log_dt[d], b_re, b_im)
        bbr, bbi = bbr.transpose(0, 2, 1), bbi.transpose(0, 2, 1)
        jj = jnp.arange(L + 1, dtype=F32)[:, None, None, None]
        dt = jnp.exp(log_dt[d])[None, :, None, None]
        mag = jnp.exp(jj * dt * a_re[d][None, :, None, :])
        pr = mag * jnp.cos(jj * dt * a_im[d][None, :, None, :])
        pi = mag * jnp.sin(jj * dt * a_im[d][None, :, None, :])
        cl = jnp.concatenate([c_re[None] * pr - c_im[None] * pi, -(c_re[None] * pi + c_im[None] * pr)], axis=-1)
        bl = jnp.concatenate([pr * bbr[None] - pi * bbi[None], pr * bbi[None] + pi * bbr[None]], axis=-1)
        kfs.append(jnp.sum(cl[:L, :, None, :, :] * bl[0][None, :, :, None, :], axis=-1))
        if d == 0:
            e_in = jnp.arange(L - 1, -1, -1)
            e_out = jnp.arange(1, L + 1)
        else:
            e_in = jnp.arange(L)
            e_out = jnp.arange(L, 0, -1)
        mins.append(bl[e_in])
        mouts.append(cl[e_out])
        lams.append(jnp.stack([pr[L, :, 0], pi[L, :, 0]], axis=1))
    kf, kb = kfs
    kfull = jnp.concatenate([kb[:0:-1], (kf[0] + kb[0])[None], kf[1:]], axis=0)
    gb = S5_GB
    nb_ = G // gb
    k2 = kfull.reshape(2 * L - 1, nb_, gb * C, C).transpose(1, 0, 2, 3)

    def rows_of_block(parts):
        a = jnp.concatenate(parts, axis=-1).reshape(L, nb_, gb * C, 4 * P)
        return a.transpose(1, 0, 2, 3).reshape(nb_, L * gb * C, 4 * P)

    lam_l = jnp.concatenate(lams, axis=1)
    lam_blk = lam_l.reshape(nb_, gb, 4, P).transpose(0, 2, 1, 3).reshape(nb_, 1, 4 * gb * P)
    return k2, rows_of_block(mins), rows_of_block(mouts), lam_blk


def _s5_state_kernel(u_ref, min_ref, lam_ref, hin_ref, h_ref, *, nb):
    nl, rows, _ = u_ref.shape
    rb = rows // nb
    hw = h_ref.shape[1] // 4
    rt = min(rows, S5_ROW_TILE)
    lo = lax.broadcasted_iota(jnp.int32, (rt, LANE), 1) < S5_P
    ng = hw // S5_P

    def pairs():
        for d in range(2):
            for k in range(ng // 2):
                yield ((d * ng + 2 * k) * LANE, (d * ng + 2 * k + 1) * LANE,
                       2 * d * hw + k * LANE, (2 * d + 1) * hw + k * LANE)

    for r0 in range(0, rows, rt):
        u = jnp.concatenate([u_ref[s, r0:r0 + rt, :] for s in range(nl)], axis=1)
        hl = jnp.dot(u, min_ref[0], preferred_element_type=F32)
        for ca, cb, cre, cim in pairs():
            a, b = hl[:, ca:ca + LANE], hl[:, cb:cb + LANE]
            h_ref[r0:r0 + rt, cre:cre + LANE] = jnp.where(lo, a, pltpu.roll(b, S5_P, 1))
            h_ref[r0:r0 + rt, cim:cim + LANE] = jnp.where(lo, pltpu.roll(a, S5_P, 1), b)
    lam = lam_ref[0]
    lrf, lif = lam[:, 0:hw], lam[:, hw:2 * hw]
    lrb, lib = lam[:, 2 * hw:3 * hw], lam[:, 3 * hw:4 * hw]
    sub = lax.broadcasted_iota(jnp.int32, (SUBLANE, hw), 0)

    def body(it, carry):
        new = []
        for b in range(nb):
            fr, fi, br, bi = carry[4 * b:4 * b + 4]
            f0 = pl.multiple_of(b * rb + it * SUBLANE, SUBLANE)
            b0 = pl.multiple_of(b * rb + rb - SUBLANE - it * SUBLANE, SUBLANE)
            tiles = [jnp.zeros((SUBLANE, hw), F32)] * 4
            lf = h_ref[pl.ds(f0, SUBLANE), 0:2 * hw]
            lb = h_ref[pl.ds(b0, SUBLANE), 2 * hw:4 * hw]
            for r in range(SUBLANE):
                q = SUBLANE - 1 - r
                tiles = [jnp.where(sub == r, fr, tiles[0]), jnp.where(sub == r, fi, tiles[1]),
                         jnp.where(sub == q, br, tiles[2]), jnp.where(sub == q, bi, tiles[3])]
                fr, fi, br, bi = (lrf * fr - lif * fi + lf[r:r + 1, 0:hw],
                                  lrf * fi + lif * fr + lf[r:r + 1, hw:2 * hw],
                                  lrb * br - lib * bi + lb[q:q + 1, 0:hw],
                                  lrb * bi + lib * br + lb[q:q + 1, hw:2 * hw])
            h_ref[pl.ds(f0, SUBLANE), 0:hw] = tiles[0]
            h_ref[pl.ds(f0, SUBLANE), hw:2 * hw] = tiles[1]
            h_ref[pl.ds(b0, SUBLANE), 2 * hw:3 * hw] = tiles[2]
            h_ref[pl.ds(b0, SUBLANE), 3 * hw:4 * hw] = tiles[3]
            new += [fr, fi, br, bi]
        return tuple(new)

    zero = jnp.zeros((1, hw), F32)
    lax.fori_loop(0, rb // SUBLANE, body, (zero,) * (4 * nb))
    for r0 in range(0, rows, rt):
        for ca, cb, cre, cim in pairs():
            re, im = h_ref[r0:r0 + rt, cre:cre + LANE], h_ref[r0:r0 + rt, cim:cim + LANE]
            hin_ref[0, r0:r0 + rt, ca:ca + LANE] = jnp.where(lo, re, pltpu.roll(im, S5_P, 1)).astype(BF16)
            hin_ref[0, r0:r0 + rt, cb:cb + LANE] = jnp.where(lo, pltpu.roll(re, S5_P, 1), im).astype(BF16)


def _s5_expand_kernel(a_ref, o_ref, *, transpose):
    a = a_ref[0]
    rows = a.shape[0]
    row_g = (lax.broadcasted_iota(jnp.int32, (rows, LANE), 0) // S5_GROUP) % S5_GB
    for d in range(2):
        ad = a[:, d * LANE:(d + 1) * LANE]
        for g in range(S5_GB):
            tile = jnp.where(row_g == g, ad, 0.0)
            col = (d * S5_GB + g) * LANE
            if transpose:
                o_ref[0, col:col + LANE, :] = tile.T.astype(BF16)
            else:
                o_ref[0, :, col:col + LANE] = tile.astype(BF16)


def _s5_expand(a, transpose):
    nblk, rows, _ = a.shape
    sw = 2 * S5_GB * LANE
    shape = (nblk, sw, rows) if transpose else (nblk, rows, sw)
    return pl.pallas_call(
        functools.partial(_s5_expand_kernel, transpose=transpose),
        grid=(nblk,),
        in_specs=[pl.BlockSpec((1,) + a.shape[1:], lambda b: (b, 0, 0))],
        out_specs=pl.BlockSpec((1,) + shape[1:], lambda b: (b, 0, 0)),
        out_shape=jax.ShapeDtypeStruct(shape, BF16),
        compiler_params=_params("arbitrary"),
        name="s5_expand",
    )(a)


def _s5_out_kernel(u_ref, hin_ref, k2_ref, mout_ref, y_ref, t_ref):
    nl = u_ref.shape[0]

    @pl.when(pl.program_id(1) == 0)
    def _():
        row_g = lax.broadcasted_iota(jnp.int32, (LANE, LANE), 0) // S5_GROUP
        col_g = lax.broadcasted_iota(jnp.int32, (LANE, LANE), 1) // S5_GROUP
        spread = (lax.broadcasted_iota(jnp.int32, (S5_GROUP, LANE), 1) % S5_GROUP
                  == lax.broadcasted_iota(jnp.int32, (S5_GROUP, LANE), 0)).astype(BF16)
        for j in range(2 * nl - 1):
            rep = jnp.dot(k2_ref[0, j].astype(BF16), spread, preferred_element_type=F32)
            tile = jnp.where(row_g == col_g, rep, 0.0).astype(BF16)
            for s in range(nl):
                t = s + j - (nl - 1)
                if 0 <= t < nl:
                    t_ref[s * LANE:(s + 1) * LANE, t * LANE:(t + 1) * LANE] = tile

    u = jnp.concatenate([u_ref[s] for s in range(nl)], axis=1)
    y = (jnp.dot(u, t_ref[...], preferred_element_type=F32)
         + jnp.dot(hin_ref[0], mout_ref[0], preferred_element_type=F32))
    for t in range(nl):
        y_ref[t] = y[:, t * LANE:(t + 1) * LANE]


def _s5(u, bsz, ops, tr=256):
    k2, a_min, a_mout, lam_blk = ops
    min_blk = _s5_expand(a_min, False)
    mout_blk = _s5_expand(a_mout, True)
    n, w = u.shape
    L = S5_CHUNK
    rows = n // L
    nblk = w // LANE
    kw = L * LANE
    sw = lam_blk.shape[2]
    tr = min(tr, rows)
    u3 = u.reshape(rows, L, w).transpose(1, 0, 2).astype(BF16)
    hin = pl.pallas_call(
        functools.partial(_s5_state_kernel, nb=bsz),
        grid=(nblk,),
        in_specs=[
            pl.BlockSpec((L, rows, LANE), lambda b: (0, 0, b)),
            pl.BlockSpec((1, kw, sw), lambda b: (b, 0, 0)),
            pl.BlockSpec((1, 1, sw), lambda b: (b, 0, 0)),
        ],
        out_specs=pl.BlockSpec((1, rows, sw), lambda b: (b, 0, 0)),
        out_shape=jax.ShapeDtypeStruct((nblk, rows, sw), BF16),
        scratch_shapes=[pltpu.VMEM((rows, sw), F32)],
        compiler_params=_params("arbitrary"),
        name="s5_state",
    )(u3, min_blk, lam_blk)
    y3 = pl.pallas_call(
        _s5_out_kernel,
        grid=(nblk, rows // tr),
        in_specs=[
            pl.BlockSpec((L, tr, LANE), lambda b, i: (0, i, b)),
            pl.BlockSpec((1, tr, sw), lambda b, i: (b, i, 0)),
            pl.BlockSpec((1,) + k2.shape[1:], lambda b, i: (b, 0, 0, 0)),
            pl.BlockSpec((1, sw, kw), lambda b, i: (b, 0, 0)),
        ],
        out_specs=pl.BlockSpec((L, tr, LANE), lambda b, i: (0, i, b)),
        out_shape=jax.ShapeDtypeStruct((L, rows, w), F32),
        scratch_shapes=[pltpu.VMEM((kw, kw), BF16)],
        compiler_params=_params("arbitrary", "arbitrary"),
        name="s5_out",
    )(u3, hin, k2, mout_blk)
    return y3.transpose(1, 0, 2).reshape(n, w)


def _log_sigmoid(z):
    return -_softplus(-z)


def _tri(length, d):
    ti = lax.broadcasted_iota(jnp.int32, (length, length), 0)
    si = lax.broadcasted_iota(jnp.int32, (length, length), 1)
    return (ti - si) * (1 - 2 * d) >= 0


def _mlstm_kernel(q_ref, k_ref, v_ref, g_ref, gb_ref, o_ref, c_ref, n_ref, m_ref):
    d = pl.program_id(0)
    length = q_ref.shape[0]
    nh, dh = MLSTM_HEADS, MLSTM_DH

    @pl.when(pl.program_id(2) == 0)
    def _():
        c_ref[...] = jnp.zeros_like(c_ref)
        n_ref[...] = jnp.zeros_like(n_ref)
        m_ref[...] = jnp.zeros_like(m_ref)

    causal = _tri(length, d)
    gp = g_ref[...] + gb_ref[...]
    gp = jnp.where(d == 0, gp, pltpu.roll(gp, LANE - 2 * nh, 1))
    lf = _log_sigmoid(gp)
    cum = jnp.dot(causal.astype(F32), lf, precision=HI, preferred_element_type=F32)
    tot = jnp.sum(lf, axis=0, keepdims=True)
    gp_t = gp.T
    cum_t = cum.T
    nt = (((1,), (1,)), ((), ()))
    tn = (((0,), (0,)), ((), ()))

    for h in range(nh):
        sl = slice(h * dh, (h + 1) * dh)
        qh = (q_ref[:, sl] * (dh ** -0.5)).astype(BF16)
        kf = k_ref[:, sl]
        kh = kf.astype(BF16)
        vh = v_ref[:, sl].astype(BF16)
        ig_c, cum_c = gp[:, h:h + 1], cum[:, nh + h:nh + h + 1]
        ig_r, cum_r = gp_t[h:h + 1, :], cum_t[nh + h:nh + h + 1, :]
        tot_h = tot[:, nh + h:nh + h + 1]
        m_st = m_ref[h:h + 1, 0:1]
        c_st = c_ref[h]
        dmat = jnp.where(causal, cum_c - cum_r + ig_r, -jnp.inf)
        m_inter = cum_c + m_st
        m_t = jnp.maximum(jnp.max(dmat, axis=1, keepdims=True), m_inter)
        w_inter = jnp.exp(m_inter - m_t)
        s = lax.dot_general(qh, kh, nt, preferred_element_type=F32) * jnp.exp(dmat - m_t)
        num = (jnp.dot(s.astype(BF16), vh, preferred_element_type=F32)
               + w_inter * jnp.dot(qh, c_st.astype(BF16), preferred_element_type=F32))
        den = (jnp.sum(s, axis=1, keepdims=True)
               + w_inter * jnp.sum(qh.astype(F32) * n_ref[h:h + 1, :], axis=1, keepdims=True))
        o_ref[0, :, sl] = num / jnp.maximum(jnp.abs(den), jnp.exp(-m_t))
        dec = tot_h - cum_c + ig_c
        m_new = jnp.maximum(tot_h + m_st, jnp.max(dec, axis=0, keepdims=True))
        wc = jnp.exp(tot_h + m_st - m_new)
        kw = jnp.exp(dec - m_new) * kf
        c_ref[h] = wc * c_st + lax.dot_general(kw.astype(BF16), vh, tn, preferred_element_type=F32)
        n_ref[h:h + 1, :] = wc * n_ref[h:h + 1, :] + jnp.sum(kw, axis=0, keepdims=True)
        m_ref[h:h + 1, :] = jnp.broadcast_to(m_new, (1, LANE))


def _mlstm(proj, tail, bsz, gate_b, length=256):
    n = proj.shape[0]
    seq = n // bsz
    length = min(length, seq)
    nc = seq // length
    w = W_MLSTM
    q_blk = 2 * W_LRU // w
    gb = jnp.pad(gate_b.reshape(1, -1), ((0, 0), (0, LANE - gate_b.size)))

    def row(d, b, c):
        return b * nc + c + d * (nc - 1 - 2 * c)

    return pl.pallas_call(
        _mlstm_kernel,
        grid=(2, bsz, nc),
        in_specs=[
            pl.BlockSpec((length, w), lambda d, b, c: (row(d, b, c), q_blk)),
            pl.BlockSpec((length, w), lambda d, b, c: (row(d, b, c), q_blk + 1)),
            pl.BlockSpec((length, w), lambda d, b, c: (row(d, b, c), q_blk + 2)),
            pl.BlockSpec((length, LANE), lambda d, b, c: (row(d, b, c), 0)),
            pl.BlockSpec((1, LANE), lambda d, b, c: (0, 0)),
        ],
        out_specs=pl.BlockSpec((1, length, w), lambda d, b, c: (d, row(d, b, c), 0)),
        out_shape=jax.ShapeDtypeStruct((2, n, w), F32),
        scratch_shapes=[pltpu.VMEM((MLSTM_HEADS, MLSTM_DH, MLSTM_DH), F32),
                        pltpu.VMEM((SUBLANE, MLSTM_DH), F32), pltpu.VMEM((SUBLANE, LANE), F32)],
        compiler_params=_params("arbitrary", "arbitrary", "arbitrary"),
        name="mlstm",
    )(proj, proj, proj, tail, gb)


GLA_SUB = 16
GLA_SAFE_DECAY = 40.0


def _gla_chunk_dense(q_ref, k_ref, v_ref, cum, o_ref, s_ref, rev):
    length = q_ref.shape[0]
    nt = (((1,), (1,)), ((), ()))
    tn = (((0,), (0,)), ((), ()))
    causal = _tri(length, int(rev))
    last = slice(0, 1) if rev else slice(length - 1, length)
    for h in range(GLA_HEADS):
        kl = slice(h * GLA_DK, (h + 1) * GLA_DK)
        vl = slice(h * GLA_DV, (h + 1) * GLA_DV)
        ch = cum[:, kl]
        kk = k_ref[:, kl]
        vv = v_ref[:, vl].astype(BF16)
        st = s_ref[h]
        tot = ch[last]
        qe = (q_ref[:, kl] * (GLA_DK ** -0.5) * jnp.exp(ch)).astype(BF16)
        ke = (kk * jnp.exp(-ch)).astype(BF16)
        att = jnp.where(causal, lax.dot_general(qe, ke, nt, preferred_element_type=F32), 0.0)
        o_ref[0, :, vl] = (jnp.dot(att.astype(BF16), vv, preferred_element_type=F32)
                           + lax.dot_general(qe, st.astype(BF16), nt, preferred_element_type=F32))
        kd = (kk * jnp.exp(tot - ch)).astype(BF16)
        s_ref[h] = jnp.exp(tot) * st + lax.dot_general(vv, kd, tn, preferred_element_type=F32)


def _gla_chunk(q_ref, k_ref, v_ref, cum, o_ref, s_ref, rev):
    length = q_ref.shape[0]
    c = GLA_SUB
    nt = (((1,), (1,)), ((), ()))
    tn = (((0,), (0,)), ((), ()))

    def rows(a, b):
        return slice(length - b, length - a) if rev else slice(a, b)

    def row(a):
        i = length - 1 - a if rev else a
        return slice(i, i + 1)

    ti = lax.broadcasted_iota(jnp.int32, (c, 1), 0)
    for h in range(GLA_HEADS):
        kl = slice(h * GLA_DK, (h + 1) * GLA_DK)
        vl = slice(h * GLA_DV, (h + 1) * GLA_DV)
        ch = cum[:, kl]
        qs = q_ref[:, kl] * (GLA_DK ** -0.5)
        kk = k_ref[:, kl]
        vf = v_ref[:, vl]
        vv = vf.astype(BF16)
        st = s_ref[h]
        tot = ch[row(length - 1)]
        o = lax.dot_general((qs * jnp.exp(ch)).astype(BF16), st.astype(BF16), nt, preferred_element_type=F32)
        ob = [o[rows(p * c, (p + 1) * c)] for p in range(length // c)]
        m = length // 2
        while m >= c:
            for start in range(0, length, 2 * m):
                fst, sec = rows(start, start + m), rows(start + m, start + 2 * m)
                r = ch[row(start + m)]
                qh = (qs[sec] * jnp.exp(ch[sec] - r)).astype(BF16)
                kh = (kk[fst] * jnp.exp(r - ch[fst])).astype(BF16)
                att = lax.dot_general(qh, kh, nt, preferred_element_type=F32)
                contrib = jnp.dot(att.astype(BF16), vv[fst], preferred_element_type=F32)
                for p in range((start + m) // c, (start + 2 * m) // c):
                    lo = rows(p * c, (p + 1) * c).start - sec.start
                    ob[p] = ob[p] + contrib[lo:lo + c]
            m //= 2
        for p in range(length // c):
            blk = rows(p * c, (p + 1) * c)
            cb, qb, kb, vb = ch[blk], qs[blk], kk[blk], vf[blk]
            acc = ob[p]
            for s in range(c):
                sees = (ti <= s) if rev else (ti >= s)
                e = jnp.exp(jnp.where(sees, cb - cb[s:s + 1], -jnp.inf))
                a = jnp.sum(qb * kb[s:s + 1] * e, axis=1, keepdims=True)
                acc = acc + a * vb[s:s + 1]
            o_ref[0, blk, vl] = acc
        kd = (kk * jnp.exp(tot - ch)).astype(BF16)
        s_ref[h] = jnp.exp(tot) * st + lax.dot_general(vv, kd, tn, preferred_element_type=F32)


def _gla_kernel(q_ref, k_ref, v_ref, low_ref, wg_ref, bg_ref, o_ref, s_ref):
    d = pl.program_id(0)
    length = q_ref.shape[0]

    @pl.when(pl.program_id(2) == 0)
    def _():
        s_ref[...] = jnp.zeros_like(s_ref)

    gate_pre = jnp.dot(low_ref[...], wg_ref[0], precision=HI, preferred_element_type=F32) + bg_ref[0]
    la = _log_sigmoid(gate_pre) * (1.0 / GLA_TAU)
    cum = jnp.dot(_tri(length, d).astype(F32), la, precision=HI, preferred_element_type=F32)
    safe = jnp.sum(la, axis=0, keepdims=True).min() >= -GLA_SAFE_DECAY
    for rev in (False, True):
        pl.when((d == int(rev)) & safe)(
            functools.partial(_gla_chunk_dense, q_ref, k_ref, v_ref, cum, o_ref, s_ref, rev))
        pl.when((d == int(rev)) & jnp.logical_not(safe))(
            functools.partial(_gla_chunk, q_ref, k_ref, v_ref, cum, o_ref, s_ref, rev))


def _gla(proj, tail, bsz, w_gate2, gate_b, length=128):
    n = proj.shape[0]
    seq = n // bsz
    length = min(length, seq)
    nc = seq // length
    wg = jnp.zeros((2, LANE, GLA_QK), F32)
    for d in range(2):
        wg = wg.at[d, d * GLA_RANK:(d + 1) * GLA_RANK].set(w_gate2[d])

    def row(d, b, c):
        return b * nc + c + d * (nc - 1 - 2 * c)

    return pl.pallas_call(
        _gla_kernel,
        grid=(2, bsz, nc),
        in_specs=[
            pl.BlockSpec((length, GLA_QK), lambda d, b, c: (row(d, b, c), 0)),
            pl.BlockSpec((length, GLA_QK), lambda d, b, c: (row(d, b, c), 1)),
            pl.BlockSpec((length, GLA_V), lambda d, b, c: (row(d, b, c), 2 * GLA_QK // GLA_V)),
            pl.BlockSpec((length, LANE), lambda d, b, c: (row(d, b, c), 0)),
            pl.BlockSpec((1, LANE, GLA_QK), lambda d, b, c: (d, 0, 0)),
            pl.BlockSpec((1, 1, GLA_QK), lambda d, b, c: (d, 0, 0)),
        ],
        out_specs=pl.BlockSpec((1, length, GLA_V), lambda d, b, c: (d, row(d, b, c), 0)),
        out_shape=jax.ShapeDtypeStruct((2, n, GLA_V), F32),
        scratch_shapes=[pltpu.VMEM((GLA_HEADS, GLA_DV, GLA_DK), F32)],
        compiler_params=_params("arbitrary", "arbitrary", "arbitrary"),
        name="gla",
    )(proj, proj, proj, tail, wg, gate_b.reshape(2, 1, GLA_QK))


def _head_norm(t, g, heads):
    dh = t.shape[1] // heads
    outs = []
    for h in range(heads):
        th = t[:, h * dh:(h + 1) * dh]
        outs.append(th * lax.rsqrt(jnp.mean(th * th, axis=1, keepdims=True) + NORM_EPS))
    return jnp.concatenate(outs, axis=1) * g


def _ab_out_kernel(x_ref, gr_ref, og_ref, hl_ref, hm_ref, g_ref, w_ref, o_ref):
    k = gr_ref.shape[1]
    ya = jax.nn.gelu(gr_ref[...]) * (hl_ref[0] + hl_ref[1])
    yb = jax.nn.sigmoid(og_ref[...]) * _head_norm(hm_ref[0] + hm_ref[1], g_ref[...], MLSTM_HEADS)
    o_ref[...] = (x_ref[...]
                  + jnp.dot(ya.astype(BF16), w_ref[0:k, :], preferred_element_type=F32)
                  + jnp.dot(yb.astype(BF16), w_ref[k:, :], preferred_element_type=F32))


def _ab_out(x, proj, hl, hm, norm_g, w, tm=512):
    n, d = x.shape
    k = W_LRU
    tm = min(tm, n)
    return pl.pallas_call(
        _ab_out_kernel,
        grid=(n // tm,),
        in_specs=[
            pl.BlockSpec((tm, d), lambda i: (i, 0)),
            pl.BlockSpec((tm, k), lambda i: (i, 1)),
            pl.BlockSpec((tm, k), lambda i: (i, 5)),
            pl.BlockSpec((2, tm, k), lambda i: (0, i, 0)),
            pl.BlockSpec((2, tm, k), lambda i: (0, i, 0)),
            pl.BlockSpec((1, k), lambda i: (0, 0)),
            pl.BlockSpec((2 * k, d), lambda i: (0, 0)),
        ],
        out_specs=pl.BlockSpec((tm, d), lambda i: (i, 0)),
        out_shape=jax.ShapeDtypeStruct((n, d), F32),
        compiler_params=_params("arbitrary"),
        name="ab_out",
    )(x, proj, proj, hl, hm, norm_g.reshape(1, k), w)


def _cd_out_kernel(x_ref, r_ref, og_ref, g_ref, ys_ref, u_ref, d_ref, wg_ref, w_ref, o_ref):
    k = r_ref.shape[1]
    r = r_ref[...]
    yc = _head_norm(og_ref[0] + og_ref[1], g_ref[...], GLA_HEADS) * (r * jax.nn.sigmoid(r))
    y = jax.nn.gelu(ys_ref[...] + d_ref[...] * u_ref[...])
    yd = y * jax.nn.sigmoid(jnp.dot(y.astype(BF16), wg_ref[...], preferred_element_type=F32))
    o_ref[...] = (x_ref[...]
                  + jnp.dot(yc.astype(BF16), w_ref[0:k, :], preferred_element_type=F32)
                  + jnp.dot(yd.astype(BF16), w_ref[k:, :], preferred_element_type=F32))


def _cd_out(x, proj, og, norm_g, ys, s5_d, w_glu, w, tm=512):
    n, d = x.shape
    k = S5_W
    tm = min(tm, n)
    return pl.pallas_call(
        _cd_out_kernel,
        grid=(n // tm,),
        in_specs=[
            pl.BlockSpec((tm, d), lambda i: (i, 0)),
            pl.BlockSpec((tm, k), lambda i: (i, 2)),
            pl.BlockSpec((2, tm, k), lambda i: (0, i, 0)),
            pl.BlockSpec((1, k), lambda i: (0, 0)),
            pl.BlockSpec((tm, k), lambda i: (i, 0)),
            pl.BlockSpec((tm, k), lambda i: (i, 3)),
            pl.BlockSpec((1, k), lambda i: (0, 0)),
            pl.BlockSpec((k, k), lambda i: (0, 0)),
            pl.BlockSpec((2 * k, d), lambda i: (0, 0)),
        ],
        out_specs=pl.BlockSpec((tm, d), lambda i: (i, 0)),
        out_shape=jax.ShapeDtypeStruct((n, d), F32),
        compiler_params=_params("arbitrary"),
        name="cd_out",
    )(x, proj, og, norm_g.reshape(1, k), ys, proj, s5_d.reshape(1, k), w_glu, w)


def _split_small(w, n_main):
    tail = w[:, n_main:]
    tail = jnp.pad(tail, ((0, 0), (0, LANE - tail.shape[1])))
    return w[:, :n_main].astype(BF16), tail.astype(BF16)


def kernel(x, norm_ffn1, ffn1_w_gu, ffn1_w_down, norm_mix, norm_ffn2, ffn2_w_gu, ffn2_w_down,
           ab_w_in, lru_conv_w, lru_conv_b, lru_gate_w, lru_gate_b, lru_lambda, mlstm_gate_b,
           mlstm_norm, ab_w_out, cd_w_in, gla_w_gate2, gla_gate_b, gla_norm, s5_a_re, s5_a_im,
           s5_log_dt, s5_b_re, s5_b_im, s5_c_re, s5_c_im, s5_d, s5_w_glu, cd_w_out, final_norm):
    bsz, seq, d = x.shape
    n = bsz * seq
    depth = norm_ffn1.shape[0]
    xf = x.reshape(n, d)
    w1_gu, w1_down = ffn1_w_gu.astype(BF16), ffn1_w_down.astype(BF16)
    w2_gu, w2_down = ffn2_w_gu.astype(BF16), ffn2_w_down.astype(BF16)
    for l in range(depth):
        xf = _ffn(xf, norm_ffn1[l], w1_gu, w1_down, l)
        j = l // 2
        if l % 2 == 0:
            w_main, w_tail = _split_small(ab_w_in[j], 2 * W_LRU + 4 * W_MLSTM)
            proj, tail = _norm_proj(xf, norm_mix[l], w_main, w_tail)
            hl = _lru(proj, bsz, lru_conv_w[j], lru_conv_b[j], lru_gate_w[j], lru_gate_b[j], lru_lambda[j])
            hm = _mlstm(proj, tail, bsz, mlstm_gate_b[j])
            xf = _ab_out(xf, proj, hl, hm, mlstm_norm[j], ab_w_out[j].astype(BF16))
        else:
            w = cd_w_in[j]
            n_gla = 2 * GLA_QK + 2 * GLA_V
            w = jnp.concatenate([w[:, :n_gla], w[:, n_gla + 2 * GLA_RANK:], w[:, n_gla:n_gla + 2 * GLA_RANK]], axis=1)
            w_main, w_tail = _split_small(w, n_gla + S5_W)
            proj, tail = _norm_proj(xf, norm_mix[l], w_main, w_tail)
            og = _gla(proj, tail, bsz, gla_w_gate2[j], gla_gate_b[j])
            ops = _s5_operators(s5_a_re[j], s5_a_im[j], s5_log_dt[j], s5_b_re[j], s5_b_im[j],
                                s5_c_re[j], s5_c_im[j])
            ys = _s5(proj[:, n_gla:], bsz, ops)
            xf = _cd_out(xf, proj, og, gla_norm[j], ys, s5_d[j],
                         s5_w_glu[j].astype(BF16), cd_w_out[j].astype(BF16))
        xf = _ffn(xf, norm_ffn2[l], w2_gu, w2_down, l, final_g=final_norm if l == depth - 1 else None)
    return xf.reshape(bsz, seq, d)
```

```python
import functools

import jax
import jax.numpy as jnp
from jax import lax
from jax.experimental import pallas as pl
from jax.experimental.pallas import tpu as pltpu

NORM_EPS = 1e-6

W_LRU = 1024
LRU_HEADS = 8
LRU_BLOCK = W_LRU // LRU_HEADS
CONV_W = 4
CONV_PAD_L = 2
LRU_C = 8.0

W_MLSTM = 1024
MLSTM_HEADS = 4
MLSTM_DH = W_MLSTM // MLSTM_HEADS

GLA_HEADS = 4
GLA_DK = 128
GLA_DV = 256
GLA_QK = GLA_HEADS * GLA_DK
GLA_V = GLA_HEADS * GLA_DV
GLA_RANK = 16
GLA_TAU = 16.0

S5_W = 1024
S5_GROUP = 16
S5_GROUPS = S5_W // S5_GROUP
S5_P = 64
S5_CHUNK = 16
S5_GB = 8
S5_ROW_TILE = 256

LANE = 128
SUBLANE = 8
VMEM_LIMIT = 52 * 1024 * 1024

BF16 = jnp.bfloat16
F32 = jnp.float32
HI = lax.Precision.HIGHEST


def _rms(x, g):
    ms = jnp.mean(x * x, axis=-1, keepdims=True)
    return x * lax.rsqrt(ms + NORM_EPS) * g


def _softplus(z):
    return jnp.maximum(z, 0.0) + jnp.log1p(jnp.exp(-jnp.abs(z)))


def _params(*sem):
    return pltpu.CompilerParams(dimension_semantics=sem, vmem_limit_bytes=VMEM_LIMIT)


def _ffn_kernel(x_ref, g_ref, wg_ref, wu_ref, wd_ref, *rest, final):
    if final:
        fg_ref, o_ref, h_ref = rest
    else:
        o_ref, h_ref = rest
    j = pl.program_id(1)

    @pl.when(j == 0)
    def _():
        x = x_ref[...]
        h_ref[...] = _rms(x, g_ref[...]).astype(BF16)
        o_ref[...] = x

    h = h_ref[...]
    g = jnp.dot(h, wg_ref[...], preferred_element_type=F32)
    u = jnp.dot(h, wu_ref[...], preferred_element_type=F32)
    a = (0.5 * g * jax.nn.sigmoid(g) * u).astype(BF16)
    o_ref[...] += jnp.dot(a, wd_ref[...], preferred_element_type=F32)

    if final:
        @pl.when(j == pl.num_programs(1) - 1)
        def _():
            o_ref[...] = _rms(o_ref[...], fg_ref[...])


def _ffn(x, g, w_gu, w_down, layer, final_g=None, tm=512, tf=512):
    n, d = x.shape
    f = w_down.shape[1]
    tm = min(tm, n)
    nj = f // tf
    final = final_g is not None
    in_specs = [
        pl.BlockSpec((tm, d), lambda i, j: (i, 0)),
        pl.BlockSpec((1, d), lambda i, j: (0, 0)),
        pl.BlockSpec((None, d, tf), lambda i, j: (layer, 0, j)),
        pl.BlockSpec((None, d, tf), lambda i, j: (layer, 0, j + nj)),
        pl.BlockSpec((None, tf, d), lambda i, j: (layer, j, 0)),
    ]
    args = [x, g.reshape(1, d), w_gu, w_gu, w_down]
    if final:
        in_specs.append(pl.BlockSpec((1, d), lambda i, j: (0, 0)))
        args.append(final_g.reshape(1, d))
    return pl.pallas_call(
        functools.partial(_ffn_kernel, final=final),
        grid=(n // tm, nj),
        in_specs=in_specs,
        out_specs=pl.BlockSpec((tm, d), lambda i, j: (i, 0)),
        out_shape=jax.ShapeDtypeStruct((n, d), F32),
        scratch_shapes=[pltpu.VMEM((tm, d), BF16)],
        compiler_params=_params("arbitrary", "arbitrary"),
        name="ffn_final" if final else "ffn",
    )(*args)


def _norm_proj_kernel(x_ref, g_ref, w_ref, ws_ref, o_ref, os_ref, h_ref):
    @pl.when(pl.program_id(1) == 0)
    def _():
        h = _rms(x_ref[...], g_ref[...]).astype(BF16)
        h_ref[...] = h
        os_ref[...] = jnp.dot(h, ws_ref[...], preferred_element_type=F32)

    o_ref[...] = jnp.dot(h_ref[...], w_ref[...], preferred_element_type=F32)


def _norm_proj(x, g, w, w_small, tm=1024, tn=1024):
    n, d = x.shape
    m = w.shape[1]
    ms = w_small.shape[1]
    tm = min(tm, n)
    return pl.pallas_call(
        _norm_proj_kernel,
        grid=(n // tm, m // tn),
        in_specs=[
            pl.BlockSpec((tm, d), lambda i, j: (i, 0)),
            pl.BlockSpec((1, d), lambda i, j: (0, 0)),
            pl.BlockSpec((d, tn), lambda i, j: (0, j)),
            pl.BlockSpec((d, ms), lambda i, j: (0, 0)),
        ],
        out_specs=[pl.BlockSpec((tm, tn), lambda i, j: (i, j)),
                   pl.BlockSpec((tm, ms), lambda i, j: (i, 0))],
        out_shape=[jax.ShapeDtypeStruct((n, m), F32), jax.ShapeDtypeStruct((n, ms), F32)],
        scratch_shapes=[pltpu.VMEM((tm, d), BF16)],
        compiler_params=_params("arbitrary", "arbitrary"),
        name="norm_proj",
    )(x, g.reshape(1, d), w, w_small)


def _lru_kernel(xp_ref, x_ref, xn_ref, cw_ref, cb_ref, gw_ref, gb_ref, lam_ref, o_ref,
                xs_ref, a_ref, b_ref, h_ref, *, tt):
    d = pl.program_id(0)
    t = pl.program_id(2)
    nt = pl.num_programs(2)
    tb = jnp.where(d == 0, t, nt - 1 - t)
    w = x_ref.shape[1]

    xs_ref[0:SUBLANE, :] = jnp.where(tb > 0, xp_ref[...], 0.0)
    xs_ref[SUBLANE:SUBLANE + tt, :] = x_ref[...]
    xs_ref[SUBLANE + tt:2 * SUBLANE + tt, :] = jnp.where(tb < nt - 1, xn_ref[...], 0.0)
    cw = cw_ref[...]
    xc = cb_ref[...]
    for k in range(CONV_W):
        off = SUBLANE + k - CONV_PAD_L
        xc = xc + cw[k:k + 1, :] * xs_ref[off:off + tt, :]

    xcb = xc.astype(BF16)
    gb = gb_ref[0]
    sp = _softplus(-lam_ref[0])
    for hd in range(w // LRU_BLOCK):
        sl = slice(hd * LRU_BLOCK, (hd + 1) * LRU_BLOCK)
        z = jnp.dot(xcb[:, sl], gw_ref[0, hd], preferred_element_type=F32)
        r = jax.nn.sigmoid(z[:, :LRU_BLOCK] + gb[0:1, sl])
        i = jax.nn.sigmoid(z[:, LRU_BLOCK:] + gb[1:2, sl])
        log_a = (-LRU_C * sp[:, sl]) * r
        a = jnp.exp(log_a)
        a_ref[:, sl] = a
        b_ref[:, sl] = jnp.sqrt(-jnp.tanh(log_a) * (a * a + 1.0)) * (i * xc[:, sl])

    @pl.when(t == 0)
    def _():
        h_ref[...] = jnp.zeros_like(h_ref)

    sub = lax.broadcasted_iota(jnp.int32, (SUBLANE, w), 0)
    ntile = tt // SUBLANE

    def scan(rev):
        def earlier(x, k, fill):
            if rev:
                return jnp.where(sub < SUBLANE - k, pltpu.roll(x, SUBLANE - k, 0), fill)
            return jnp.where(sub >= k, pltpu.roll(x, k, 0), fill)

        def body(jt, h):
            r0 = pl.multiple_of((ntile - 1 - jt if rev else jt) * SUBLANE, SUBLANE)
            a = a_ref[pl.ds(r0, SUBLANE), :]
            b = b_ref[pl.ds(r0, SUBLANE), :]
            k = 1
            while k < SUBLANE:
                a, b = a * earlier(a, k, 1.0), b + a * earlier(b, k, 0.0)
                k *= 2
            tile = a * h + b
            o_ref[0, pl.ds(r0, SUBLANE), :] = tile
            last = 0 if rev else SUBLANE - 1
            return tile[last:last + 1, :]

        h_ref[...] = lax.fori_loop(0, ntile, body, h_ref[...])

    pl.when(d == 0)(functools.partial(scan, False))
    pl.when(d == 1)(functools.partial(scan, True))


def _lru(proj, bsz, conv_w, conv_b, gate_w, gate_b, lam, tt=256):
    n = proj.shape[0]
    seq = n // bsz
    tt = min(tt, seq)
    nt = seq // tt
    w = W_LRU
    r8 = tt // SUBLANE
    gw = jnp.concatenate([gate_w[:, 0], gate_w[:, 1]], axis=-1).astype(BF16)

    def tb(d, t):
        return t + d * (nt - 1 - 2 * t)

    return pl.pallas_call(
        functools.partial(_lru_kernel, tt=tt),
        grid=(2, bsz, nt),
        in_specs=[
            pl.BlockSpec((SUBLANE, w), lambda d, b, t: (jnp.maximum((b * nt + tb(d, t)) * r8 - 1, 0), 0)),
            pl.BlockSpec((tt, w), lambda d, b, t: (b * nt + tb(d, t), 0)),
            pl.BlockSpec((SUBLANE, w), lambda d, b, t: (jnp.minimum((b * nt + tb(d, t) + 1) * r8, n // SUBLANE - 1), 0)),
            pl.BlockSpec((CONV_W, w), lambda d, b, t: (0, 0)),
            pl.BlockSpec((1, w), lambda d, b, t: (0, 0)),
            pl.BlockSpec((1, LRU_HEADS, LRU_BLOCK, 2 * LRU_BLOCK), lambda d, b, t: (d, 0, 0, 0)),
            pl.BlockSpec((1, 2, w), lambda d, b, t: (d, 0, 0)),
            pl.BlockSpec((1, 1, w), lambda d, b, t: (d, 0, 0)),
        ],
        out_specs=pl.BlockSpec((1, tt, w), lambda d, b, t: (d, b * nt + tb(d, t), 0)),
        out_shape=jax.ShapeDtypeStruct((2, n, w), F32),
        scratch_shapes=[pltpu.VMEM((tt + 2 * SUBLANE, w), F32), pltpu.VMEM((tt, w), F32),
                        pltpu.VMEM((tt, w), F32), pltpu.VMEM((1, w), F32)],
        compiler_params=_params("arbitrary", "arbitrary", "arbitrary"),
        name="rg_lru",
    )(proj, proj, proj, conv_w, conv_b.reshape(1, w), gw, gate_b, lam.reshape(2, 1, w))


def _s5_discretise(a_re, a_im, log_dt, b_re, b_im):
    dt = jnp.exp(log_dt)[:, None]
    mag = jnp.exp(dt * a_re)
    lr = mag * jnp.cos(dt * a_im)
    li = mag * jnp.sin(dt * a_im)
    den = a_re * a_re + a_im * a_im
    nr = lr - 1.0
    cr = (nr * a_re + li * a_im) / den
    ci = (li * a_re - nr * a_im) / den
    bbr = cr[..., None] * b_re - ci[..., None] * b_im
    bbi = cr[..., None] * b_im + ci[..., None] * b_re
    return lr, li, bbr, bbi


def _s5_operators(a_re, a_im, log_dt, b_re, b_im, c_re, c_im):
    L = S5_CHUNK
    G, P, C = S5_GROUPS, S5_P, S5_GROUP
    kfs, mins, mouts, lams = [], [], [], []
    for d in range(2):
        lr, li, bbr, bbi = _s5_discretise(a_re[d], a_im[d], log_dt[d], b_re, b_im)
        bbr, bbi = bbr.transpose(0, 2, 1), bbi.transpose(0, 2, 1)
        jj = jnp.arange(L + 1, dtype=F32)[:, None, None, None]
        dt = jnp.exp(log_dt[d])[None, :, None, None]
        mag = jnp.exp(jj * dt * a_re[d][None, :, None, :])
        pr = mag * jnp.cos(jj * dt * a_im[d][None, :, None, :])
        pi = mag * jnp.sin(jj * dt * a_im[d][None, :, None, :])
        cl = jnp.concatenate([c_re[None] * pr - c_im[None] * pi, -(c_re[None] * pi + c_im[None] * pr)], axis=-1)
        bl = jnp.concatenate([pr * bbr[None] - pi * bbi[None], pr * bbi[None] + pi * bbr[None]], axis=-1)
        kfs.append(jnp.sum(cl[:L, :, None, :, :] * bl[0][None, :, :, None, :], axis=-1))
        if d == 0:
            e_in = jnp.arange(L - 1, -1, -1)
            e_out = jnp.arange(1, L + 1)
        else:
            e_in = jnp.arange(L)
            e_out = jnp.arange(L, 0, -1)
        mins.append(bl[e_in])
        mouts.append(cl[e_out])
        lams.append(jnp.stack([pr[L, :, 0], pi[L, :, 0]], axis=1))
    kf, kb = kfs
    kfull = jnp.concatenate([kb[:0:-1], (kf[0] + kb[0])[None], kf[1:]], axis=0)
    gb = S5_GB
    nb_ = G // gb
    k2 = kfull.reshape(2 * L - 1, nb_, gb * C, C).transpose(1, 0, 2, 3)

    def rows_of_block(parts):
        a = jnp.concatenate(parts, axis=-1).reshape(L, nb_, gb * C, 4 * P)
        return a.transpose(1, 0, 2, 3).reshape(nb_, L * gb * C, 4 * P)

    lam_l = jnp.concatenate(lams, axis=1)
    lam_blk = lam_l.reshape(nb_, gb, 4, P).transpose(0, 2, 1, 3).reshape(nb_, 1, 4 * gb * P)
    return k2, rows_of_block(mins), rows_of_block(mouts), lam_blk


def _s5_state_kernel(u_ref, amin_ref, lam_ref, hin_ref, h_ref, min_ref, *, nb):
    nl, rows, _ = u_ref.shape
    _s5_spread(amin_ref[0], min_ref, False)
    rb = rows // nb
    hw = h_ref.shape[1] // 4
    rt = min(rows, S5_ROW_TILE)
    lo = lax.broadcasted_iota(jnp.int32, (rt, LANE), 1) < S5_P
    ng = hw // S5_P

    def pairs():
        for d in range(2):
            for k in range(ng // 2):
                yield ((d * ng + 2 * k) * LANE, (d * ng + 2 * k + 1) * LANE,
                       2 * d * hw + k * LANE, (2 * d + 1) * hw + k * LANE)

    for r0 in range(0, rows, rt):
        u = jnp.concatenate([u_ref[s, r0:r0 + rt, :] for s in range(nl)], axis=1)
        hl = jnp.dot(u, min_ref[...], preferred_element_type=F32)
        for ca, cb, cre, cim in pairs():
            a, b = hl[:, ca:ca + LANE], hl[:, cb:cb + LANE]
            h_ref[r0:r0 + rt, cre:cre + LANE] = jnp.where(lo, a, pltpu.roll(b, S5_P, 1))
            h_ref[r0:r0 + rt, cim:cim + LANE] = jnp.where(lo, pltpu.roll(a, S5_P, 1), b)
    lam = lam_ref[0]
    lrf, lif = lam[:, 0:hw], lam[:, hw:2 * hw]
    lrb, lib = lam[:, 2 * hw:3 * hw], lam[:, 3 * hw:4 * hw]
    sub = lax.broadcasted_iota(jnp.int32, (SUBLANE, hw), 0)

    def body(it, carry):
        new = []
        for b in range(nb):
            fr, fi, br, bi = carry[4 * b:4 * b + 4]
            f0 = pl.multiple_of(b * rb + it * SUBLANE, SUBLANE)
            b0 = pl.multiple_of(b * rb + rb - SUBLANE - it * SUBLANE, SUBLANE)
            tiles = [jnp.zeros((SUBLANE, hw), F32)] * 4
            lf = h_ref[pl.ds(f0, SUBLANE), 0:2 * hw]
            lb = h_ref[pl.ds(b0, SUBLANE), 2 * hw:4 * hw]
            for r in range(SUBLANE):
                q = SUBLANE - 1 - r
                tiles = [jnp.where(sub == r, fr, tiles[0]), jnp.where(sub == r, fi, tiles[1]),
                         jnp.where(sub == q, br, tiles[2]), jnp.where(sub == q, bi, tiles[3])]
                fr, fi, br, bi = (lrf * fr - lif * fi + lf[r:r + 1, 0:hw],
                                  lrf * fi + lif * fr + lf[r:r + 1, hw:2 * hw],
                                  lrb * br - lib * bi + lb[q:q + 1, 0:hw],
                                  lrb * bi + lib * br + lb[q:q + 1, hw:2 * hw])
            h_ref[pl.ds(f0, SUBLANE), 0:hw] = tiles[0]
            h_ref[pl.ds(f0, SUBLANE), hw:2 * hw] = tiles[1]
            h_ref[pl.ds(b0, SUBLANE), 2 * hw:3 * hw] = tiles[2]
            h_ref[pl.ds(b0, SUBLANE), 3 * hw:4 * hw] = tiles[3]
            new += [fr, fi, br, bi]
        return tuple(new)

    zero = jnp.zeros((1, hw), F32)
    lax.fori_loop(0, rb // SUBLANE, body, (zero,) * (4 * nb))
    for r0 in range(0, rows, rt):
        for ca, cb, cre, cim in pairs():
            re, im = h_ref[r0:r0 + rt, cre:cre + LANE], h_ref[r0:r0 + rt, cim:cim + LANE]
            hin_ref[0, r0:r0 + rt, ca:ca + LANE] = jnp.where(lo, re, pltpu.roll(im, S5_P, 1)).astype(BF16)
            hin_ref[0, r0:r0 + rt, cb:cb + LANE] = jnp.where(lo, pltpu.roll(re, S5_P, 1), im).astype(BF16)


def _s5_spread(a, o_ref, transpose):
    rows = a.shape[0]
    row_g = (lax.broadcasted_iota(jnp.int32, (rows, LANE), 0) // S5_GROUP) % S5_GB
    for d in range(2):
        ad = a[:, d * LANE:(d + 1) * LANE]
        for g in range(S5_GB):
            tile = jnp.where(row_g == g, ad, 0.0)
            col = (d * S5_GB + g) * LANE
            if transpose:
                o_ref[col:col + LANE, :] = tile.T.astype(BF16)
            else:
                o_ref[:, col:col + LANE] = tile.astype(BF16)


def _s5_out_kernel(u_ref, hin_ref, k2_ref, amout_ref, y_ref, t_ref, mout_ref):
    nl = u_ref.shape[0]

    @pl.when(pl.program_id(1) == 0)
    def _():
        _s5_spread(amout_ref[0], mout_ref, True)
        row_g = lax.broadcasted_iota(jnp.int32, (LANE, LANE), 0) // S5_GROUP
        col_g = lax.broadcasted_iota(jnp.int32, (LANE, LANE), 1) // S5_GROUP
        spread = (lax.broadcasted_iota(jnp.int32, (S5_GROUP, LANE), 1) % S5_GROUP
                  == lax.broadcasted_iota(jnp.int32, (S5_GROUP, LANE), 0)).astype(BF16)
        for j in range(2 * nl - 1):
            rep = jnp.dot(k2_ref[0, j].astype(BF16), spread, preferred_element_type=F32)
            tile = jnp.where(row_g == col_g, rep, 0.0).astype(BF16)
            for s in range(nl):
                t = s + j - (nl - 1)
                if 0 <= t < nl:
                    t_ref[s * LANE:(s + 1) * LANE, t * LANE:(t + 1) * LANE] = tile

    u = jnp.concatenate([u_ref[s] for s in range(nl)], axis=1)
    y = (jnp.dot(u, t_ref[...], preferred_element_type=F32)
         + jnp.dot(hin_ref[0], mout_ref[...], preferred_element_type=F32))
    for t in range(nl):
        y_ref[t] = y[:, t * LANE:(t + 1) * LANE]


def _s5(u, bsz, ops, tr=256):
    k2, a_min, a_mout, lam_blk = ops
    n, w = u.shape
    L = S5_CHUNK
    rows = n // L
    nblk = w // LANE
    kw = L * LANE
    sw = lam_blk.shape[2]
    tr = min(tr, rows)
    u3 = u.reshape(rows, L, w).transpose(1, 0, 2).astype(BF16)
    hin = pl.pallas_call(
        functools.partial(_s5_state_kernel, nb=bsz),
        grid=(nblk,),
        in_specs=[
            pl.BlockSpec((L, rows, LANE), lambda b: (0, 0, b)),
            pl.BlockSpec((1,) + a_min.shape[1:], lambda b: (b, 0, 0)),
            pl.BlockSpec((1, 1, sw), lambda b: (b, 0, 0)),
        ],
        out_specs=pl.BlockSpec((1, rows, sw), lambda b: (b, 0, 0)),
        out_shape=jax.ShapeDtypeStruct((nblk, rows, sw), BF16),
        scratch_shapes=[pltpu.VMEM((rows, sw), F32), pltpu.VMEM((kw, sw), BF16)],
        compiler_params=_params("arbitrary"),
        name="s5_state",
    )(u3, a_min, lam_blk)
    y3 = pl.pallas_call(
        _s5_out_kernel,
        grid=(nblk, rows // tr),
        in_specs=[
            pl.BlockSpec((L, tr, LANE), lambda b, i: (0, i, b)),
            pl.BlockSpec((1, tr, sw), lambda b, i: (b, i, 0)),
            pl.BlockSpec((1,) + k2.shape[1:], lambda b, i: (b, 0, 0, 0)),
            pl.BlockSpec((1,) + a_mout.shape[1:], lambda b, i: (b, 0, 0)),
        ],
        out_specs=pl.BlockSpec((L, tr, LANE), lambda b, i: (0, i, b)),
        out_shape=jax.ShapeDtypeStruct((L, rows, w), F32),
        scratch_shapes=[pltpu.VMEM((kw, kw), BF16), pltpu.VMEM((sw, kw), BF16)],
        compiler_params=_params("arbitrary", "arbitrary"),
        name="s5_out",
    )(u3, hin, k2, a_mout)
    return y3.transpose(1, 0, 2).reshape(n, w)


def _log_sigmoid(z):
    return -_softplus(-z)


def _tri(length, d):
    ti = lax.broadcasted_iota(jnp.int32, (length, length), 0)
    si = lax.broadcasted_iota(jnp.int32, (length, length), 1)
    return (ti - si) * (1 - 2 * d) >= 0


def _mlstm_kernel(q_ref, k_ref, v_ref, g_ref, gb_ref, o_ref, c_ref, n_ref, m_ref):
    d = pl.program_id(0)
    length = q_ref.shape[0]
    nh, dh = MLSTM_HEADS, MLSTM_DH

    @pl.when(pl.program_id(2) == 0)
    def _():
        c_ref[...] = jnp.zeros_like(c_ref)
        n_ref[...] = jnp.zeros_like(n_ref)
        m_ref[...] = jnp.zeros_like(m_ref)

    causal = _tri(length, d)
    gp = g_ref[...] + gb_ref[...]
    gp = jnp.where(d == 0, gp, pltpu.roll(gp, LANE - 2 * nh, 1))
    lf = _log_sigmoid(gp)
    cum = jnp.dot(causal.astype(F32), lf, precision=HI, preferred_element_type=F32)
    tot = jnp.sum(lf, axis=0, keepdims=True)
    gp_t = gp.T
    cum_t = cum.T
    nt = (((1,), (1,)), ((), ()))
    tn = (((0,), (0,)), ((), ()))

    for h in range(nh):
        sl = slice(h * dh, (h + 1) * dh)
        qh = (q_ref[:, sl] * (dh ** -0.5)).astype(BF16)
        kf = k_ref[:, sl]
        kh = kf.astype(BF16)
        vh = v_ref[:, sl].astype(BF16)
        ig_c, cum_c = gp[:, h:h + 1], cum[:, nh + h:nh + h + 1]
        ig_r, cum_r = gp_t[h:h + 1, :], cum_t[nh + h:nh + h + 1, :]
        tot_h = tot[:, nh + h:nh + h + 1]
        m_st = m_ref[h:h + 1, 0:1]
        c_st = c_ref[h]
        dmat = jnp.where(causal, cum_c - cum_r + ig_r, -jnp.inf)
        m_inter = cum_c + m_st
        m_t = jnp.maximum(jnp.max(dmat, axis=1, keepdims=True), m_inter)
        w_inter = jnp.exp(m_inter - m_t)
        s = lax.dot_general(qh, kh, nt, preferred_element_type=F32) * jnp.exp(dmat - m_t)
        num = (jnp.dot(s.astype(BF16), vh, preferred_element_type=F32)
               + w_inter * jnp.dot(qh, c_st.astype(BF16), preferred_element_type=F32))
        den = (jnp.sum(s, axis=1, keepdims=True)
               + w_inter * jnp.sum(qh.astype(F32) * n_ref[h:h + 1, :], axis=1, keepdims=True))
        o_ref[0, :, sl] = num / jnp.maximum(jnp.abs(den), jnp.exp(-m_t))
        dec = tot_h - cum_c + ig_c
        m_new = jnp.maximum(tot_h + m_st, jnp.max(dec, axis=0, keepdims=True))
        wc = jnp.exp(tot_h + m_st - m_new)
        kw = jnp.exp(dec - m_new) * kf
        c_ref[h] = wc * c_st + lax.dot_general(kw.astype(BF16), vh, tn, preferred_element_type=F32)
        n_ref[h:h + 1, :] = wc * n_ref[h:h + 1, :] + jnp.sum(kw, axis=0, keepdims=True)
        m_ref[h:h + 1, :] = jnp.broadcast_to(m_new, (1, LANE))


def _mlstm(proj, tail, bsz, gate_b, length=256):
    n = proj.shape[0]
    seq = n // bsz
    length = min(length, seq)
    nc = seq // length
    w = W_MLSTM
    q_blk = 2 * W_LRU // w
    gb = jnp.pad(gate_b.reshape(1, -1), ((0, 0), (0, LANE - gate_b.size)))

    def row(d, b, c):
        return b * nc + c + d * (nc - 1 - 2 * c)

    return pl.pallas_call(
        _mlstm_kernel,
        grid=(2, bsz, nc),
        in_specs=[
            pl.BlockSpec((length, w), lambda d, b, c: (row(d, b, c), q_blk)),
            pl.BlockSpec((length, w), lambda d, b, c: (row(d, b, c), q_blk + 1)),
            pl.BlockSpec((length, w), lambda d, b, c: (row(d, b, c), q_blk + 2)),
            pl.BlockSpec((length, LANE), lambda d, b, c: (row(d, b, c), 0)),
            pl.BlockSpec((1, LANE), lambda d, b, c: (0, 0)),
        ],
        out_specs=pl.BlockSpec((1, length, w), lambda d, b, c: (d, row(d, b, c), 0)),
        out_shape=jax.ShapeDtypeStruct((2, n, w), F32),
        scratch_shapes=[pltpu.VMEM((MLSTM_HEADS, MLSTM_DH, MLSTM_DH), F32),
                        pltpu.VMEM((SUBLANE, MLSTM_DH), F32), pltpu.VMEM((SUBLANE, LANE), F32)],
        compiler_params=_params("arbitrary", "arbitrary", "arbitrary"),
        name="mlstm",
    )(proj, proj, proj, tail, gb)


GLA_SUB = 16
GLA_SAFE_DECAY = 40.0


def _gla_chunk_dense(q_ref, k_ref, v_ref, cum, o_ref, s_ref, rev):
    length = q_ref.shape[0]
    nt = (((1,), (1,)), ((), ()))
    tn = (((0,), (0,)), ((), ()))
    causal = _tri(length, int(rev))
    last = slice(0, 1) if rev else slice(length - 1, length)
    for h in range(GLA_HEADS):
        kl = slice(h * GLA_DK, (h + 1) * GLA_DK)
        vl = slice(h * GLA_DV, (h + 1) * GLA_DV)
        ch = cum[:, kl]
        kk = k_ref[:, kl]
        vv = v_ref[:, vl].astype(BF16)
        st = s_ref[h]
        tot = ch[last]
        qe = (q_ref[:, kl] * (GLA_DK ** -0.5) * jnp.exp(ch)).astype(BF16)
        ke = (kk * jnp.exp(-ch)).astype(BF16)
        att = jnp.where(causal, lax.dot_general(qe, ke, nt, preferred_element_type=F32), 0.0)
        o_ref[0, :, vl] = (jnp.dot(att.astype(BF16), vv, preferred_element_type=F32)
                           + lax.dot_general(qe, st.astype(BF16), nt, preferred_element_type=F32))
        kd = (kk * jnp.exp(tot - ch)).astype(BF16)
        s_ref[h] = jnp.exp(tot) * st + lax.dot_general(vv, kd, tn, preferred_element_type=F32)


def _gla_chunk(q_ref, k_ref, v_ref, cum, o_ref, s_ref, rev):
    length = q_ref.shape[0]
    c = GLA_SUB
    nt = (((1,), (1,)), ((), ()))
    tn = (((0,), (0,)), ((), ()))

    def rows(a, b):
        return slice(length - b, length - a) if rev else slice(a, b)

    def row(a):
        i = length - 1 - a if rev else a
        return slice(i, i + 1)

    ti = lax.broadcasted_iota(jnp.int32, (c, 1), 0)
    for h in range(GLA_HEADS):
        kl = slice(h * GLA_DK, (h + 1) * GLA_DK)
        vl = slice(h * GLA_DV, (h + 1) * GLA_DV)
        ch = cum[:, kl]
        qs = q_ref[:, kl] * (GLA_DK ** -0.5)
        kk = k_ref[:, kl]
        vf = v_ref[:, vl]
        vv = vf.astype(BF16)
        st = s_ref[h]
        tot = ch[row(length - 1)]
        o = lax.dot_general((qs * jnp.exp(ch)).astype(BF16), st.astype(BF16), nt, preferred_element_type=F32)
        ob = [o[rows(p * c, (p + 1) * c)] for p in range(length // c)]
        m = length // 2
        while m >= c:
            for start in range(0, length, 2 * m):
                fst, sec = rows(start, start + m), rows(start + m, start + 2 * m)
                r = ch[row(start + m)]
                qh = (qs[sec] * jnp.exp(ch[sec] - r)).astype(BF16)
                kh = (kk[fst] * jnp.exp(r - ch[fst])).astype(BF16)
                att = lax.dot_general(qh, kh, nt, preferred_element_type=F32)
                contrib = jnp.dot(att.astype(BF16), vv[fst], preferred_element_type=F32)
                for p in range((start + m) // c, (start + 2 * m) // c):
                    lo = rows(p * c, (p + 1) * c).start - sec.start
                    ob[p] = ob[p] + contrib[lo:lo + c]
            m //= 2
        for p in range(length // c):
            blk = rows(p * c, (p + 1) * c)
            cb, qb, kb, vb = ch[blk], qs[blk], kk[blk], vf[blk]
            acc = ob[p]
            for s in range(c):
                sees = (ti <= s) if rev else (ti >= s)
                e = jnp.exp(jnp.where(sees, cb - cb[s:s + 1], -jnp.inf))
                a = jnp.sum(qb * kb[s:s + 1] * e, axis=1, keepdims=True)
                acc = acc + a * vb[s:s + 1]
            o_ref[0, blk, vl] = acc
        kd = (kk * jnp.exp(tot - ch)).astype(BF16)
        s_ref[h] = jnp.exp(tot) * st + lax.dot_general(vv, kd, tn, preferred_element_type=F32)


def _gla_kernel(q_ref, k_ref, v_ref, low_ref, wg_ref, bg_ref, o_ref, s_ref):
    d = pl.program_id(0)
    length = q_ref.shape[0]

    @pl.when(pl.program_id(2) == 0)
    def _():
        s_ref[...] = jnp.zeros_like(s_ref)

    gate_pre = jnp.dot(low_ref[...], wg_ref[0], precision=HI, preferred_element_type=F32) + bg_ref[0]
    la = _log_sigmoid(gate_pre) * (1.0 / GLA_TAU)
    cum = jnp.dot(_tri(length, d).astype(F32), la, precision=HI, preferred_element_type=F32)
    safe = jnp.sum(la, axis=0, keepdims=True).min() >= -GLA_SAFE_DECAY
    for rev in (False, True):
        pl.when((d == int(rev)) & safe)(
            functools.partial(_gla_chunk_dense, q_ref, k_ref, v_ref, cum, o_ref, s_ref, rev))
        pl.when((d == int(rev)) & jnp.logical_not(safe))(
            functools.partial(_gla_chunk, q_ref, k_ref, v_ref, cum, o_ref, s_ref, rev))


def _gla(proj, tail, bsz, w_gate2, gate_b, length=128):
    n = proj.shape[0]
    seq = n // bsz
    length = min(length, seq)
    nc = seq // length
    wg = jnp.zeros((2, LANE, GLA_QK), F32)
    for d in range(2):
        wg = wg.at[d, d * GLA_RANK:(d + 1) * GLA_RANK].set(w_gate2[d])

    def row(d, b, c):
        return b * nc + c + d * (nc - 1 - 2 * c)

    return pl.pallas_call(
        _gla_kernel,
        grid=(2, bsz, nc),
        in_specs=[
            pl.BlockSpec((length, GLA_QK), lambda d, b, c: (row(d, b, c), 0)),
            pl.BlockSpec((length, GLA_QK), lambda d, b, c: (row(d, b, c), 1)),
            pl.BlockSpec((length, GLA_V), lambda d, b, c: (row(d, b, c), 2 * GLA_QK // GLA_V)),
            pl.BlockSpec((length, LANE), lambda d, b, c: (row(d, b, c), 0)),
            pl.BlockSpec((1, LANE, GLA_QK), lambda d, b, c: (d, 0, 0)),
            pl.BlockSpec((1, 1, GLA_QK), lambda d, b, c: (d, 0, 0)),
        ],
        out_specs=pl.BlockSpec((1, length, GLA_V), lambda d, b, c: (d, row(d, b, c), 0)),
        out_shape=jax.ShapeDtypeStruct((2, n, GLA_V), F32),
        scratch_shapes=[pltpu.VMEM((GLA_HEADS, GLA_DV, GLA_DK), F32)],
        compiler_params=_params("arbitrary", "arbitrary", "arbitrary"),
        name="gla",
    )(proj, proj, proj, tail, wg, gate_b.reshape(2, 1, GLA_QK))


def _head_norm(t, g, heads):
    dh = t.shape[1] // heads
    outs = []
    for h in range(heads):
        th = t[:, h * dh:(h + 1) * dh]
        outs.append(th * lax.rsqrt(jnp.mean(th * th, axis=1, keepdims=True) + NORM_EPS))
    return jnp.concatenate(outs, axis=1) * g


def _ab_out_kernel(x_ref, gr_ref, og_ref, hl_ref, hm_ref, g_ref, w_ref, o_ref):
    k = gr_ref.shape[1]
    ya = jax.nn.gelu(gr_ref[...]) * (hl_ref[0] + hl_ref[1])
    yb = jax.nn.sigmoid(og_ref[...]) * _head_norm(hm_ref[0] + hm_ref[1], g_ref[...], MLSTM_HEADS)
    o_ref[...] = (x_ref[...]
                  + jnp.dot(ya.astype(BF16), w_ref[0:k, :], preferred_element_type=F32)
                  + jnp.dot(yb.astype(BF16), w_ref[k:, :], preferred_element_type=F32))


def _ab_out(x, proj, hl, hm, norm_g, w, tm=512):
    n, d = x.shape
    k = W_LRU
    tm = min(tm, n)
    return pl.pallas_call(
        _ab_out_kernel,
        grid=(n // tm,),
        in_specs=[
            pl.BlockSpec((tm, d), lambda i: (i, 0)),
            pl.BlockSpec((tm, k), lambda i: (i, 1)),
            pl.BlockSpec((tm, k), lambda i: (i, 5)),
            pl.BlockSpec((2, tm, k), lambda i: (0, i, 0)),
            pl.BlockSpec((2, tm, k), lambda i: (0, i, 0)),
            pl.BlockSpec((1, k), lambda i: (0, 0)),
            pl.BlockSpec((2 * k, d), lambda i: (0, 0)),
        ],
        out_specs=pl.BlockSpec((tm, d), lambda i: (i, 0)),
        out_shape=jax.ShapeDtypeStruct((n, d), F32),
        compiler_params=_params("arbitrary"),
        name="ab_out",
    )(x, proj, proj, hl, hm, norm_g.reshape(1, k), w)


def _cd_out_kernel(x_ref, r_ref, og_ref, g_ref, ys_ref, u_ref, d_ref, wg_ref, w_ref, o_ref):
    k = r_ref.shape[1]
    r = r_ref[...]
    yc = _head_norm(og_ref[0] + og_ref[1], g_ref[...], GLA_HEADS) * (r * jax.nn.sigmoid(r))
    y = jax.nn.gelu(ys_ref[...] + d_ref[...] * u_ref[...])
    yd = y * jax.nn.sigmoid(jnp.dot(y.astype(BF16), wg_ref[...], preferred_element_type=F32))
    o_ref[...] = (x_ref[...]
                  + jnp.dot(yc.astype(BF16), w_ref[0:k, :], preferred_element_type=F32)
                  + jnp.dot(yd.astype(BF16), w_ref[k:, :], preferred_element_type=F32))


def _cd_out(x, proj, og, norm_g, ys, s5_d, w_glu, w, tm=512):
    n, d = x.shape
    k = S5_W
    tm = min(tm, n)
    return pl.pallas_call(
        _cd_out_kernel,
        grid=(n // tm,),
        in_specs=[
            pl.BlockSpec((tm, d), lambda i: (i, 0)),
            pl.BlockSpec((tm, k), lambda i: (i, 2)),
            pl.BlockSpec((2, tm, k), lambda i: (0, i, 0)),
            pl.BlockSpec((1, k), lambda i: (0, 0)),
            pl.BlockSpec((tm, k), lambda i: (i, 0)),
            pl.BlockSpec((tm, k), lambda i: (i, 3)),
            pl.BlockSpec((1, k), lambda i: (0, 0)),
            pl.BlockSpec((k, k), lambda i: (0, 0)),
            pl.BlockSpec((2 * k, d), lambda i: (0, 0)),
        ],
        out_specs=pl.BlockSpec((tm, d), lambda i: (i, 0)),
        out_shape=jax.ShapeDtypeStruct((n, d), F32),
        compiler_params=_params("arbitrary"),
        name="cd_out",
    )(x, proj, og, norm_g.reshape(1, k), ys, proj, s5_d.reshape(1, k), w_glu, w)


def _split_small(w, n_main):
    tail = w[:, n_main:]
    tail = jnp.pad(tail, ((0, 0), (0, LANE - tail.shape[1])))
    return w[:, :n_main].astype(BF16), tail.astype(BF16)


def kernel(x, norm_ffn1, ffn1_w_gu, ffn1_w_down, norm_mix, norm_ffn2, ffn2_w_gu, ffn2_w_down,
           ab_w_in, lru_conv_w, lru_conv_b, lru_gate_w, lru_gate_b, lru_lambda, mlstm_gate_b,
           mlstm_norm, ab_w_out, cd_w_in, gla_w_gate2, gla_gate_b, gla_norm, s5_a_re, s5_a_im,
           s5_log_dt, s5_b_re, s5_b_im, s5_c_re, s5_c_im, s5_d, s5_w_glu, cd_w_out, final_norm):
    bsz, seq, d = x.shape
    n = bsz * seq
    depth = norm_ffn1.shape[0]
    xf = x.reshape(n, d)
    w1_gu, w1_down = ffn1_w_gu.astype(BF16), ffn1_w_down.astype(BF16)
    w2_gu, w2_down = ffn2_w_gu.astype(BF16), ffn2_w_down.astype(BF16)
    for l in range(depth):
        xf = _ffn(xf, norm_ffn1[l], w1_gu, w1_down, l)
        j = l // 2
        if l % 2 == 0:
            w_main, w_tail = _split_small(ab_w_in[j], 2 * W_LRU + 4 * W_MLSTM)
            proj, tail = _norm_proj(xf, norm_mix[l], w_main, w_tail)
            hl = _lru(proj, bsz, lru_conv_w[j], lru_conv_b[j], lru_gate_w[j], lru_gate_b[j], lru_lambda[j])
            hm = _mlstm(proj, tail, bsz, mlstm_gate_b[j])
            xf = _ab_out(xf, proj, hl, hm, mlstm_norm[j], ab_w_out[j].astype(BF16))
        else:
            w = cd_w_in[j]
            n_gla = 2 * GLA_QK + 2 * GLA_V
            w = jnp.concatenate([w[:, :n_gla], w[:, n_gla + 2 * GLA_RANK:], w[:, n_gla:n_gla + 2 * GLA_RANK]], axis=1)
            w_main, w_tail = _split_small(w, n_gla + S5_W)
            proj, tail = _norm_proj(xf, norm_mix[l], w_main, w_tail)
            og = _gla(proj, tail, bsz, gla_w_gate2[j], gla_gate_b[j])
            ops = _s5_operators(s5_a_re[j], s5_a_im[j], s5_log_dt[j], s5_b_re[j], s5_b_im[j],
                                s5_c_re[j], s5_c_im[j])
            ys = _s5(proj[:, n_gla:], bsz, ops)
            xf = _cd_out(xf, proj, og, gla_norm[j], ys, s5_d[j],
                         s5_w_glu[j].astype(BF16), cd_w_out[j].astype(BF16))
        xf = _ffn(xf, norm_ffn2[l], w2_gu, w2_down, l, final_g=final_norm if l == depth - 1 else None)
    return xf.reshape(bsz, seq, d)
```

```python
import functools

import jax
import jax.numpy as jnp
from jax import lax
from jax.experimental import pallas as pl
from jax.experimental.pallas import tpu as pltpu

NORM_EPS = 1e-6

W_LRU = 1024
LRU_HEADS = 8
LRU_BLOCK = W_LRU // LRU_HEADS
CONV_W = 4
CONV_PAD_L = 2
LRU_C = 8.0

W_MLSTM = 1024
MLSTM_HEADS = 4
MLSTM_DH = W_MLSTM // MLSTM_HEADS

GLA_HEADS = 4
GLA_DK = 128
GLA_DV = 256
GLA_QK = GLA_HEADS * GLA_DK
GLA_V = GLA_HEADS * GLA_DV
GLA_RANK = 16
GLA_TAU = 16.0

S5_W = 1024
S5_GROUP = 16
S5_GROUPS = S5_W // S5_GROUP
S5_P = 64
S5_CHUNK = 16
S5_GB = 8
S5_ROW_TILE = 256

LANE = 128
SUBLANE = 8
VMEM_LIMIT = 52 * 1024 * 1024

BF16 = jnp.bfloat16
F32 = jnp.float32


def _rms(x, g):
    ms = jnp.mean(x * x, axis=-1, keepdims=True)
    return x * lax.rsqrt(ms + NORM_EPS) * g


def _softplus(z):
    return jnp.maximum(z, 0.0) + jnp.log1p(jnp.exp(-jnp.abs(z)))


def _params(*sem):
    return pltpu.CompilerParams(dimension_semantics=sem, vmem_limit_bytes=VMEM_LIMIT)


def _ffn_kernel(x_ref, g_ref, wg_ref, wu_ref, wd_ref, *rest, final):
    if final:
        fg_ref, o_ref, h_ref = rest
    else:
        o_ref, h_ref = rest
    j = pl.program_id(1)

    @pl.when(j == 0)
    def _():
        x = x_ref[...]
        h_ref[...] = _rms(x, g_ref[...]).astype(BF16)
        o_ref[...] = x

    h = h_ref[...]
    g = jnp.dot(h, wg_ref[...], preferred_element_type=F32)
    u = jnp.dot(h, wu_ref[...], preferred_element_type=F32)
    a = (0.5 * g * jax.nn.sigmoid(g) * u).astype(BF16)
    o_ref[...] += jnp.dot(a, wd_ref[...], preferred_element_type=F32)

    if final:
        @pl.when(j == pl.num_programs(1) - 1)
        def _():
            o_ref[...] = _rms(o_ref[...], fg_ref[...])


def _ffn(x, g, w_gu, w_down, layer, final_g=None, tm=512, tf=512):
    n, d = x.shape
    f = w_down.shape[1]
    tm = min(tm, n)
    nj = f // tf
    final = final_g is not None
    in_specs = [
        pl.BlockSpec((tm, d), lambda i, j: (i, 0)),
        pl.BlockSpec((1, d), lambda i, j: (0, 0)),
        pl.BlockSpec((None, d, tf), lambda i, j: (layer, 0, j)),
        pl.BlockSpec((None, d, tf), lambda i, j: (layer, 0, j + nj)),
        pl.BlockSpec((None, tf, d), lambda i, j: (layer, j, 0)),
    ]
    args = [x, g.reshape(1, d), w_gu, w_gu, w_down]
    if final:
        in_specs.append(pl.BlockSpec((1, d), lambda i, j: (0, 0)))
        args.append(final_g.reshape(1, d))
    return pl.pallas_call(
        functools.partial(_ffn_kernel, final=final),
        grid=(n // tm, nj),
        in_specs=in_specs,
        out_specs=pl.BlockSpec((tm, d), lambda i, j: (i, 0)),
        out_shape=jax.ShapeDtypeStruct((n, d), F32),
        scratch_shapes=[pltpu.VMEM((tm, d), BF16)],
        compiler_params=_params("arbitrary", "arbitrary"),
        name="ffn_final" if final else "ffn",
    )(*args)


def _norm_proj_kernel(x_ref, g_ref, w_ref, ws_ref, o_ref, os_ref, h_ref):
    @pl.when(pl.program_id(1) == 0)
    def _():
        h = _rms(x_ref[...], g_ref[...]).astype(BF16)
        h_ref[...] = h
        os_ref[...] = jnp.dot(h, ws_ref[...], preferred_element_type=F32)

    o_ref[...] = jnp.dot(h_ref[...], w_ref[...], preferred_element_type=F32)


def _norm_proj(x, g, w, w_small, tm=1024, tn=1024):
    n, d = x.shape
    m = w.shape[1]
    ms = w_small.shape[1]
    tm = min(tm, n)
    return pl.pallas_call(
        _norm_proj_kernel,
        grid=(n // tm, m // tn),
        in_specs=[
            pl.BlockSpec((tm, d), lambda i, j: (i, 0)),
            pl.BlockSpec((1, d), lambda i, j: (0, 0)),
            pl.BlockSpec((d, tn), lambda i, j: (0, j)),
            pl.BlockSpec((d, ms), lambda i, j: (0, 0)),
        ],
        out_specs=[pl.BlockSpec((tm, tn), lambda i, j: (i, j)),
                   pl.BlockSpec((tm, ms), lambda i, j: (i, 0))],
        out_shape=[jax.ShapeDtypeStruct((n, m), F32), jax.ShapeDtypeStruct((n, ms), F32)],
        scratch_shapes=[pltpu.VMEM((tm, d), BF16)],
        compiler_params=_params("arbitrary", "arbitrary"),
        name="norm_proj",
    )(x, g.reshape(1, d), w, w_small)


def _lru_kernel(xp_ref, x_ref, xn_ref, cw_ref, cb_ref, gw_ref, gb_ref, lam_ref, o_ref,
                xs_ref, a_ref, b_ref, h_ref, *, tt):
    d = pl.program_id(0)
    t = pl.program_id(2)
    nt = pl.num_programs(2)
    tb = jnp.where(d == 0, t, nt - 1 - t)
    w = x_ref.shape[1]

    xs_ref[0:SUBLANE, :] = jnp.where(tb > 0, xp_ref[...], 0.0)
    xs_ref[SUBLANE:SUBLANE + tt, :] = x_ref[...]
    xs_ref[SUBLANE + tt:2 * SUBLANE + tt, :] = jnp.where(tb < nt - 1, xn_ref[...], 0.0)
    cw = cw_ref[...]
    xc = cb_ref[...]
    for k in range(CONV_W):
        off = SUBLANE + k - CONV_PAD_L
        xc = xc + cw[k:k + 1, :] * xs_ref[off:off + tt, :]

    xcb = xc.astype(BF16)
    gb = gb_ref[0]
    sp = _softplus(-lam_ref[0])
    for hd in range(w // LRU_BLOCK):
        sl = slice(hd * LRU_BLOCK, (hd + 1) * LRU_BLOCK)
        z = jnp.dot(xcb[:, sl], gw_ref[0, hd], preferred_element_type=F32)
        r = jax.nn.sigmoid(z[:, :LRU_BLOCK] + gb[0:1, sl])
        i = jax.nn.sigmoid(z[:, LRU_BLOCK:] + gb[1:2, sl])
        log_a = (-LRU_C * sp[:, sl]) * r
        a = jnp.exp(log_a)
        a_ref[:, sl] = a
        b_ref[:, sl] = jnp.sqrt(-jnp.tanh(log_a) * (a * a + 1.0)) * (i * xc[:, sl])

    @pl.when(t == 0)
    def _():
        h_ref[...] = jnp.zeros_like(h_ref)

    sub = lax.broadcasted_iota(jnp.int32, (SUBLANE, w), 0)
    ntile = tt // SUBLANE

    def scan(rev):
        def earlier(x, k, fill):
            if rev:
                return jnp.where(sub < SUBLANE - k, pltpu.roll(x, SUBLANE - k, 0), fill)
            return jnp.where(sub >= k, pltpu.roll(x, k, 0), fill)

        def body(jt, h):
            r0 = pl.multiple_of((ntile - 1 - jt if rev else jt) * SUBLANE, SUBLANE)
            a = a_ref[pl.ds(r0, SUBLANE), :]
            b = b_ref[pl.ds(r0, SUBLANE), :]
            k = 1
            while k < SUBLANE:
                a, b = a * earlier(a, k, 1.0), b + a * earlier(b, k, 0.0)
                k *= 2
            tile = a * h + b
            o_ref[0, pl.ds(r0, SUBLANE), :] = tile
            last = 0 if rev else SUBLANE - 1
            return tile[last:last + 1, :]

        h_ref[...] = lax.fori_loop(0, ntile, body, h_ref[...])

    pl.when(d == 0)(functools.partial(scan, False))
    pl.when(d == 1)(functools.partial(scan, True))


def _lru(proj, bsz, conv_w, conv_b, gate_w, gate_b, lam, tt=256):
    n = proj.shape[0]
    seq = n // bsz
    tt = min(tt, seq)
    nt = seq // tt
    w = W_LRU
    r8 = tt // SUBLANE
    gw = jnp.concatenate([gate_w[:, 0], gate_w[:, 1]], axis=-1).astype(BF16)

    def tb(d, t):
        return t + d * (nt - 1 - 2 * t)

    return pl.pallas_call(
        functools.partial(_lru_kernel, tt=tt),
        grid=(2, bsz, nt),
        in_specs=[
            pl.BlockSpec((SUBLANE, w), lambda d, b, t: (jnp.maximum((b * nt + tb(d, t)) * r8 - 1, 0), 0)),
            pl.BlockSpec((tt, w), lambda d, b, t: (b * nt + tb(d, t), 0)),
            pl.BlockSpec((SUBLANE, w), lambda d, b, t: (jnp.minimum((b * nt + tb(d, t) + 1) * r8, n // SUBLANE - 1), 0)),
            pl.BlockSpec((CONV_W, w), lambda d, b, t: (0, 0)),
            pl.BlockSpec((1, w), lambda d, b, t: (0, 0)),
            pl.BlockSpec((1, LRU_HEADS, LRU_BLOCK, 2 * LRU_BLOCK), lambda d, b, t: (d, 0, 0, 0)),
            pl.BlockSpec((1, 2, w), lambda d, b, t: (d, 0, 0)),
            pl.BlockSpec((1, 1, w), lambda d, b, t: (d, 0, 0)),
        ],
        out_specs=pl.BlockSpec((1, tt, w), lambda d, b, t: (d, b * nt + tb(d, t), 0)),
        out_shape=jax.ShapeDtypeStruct((2, n, w), F32),
        scratch_shapes=[pltpu.VMEM((tt + 2 * SUBLANE, w), F32), pltpu.VMEM((tt, w), F32),
                        pltpu.VMEM((tt, w), F32), pltpu.VMEM((1, w), F32)],
        compiler_params=_params("arbitrary", "arbitrary", "arbitrary"),
        name="rg_lru",
    )(proj, proj, proj, conv_w, conv_b.reshape(1, w), gw, gate_b, lam.reshape(2, 1, w))


def _s5_discretise(a_re, a_im, log_dt, b_re, b_im):
    dt = jnp.exp(log_dt)[:, None]
    mag = jnp.exp(dt * a_re)
    lr = mag * jnp.cos(dt * a_im)
    li = mag * jnp.sin(dt * a_im)
    den = a_re * a_re + a_im * a_im
    nr = lr - 1.0
    cr = (nr * a_re + li * a_im) / den
    ci = (li * a_re - nr * a_im) / den
    bbr = cr[..., None] * b_re - ci[..., None] * b_im
    bbi = cr[..., None] * b_im + ci[..., None] * b_re
    return lr, li, bbr, bbi


def _s5_operators(a_re, a_im, log_dt, b_re, b_im, c_re, c_im):
    L = S5_CHUNK
    G, P, C = S5_GROUPS, S5_P, S5_GROUP
    kfs, mins, mouts, lams = [], [], [], []
    for d in range(2):
        lr, li, bbr, bbi = _s5_discretise(a_re[d], a_im[d], log_dt[d], b_re, b_im)
        bbr, bbi = bbr.transpose(0, 2, 1), bbi.transpose(0, 2, 1)
        jj = jnp.arange(L + 1, dtype=F32)[:, None, None, None]
        dt = jnp.exp(log_dt[d])[None, :, None, None]
        mag = jnp.exp(jj * dt * a_re[d][None, :, None, :])
        pr = mag * jnp.cos(jj * dt * a_im[d][None, :, None, :])
        pi = mag * jnp.sin(jj * dt * a_im[d][None, :, None, :])
        cl = jnp.concatenate([c_re[None] * pr - c_im[None] * pi, -(c_re[None] * pi + c_im[None] * pr)], axis=-1)
        bl = jnp.concatenate([pr * bbr[None] - pi * bbi[None], pr * bbi[None] + pi * bbr[None]], axis=-1)
        kfs.append(jnp.sum(cl[:L, :, None, :, :] * bl[0][None, :, :, None, :], axis=-1))
        if d == 0:
            e_in = jnp.arange(L - 1, -1, -1)
            e_out = jnp.arange(1, L + 1)
        else:
            e_in = jnp.arange(L)
            e_out = jnp.arange(L, 0, -1)
        mins.append(bl[e_in])
        mouts.append(cl[e_out])
        lams.append(jnp.stack([pr[L, :, 0], pi[L, :, 0]], axis=1))
    kf, kb = kfs
    kfull = jnp.concatenate([kb[:0:-1], (kf[0] + kb[0])[None], kf[1:]], axis=0)
    gb = S5_GB
    nb_ = G // gb
    k2 = kfull.reshape(2 * L - 1, nb_, gb * C, C).transpose(1, 0, 2, 3)

    def rows_of_block(parts):
        a = jnp.concatenate(parts, axis=-1).reshape(L, nb_, gb * C, 4 * P)
        return a.transpose(1, 0, 2, 3).reshape(nb_, L * gb * C, 4 * P)

    lam_l = jnp.concatenate(lams, axis=1)
    lam_blk = lam_l.reshape(nb_, gb, 4, P).transpose(0, 2, 1, 3).reshape(nb_, 1, 4 * gb * P)
    return k2, rows_of_block(mins), rows_of_block(mouts), lam_blk


def _s5_state_kernel(u_ref, amin_ref, lam_ref, hin_ref, h_ref, min_ref, *, nb):
    nl, rows, _ = u_ref.shape
    _s5_spread(amin_ref[0], min_ref, False)
    rb = rows // nb
    hw = h_ref.shape[1] // 4
    rt = min(rows, S5_ROW_TILE)
    lo = lax.broadcasted_iota(jnp.int32, (rt, LANE), 1) < S5_P
    ng = hw // S5_P

    def pairs():
        for d in range(2):
            for k in range(ng // 2):
                yield ((d * ng + 2 * k) * LANE, (d * ng + 2 * k + 1) * LANE,
                       2 * d * hw + k * LANE, (2 * d + 1) * hw + k * LANE)

    for r0 in range(0, rows, rt):
        u = jnp.concatenate([u_ref[s, r0:r0 + rt, :] for s in range(nl)], axis=1)
        hl = jnp.dot(u, min_ref[...], preferred_element_type=F32)
        for ca, cb, cre, cim in pairs():
            a, b = hl[:, ca:ca + LANE], hl[:, cb:cb + LANE]
            h_ref[r0:r0 + rt, cre:cre + LANE] = jnp.where(lo, a, pltpu.roll(b, S5_P, 1))
            h_ref[r0:r0 + rt, cim:cim + LANE] = jnp.where(lo, pltpu.roll(a, S5_P, 1), b)
    lam = lam_ref[0]
    lrf, lif = lam[:, 0:hw], lam[:, hw:2 * hw]
    lrb, lib = lam[:, 2 * hw:3 * hw], lam[:, 3 * hw:4 * hw]
    sub = lax.broadcasted_iota(jnp.int32, (SUBLANE, hw), 0)

    def body(it, carry):
        new = []
        for b in range(nb):
            fr, fi, br, bi = carry[4 * b:4 * b + 4]
            f0 = pl.multiple_of(b * rb + it * SUBLANE, SUBLANE)
            b0 = pl.multiple_of(b * rb + rb - SUBLANE - it * SUBLANE, SUBLANE)
            tiles = [jnp.zeros((SUBLANE, hw), F32)] * 4
            lf = h_ref[pl.ds(f0, SUBLANE), 0:2 * hw]
            lb = h_ref[pl.ds(b0, SUBLANE), 2 * hw:4 * hw]
            for r in range(SUBLANE):
                q = SUBLANE - 1 - r
                tiles = [jnp.where(sub == r, fr, tiles[0]), jnp.where(sub == r, fi, tiles[1]),
                         jnp.where(sub == q, br, tiles[2]), jnp.where(sub == q, bi, tiles[3])]
                fr, fi, br, bi = (lrf * fr - lif * fi + lf[r:r + 1, 0:hw],
                                  lrf * fi + lif * fr + lf[r:r + 1, hw:2 * hw],
                                  lrb * br - lib * bi + lb[q:q + 1, 0:hw],
                                  lrb * bi + lib * br + lb[q:q + 1, hw:2 * hw])
            h_ref[pl.ds(f0, SUBLANE), 0:hw] = tiles[0]
            h_ref[pl.ds(f0, SUBLANE), hw:2 * hw] = tiles[1]
            h_ref[pl.ds(b0, SUBLANE), 2 * hw:3 * hw] = tiles[2]
            h_ref[pl.ds(b0, SUBLANE), 3 * hw:4 * hw] = tiles[3]
            new += [fr, fi, br, bi]
        return tuple(new)

    zero = jnp.zeros((1, hw), F32)
    lax.fori_loop(0, rb // SUBLANE, body, (zero,) * (4 * nb))
    for r0 in range(0, rows, rt):
        for ca, cb, cre, cim in pairs():
            re, im = h_ref[r0:r0 + rt, cre:cre + LANE], h_ref[r0:r0 + rt, cim:cim + LANE]
            hin_ref[0, r0:r0 + rt, ca:ca + LANE] = jnp.where(lo, re, pltpu.roll(im, S5_P, 1)).astype(BF16)
            hin_ref[0, r0:r0 + rt, cb:cb + LANE] = jnp.where(lo, pltpu.roll(re, S5_P, 1), im).astype(BF16)


def _s5_spread(a, o_ref, transpose):
    rows = a.shape[0]
    row_g = (lax.broadcasted_iota(jnp.int32, (rows, LANE), 0) // S5_GROUP) % S5_GB
    for d in range(2):
        ad = a[:, d * LANE:(d + 1) * LANE]
        for g in range(S5_GB):
            tile = jnp.where(row_g == g, ad, 0.0)
            col = (d * S5_GB + g) * LANE
            if transpose:
                o_ref[col:col + LANE, :] = tile.T.astype(BF16)
            else:
                o_ref[:, col:col + LANE] = tile.astype(BF16)


def _s5_out_kernel(u_ref, hin_ref, k2_ref, amout_ref, y_ref, t_ref, mout_ref):
    nl = u_ref.shape[0]

    @pl.when(pl.program_id(1) == 0)
    def _():
        _s5_spread(amout_ref[0], mout_ref, True)
        row_g = lax.broadcasted_iota(jnp.int32, (LANE, LANE), 0) // S5_GROUP
        col_g = lax.broadcasted_iota(jnp.int32, (LANE, LANE), 1) // S5_GROUP
        spread = (lax.broadcasted_iota(jnp.int32, (S5_GROUP, LANE), 1) % S5_GROUP
                  == lax.broadcasted_iota(jnp.int32, (S5_GROUP, LANE), 0)).astype(BF16)
        for j in range(2 * nl - 1):
            rep = jnp.dot(k2_ref[0, j].astype(BF16), spread, preferred_element_type=F32)
            tile = jnp.where(row_g == col_g, rep, 0.0).astype(BF16)
            for s in range(nl):
                t = s + j - (nl - 1)
                if 0 <= t < nl:
                    t_ref[s * LANE:(s + 1) * LANE, t * LANE:(t + 1) * LANE] = tile

    u = jnp.concatenate([u_ref[s] for s in range(nl)], axis=1)
    y = (jnp.dot(u, t_ref[...], preferred_element_type=F32)
         + jnp.dot(hin_ref[0], mout_ref[...], preferred_element_type=F32))
    for t in range(nl):
        y_ref[t] = y[:, t * LANE:(t + 1) * LANE]


def _s5(u, bsz, ops, tr=256):
    k2, a_min, a_mout, lam_blk = ops
    n, w = u.shape
    L = S5_CHUNK
    rows = n // L
    nblk = w // LANE
    kw = L * LANE
    sw = lam_blk.shape[2]
    tr = min(tr, rows)
    u3 = u.reshape(rows, L, w).transpose(1, 0, 2).astype(BF16)
    hin = pl.pallas_call(
        functools.partial(_s5_state_kernel, nb=bsz),
        grid=(nblk,),
        in_specs=[
            pl.BlockSpec((L, rows, LANE), lambda b: (0, 0, b)),
            pl.BlockSpec((1,) + a_min.shape[1:], lambda b: (b, 0, 0)),
            pl.BlockSpec((1, 1, sw), lambda b: (b, 0, 0)),
        ],
        out_specs=pl.BlockSpec((1, rows, sw), lambda b: (b, 0, 0)),
        out_shape=jax.ShapeDtypeStruct((nblk, rows, sw), BF16),
        scratch_shapes=[pltpu.VMEM((rows, sw), F32), pltpu.VMEM((kw, sw), BF16)],
        compiler_params=_params("arbitrary"),
        name="s5_state",
    )(u3, a_min, lam_blk)
    y3 = pl.pallas_call(
        _s5_out_kernel,
        grid=(nblk, rows // tr),
        in_specs=[
            pl.BlockSpec((L, tr, LANE), lambda b, i: (0, i, b)),
            pl.BlockSpec((1, tr, sw), lambda b, i: (b, i, 0)),
            pl.BlockSpec((1,) + k2.shape[1:], lambda b, i: (b, 0, 0, 0)),
            pl.BlockSpec((1,) + a_mout.shape[1:], lambda b, i: (b, 0, 0)),
        ],
        out_specs=pl.BlockSpec((L, tr, LANE), lambda b, i: (0, i, b)),
        out_shape=jax.ShapeDtypeStruct((L, rows, w), F32),
        scratch_shapes=[pltpu.VMEM((kw, kw), BF16), pltpu.VMEM((sw, kw), BF16)],
        compiler_params=_params("arbitrary", "arbitrary"),
        name="s5_out",
    )(u3, hin, k2, a_mout)
    return y3.transpose(1, 0, 2).reshape(n, w)


def _log_sigmoid(z):
    return -_softplus(-z)


def _tri(length, d):
    ti = lax.broadcasted_iota(jnp.int32, (length, length), 0)
    si = lax.broadcasted_iota(jnp.int32, (length, length), 1)
    return (ti - si) * (1 - 2 * d) >= 0


def _masked_sums(mask, x):
    m = jnp.where(mask, 1.0, 0.0).astype(BF16)
    hi = x.astype(BF16)
    rest = x - hi.astype(F32)
    mid = rest.astype(BF16)
    lo = (rest - mid.astype(F32)).astype(BF16)
    return (jnp.dot(m, hi, preferred_element_type=F32) + jnp.dot(m, mid, preferred_element_type=F32)
            + jnp.dot(m, lo, preferred_element_type=F32))


def _mlstm_kernel(q_ref, k_ref, v_ref, g_ref, gb_ref, o_ref, c_ref, n_ref, m_ref):
    d = pl.program_id(0)
    length = q_ref.shape[0]
    nh, dh = MLSTM_HEADS, MLSTM_DH

    @pl.when(pl.program_id(2) == 0)
    def _():
        c_ref[...] = jnp.zeros_like(c_ref)
        n_ref[...] = jnp.zeros_like(n_ref)
        m_ref[...] = jnp.zeros_like(m_ref)

    causal = _tri(length, d)
    gp = g_ref[...] + gb_ref[...]
    gp = jnp.where(d == 0, gp, pltpu.roll(gp, LANE - 2 * nh, 1))
    lf = _log_sigmoid(gp)
    cum = _masked_sums(causal, lf)
    tot = jnp.sum(lf, axis=0, keepdims=True)
    gp_t = gp.T
    cum_t = cum.T
    nt = (((1,), (1,)), ((), ()))
    tn = (((0,), (0,)), ((), ()))

    for h in range(nh):
        sl = slice(h * dh, (h + 1) * dh)
        qh = (q_ref[:, sl] * (dh ** -0.5)).astype(BF16)
        kf = k_ref[:, sl]
        kh = kf.astype(BF16)
        vh = v_ref[:, sl].astype(BF16)
        ig_c, cum_c = gp[:, h:h + 1], cum[:, nh + h:nh + h + 1]
        ig_r, cum_r = gp_t[h:h + 1, :], cum_t[nh + h:nh + h + 1, :]
        tot_h = tot[:, nh + h:nh + h + 1]
        m_st = m_ref[h:h + 1, 0:1]
        c_st = c_ref[h]
        dmat = jnp.where(causal, cum_c - cum_r + ig_r, -jnp.inf)
        m_inter = cum_c + m_st
        m_t = jnp.maximum(jnp.max(dmat, axis=1, keepdims=True), m_inter)
        w_inter = jnp.exp(m_inter - m_t)
        s = lax.dot_general(qh, kh, nt, preferred_element_type=F32) * jnp.exp(dmat - m_t)
        num = (jnp.dot(s.astype(BF16), vh, preferred_element_type=F32)
               + w_inter * jnp.dot(qh, c_st.astype(BF16), preferred_element_type=F32))
        den = (jnp.sum(s, axis=1, keepdims=True)
               + w_inter * jnp.sum(qh.astype(F32) * n_ref[h:h + 1, :], axis=1, keepdims=True))
        o_ref[0, :, sl] = num / jnp.maximum(jnp.abs(den), jnp.exp(-m_t))
        dec = tot_h - cum_c + ig_c
        m_new = jnp.maximum(tot_h + m_st, jnp.max(dec, axis=0, keepdims=True))
        wc = jnp.exp(tot_h + m_st - m_new)
        kw = jnp.exp(dec - m_new) * kf
        c_ref[h] = wc * c_st + lax.dot_general(kw.astype(BF16), vh, tn, preferred_element_type=F32)
        n_ref[h:h + 1, :] = wc * n_ref[h:h + 1, :] + jnp.sum(kw, axis=0, keepdims=True)
        m_ref[h:h + 1, :] = jnp.broadcast_to(m_new, (1, LANE))


def _mlstm(proj, tail, bsz, gate_b, length=256):
    n = proj.shape[0]
    seq = n // bsz
    length = min(length, seq)
    nc = seq // length
    w = W_MLSTM
    q_blk = 2 * W_LRU // w
    gb = jnp.pad(gate_b.reshape(1, -1), ((0, 0), (0, LANE - gate_b.size)))

    def row(d, b, c):
        return b * nc + c + d * (nc - 1 - 2 * c)

    return pl.pallas_call(
        _mlstm_kernel,
        grid=(2, bsz, nc),
        in_specs=[
            pl.BlockSpec((length, w), lambda d, b, c: (row(d, b, c), q_blk)),
            pl.BlockSpec((length, w), lambda d, b, c: (row(d, b, c), q_blk + 1)),
            pl.BlockSpec((length, w), lambda d, b, c: (row(d, b, c), q_blk + 2)),
            pl.BlockSpec((length, LANE), lambda d, b, c: (row(d, b, c), 0)),
            pl.BlockSpec((1, LANE), lambda d, b, c: (0, 0)),
        ],
        out_specs=pl.BlockSpec((1, length, w), lambda d, b, c: (d, row(d, b, c), 0)),
        out_shape=jax.ShapeDtypeStruct((2, n, w), F32),
        scratch_shapes=[pltpu.VMEM((MLSTM_HEADS, MLSTM_DH, MLSTM_DH), F32),
                        pltpu.VMEM((SUBLANE, MLSTM_DH), F32), pltpu.VMEM((SUBLANE, LANE), F32)],
        compiler_params=_params("arbitrary", "arbitrary", "arbitrary"),
        name="mlstm",
    )(proj, proj, proj, tail, gb)


GLA_SUB = 16
GLA_SAFE_DECAY = 40.0


def _gla_chunk_dense(q_ref, k_ref, v_ref, cum, o_ref, s_ref, rev):
    length = q_ref.shape[0]
    nt = (((1,), (1,)), ((), ()))
    tn = (((0,), (0,)), ((), ()))
    causal = _tri(length, int(rev))
    last = slice(0, 1) if rev else slice(length - 1, length)
    for h in range(GLA_HEADS):
        kl = slice(h * GLA_DK, (h + 1) * GLA_DK)
        vl = slice(h * GLA_DV, (h + 1) * GLA_DV)
        ch = cum[:, kl]
        kk = k_ref[:, kl]
        vv = v_ref[:, vl].astype(BF16)
        st = s_ref[h]
        tot = ch[last]
        qe = (q_ref[:, kl] * (GLA_DK ** -0.5) * jnp.exp(ch)).astype(BF16)
        ke = (kk * jnp.exp(-ch)).astype(BF16)
        att = jnp.where(causal, lax.dot_general(qe, ke, nt, preferred_element_type=F32), 0.0)
        o_ref[0, :, vl] = (jnp.dot(att.astype(BF16), vv, preferred_element_type=F32)
                           + lax.dot_general(qe, st.astype(BF16), nt, preferred_element_type=F32))
        kd = (kk * jnp.exp(tot - ch)).astype(BF16)
        s_ref[h] = jnp.exp(tot) * st + lax.dot_general(vv, kd, tn, preferred_element_type=F32)


def _gla_chunk(q_ref, k_ref, v_ref, cum, o_ref, s_ref, rev):
    length = q_ref.shape[0]
    c = GLA_SUB
    nt = (((1,), (1,)), ((), ()))
    tn = (((0,), (0,)), ((), ()))

    def rows(a, b):
        return slice(length - b, length - a) if rev else slice(a, b)

    def row(a):
        i = length - 1 - a if rev else a
        return slice(i, i + 1)

    ti = lax.broadcasted_iota(jnp.int32, (c, 1), 0)
    for h in range(GLA_HEADS):
        kl = slice(h * GLA_DK, (h + 1) * GLA_DK)
        vl = slice(h * GLA_DV, (h + 1) * GLA_DV)
        ch = cum[:, kl]
        qs = q_ref[:, kl] * (GLA_DK ** -0.5)
        kk = k_ref[:, kl]
        vf = v_ref[:, vl]
        vv = vf.astype(BF16)
        st = s_ref[h]
        tot = ch[row(length - 1)]
        o = lax.dot_general((qs * jnp.exp(ch)).astype(BF16), st.astype(BF16), nt, preferred_element_type=F32)
        ob = [o[rows(p * c, (p + 1) * c)] for p in range(length // c)]
        m = length // 2
        while m >= c:
            for start in range(0, length, 2 * m):
                fst, sec = rows(start, start + m), rows(start + m, start + 2 * m)
                r = ch[row(start + m)]
                qh = (qs[sec] * jnp.exp(ch[sec] - r)).astype(BF16)
                kh = (kk[fst] * jnp.exp(r - ch[fst])).astype(BF16)
                att = lax.dot_general(qh, kh, nt, preferred_element_type=F32)
                contrib = jnp.dot(att.astype(BF16), vv[fst], preferred_element_type=F32)
                for p in range((start + m) // c, (start + 2 * m) // c):
                    lo = rows(p * c, (p + 1) * c).start - sec.start
                    ob[p] = ob[p] + contrib[lo:lo + c]
            m //= 2
        for p in range(length // c):
            blk = rows(p * c, (p + 1) * c)
            cb, qb, kb, vb = ch[blk], qs[blk], kk[blk], vf[blk]
            acc = ob[p]
            for s in range(c):
                sees = (ti <= s) if rev else (ti >= s)
                e = jnp.exp(jnp.where(sees, cb - cb[s:s + 1], -jnp.inf))
                a = jnp.sum(qb * kb[s:s + 1] * e, axis=1, keepdims=True)
                acc = acc + a * vb[s:s + 1]
            o_ref[0, blk, vl] = acc
        kd = (kk * jnp.exp(tot - ch)).astype(BF16)
        s_ref[h] = jnp.exp(tot) * st + lax.dot_general(vv, kd, tn, preferred_element_type=F32)


def _gla_kernel(q_ref, k_ref, v_ref, low_ref, wg_ref, bg_ref, o_ref, s_ref):
    d = pl.program_id(0)
    length = q_ref.shape[0]

    @pl.when(pl.program_id(2) == 0)
    def _():
        s_ref[...] = jnp.zeros_like(s_ref)

    gate_pre = jnp.dot(low_ref[...].astype(BF16), wg_ref[0], preferred_element_type=F32) + bg_ref[0]
    la = _log_sigmoid(gate_pre) * (1.0 / GLA_TAU)
    cum = _masked_sums(_tri(length, d), la)
    safe = jnp.sum(la, axis=0, keepdims=True).min() >= -GLA_SAFE_DECAY
    for rev in (False, True):
        pl.when((d == int(rev)) & safe)(
            functools.partial(_gla_chunk_dense, q_ref, k_ref, v_ref, cum, o_ref, s_ref, rev))
        pl.when((d == int(rev)) & jnp.logical_not(safe))(
            functools.partial(_gla_chunk, q_ref, k_ref, v_ref, cum, o_ref, s_ref, rev))


def _gla(proj, tail, bsz, w_gate2, gate_b, length=128):
    n = proj.shape[0]
    seq = n // bsz
    length = min(length, seq)
    nc = seq // length
    wg = jnp.zeros((2, LANE, GLA_QK), F32)
    for d in range(2):
        wg = wg.at[d, d * GLA_RANK:(d + 1) * GLA_RANK].set(w_gate2[d])

    def row(d, b, c):
        return b * nc + c + d * (nc - 1 - 2 * c)

    return pl.pallas_call(
        _gla_kernel,
        grid=(2, bsz, nc),
        in_specs=[
            pl.BlockSpec((length, GLA_QK), lambda d, b, c: (row(d, b, c), 0)),
            pl.BlockSpec((length, GLA_QK), lambda d, b, c: (row(d, b, c), 1)),
            pl.BlockSpec((length, GLA_V), lambda d, b, c: (row(d, b, c), 2 * GLA_QK // GLA_V)),
            pl.BlockSpec((length, LANE), lambda d, b, c: (row(d, b, c), 0)),
            pl.BlockSpec((1, LANE, GLA_QK), lambda d, b, c: (d, 0, 0)),
            pl.BlockSpec((1, 1, GLA_QK), lambda d, b, c: (d, 0, 0)),
        ],
        out_specs=pl.BlockSpec((1, length, GLA_V), lambda d, b, c: (d, row(d, b, c), 0)),
        out_shape=jax.ShapeDtypeStruct((2, n, GLA_V), F32),
        scratch_shapes=[pltpu.VMEM((GLA_HEADS, GLA_DV, GLA_DK), F32)],
        compiler_params=_params("arbitrary", "arbitrary", "arbitrary"),
        name="gla",
    )(proj, proj, proj, tail, wg.astype(BF16), gate_b.reshape(2, 1, GLA_QK))


def _head_norm(t, g, heads):
    dh = t.shape[1] // heads
    outs = []
    for h in range(heads):
        th = t[:, h * dh:(h + 1) * dh]
        outs.append(th * lax.rsqrt(jnp.mean(th * th, axis=1, keepdims=True) + NORM_EPS))
    return jnp.concatenate(outs, axis=1) * g


def _ab_out_kernel(x_ref, gr_ref, og_ref, hl_ref, hm_ref, g_ref, w_ref, o_ref):
    k = gr_ref.shape[1]
    ya = jax.nn.gelu(gr_ref[...]) * (hl_ref[0] + hl_ref[1])
    yb = jax.nn.sigmoid(og_ref[...]) * _head_norm(hm_ref[0] + hm_ref[1], g_ref[...], MLSTM_HEADS)
    o_ref[...] = (x_ref[...]
                  + jnp.dot(ya.astype(BF16), w_ref[0:k, :], preferred_element_type=F32)
                  + jnp.dot(yb.astype(BF16), w_ref[k:, :], preferred_element_type=F32))


def _ab_out(x, proj, hl, hm, norm_g, w, tm=512):
    n, d = x.shape
    k = W_LRU
    tm = min(tm, n)
    return pl.pallas_call(
        _ab_out_kernel,
        grid=(n // tm,),
        in_specs=[
            pl.BlockSpec((tm, d), lambda i: (i, 0)),
            pl.BlockSpec((tm, k), lambda i: (i, 1)),
            pl.BlockSpec((tm, k), lambda i: (i, 5)),
            pl.BlockSpec((2, tm, k), lambda i: (0, i, 0)),
            pl.BlockSpec((2, tm, k), lambda i: (0, i, 0)),
            pl.BlockSpec((1, k), lambda i: (0, 0)),
            pl.BlockSpec((2 * k, d), lambda i: (0, 0)),
        ],
        out_specs=pl.BlockSpec((tm, d), lambda i: (i, 0)),
        out_shape=jax.ShapeDtypeStruct((n, d), F32),
        compiler_params=_params("arbitrary"),
        name="ab_out",
    )(x, proj, proj, hl, hm, norm_g.reshape(1, k), w)


def _cd_out_kernel(x_ref, r_ref, og_ref, g_ref, ys_ref, u_ref, d_ref, wg_ref, w_ref, o_ref):
    k = r_ref.shape[1]
    r = r_ref[...]
    yc = _head_norm(og_ref[0] + og_ref[1], g_ref[...], GLA_HEADS) * (r * jax.nn.sigmoid(r))
    y = jax.nn.gelu(ys_ref[...] + d_ref[...] * u_ref[...])
    yd = y * jax.nn.sigmoid(jnp.dot(y.astype(BF16), wg_ref[...], preferred_element_type=F32))
    o_ref[...] = (x_ref[...]
                  + jnp.dot(yc.astype(BF16), w_ref[0:k, :], preferred_element_type=F32)
                  + jnp.dot(yd.astype(BF16), w_ref[k:, :], preferred_element_type=F32))


def _cd_out(x, proj, og, norm_g, ys, s5_d, w_glu, w, tm=512):
    n, d = x.shape
    k = S5_W
    tm = min(tm, n)
    return pl.pallas_call(
        _cd_out_kernel,
        grid=(n // tm,),
        in_specs=[
            pl.BlockSpec((tm, d), lambda i: (i, 0)),
            pl.BlockSpec((tm, k), lambda i: (i, 2)),
            pl.BlockSpec((2, tm, k), lambda i: (0, i, 0)),
            pl.BlockSpec((1, k), lambda i: (0, 0)),
            pl.BlockSpec((tm, k), lambda i: (i, 0)),
            pl.BlockSpec((tm, k), lambda i: (i, 3)),
            pl.BlockSpec((1, k), lambda i: (0, 0)),
            pl.BlockSpec((k, k), lambda i: (0, 0)),
            pl.BlockSpec((2 * k, d), lambda i: (0, 0)),
        ],
        out_specs=pl.BlockSpec((tm, d), lambda i: (i, 0)),
        out_shape=jax.ShapeDtypeStruct((n, d), F32),
        compiler_params=_params("arbitrary"),
        name="cd_out",
    )(x, proj, og, norm_g.reshape(1, k), ys, proj, s5_d.reshape(1, k), w_glu, w)


def _split_small(w, n_main):
    tail = w[:, n_main:]
    tail = jnp.pad(tail, ((0, 0), (0, LANE - tail.shape[1])))
    return w[:, :n_main].astype(BF16), tail.astype(BF16)


def kernel(x, norm_ffn1, ffn1_w_gu, ffn1_w_down, norm_mix, norm_ffn2, ffn2_w_gu, ffn2_w_down,
           ab_w_in, lru_conv_w, lru_conv_b, lru_gate_w, lru_gate_b, lru_lambda, mlstm_gate_b,
           mlstm_norm, ab_w_out, cd_w_in, gla_w_gate2, gla_gate_b, gla_norm, s5_a_re, s5_a_im,
           s5_log_dt, s5_b_re, s5_b_im, s5_c_re, s5_c_im, s5_d, s5_w_glu, cd_w_out, final_norm):
    bsz, seq, d = x.shape
    n = bsz * seq
    depth = norm_ffn1.shape[0]
    xf = x.reshape(n, d)
    w1_gu, w1_down = ffn1_w_gu.astype(BF16), ffn1_w_down.astype(BF16)
    w2_gu, w2_down = ffn2_w_gu.astype(BF16), ffn2_w_down.astype(BF16)
    for l in range(depth):
        xf = _ffn(xf, norm_ffn1[l], w1_gu, w1_down, l)
        j = l // 2
        if l % 2 == 0:
            w_main, w_tail = _split_small(ab_w_in[j], 2 * W_LRU + 4 * W_MLSTM)
            proj, tail = _norm_proj(xf, norm_mix[l], w_main, w_tail)
            hl = _lru(proj, bsz, lru_conv_w[j], lru_conv_b[j], lru_gate_w[j], lru_gate_b[j], lru_lambda[j])
            hm = _mlstm(proj, tail, bsz, mlstm_gate_b[j])
            xf = _ab_out(xf, proj, hl, hm, mlstm_norm[j], ab_w_out[j].astype(BF16))
        else:
            w = cd_w_in[j]
            n_gla = 2 * GLA_QK + 2 * GLA_V
            w = jnp.concatenate([w[:, :n_gla], w[:, n_gla + 2 * GLA_RANK:], w[:, n_gla:n_gla + 2 * GLA_RANK]], axis=1)
            w_main, w_tail = _split_small(w, n_gla + S5_W)
            proj, tail = _norm_proj(xf, norm_mix[l], w_main, w_tail)
            og = _gla(proj, tail, bsz, gla_w_gate2[j], gla_gate_b[j])
            ops = _s5_operators(s5_a_re[j], s5_a_im[j], s5_log_dt[j], s5_b_re[j], s5_b_im[j],
                                s5_c_re[j], s5_c_im[j])
            ys = _s5(proj[:, n_gla:], bsz, ops)
            xf = _cd_out(xf, proj, og, gla_norm[j], ys, s5_d[j],
                         s5_w_glu[j].astype(BF16), cd_w_out[j].astype(BF16))
        xf = _ffn(xf, norm_ffn2[l], w2_gu, w2_down, l, final_g=final_norm if l == depth - 1 else None)
    return xf.reshape(bsz, seq, d)
```

```python
import functools

import jax
import jax.numpy as jnp
from jax import lax
from jax.experimental import pallas as pl
from jax.experimental.pallas import tpu as pltpu

NORM_EPS = 1e-6

W_LRU = 1024
LRU_HEADS = 8
LRU_BLOCK = W_LRU // LRU_HEADS
CONV_W = 4
CONV_PAD_L = 2
LRU_C = 8.0

W_MLSTM = 1024
MLSTM_HEADS = 4
MLSTM_DH = W_MLSTM // MLSTM_HEADS

GLA_HEADS = 4
GLA_DK = 128
GLA_DV = 256
GLA_QK = GLA_HEADS * GLA_DK
GLA_V = GLA_HEADS * GLA_DV
GLA_RANK = 16
GLA_TAU = 16.0

S5_W = 1024
S5_GROUP = 16
S5_GROUPS = S5_W // S5_GROUP
S5_P = 64
S5_CHUNK = 16
S5_GB = 8
S5_ROW_TILE = 256

LANE = 128
SUBLANE = 8
VMEM_LIMIT = 52 * 1024 * 1024

BF16 = jnp.bfloat16
F32 = jnp.float32


def _rms(x, g):
    ms = jnp.mean(x * x, axis=-1, keepdims=True)
    return x * lax.rsqrt(ms + NORM_EPS) * g


def _sigmoid(z):
    return 0.5 * jnp.tanh(0.5 * z) + 0.5


def _softplus(z):
    return jnp.maximum(z, 0.0) + jnp.log1p(jnp.exp(-jnp.abs(z)))


def _params(*sem):
    return pltpu.CompilerParams(dimension_semantics=sem, vmem_limit_bytes=VMEM_LIMIT)


def _ffn_kernel(x_ref, g_ref, wg_ref, wu_ref, wd_ref, *rest, final):
    if final:
        fg_ref, o_ref, h_ref = rest
    else:
        o_ref, h_ref = rest
    j = pl.program_id(1)

    @pl.when(j == 0)
    def _():
        x = x_ref[...]
        h_ref[...] = _rms(x, g_ref[...]).astype(BF16)
        o_ref[...] = x

    h = h_ref[...]
    g = jnp.dot(h, wg_ref[...], preferred_element_type=F32)
    u = jnp.dot(h, wu_ref[...], preferred_element_type=F32)
    hg = 0.5 * g
    a = (hg * (1.0 + jnp.tanh(hg)) * (0.5 * u)).astype(BF16)
    o_ref[...] += jnp.dot(a, wd_ref[...], preferred_element_type=F32)

    if final:
        @pl.when(j == pl.num_programs(1) - 1)
        def _():
            o_ref[...] = _rms(o_ref[...], fg_ref[...])


def _ffn(x, g, w_gu, w_down, layer, final_g=None, tm=512, tf=512):
    n, d = x.shape
    f = w_down.shape[1]
    tm = min(tm, n)
    nj = f // tf
    final = final_g is not None
    in_specs = [
        pl.BlockSpec((tm, d), lambda i, j: (i, 0)),
        pl.BlockSpec((1, d), lambda i, j: (0, 0)),
        pl.BlockSpec((None, d, tf), lambda i, j: (layer, 0, j)),
        pl.BlockSpec((None, d, tf), lambda i, j: (layer, 0, j + nj)),
        pl.BlockSpec((None, tf, d), lambda i, j: (layer, j, 0)),
    ]
    args = [x, g.reshape(1, d), w_gu, w_gu, w_down]
    if final:
        in_specs.append(pl.BlockSpec((1, d), lambda i, j: (0, 0)))
        args.append(final_g.reshape(1, d))
    return pl.pallas_call(
        functools.partial(_ffn_kernel, final=final),
        grid=(n // tm, nj),
        in_specs=in_specs,
        out_specs=pl.BlockSpec((tm, d), lambda i, j: (i, 0)),
        out_shape=jax.ShapeDtypeStruct((n, d), F32),
        scratch_shapes=[pltpu.VMEM((tm, d), BF16)],
        compiler_params=_params("arbitrary", "arbitrary"),
        name="ffn_final" if final else "ffn",
    )(*args)


def _norm_proj_kernel(x_ref, g_ref, w_ref, ws_ref, o_ref, os_ref, h_ref):
    @pl.when(pl.program_id(1) == 0)
    def _():
        h = _rms(x_ref[...], g_ref[...]).astype(BF16)
        h_ref[...] = h
        os_ref[...] = jnp.dot(h, ws_ref[...], preferred_element_type=F32)

    o_ref[...] = jnp.dot(h_ref[...], w_ref[...], preferred_element_type=F32)


def _norm_proj(x, g, w, w_small, tm=1024, tn=1024):
    n, d = x.shape
    m = w.shape[1]
    ms = w_small.shape[1]
    tm = min(tm, n)
    return pl.pallas_call(
        _norm_proj_kernel,
        grid=(n // tm, m // tn),
        in_specs=[
            pl.BlockSpec((tm, d), lambda i, j: (i, 0)),
            pl.BlockSpec((1, d), lambda i, j: (0, 0)),
            pl.BlockSpec((d, tn), lambda i, j: (0, j)),
            pl.BlockSpec((d, ms), lambda i, j: (0, 0)),
        ],
        out_specs=[pl.BlockSpec((tm, tn), lambda i, j: (i, j)),
                   pl.BlockSpec((tm, ms), lambda i, j: (i, 0))],
        out_shape=[jax.ShapeDtypeStruct((n, m), F32), jax.ShapeDtypeStruct((n, ms), F32)],
        scratch_shapes=[pltpu.VMEM((tm, d), BF16)],
        compiler_params=_params("arbitrary", "arbitrary"),
        name="norm_proj",
    )(x, g.reshape(1, d), w, w_small)


def _lru_kernel(xp_ref, x_ref, xn_ref, cw_ref, cb_ref, gw_ref, gb_ref, lam_ref, o_ref,
                xs_ref, a_ref, b_ref, h_ref, *, tt):
    d = pl.program_id(0)
    t = pl.program_id(2)
    nt = pl.num_programs(2)
    tb = jnp.where(d == 0, t, nt - 1 - t)
    w = x_ref.shape[1]

    xs_ref[0:SUBLANE, :] = jnp.where(tb > 0, xp_ref[...], 0.0)
    xs_ref[SUBLANE:SUBLANE + tt, :] = x_ref[...]
    xs_ref[SUBLANE + tt:2 * SUBLANE + tt, :] = jnp.where(tb < nt - 1, xn_ref[...], 0.0)
    cw = cw_ref[...]
    xc = cb_ref[...]
    for k in range(CONV_W):
        off = SUBLANE + k - CONV_PAD_L
        xc = xc + cw[k:k + 1, :] * xs_ref[off:off + tt, :]

    xcb = xc.astype(BF16)
    gb = gb_ref[0]
    sp = _softplus(-lam_ref[0])
    for hd in range(w // LRU_BLOCK):
        sl = slice(hd * LRU_BLOCK, (hd + 1) * LRU_BLOCK)
        z = jnp.dot(xcb[:, sl], gw_ref[0, hd], preferred_element_type=F32)
        r = _sigmoid(z[:, :LRU_BLOCK] + gb[0:1, sl])
        i = _sigmoid(z[:, LRU_BLOCK:] + gb[1:2, sl])
        log_a = (-LRU_C * sp[:, sl]) * r
        a = jnp.exp(log_a)
        a_ref[:, sl] = a
        b_ref[:, sl] = jnp.sqrt(-jnp.tanh(log_a) * (a * a + 1.0)) * (i * xc[:, sl])

    @pl.when(t == 0)
    def _():
        h_ref[...] = jnp.zeros_like(h_ref)

    sub = lax.broadcasted_iota(jnp.int32, (SUBLANE, w), 0)
    ntile = tt // SUBLANE

    def scan(rev):
        def earlier(x, k, fill):
            if rev:
                return jnp.where(sub < SUBLANE - k, pltpu.roll(x, SUBLANE - k, 0), fill)
            return jnp.where(sub >= k, pltpu.roll(x, k, 0), fill)

        def body(jt, h):
            r0 = pl.multiple_of((ntile - 1 - jt if rev else jt) * SUBLANE, SUBLANE)
            a = a_ref[pl.ds(r0, SUBLANE), :]
            b = b_ref[pl.ds(r0, SUBLANE), :]
            k = 1
            while k < SUBLANE:
                a, b = a * earlier(a, k, 1.0), b + a * earlier(b, k, 0.0)
                k *= 2
            tile = a * h + b
            o_ref[0, pl.ds(r0, SUBLANE), :] = tile
            last = 0 if rev else SUBLANE - 1
            return tile[last:last + 1, :]

        h_ref[...] = lax.fori_loop(0, ntile, body, h_ref[...])

    pl.when(d == 0)(functools.partial(scan, False))
    pl.when(d == 1)(functools.partial(scan, True))


def _lru(proj, bsz, conv_w, conv_b, gate_w, gate_b, lam, tt=256):
    n = proj.shape[0]
    seq = n // bsz
    tt = min(tt, seq)
    nt = seq // tt
    w = W_LRU
    r8 = tt // SUBLANE
    gw = jnp.concatenate([gate_w[:, 0], gate_w[:, 1]], axis=-1).astype(BF16)

    def tb(d, t):
        return t + d * (nt - 1 - 2 * t)

    return pl.pallas_call(
        functools.partial(_lru_kernel, tt=tt),
        grid=(2, bsz, nt),
        in_specs=[
            pl.BlockSpec((SUBLANE, w), lambda d, b, t: (jnp.maximum((b * nt + tb(d, t)) * r8 - 1, 0), 0)),
            pl.BlockSpec((tt, w), lambda d, b, t: (b * nt + tb(d, t), 0)),
            pl.BlockSpec((SUBLANE, w), lambda d, b, t: (jnp.minimum((b * nt + tb(d, t) + 1) * r8, n // SUBLANE - 1), 0)),
            pl.BlockSpec((CONV_W, w), lambda d, b, t: (0, 0)),
            pl.BlockSpec((1, w), lambda d, b, t: (0, 0)),
            pl.BlockSpec((1, LRU_HEADS, LRU_BLOCK, 2 * LRU_BLOCK), lambda d, b, t: (d, 0, 0, 0)),
            pl.BlockSpec((1, 2, w), lambda d, b, t: (d, 0, 0)),
            pl.BlockSpec((1, 1, w), lambda d, b, t: (d, 0, 0)),
        ],
        out_specs=pl.BlockSpec((1, tt, w), lambda d, b, t: (d, b * nt + tb(d, t), 0)),
        out_shape=jax.ShapeDtypeStruct((2, n, w), F32),
        scratch_shapes=[pltpu.VMEM((tt + 2 * SUBLANE, w), F32), pltpu.VMEM((tt, w), F32),
                        pltpu.VMEM((tt, w), F32), pltpu.VMEM((1, w), F32)],
        compiler_params=_params("arbitrary", "arbitrary", "arbitrary"),
        name="rg_lru",
    )(proj, proj, proj, conv_w, conv_b.reshape(1, w), gw, gate_b, lam.reshape(2, 1, w))


def _s5_discretise(a_re, a_im, log_dt, b_re, b_im):
    dt = jnp.exp(log_dt)[:, None]
    mag = jnp.exp(dt * a_re)
    lr = mag * jnp.cos(dt * a_im)
    li = mag * jnp.sin(dt * a_im)
    den = a_re * a_re + a_im * a_im
    nr = lr - 1.0
    cr = (nr * a_re + li * a_im) / den
    ci = (li * a_re - nr * a_im) / den
    bbr = cr[..., None] * b_re - ci[..., None] * b_im
    bbi = cr[..., None] * b_im + ci[..., None] * b_re
    return lr, li, bbr, bbi


def _s5_operators(a_re, a_im, log_dt, b_re, b_im, c_re, c_im):
    L = S5_CHUNK
    G, P, C = S5_GROUPS, S5_P, S5_GROUP
    kfs, mins, mouts, lams = [], [], [], []
    for d in range(2):
        lr, li, bbr, bbi = _s5_discretise(a_re[d], a_im[d], log_dt[d], b_re, b_im)
        bbr, bbi = bbr.transpose(0, 2, 1), bbi.transpose(0, 2, 1)
        jj = jnp.arange(L + 1, dtype=F32)[:, None, None, None]
        dt = jnp.exp(log_dt[d])[None, :, None, None]
        mag = jnp.exp(jj * dt * a_re[d][None, :, None, :])
        pr = mag * jnp.cos(jj * dt * a_im[d][None, :, None, :])
        pi = mag * jnp.sin(jj * dt * a_im[d][None, :, None, :])
        cl = jnp.concatenate([c_re[None] * pr - c_im[None] * pi, -(c_re[None] * pi + c_im[None] * pr)], axis=-1)
        bl = jnp.concatenate([pr * bbr[None] - pi * bbi[None], pr * bbi[None] + pi * bbr[None]], axis=-1)
        kfs.append(jnp.sum(cl[:L, :, None, :, :] * bl[0][None, :, :, None, :], axis=-1))
        if d == 0:
            e_in = jnp.arange(L - 1, -1, -1)
            e_out = jnp.arange(1, L + 1)
        else:
            e_in = jnp.arange(L)
            e_out = jnp.arange(L, 0, -1)
        mins.append(bl[e_in])
        mouts.append(cl[e_out])
        lams.append(jnp.stack([pr[L, :, 0], pi[L, :, 0]], axis=1))
    kf, kb = kfs
    kfull = jnp.concatenate([kb[:0:-1], (kf[0] + kb[0])[None], kf[1:]], axis=0)
    gb = S5_GB
    nb_ = G // gb
    k2 = kfull.reshape(2 * L - 1, nb_, gb * C, C).transpose(1, 0, 2, 3)

    def rows_of_block(parts):
        a = jnp.concatenate(parts, axis=-1).reshape(L, nb_, gb * C, 4 * P)
        return a.transpose(1, 0, 2, 3).reshape(nb_, L * gb * C, 4 * P)

    lam_l = jnp.concatenate(lams, axis=1)
    lam_blk = lam_l.reshape(nb_, gb, 4, P).transpose(0, 2, 1, 3).reshape(nb_, 1, 4 * gb * P)
    return k2, rows_of_block(mins), rows_of_block(mouts), lam_blk


def _s5_state_kernel(u_ref, amin_ref, lam_ref, hin_ref, h_ref, min_ref, *, nb):
    nl, rows, _ = u_ref.shape
    _s5_spread(amin_ref[0], min_ref, False)
    rb = rows // nb
    hw = h_ref.shape[1] // 4
    rt = min(rows, S5_ROW_TILE)
    lo = lax.broadcasted_iota(jnp.int32, (rt, LANE), 1) < S5_P
    ng = hw // S5_P

    def pairs():
        for d in range(2):
            for k in range(ng // 2):
                yield ((d * ng + 2 * k) * LANE, (d * ng + 2 * k + 1) * LANE,
                       2 * d * hw + k * LANE, (2 * d + 1) * hw + k * LANE)

    for r0 in range(0, rows, rt):
        u = jnp.concatenate([u_ref[s, r0:r0 + rt, :] for s in range(nl)], axis=1)
        hl = jnp.dot(u, min_ref[...], preferred_element_type=F32)
        for ca, cb, cre, cim in pairs():
            a, b = hl[:, ca:ca + LANE], hl[:, cb:cb + LANE]
            h_ref[r0:r0 + rt, cre:cre + LANE] = jnp.where(lo, a, pltpu.roll(b, S5_P, 1))
            h_ref[r0:r0 + rt, cim:cim + LANE] = jnp.where(lo, pltpu.roll(a, S5_P, 1), b)
    lam = lam_ref[0]
    lrf, lif = lam[:, 0:hw], lam[:, hw:2 * hw]
    lrb, lib = lam[:, 2 * hw:3 * hw], lam[:, 3 * hw:4 * hw]
    sub = lax.broadcasted_iota(jnp.int32, (SUBLANE, hw), 0)

    def body(it, carry):
        new = []
        for b in range(nb):
            fr, fi, br, bi = carry[4 * b:4 * b + 4]
            f0 = pl.multiple_of(b * rb + it * SUBLANE, SUBLANE)
            b0 = pl.multiple_of(b * rb + rb - SUBLANE - it * SUBLANE, SUBLANE)
            tiles = [jnp.zeros((SUBLANE, hw), F32)] * 4
            lf = h_ref[pl.ds(f0, SUBLANE), 0:2 * hw]
            lb = h_ref[pl.ds(b0, SUBLANE), 2 * hw:4 * hw]
            for r in range(SUBLANE):
                q = SUBLANE - 1 - r
                tiles = [jnp.where(sub == r, fr, tiles[0]), jnp.where(sub == r, fi, tiles[1]),
                         jnp.where(sub == q, br, tiles[2]), jnp.where(sub == q, bi, tiles[3])]
                fr, fi, br, bi = (lrf * fr - lif * fi + lf[r:r + 1, 0:hw],
                                  lrf * fi + lif * fr + lf[r:r + 1, hw:2 * hw],
                                  lrb * br - lib * bi + lb[q:q + 1, 0:hw],
                                  lrb * bi + lib * br + lb[q:q + 1, hw:2 * hw])
            h_ref[pl.ds(f0, SUBLANE), 0:hw] = tiles[0]
            h_ref[pl.ds(f0, SUBLANE), hw:2 * hw] = tiles[1]
            h_ref[pl.ds(b0, SUBLANE), 2 * hw:3 * hw] = tiles[2]
            h_ref[pl.ds(b0, SUBLANE), 3 * hw:4 * hw] = tiles[3]
            new += [fr, fi, br, bi]
        return tuple(new)

    zero = jnp.zeros((1, hw), F32)
    lax.fori_loop(0, rb // SUBLANE, body, (zero,) * (4 * nb))
    for r0 in range(0, rows, rt):
        for ca, cb, cre, cim in pairs():
            re, im = h_ref[r0:r0 + rt, cre:cre + LANE], h_ref[r0:r0 + rt, cim:cim + LANE]
            hin_ref[0, r0:r0 + rt, ca:ca + LANE] = jnp.where(lo, re, pltpu.roll(im, S5_P, 1)).astype(BF16)
            hin_ref[0, r0:r0 + rt, cb:cb + LANE] = jnp.where(lo, pltpu.roll(re, S5_P, 1), im).astype(BF16)


def _s5_spread(a, o_ref, transpose):
    rows = a.shape[0]
    row_g = (lax.broadcasted_iota(jnp.int32, (rows, LANE), 0) // S5_GROUP) % S5_GB
    for d in range(2):
        ad = a[:, d * LANE:(d + 1) * LANE]
        for g in range(S5_GB):
            tile = jnp.where(row_g == g, ad, 0.0)
            col = (d * S5_GB + g) * LANE
            if transpose:
                o_ref[col:col + LANE, :] = tile.T.astype(BF16)
            else:
                o_ref[:, col:col + LANE] = tile.astype(BF16)


def _s5_out_kernel(u_ref, hin_ref, k2_ref, amout_ref, y_ref, t_ref, mout_ref):
    nl = u_ref.shape[0]

    @pl.when(pl.program_id(1) == 0)
    def _():
        _s5_spread(amout_ref[0], mout_ref, True)
        row_g = lax.broadcasted_iota(jnp.int32, (LANE, LANE), 0) // S5_GROUP
        col_g = lax.broadcasted_iota(jnp.int32, (LANE, LANE), 1) // S5_GROUP
        spread = (lax.broadcasted_iota(jnp.int32, (S5_GROUP, LANE), 1) % S5_GROUP
                  == lax.broadcasted_iota(jnp.int32, (S5_GROUP, LANE), 0)).astype(BF16)
        for j in range(2 * nl - 1):
            rep = jnp.dot(k2_ref[0, j].astype(BF16), spread, preferred_element_type=F32)
            tile = jnp.where(row_g == col_g, rep, 0.0).astype(BF16)
            for s in range(nl):
                t = s + j - (nl - 1)
                if 0 <= t < nl:
                    t_ref[s * LANE:(s + 1) * LANE, t * LANE:(t + 1) * LANE] = tile

    u = jnp.concatenate([u_ref[s] for s in range(nl)], axis=1)
    y = (jnp.dot(u, t_ref[...], preferred_element_type=F32)
         + jnp.dot(hin_ref[0], mout_ref[...], preferred_element_type=F32))
    for t in range(nl):
        y_ref[t] = y[:, t * LANE:(t + 1) * LANE]


def _s5(u, bsz, ops, tr=256):
    k2, a_min, a_mout, lam_blk = ops
    n, w = u.shape
    L = S5_CHUNK
    rows = n // L
    nblk = w // LANE
    kw = L * LANE
    sw = lam_blk.shape[2]
    tr = min(tr, rows)
    u3 = u.reshape(rows, L, w).transpose(1, 0, 2).astype(BF16)
    hin = pl.pallas_call(
        functools.partial(_s5_state_kernel, nb=bsz),
        grid=(nblk,),
        in_specs=[
            pl.BlockSpec((L, rows, LANE), lambda b: (0, 0, b)),
            pl.BlockSpec((1,) + a_min.shape[1:], lambda b: (b, 0, 0)),
            pl.BlockSpec((1, 1, sw), lambda b: (b, 0, 0)),
        ],
        out_specs=pl.BlockSpec((1, rows, sw), lambda b: (b, 0, 0)),
        out_shape=jax.ShapeDtypeStruct((nblk, rows, sw), BF16),
        scratch_shapes=[pltpu.VMEM((rows, sw), F32), pltpu.VMEM((kw, sw), BF16)],
        compiler_params=_params("arbitrary"),
        name="s5_state",
    )(u3, a_min, lam_blk)
    y3 = pl.pallas_call(
        _s5_out_kernel,
        grid=(nblk, rows // tr),
        in_specs=[
            pl.BlockSpec((L, tr, LANE), lambda b, i: (0, i, b)),
            pl.BlockSpec((1, tr, sw), lambda b, i: (b, i, 0)),
            pl.BlockSpec((1,) + k2.shape[1:], lambda b, i: (b, 0, 0, 0)),
            pl.BlockSpec((1,) + a_mout.shape[1:], lambda b, i: (b, 0, 0)),
        ],
        out_specs=pl.BlockSpec((L, tr, LANE), lambda b, i: (0, i, b)),
        out_shape=jax.ShapeDtypeStruct((L, rows, w), F32),
        scratch_shapes=[pltpu.VMEM((kw, kw), BF16), pltpu.VMEM((sw, kw), BF16)],
        compiler_params=_params("arbitrary", "arbitrary"),
        name="s5_out",
    )(u3, hin, k2, a_mout)
    return y3.transpose(1, 0, 2).reshape(n, w)


def _log_sigmoid(z):
    return -_softplus(-z)


def _tri(length, d):
    ti = lax.broadcasted_iota(jnp.int32, (length, length), 0)
    si = lax.broadcasted_iota(jnp.int32, (length, length), 1)
    return (ti - si) * (1 - 2 * d) >= 0


def _masked_sums(mask, x):
    m = jnp.where(mask, 1.0, 0.0).astype(BF16)
    hi = x.astype(BF16)
    rest = x - hi.astype(F32)
    mid = rest.astype(BF16)
    lo = (rest - mid.astype(F32)).astype(BF16)
    return (jnp.dot(m, hi, preferred_element_type=F32) + jnp.dot(m, mid, preferred_element_type=F32)
            + jnp.dot(m, lo, preferred_element_type=F32))


def _mlstm_kernel(q_ref, k_ref, v_ref, g_ref, gb_ref, o_ref, c_ref, n_ref, m_ref):
    d = pl.program_id(0)
    length = q_ref.shape[0]
    nh, dh = MLSTM_HEADS, MLSTM_DH

    @pl.when(pl.program_id(2) == 0)
    def _():
        c_ref[...] = jnp.zeros_like(c_ref)
        n_ref[...] = jnp.zeros_like(n_ref)
        m_ref[...] = jnp.zeros_like(m_ref)

    causal = _tri(length, d)
    gp = g_ref[...] + gb_ref[...]
    gp = jnp.where(d == 0, gp, pltpu.roll(gp, LANE - 2 * nh, 1))
    lf = _log_sigmoid(gp)
    cum = _masked_sums(causal, lf)
    tot = jnp.sum(lf, axis=0, keepdims=True)
    gp_t = gp.T
    cum_t = cum.T
    nt = (((1,), (1,)), ((), ()))
    tn = (((0,), (0,)), ((), ()))

    for h in range(nh):
        sl = slice(h * dh, (h + 1) * dh)
        qh = (q_ref[:, sl] * (dh ** -0.5)).astype(BF16)
        kf = k_ref[:, sl]
        kh = kf.astype(BF16)
        vh = v_ref[:, sl].astype(BF16)
        ig_c, cum_c = gp[:, h:h + 1], cum[:, nh + h:nh + h + 1]
        ig_r, cum_r = gp_t[h:h + 1, :], cum_t[nh + h:nh + h + 1, :]
        tot_h = tot[:, nh + h:nh + h + 1]
        m_st = m_ref[h:h + 1, 0:1]
        c_st = c_ref[h]
        dmat = jnp.where(causal, cum_c - cum_r + ig_r, -jnp.inf)
        m_inter = cum_c + m_st
        m_t = jnp.maximum(jnp.max(dmat, axis=1, keepdims=True), m_inter)
        w_inter = jnp.exp(m_inter - m_t)
        s = lax.dot_general(qh, kh, nt, preferred_element_type=F32) * jnp.exp(dmat - m_t)
        num = (jnp.dot(s.astype(BF16), vh, preferred_element_type=F32)
               + w_inter * jnp.dot(qh, c_st.astype(BF16), preferred_element_type=F32))
        den = (jnp.sum(s, axis=1, keepdims=True)
               + w_inter * jnp.sum(qh.astype(F32) * n_ref[h:h + 1, :], axis=1, keepdims=True))
        o_ref[0, :, sl] = num / jnp.maximum(jnp.abs(den), jnp.exp(-m_t))
        dec = tot_h - cum_c + ig_c
        m_new = jnp.maximum(tot_h + m_st, jnp.max(dec, axis=0, keepdims=True))
        wc = jnp.exp(tot_h + m_st - m_new)
        kw = jnp.exp(dec - m_new) * kf
        c_ref[h] = wc * c_st + lax.dot_general(kw.astype(BF16), vh, tn, preferred_element_type=F32)
        n_ref[h:h + 1, :] = wc * n_ref[h:h + 1, :] + jnp.sum(kw, axis=0, keepdims=True)
        m_ref[h:h + 1, :] = jnp.broadcast_to(m_new, (1, LANE))


def _mlstm(proj, tail, bsz, gate_b, length=256):
    n = proj.shape[0]
    seq = n // bsz
    length = min(length, seq)
    nc = seq // length
    w = W_MLSTM
    q_blk = 2 * W_LRU // w
    gb = jnp.pad(gate_b.reshape(1, -1), ((0, 0), (0, LANE - gate_b.size)))

    def row(d, b, c):
        return b * nc + c + d * (nc - 1 - 2 * c)

    return pl.pallas_call(
        _mlstm_kernel,
        grid=(2, bsz, nc),
        in_specs=[
            pl.BlockSpec((length, w), lambda d, b, c: (row(d, b, c), q_blk)),
            pl.BlockSpec((length, w), lambda d, b, c: (row(d, b, c), q_blk + 1)),
            pl.BlockSpec((length, w), lambda d, b, c: (row(d, b, c), q_blk + 2)),
            pl.BlockSpec((length, LANE), lambda d, b, c: (row(d, b, c), 0)),
            pl.BlockSpec((1, LANE), lambda d, b, c: (0, 0)),
        ],
        out_specs=pl.BlockSpec((1, length, w), lambda d, b, c: (d, row(d, b, c), 0)),
        out_shape=jax.ShapeDtypeStruct((2, n, w), F32),
        scratch_shapes=[pltpu.VMEM((MLSTM_HEADS, MLSTM_DH, MLSTM_DH), F32),
                        pltpu.VMEM((SUBLANE, MLSTM_DH), F32), pltpu.VMEM((SUBLANE, LANE), F32)],
        compiler_params=_params("arbitrary", "arbitrary", "arbitrary"),
        name="mlstm",
    )(proj, proj, proj, tail, gb)


GLA_SUB = 16
GLA_SAFE_DECAY = 40.0


def _gla_chunk_dense(q_ref, k_ref, v_ref, cum, o_ref, s_ref, rev):
    length = q_ref.shape[0]
    nt = (((1,), (1,)), ((), ()))
    tn = (((0,), (0,)), ((), ()))
    causal = _tri(length, int(rev))
    last = slice(0, 1) if rev else slice(length - 1, length)
    for h in range(GLA_HEADS):
        kl = slice(h * GLA_DK, (h + 1) * GLA_DK)
        vl = slice(h * GLA_DV, (h + 1) * GLA_DV)
        ch = cum[:, kl]
        kk = k_ref[:, kl]
        vv = v_ref[:, vl].astype(BF16)
        st = s_ref[h]
        tot = ch[last]
        qe = (q_ref[:, kl] * (GLA_DK ** -0.5) * jnp.exp(ch)).astype(BF16)
        ke = (kk * jnp.exp(-ch)).astype(BF16)
        att = jnp.where(causal, lax.dot_general(qe, ke, nt, preferred_element_type=F32), 0.0)
        o_ref[0, :, vl] = (jnp.dot(att.astype(BF16), vv, preferred_element_type=F32)
                           + lax.dot_general(qe, st.astype(BF16), nt, preferred_element_type=F32))
        kd = (kk * jnp.exp(tot - ch)).astype(BF16)
        s_ref[h] = jnp.exp(tot) * st + lax.dot_general(vv, kd, tn, preferred_element_type=F32)


def _gla_chunk(q_ref, k_ref, v_ref, cum, o_ref, s_ref, rev):
    length = q_ref.shape[0]
    c = GLA_SUB
    nt = (((1,), (1,)), ((), ()))
    tn = (((0,), (0,)), ((), ()))

    def rows(a, b):
        return slice(length - b, length - a) if rev else slice(a, b)

    def row(a):
        i = length - 1 - a if rev else a
        return slice(i, i + 1)

    ti = lax.broadcasted_iota(jnp.int32, (c, 1), 0)
    for h in range(GLA_HEADS):
        kl = slice(h * GLA_DK, (h + 1) * GLA_DK)
        vl = slice(h * GLA_DV, (h + 1) * GLA_DV)
        ch = cum[:, kl]
        qs = q_ref[:, kl] * (GLA_DK ** -0.5)
        kk = k_ref[:, kl]
        vf = v_ref[:, vl]
        vv = vf.astype(BF16)
        st = s_ref[h]
        tot = ch[row(length - 1)]
        o = lax.dot_general((qs * jnp.exp(ch)).astype(BF16), st.astype(BF16), nt, preferred_element_type=F32)
        ob = [o[rows(p * c, (p + 1) * c)] for p in range(length // c)]
        m = length // 2
        while m >= c:
            for start in range(0, length, 2 * m):
                fst, sec = rows(start, start + m), rows(start + m, start + 2 * m)
                r = ch[row(start + m)]
                qh = (qs[sec] * jnp.exp(ch[sec] - r)).astype(BF16)
                kh = (kk[fst] * jnp.exp(r - ch[fst])).astype(BF16)
                att = lax.dot_general(qh, kh, nt, preferred_element_type=F32)
                contrib = jnp.dot(att.astype(BF16), vv[fst], preferred_element_type=F32)
                for p in range((start + m) // c, (start + 2 * m) // c):
                    lo = rows(p * c, (p + 1) * c).start - sec.start
                    ob[p] = ob[p] + contrib[lo:lo + c]
            m //= 2
        for p in range(length // c):
            blk = rows(p * c, (p + 1) * c)
            cb, qb, kb, vb = ch[blk], qs[blk], kk[blk], vf[blk]
            acc = ob[p]
            for s in range(c):
                sees = (ti <= s) if rev else (ti >= s)
                e = jnp.exp(jnp.where(sees, cb - cb[s:s + 1], -jnp.inf))
                a = jnp.sum(qb * kb[s:s + 1] * e, axis=1, keepdims=True)
                acc = acc + a * vb[s:s + 1]
            o_ref[0, blk, vl] = acc
        kd = (kk * jnp.exp(tot - ch)).astype(BF16)
        s_ref[h] = jnp.exp(tot) * st + lax.dot_general(vv, kd, tn, preferred_element_type=F32)


def _gla_kernel(q_ref, k_ref, v_ref, low_ref, wg_ref, bg_ref, o_ref, s_ref):
    d = pl.program_id(0)
    length = q_ref.shape[0]

    @pl.when(pl.program_id(2) == 0)
    def _():
        s_ref[...] = jnp.zeros_like(s_ref)

    gate_pre = jnp.dot(low_ref[...].astype(BF16), wg_ref[0], preferred_element_type=F32) + bg_ref[0]
    la = _log_sigmoid(gate_pre) * (1.0 / GLA_TAU)
    cum = _masked_sums(_tri(length, d), la)
    safe = jnp.sum(la, axis=0, keepdims=True).min() >= -GLA_SAFE_DECAY
    for rev in (False, True):
        pl.when((d == int(rev)) & safe)(
            functools.partial(_gla_chunk_dense, q_ref, k_ref, v_ref, cum, o_ref, s_ref, rev))
        pl.when((d == int(rev)) & jnp.logical_not(safe))(
            functools.partial(_gla_chunk, q_ref, k_ref, v_ref, cum, o_ref, s_ref, rev))


def _gla(proj, tail, bsz, w_gate2, gate_b, length=128):
    n = proj.shape[0]
    seq = n // bsz
    length = min(length, seq)
    nc = seq // length
    wg = jnp.zeros((2, LANE, GLA_QK), F32)
    for d in range(2):
        wg = wg.at[d, d * GLA_RANK:(d + 1) * GLA_RANK].set(w_gate2[d])

    def row(d, b, c):
        return b * nc + c + d * (nc - 1 - 2 * c)

    return pl.pallas_call(
        _gla_kernel,
        grid=(2, bsz, nc),
        in_specs=[
            pl.BlockSpec((length, GLA_QK), lambda d, b, c: (row(d, b, c), 0)),
            pl.BlockSpec((length, GLA_QK), lambda d, b, c: (row(d, b, c), 1)),
            pl.BlockSpec((length, GLA_V), lambda d, b, c: (row(d, b, c), 2 * GLA_QK // GLA_V)),
            pl.BlockSpec((length, LANE), lambda d, b, c: (row(d, b, c), 0)),
            pl.BlockSpec((1, LANE, GLA_QK), lambda d, b, c: (d, 0, 0)),
            pl.BlockSpec((1, 1, GLA_QK), lambda d, b, c: (d, 0, 0)),
        ],
        out_specs=pl.BlockSpec((1, length, GLA_V), lambda d, b, c: (d, row(d, b, c), 0)),
        out_shape=jax.ShapeDtypeStruct((2, n, GLA_V), F32),
        scratch_shapes=[pltpu.VMEM((GLA_HEADS, GLA_DV, GLA_DK), F32)],
        compiler_params=_params("arbitrary", "arbitrary", "arbitrary"),
        name="gla",
    )(proj, proj, proj, tail, wg.astype(BF16), gate_b.reshape(2, 1, GLA_QK))


def _head_norm(t, g, heads):
    dh = t.shape[1] // heads
    outs = []
    for h in range(heads):
        th = t[:, h * dh:(h + 1) * dh]
        outs.append(th * lax.rsqrt(jnp.mean(th * th, axis=1, keepdims=True) + NORM_EPS))
    return jnp.concatenate(outs, axis=1) * g


def _ab_out_kernel(x_ref, gr_ref, og_ref, hl_ref, hm_ref, g_ref, w_ref, o_ref):
    k = gr_ref.shape[1]
    ya = jax.nn.gelu(gr_ref[...]) * (hl_ref[0] + hl_ref[1])
    yb = jax.nn.sigmoid(og_ref[...]) * _head_norm(hm_ref[0] + hm_ref[1], g_ref[...], MLSTM_HEADS)
    o_ref[...] = (x_ref[...]
                  + jnp.dot(ya.astype(BF16), w_ref[0:k, :], preferred_element_type=F32)
                  + jnp.dot(yb.astype(BF16), w_ref[k:, :], preferred_element_type=F32))


def _ab_out(x, proj, hl, hm, norm_g, w, tm=512):
    n, d = x.shape
    k = W_LRU
    tm = min(tm, n)
    return pl.pallas_call(
        _ab_out_kernel,
        grid=(n // tm,),
        in_specs=[
            pl.BlockSpec((tm, d), lambda i: (i, 0)),
            pl.BlockSpec((tm, k), lambda i: (i, 1)),
            pl.BlockSpec((tm, k), lambda i: (i, 5)),
            pl.BlockSpec((2, tm, k), lambda i: (0, i, 0)),
            pl.BlockSpec((2, tm, k), lambda i: (0, i, 0)),
            pl.BlockSpec((1, k), lambda i: (0, 0)),
            pl.BlockSpec((2 * k, d), lambda i: (0, 0)),
        ],
        out_specs=pl.BlockSpec((tm, d), lambda i: (i, 0)),
        out_shape=jax.ShapeDtypeStruct((n, d), F32),
        compiler_params=_params("arbitrary"),
        name="ab_out",
    )(x, proj, proj, hl, hm, norm_g.reshape(1, k), w)


def _cd_out_kernel(x_ref, r_ref, og_ref, g_ref, ys_ref, u_ref, d_ref, wg_ref, w_ref, o_ref):
    k = r_ref.shape[1]
    r = r_ref[...]
    yc = _head_norm(og_ref[0] + og_ref[1], g_ref[...], GLA_HEADS) * (r * jax.nn.sigmoid(r))
    y = jax.nn.gelu(ys_ref[...] + d_ref[...] * u_ref[...])
    yd = y * jax.nn.sigmoid(jnp.dot(y.astype(BF16), wg_ref[...], preferred_element_type=F32))
    o_ref[...] = (x_ref[...]
                  + jnp.dot(yc.astype(BF16), w_ref[0:k, :], preferred_element_type=F32)
                  + jnp.dot(yd.astype(BF16), w_ref[k:, :], preferred_element_type=F32))


def _cd_out(x, proj, og, norm_g, ys, s5_d, w_glu, w, tm=512):
    n, d = x.shape
    k = S5_W
    tm = min(tm, n)
    return pl.pallas_call(
        _cd_out_kernel,
        grid=(n // tm,),
        in_specs=[
            pl.BlockSpec((tm, d), lambda i: (i, 0)),
            pl.BlockSpec((tm, k), lambda i: (i, 2)),
            pl.BlockSpec((2, tm, k), lambda i: (0, i, 0)),
            pl.BlockSpec((1, k), lambda i: (0, 0)),
            pl.BlockSpec((tm, k), lambda i: (i, 0)),
            pl.BlockSpec((tm, k), lambda i: (i, 3)),
            pl.BlockSpec((1, k), lambda i: (0, 0)),
            pl.BlockSpec((k, k), lambda i: (0, 0)),
            pl.BlockSpec((2 * k, d), lambda i: (0, 0)),
        ],
        out_specs=pl.BlockSpec((tm, d), lambda i: (i, 0)),
        out_shape=jax.ShapeDtypeStruct((n, d), F32),
        compiler_params=_params("arbitrary"),
        name="cd_out",
    )(x, proj, og, norm_g.reshape(1, k), ys, proj, s5_d.reshape(1, k), w_glu, w)


def _split_small(w, n_main):
    tail = w[:, n_main:]
    tail = jnp.pad(tail, ((0, 0), (0, LANE - tail.shape[1])))
    return w[:, :n_main].astype(BF16), tail.astype(BF16)


def kernel(x, norm_ffn1, ffn1_w_gu, ffn1_w_down, norm_mix, norm_ffn2, ffn2_w_gu, ffn2_w_down,
           ab_w_in, lru_conv_w, lru_conv_b, lru_gate_w, lru_gate_b, lru_lambda, mlstm_gate_b,
           mlstm_norm, ab_w_out, cd_w_in, gla_w_gate2, gla_gate_b, gla_norm, s5_a_re, s5_a_im,
           s5_log_dt, s5_b_re, s5_b_im, s5_c_re, s5_c_im, s5_d, s5_w_glu, cd_w_out, final_norm):
    bsz, seq, d = x.shape
    n = bsz * seq
    depth = norm_ffn1.shape[0]
    xf = x.reshape(n, d)
    w1_gu, w1_down = ffn1_w_gu.astype(BF16), ffn1_w_down.astype(BF16)
    w2_gu, w2_down = ffn2_w_gu.astype(BF16), ffn2_w_down.astype(BF16)
    for l in range(depth):
        xf = _ffn(xf, norm_ffn1[l], w1_gu, w1_down, l)
        j = l // 2
        if l % 2 == 0:
            w_main, w_tail = _split_small(ab_w_in[j], 2 * W_LRU + 4 * W_MLSTM)
            proj, tail = _norm_proj(xf, norm_mix[l], w_main, w_tail)
            hl = _lru(proj, bsz, lru_conv_w[j], lru_conv_b[j], lru_gate_w[j], lru_gate_b[j], lru_lambda[j])
            hm = _mlstm(proj, tail, bsz, mlstm_gate_b[j])
            xf = _ab_out(xf, proj, hl, hm, mlstm_norm[j], ab_w_out[j].astype(BF16))
        else:
            w = cd_w_in[j]
            n_gla = 2 * GLA_QK + 2 * GLA_V
            w = jnp.concatenate([w[:, :n_gla], w[:, n_gla + 2 * GLA_RANK:], w[:, n_gla:n_gla + 2 * GLA_RANK]], axis=1)
            w_main, w_tail = _split_small(w, n_gla + S5_W)
            proj, tail = _norm_proj(xf, norm_mix[l], w_main, w_tail)
            og = _gla(proj, tail, bsz, gla_w_gate2[j], gla_gate_b[j])
            ops = _s5_operators(s5_a_re[j], s5_a_im[j], s5_log_dt[j], s5_b_re[j], s5_b_im[j],
                                s5_c_re[j], s5_c_im[j])
            ys = _s5(proj[:, n_gla:], bsz, ops)
            xf = _cd_out(xf, proj, og, gla_norm[j], ys, s5_d[j],
                         s5_w_glu[j].astype(BF16), cd_w_out[j].astype(BF16))
        xf = _ffn(xf, norm_ffn2[l], w2_gu, w2_down, l, final_g=final_norm if l == depth - 1 else None)
    return xf.reshape(bsz, seq, d)
```

```python
import functools

import jax
import jax.numpy as jnp
from jax import lax
from jax.experimental import pallas as pl
from jax.experimental.pallas import tpu as pltpu

NORM_EPS = 1e-6

W_LRU = 1024
LRU_HEADS = 8
LRU_BLOCK = W_LRU // LRU_HEADS
CONV_W = 4
CONV_PAD_L = 2
LRU_C = 8.0

W_MLSTM = 1024
MLSTM_HEADS = 4
MLSTM_DH = W_MLSTM // MLSTM_HEADS

GLA_HEADS = 4
GLA_DK = 128
GLA_DV = 256
GLA_QK = GLA_HEADS * GLA_DK
GLA_V = GLA_HEADS * GLA_DV
GLA_RANK = 16
GLA_TAU = 16.0

S5_W = 1024
S5_GROUP = 16
S5_GROUPS = S5_W // S5_GROUP
S5_P = 64
S5_CHUNK = 16
S5_GB = 8
S5_ROW_TILE = 256

LANE = 128
SUBLANE = 8
VMEM_LIMIT = 52 * 1024 * 1024

BF16 = jnp.bfloat16
F32 = jnp.float32


def _rms(x, g):
    ms = jnp.mean(x * x, axis=-1, keepdims=True)
    return x * lax.rsqrt(ms + NORM_EPS) * g


def _sigmoid(z):
    return 0.5 * jnp.tanh(0.5 * z) + 0.5


def _softplus(z):
    return jnp.maximum(z, 0.0) + jnp.log1p(jnp.exp(-jnp.abs(z)))


def _params(*sem):
    return pltpu.CompilerParams(dimension_semantics=sem, vmem_limit_bytes=VMEM_LIMIT)


def _ffn_kernel(x_ref, g_ref, wg_ref, wu_ref, wd_ref, *rest, final):
    if final:
        fg_ref, o_ref, h_ref = rest
    else:
        o_ref, h_ref = rest
    j = pl.program_id(1)

    @pl.when(j == 0)
    def _():
        x = x_ref[...]
        h_ref[...] = _rms(x, g_ref[...]).astype(BF16)
        o_ref[...] = x

    h = h_ref[...]
    g = jnp.dot(h, wg_ref[...], preferred_element_type=F32)
    u = jnp.dot(h, wu_ref[...], preferred_element_type=F32)
    hg = 0.5 * g
    a = (hg * (1.0 + jnp.tanh(hg)) * (0.5 * u)).astype(BF16)
    o_ref[...] += jnp.dot(a, wd_ref[...], preferred_element_type=F32)

    if final:
        @pl.when(j == pl.num_programs(1) - 1)
        def _():
            o_ref[...] = _rms(o_ref[...], fg_ref[...])


def _ffn(x, g, w_gu, w_down, layer, final_g=None, tm=512, tf=512):
    n, d = x.shape
    f = w_down.shape[1]
    tm = min(tm, n)
    nj = f // tf
    final = final_g is not None
    in_specs = [
        pl.BlockSpec((tm, d), lambda i, j: (i, 0)),
        pl.BlockSpec((1, d), lambda i, j: (0, 0)),
        pl.BlockSpec((None, d, tf), lambda i, j: (layer, 0, j)),
        pl.BlockSpec((None, d, tf), lambda i, j: (layer, 0, j + nj)),
        pl.BlockSpec((None, tf, d), lambda i, j: (layer, j, 0)),
    ]
    args = [x, g.reshape(1, d), w_gu, w_gu, w_down]
    if final:
        in_specs.append(pl.BlockSpec((1, d), lambda i, j: (0, 0)))
        args.append(final_g.reshape(1, d))
    return pl.pallas_call(
        functools.partial(_ffn_kernel, final=final),
        grid=(n // tm, nj),
        in_specs=in_specs,
        out_specs=pl.BlockSpec((tm, d), lambda i, j: (i, 0)),
        out_shape=jax.ShapeDtypeStruct((n, d), F32),
        scratch_shapes=[pltpu.VMEM((tm, d), BF16)],
        compiler_params=_params("arbitrary", "arbitrary"),
        name="ffn_final" if final else "ffn",
    )(*args)


def _norm_proj_kernel(x_ref, g_ref, w_ref, ws_ref, o_ref, os_ref, h_ref):
    @pl.when(pl.program_id(1) == 0)
    def _():
        h = _rms(x_ref[...], g_ref[...]).astype(BF16)
        h_ref[...] = h
        os_ref[...] = jnp.dot(h, ws_ref[...], preferred_element_type=F32)

    o_ref[...] = jnp.dot(h_ref[...], w_ref[...], preferred_element_type=F32)


def _norm_proj(x, g, w, w_small, tm=1024, tn=1024):
    n, d = x.shape
    m = w.shape[1]
    ms = w_small.shape[1]
    tm = min(tm, n)
    return pl.pallas_call(
        _norm_proj_kernel,
        grid=(n // tm, m // tn),
        in_specs=[
            pl.BlockSpec((tm, d), lambda i, j: (i, 0)),
            pl.BlockSpec((1, d), lambda i, j: (0, 0)),
            pl.BlockSpec((d, tn), lambda i, j: (0, j)),
            pl.BlockSpec((d, ms), lambda i, j: (0, 0)),
        ],
        out_specs=[pl.BlockSpec((tm, tn), lambda i, j: (i, j)),
                   pl.BlockSpec((tm, ms), lambda i, j: (i, 0))],
        out_shape=[jax.ShapeDtypeStruct((n, m), F32), jax.ShapeDtypeStruct((n, ms), F32)],
        scratch_shapes=[pltpu.VMEM((tm, d), BF16)],
        compiler_params=_params("arbitrary", "arbitrary"),
        name="norm_proj",
    )(x, g.reshape(1, d), w, w_small)


def _lru_kernel(xp_ref, x_ref, xn_ref, cw_ref, cb_ref, gw_ref, gb_ref, lam_ref, o_ref,
                xs_ref, a_ref, b_ref, h_ref, *, tt):
    d = pl.program_id(0)
    t = pl.program_id(2)
    nt = pl.num_programs(2)
    tb = jnp.where(d == 0, t, nt - 1 - t)
    w = x_ref.shape[1]

    xs_ref[0:SUBLANE, :] = jnp.where(tb > 0, xp_ref[...], 0.0)
    xs_ref[SUBLANE:SUBLANE + tt, :] = x_ref[...]
    xs_ref[SUBLANE + tt:2 * SUBLANE + tt, :] = jnp.where(tb < nt - 1, xn_ref[...], 0.0)
    cw = cw_ref[...]
    xc = cb_ref[...]
    for k in range(CONV_W):
        off = SUBLANE + k - CONV_PAD_L
        xc = xc + cw[k:k + 1, :] * xs_ref[off:off + tt, :]

    xcb = xc.astype(BF16)
    gb = gb_ref[0]
    sp = _softplus(-lam_ref[0])
    for hd in range(w // LRU_BLOCK):
        sl = slice(hd * LRU_BLOCK, (hd + 1) * LRU_BLOCK)
        z = jnp.dot(xcb[:, sl], gw_ref[0, hd], preferred_element_type=F32)
        r = _sigmoid(z[:, :LRU_BLOCK] + gb[0:1, sl])
        i = _sigmoid(z[:, LRU_BLOCK:] + gb[1:2, sl])
        log_a = (-LRU_C * sp[:, sl]) * r
        a = jnp.exp(log_a)
        a_ref[:, sl] = a
        b_ref[:, sl] = jnp.sqrt(-jnp.tanh(log_a) * (a * a + 1.0)) * (i * xc[:, sl])

    @pl.when(t == 0)
    def _():
        h_ref[...] = jnp.zeros_like(h_ref)

    sub = lax.broadcasted_iota(jnp.int32, (SUBLANE, w), 0)
    ntile = tt // SUBLANE

    def scan(rev):
        def earlier(x, k, fill):
            if rev:
                return jnp.where(sub < SUBLANE - k, pltpu.roll(x, SUBLANE - k, 0), fill)
            return jnp.where(sub >= k, pltpu.roll(x, k, 0), fill)

        def body(jt, h):
            r0 = pl.multiple_of((ntile - 1 - jt if rev else jt) * SUBLANE, SUBLANE)
            a = a_ref[pl.ds(r0, SUBLANE), :]
            b = b_ref[pl.ds(r0, SUBLANE), :]
            k = 1
            while k < SUBLANE:
                a, b = a * earlier(a, k, 1.0), b + a * earlier(b, k, 0.0)
                k *= 2
            tile = a * h + b
            o_ref[0, pl.ds(r0, SUBLANE), :] = tile
            last = 0 if rev else SUBLANE - 1
            return tile[last:last + 1, :]

        h_ref[...] = lax.fori_loop(0, ntile, body, h_ref[...])

    pl.when(d == 0)(functools.partial(scan, False))
    pl.when(d == 1)(functools.partial(scan, True))


def _lru(proj, bsz, conv_w, conv_b, gate_w, gate_b, lam, tt=256):
    n = proj.shape[0]
    seq = n // bsz
    tt = min(tt, seq)
    nt = seq // tt
    w = W_LRU
    r8 = tt // SUBLANE
    gw = jnp.concatenate([gate_w[:, 0], gate_w[:, 1]], axis=-1).astype(BF16)

    def tb(d, t):
        return t + d * (nt - 1 - 2 * t)

    return pl.pallas_call(
        functools.partial(_lru_kernel, tt=tt),
        grid=(2, bsz, nt),
        in_specs=[
            pl.BlockSpec((SUBLANE, w), lambda d, b, t: (jnp.maximum((b * nt + tb(d, t)) * r8 - 1, 0), 0)),
            pl.BlockSpec((tt, w), lambda d, b, t: (b * nt + tb(d, t), 0)),
            pl.BlockSpec((SUBLANE, w), lambda d, b, t: (jnp.minimum((b * nt + tb(d, t) + 1) * r8, n // SUBLANE - 1), 0)),
            pl.BlockSpec((CONV_W, w), lambda d, b, t: (0, 0)),
            pl.BlockSpec((1, w), lambda d, b, t: (0, 0)),
            pl.BlockSpec((1, LRU_HEADS, LRU_BLOCK, 2 * LRU_BLOCK), lambda d, b, t: (d, 0, 0, 0)),
            pl.BlockSpec((1, 2, w), lambda d, b, t: (d, 0, 0)),
            pl.BlockSpec((1, 1, w), lambda d, b, t: (d, 0, 0)),
        ],
        out_specs=pl.BlockSpec((1, tt, w), lambda d, b, t: (d, b * nt + tb(d, t), 0)),
        out_shape=jax.ShapeDtypeStruct((2, n, w), F32),
        scratch_shapes=[pltpu.VMEM((tt + 2 * SUBLANE, w), F32), pltpu.VMEM((tt, w), F32),
                        pltpu.VMEM((tt, w), F32), pltpu.VMEM((1, w), F32)],
        compiler_params=_params("arbitrary", "arbitrary", "arbitrary"),
        name="rg_lru",
    )(proj, proj, proj, conv_w, conv_b.reshape(1, w), gw, gate_b, lam.reshape(2, 1, w))


def _s5_discretise(a_re, a_im, log_dt, b_re, b_im):
    dt = jnp.exp(log_dt)[:, None]
    mag = jnp.exp(dt * a_re)
    lr = mag * jnp.cos(dt * a_im)
    li = mag * jnp.sin(dt * a_im)
    den = a_re * a_re + a_im * a_im
    nr = lr - 1.0
    cr = (nr * a_re + li * a_im) / den
    ci = (li * a_re - nr * a_im) / den
    bbr = cr[..., None] * b_re - ci[..., None] * b_im
    bbi = cr[..., None] * b_im + ci[..., None] * b_re
    return lr, li, bbr, bbi


def _s5_operators(a_re, a_im, log_dt, b_re, b_im, c_re, c_im):
    L = S5_CHUNK
    G, P, C = S5_GROUPS, S5_P, S5_GROUP
    kfs, mins, mouts, lams = [], [], [], []
    for d in range(2):
        _, _, bbr, bbi = _s5_discretise(a_re[d], a_im[d], log_dt[d], b_re, b_im)
        bbr, bbi = bbr.transpose(0, 2, 1), bbi.transpose(0, 2, 1)
        jj = jnp.arange(L + 1, dtype=F32)[:, None, None, None]
        dt = jnp.exp(log_dt[d])[None, :, None, None]
        mag = jnp.exp(jj * dt * a_re[d][None, :, None, :])
        pr = mag * jnp.cos(jj * dt * a_im[d][None, :, None, :])
        pi = mag * jnp.sin(jj * dt * a_im[d][None, :, None, :])
        cl = jnp.concatenate([c_re[None] * pr - c_im[None] * pi, -(c_re[None] * pi + c_im[None] * pr)], axis=-1)
        bl = jnp.concatenate([pr * bbr[None] - pi * bbi[None], pr * bbi[None] + pi * bbr[None]], axis=-1)
        kfs.append(jnp.sum(cl[:L, :, None, :, :] * bl[0][None, :, :, None, :], axis=-1))
        if d == 0:
            e_in = jnp.arange(L - 1, -1, -1)
            e_out = jnp.arange(1, L + 1)
        else:
            e_in = jnp.arange(L)
            e_out = jnp.arange(L, 0, -1)
        mins.append(bl[e_in])
        mouts.append(cl[e_out])
        lams.append(jnp.stack([pr[L, :, 0], pi[L, :, 0]], axis=1))
    kf, kb = kfs
    kfull = jnp.concatenate([kb[:0:-1], (kf[0] + kb[0])[None], kf[1:]], axis=0)
    gb = S5_GB
    nb_ = G // gb
    k2 = kfull.reshape(2 * L - 1, nb_, gb * C, C).transpose(1, 0, 2, 3)

    def rows_of_block(parts):
        a = jnp.concatenate(parts, axis=-1).reshape(L, nb_, gb * C, 4 * P)
        return a.transpose(1, 0, 2, 3).reshape(nb_, L * gb * C, 4 * P)

    lam_l = jnp.concatenate(lams, axis=1)
    lam_blk = lam_l.reshape(nb_, gb, 4, P).transpose(0, 2, 1, 3).reshape(nb_, 1, 4 * gb * P)
    return k2, rows_of_block(mins), rows_of_block(mouts), lam_blk


def _s5_state_kernel(u_ref, amin_ref, lam_ref, hin_ref, h_ref, min_ref, *, nb):
    nl, rows, _ = u_ref.shape
    _s5_spread(amin_ref[0], min_ref, False)
    rb = rows // nb
    hw = h_ref.shape[1] // 4
    rt = min(rows, S5_ROW_TILE)
    lo = lax.broadcasted_iota(jnp.int32, (rt, LANE), 1) < S5_P
    ng = hw // S5_P

    def pairs():
        for d in range(2):
            for k in range(ng // 2):
                yield ((d * ng + 2 * k) * LANE, (d * ng + 2 * k + 1) * LANE,
                       2 * d * hw + k * LANE, (2 * d + 1) * hw + k * LANE)

    for r0 in range(0, rows, rt):
        u = jnp.concatenate([u_ref[s, r0:r0 + rt, :] for s in range(nl)], axis=1)
        hl = jnp.dot(u, min_ref[...], preferred_element_type=F32)
        for ca, cb, cre, cim in pairs():
            a, b = hl[:, ca:ca + LANE], hl[:, cb:cb + LANE]
            h_ref[r0:r0 + rt, cre:cre + LANE] = jnp.where(lo, a, pltpu.roll(b, S5_P, 1))
            h_ref[r0:r0 + rt, cim:cim + LANE] = jnp.where(lo, pltpu.roll(a, S5_P, 1), b)
    lam = lam_ref[0]
    lrf, lif = lam[:, 0:hw], lam[:, hw:2 * hw]
    lrb, lib = lam[:, 2 * hw:3 * hw], lam[:, 3 * hw:4 * hw]
    sub = lax.broadcasted_iota(jnp.int32, (SUBLANE, hw), 0)

    def body(it, carry):
        new = []
        for b in range(nb):
            fr, fi, br, bi = carry[4 * b:4 * b + 4]
            f0 = pl.multiple_of(b * rb + it * SUBLANE, SUBLANE)
            b0 = pl.multiple_of(b * rb + rb - SUBLANE - it * SUBLANE, SUBLANE)
            tiles = [jnp.zeros((SUBLANE, hw), F32)] * 4
            lf = h_ref[pl.ds(f0, SUBLANE), 0:2 * hw]
            lb = h_ref[pl.ds(b0, SUBLANE), 2 * hw:4 * hw]
            for r in range(SUBLANE):
                q = SUBLANE - 1 - r
                tiles = [jnp.where(sub == r, fr, tiles[0]), jnp.where(sub == r, fi, tiles[1]),
                         jnp.where(sub == q, br, tiles[2]), jnp.where(sub == q, bi, tiles[3])]
                fr, fi, br, bi = (lrf * fr - lif * fi + lf[r:r + 1, 0:hw],
                                  lrf * fi + lif * fr + lf[r:r + 1, hw:2 * hw],
                                  lrb * br - lib * bi + lb[q:q + 1, 0:hw],
                                  lrb * bi + lib * br + lb[q:q + 1, hw:2 * hw])
            h_ref[pl.ds(f0, SUBLANE), 0:hw] = tiles[0]
            h_ref[pl.ds(f0, SUBLANE), hw:2 * hw] = tiles[1]
            h_ref[pl.ds(b0, SUBLANE), 2 * hw:3 * hw] = tiles[2]
            h_ref[pl.ds(b0, SUBLANE), 3 * hw:4 * hw] = tiles[3]
            new += [fr, fi, br, bi]
        return tuple(new)

    zero = jnp.zeros((1, hw), F32)
    lax.fori_loop(0, rb // SUBLANE, body, (zero,) * (4 * nb))
    for r0 in range(0, rows, rt):
        for ca, cb, cre, cim in pairs():
            re, im = h_ref[r0:r0 + rt, cre:cre + LANE], h_ref[r0:r0 + rt, cim:cim + LANE]
            hin_ref[0, r0:r0 + rt, ca:ca + LANE] = jnp.where(lo, re, pltpu.roll(im, S5_P, 1)).astype(BF16)
            hin_ref[0, r0:r0 + rt, cb:cb + LANE] = jnp.where(lo, pltpu.roll(re, S5_P, 1), im).astype(BF16)


def _s5_spread(a, o_ref, transpose):
    rows = a.shape[0]
    row_g = (lax.broadcasted_iota(jnp.int32, (rows, LANE), 0) // S5_GROUP) % S5_GB
    for d in range(2):
        ad = a[:, d * LANE:(d + 1) * LANE]
        for g in range(S5_GB):
            tile = jnp.where(row_g == g, ad, 0.0)
            col = (d * S5_GB + g) * LANE
            if transpose:
                o_ref[col:col + LANE, :] = tile.T.astype(BF16)
            else:
                o_ref[:, col:col + LANE] = tile.astype(BF16)


def _s5_out_kernel(u_ref, hin_ref, k2_ref, amout_ref, y_ref, t_ref, mout_ref):
    nl = u_ref.shape[0]

    @pl.when(pl.program_id(1) == 0)
    def _():
        _s5_spread(amout_ref[0], mout_ref, True)
        row_g = lax.broadcasted_iota(jnp.int32, (LANE, LANE), 0) // S5_GROUP
        col_g = lax.broadcasted_iota(jnp.int32, (LANE, LANE), 1) // S5_GROUP
        spread = (lax.broadcasted_iota(jnp.int32, (S5_GROUP, LANE), 1) % S5_GROUP
                  == lax.broadcasted_iota(jnp.int32, (S5_GROUP, LANE), 0)).astype(BF16)
        for j in range(2 * nl - 1):
            rep = jnp.dot(k2_ref[0, j].astype(BF16), spread, preferred_element_type=F32)
            tile = jnp.where(row_g == col_g, rep, 0.0).astype(BF16)
            for s in range(nl):
                t = s + j - (nl - 1)
                if 0 <= t < nl:
                    t_ref[s * LANE:(s + 1) * LANE, t * LANE:(t + 1) * LANE] = tile

    u = jnp.concatenate([u_ref[s] for s in range(nl)], axis=1)
    y = (jnp.dot(u, t_ref[...], preferred_element_type=F32)
         + jnp.dot(hin_ref[0], mout_ref[...], preferred_element_type=F32))
    for t in range(nl):
        y_ref[t] = y[:, t * LANE:(t + 1) * LANE]


def _s5(u, bsz, ops, tr=256):
    k2, a_min, a_mout, lam_blk = ops
    n, w = u.shape
    L = S5_CHUNK
    rows = n // L
    nblk = w // LANE
    kw = L * LANE
    sw = lam_blk.shape[2]
    tr = min(tr, rows)
    u3 = u.reshape(rows, L, w).transpose(1, 0, 2).astype(BF16)
    hin = pl.pallas_call(
        functools.partial(_s5_state_kernel, nb=bsz),
        grid=(nblk,),
        in_specs=[
            pl.BlockSpec((L, rows, LANE), lambda b: (0, 0, b)),
            pl.BlockSpec((1,) + a_min.shape[1:], lambda b: (b, 0, 0)),
            pl.BlockSpec((1, 1, sw), lambda b: (b, 0, 0)),
        ],
        out_specs=pl.BlockSpec((1, rows, sw), lambda b: (b, 0, 0)),
        out_shape=jax.ShapeDtypeStruct((nblk, rows, sw), BF16),
        scratch_shapes=[pltpu.VMEM((rows, sw), F32), pltpu.VMEM((kw, sw), BF16)],
        compiler_params=_params("arbitrary"),
        name="s5_state",
    )(u3, a_min, lam_blk)
    y3 = pl.pallas_call(
        _s5_out_kernel,
        grid=(nblk, rows // tr),
        in_specs=[
            pl.BlockSpec((L, tr, LANE), lambda b, i: (0, i, b)),
            pl.BlockSpec((1, tr, sw), lambda b, i: (b, i, 0)),
            pl.BlockSpec((1,) + k2.shape[1:], lambda b, i: (b, 0, 0, 0)),
            pl.BlockSpec((1,) + a_mout.shape[1:], lambda b, i: (b, 0, 0)),
        ],
        out_specs=pl.BlockSpec((L, tr, LANE), lambda b, i: (0, i, b)),
        out_shape=jax.ShapeDtypeStruct((L, rows, w), F32),
        scratch_shapes=[pltpu.VMEM((kw, kw), BF16), pltpu.VMEM((sw, kw), BF16)],
        compiler_params=_params("arbitrary", "arbitrary"),
        name="s5_out",
    )(u3, hin, k2, a_mout)
    return y3.transpose(1, 0, 2).reshape(n, w)


def _log_sigmoid(z):
    return -_softplus(-z)


def _tri(length, d):
    ti = lax.broadcasted_iota(jnp.int32, (length, length), 0)
    si = lax.broadcasted_iota(jnp.int32, (length, length), 1)
    return (ti - si) * (1 - 2 * d) >= 0


def _masked_sums(mask, x):
    m = jnp.where(mask, 1.0, 0.0).astype(BF16)
    hi = x.astype(BF16)
    rest = x - hi.astype(F32)
    mid = rest.astype(BF16)
    lo = (rest - mid.astype(F32)).astype(BF16)
    return (jnp.dot(m, hi, preferred_element_type=F32) + jnp.dot(m, mid, preferred_element_type=F32)
            + jnp.dot(m, lo, preferred_element_type=F32))


def _mlstm_kernel(q_ref, k_ref, v_ref, g_ref, gb_ref, o_ref, c_ref, n_ref, m_ref):
    d = pl.program_id(0)
    length = q_ref.shape[0]
    nh, dh = MLSTM_HEADS, MLSTM_DH

    @pl.when(pl.program_id(2) == 0)
    def _():
        c_ref[...] = jnp.zeros_like(c_ref)
        n_ref[...] = jnp.zeros_like(n_ref)
        m_ref[...] = jnp.zeros_like(m_ref)

    causal = _tri(length, d)
    gp = g_ref[...] + gb_ref[...]
    gp = jnp.where(d == 0, gp, pltpu.roll(gp, LANE - 2 * nh, 1))
    lf = _log_sigmoid(gp)
    cum = _masked_sums(causal, lf)
    tot = jnp.sum(lf, axis=0, keepdims=True)
    gp_t = gp.T
    cum_t = cum.T
    nt = (((1,), (1,)), ((), ()))
    tn = (((0,), (0,)), ((), ()))

    for h in range(nh):
        sl = slice(h * dh, (h + 1) * dh)
        qh = (q_ref[:, sl] * (dh ** -0.5)).astype(BF16)
        kf = k_ref[:, sl]
        kh = kf.astype(BF16)
        vh = v_ref[:, sl].astype(BF16)
        ig_c, cum_c = gp[:, h:h + 1], cum[:, nh + h:nh + h + 1]
        ig_r, cum_r = gp_t[h:h + 1, :], cum_t[nh + h:nh + h + 1, :]
        tot_h = tot[:, nh + h:nh + h + 1]
        m_st = m_ref[h:h + 1, 0:1]
        c_st = c_ref[h]
        dmat = jnp.where(causal, cum_c - cum_r + ig_r, -jnp.inf)
        m_inter = cum_c + m_st
        m_t = jnp.maximum(jnp.max(dmat, axis=1, keepdims=True), m_inter)
        w_inter = jnp.exp(m_inter - m_t)
        s = lax.dot_general(qh, kh, nt, preferred_element_type=F32) * jnp.exp(dmat - m_t)
        num = (jnp.dot(s.astype(BF16), vh, preferred_element_type=F32)
               + w_inter * jnp.dot(qh, c_st.astype(BF16), preferred_element_type=F32))
        den = (jnp.sum(s, axis=1, keepdims=True)
               + w_inter * jnp.sum(qh.astype(F32) * n_ref[h:h + 1, :], axis=1, keepdims=True))
        o_ref[0, :, sl] = (num / jnp.maximum(jnp.abs(den), jnp.exp(-m_t))).astype(o_ref.dtype)
        dec = tot_h - cum_c + ig_c
        m_new = jnp.maximum(tot_h + m_st, jnp.max(dec, axis=0, keepdims=True))
        wc = jnp.exp(tot_h + m_st - m_new)
        kw = jnp.exp(dec - m_new) * kf
        c_ref[h] = wc * c_st + lax.dot_general(kw.astype(BF16), vh, tn, preferred_element_type=F32)
        n_ref[h:h + 1, :] = wc * n_ref[h:h + 1, :] + jnp.sum(kw, axis=0, keepdims=True)
        m_ref[h:h + 1, :] = jnp.broadcast_to(m_new, (1, LANE))


def _mlstm(proj, tail, bsz, gate_b, length=256):
    n = proj.shape[0]
    seq = n // bsz
    length = min(length, seq)
    nc = seq // length
    w = W_MLSTM
    q_blk = 2 * W_LRU // w
    gb = jnp.pad(gate_b.reshape(1, -1), ((0, 0), (0, LANE - gate_b.size)))

    def row(d, b, c):
        return b * nc + c + d * (nc - 1 - 2 * c)

    return pl.pallas_call(
        _mlstm_kernel,
        grid=(2, bsz, nc),
        in_specs=[
            pl.BlockSpec((length, w), lambda d, b, c: (row(d, b, c), q_blk)),
            pl.BlockSpec((length, w), lambda d, b, c: (row(d, b, c), q_blk + 1)),
            pl.BlockSpec((length, w), lambda d, b, c: (row(d, b, c), q_blk + 2)),
            pl.BlockSpec((length, LANE), lambda d, b, c: (row(d, b, c), 0)),
            pl.BlockSpec((1, LANE), lambda d, b, c: (0, 0)),
        ],
        out_specs=pl.BlockSpec((1, length, w), lambda d, b, c: (d, row(d, b, c), 0)),
        out_shape=jax.ShapeDtypeStruct((2, n, w), BF16),
        scratch_shapes=[pltpu.VMEM((MLSTM_HEADS, MLSTM_DH, MLSTM_DH), F32),
                        pltpu.VMEM((SUBLANE, MLSTM_DH), F32), pltpu.VMEM((SUBLANE, LANE), F32)],
        compiler_params=_params("arbitrary", "arbitrary", "arbitrary"),
        name="mlstm",
    )(proj, proj, proj, tail, gb)


GLA_SUB = 16
GLA_SAFE_DECAY = 40.0


def _gla_chunk_dense(q_ref, k_ref, v_ref, cum, o_ref, s_ref, rev):
    length = q_ref.shape[0]
    nt = (((1,), (1,)), ((), ()))
    tn = (((0,), (0,)), ((), ()))
    causal = _tri(length, int(rev))
    last = slice(0, 1) if rev else slice(length - 1, length)
    for h in range(GLA_HEADS):
        kl = slice(h * GLA_DK, (h + 1) * GLA_DK)
        vl = slice(h * GLA_DV, (h + 1) * GLA_DV)
        ch = cum[:, kl]
        kk = k_ref[:, kl]
        vv = v_ref[:, vl].astype(BF16)
        st = s_ref[h]
        tot = ch[last]
        qe = (q_ref[:, kl] * (GLA_DK ** -0.5) * jnp.exp(ch)).astype(BF16)
        ke = (kk * jnp.exp(-ch)).astype(BF16)
        att = jnp.where(causal, lax.dot_general(qe, ke, nt, preferred_element_type=F32), 0.0)
        o_ref[0, :, vl] = (jnp.dot(att.astype(BF16), vv, preferred_element_type=F32)
                           + lax.dot_general(qe, st.astype(BF16), nt, preferred_element_type=F32)).astype(o_ref.dtype)
        kd = (kk * jnp.exp(tot - ch)).astype(BF16)
        s_ref[h] = jnp.exp(tot) * st + lax.dot_general(vv, kd, tn, preferred_element_type=F32)


def _gla_chunk(q_ref, k_ref, v_ref, cum, o_ref, s_ref, rev):
    length = q_ref.shape[0]
    c = GLA_SUB
    nt = (((1,), (1,)), ((), ()))
    tn = (((0,), (0,)), ((), ()))

    def rows(a, b):
        return slice(length - b, length - a) if rev else slice(a, b)

    def row(a):
        i = length - 1 - a if rev else a
        return slice(i, i + 1)

    ti = lax.broadcasted_iota(jnp.int32, (c, 1), 0)
    for h in range(GLA_HEADS):
        kl = slice(h * GLA_DK, (h + 1) * GLA_DK)
        vl = slice(h * GLA_DV, (h + 1) * GLA_DV)
        ch = cum[:, kl]
        qs = q_ref[:, kl] * (GLA_DK ** -0.5)
        kk = k_ref[:, kl]
        vf = v_ref[:, vl]
        vv = vf.astype(BF16)
        st = s_ref[h]
        tot = ch[row(length - 1)]
        o = lax.dot_general((qs * jnp.exp(ch)).astype(BF16), st.astype(BF16), nt, preferred_element_type=F32)
        ob = [o[rows(p * c, (p + 1) * c)] for p in range(length // c)]
        m = length // 2
        while m >= c:
            for start in range(0, length, 2 * m):
                fst, sec = rows(start, start + m), rows(start + m, start + 2 * m)
                r = ch[row(start + m)]
                qh = (qs[sec] * jnp.exp(ch[sec] - r)).astype(BF16)
                kh = (kk[fst] * jnp.exp(r - ch[fst])).astype(BF16)
                att = lax.dot_general(qh, kh, nt, preferred_element_type=F32)
                contrib = jnp.dot(att.astype(BF16), vv[fst], preferred_element_type=F32)
                for p in range((start + m) // c, (start + 2 * m) // c):
                    lo = rows(p * c, (p + 1) * c).start - sec.start
                    ob[p] = ob[p] + contrib[lo:lo + c]
            m //= 2
        for p in range(length // c):
            blk = rows(p * c, (p + 1) * c)
            cb, qb, kb, vb = ch[blk], qs[blk], kk[blk], vf[blk]
            acc = ob[p]
            for s in range(c):
                sees = (ti <= s) if rev else (ti >= s)
                e = jnp.exp(jnp.where(sees, cb - cb[s:s + 1], -jnp.inf))
                a = jnp.sum(qb * kb[s:s + 1] * e, axis=1, keepdims=True)
                acc = acc + a * vb[s:s + 1]
            o_ref[0, blk, vl] = acc.astype(o_ref.dtype)
        kd = (kk * jnp.exp(tot - ch)).astype(BF16)
        s_ref[h] = jnp.exp(tot) * st + lax.dot_general(vv, kd, tn, preferred_element_type=F32)


def _gla_kernel(q_ref, k_ref, v_ref, low_ref, wg_ref, bg_ref, o_ref, s_ref):
    d = pl.program_id(0)
    length = q_ref.shape[0]

    @pl.when(pl.program_id(2) == 0)
    def _():
        s_ref[...] = jnp.zeros_like(s_ref)

    gate_pre = jnp.dot(low_ref[...].astype(BF16), wg_ref[0], preferred_element_type=F32) + bg_ref[0]
    la = _log_sigmoid(gate_pre) * (1.0 / GLA_TAU)
    cum = _masked_sums(_tri(length, d), la)
    safe = jnp.sum(la, axis=0, keepdims=True).min() >= -GLA_SAFE_DECAY
    for rev in (False, True):
        pl.when((d == int(rev)) & safe)(
            functools.partial(_gla_chunk_dense, q_ref, k_ref, v_ref, cum, o_ref, s_ref, rev))
        pl.when((d == int(rev)) & jnp.logical_not(safe))(
            functools.partial(_gla_chunk, q_ref, k_ref, v_ref, cum, o_ref, s_ref, rev))


def _gla(proj, tail, bsz, w_gate2, gate_b, length=128):
    n = proj.shape[0]
    seq = n // bsz
    length = min(length, seq)
    nc = seq // length
    wg = jnp.zeros((2, LANE, GLA_QK), F32)
    for d in range(2):
        wg = wg.at[d, d * GLA_RANK:(d + 1) * GLA_RANK].set(w_gate2[d])

    def row(d, b, c):
        return b * nc + c + d * (nc - 1 - 2 * c)

    return pl.pallas_call(
        _gla_kernel,
        grid=(2, bsz, nc),
        in_specs=[
            pl.BlockSpec((length, GLA_QK), lambda d, b, c: (row(d, b, c), 0)),
            pl.BlockSpec((length, GLA_QK), lambda d, b, c: (row(d, b, c), 1)),
            pl.BlockSpec((length, GLA_V), lambda d, b, c: (row(d, b, c), 2 * GLA_QK // GLA_V)),
            pl.BlockSpec((length, LANE), lambda d, b, c: (row(d, b, c), 0)),
            pl.BlockSpec((1, LANE, GLA_QK), lambda d, b, c: (d, 0, 0)),
            pl.BlockSpec((1, 1, GLA_QK), lambda d, b, c: (d, 0, 0)),
        ],
        out_specs=pl.BlockSpec((1, length, GLA_V), lambda d, b, c: (d, row(d, b, c), 0)),
        out_shape=jax.ShapeDtypeStruct((2, n, GLA_V), BF16),
        scratch_shapes=[pltpu.VMEM((GLA_HEADS, GLA_DV, GLA_DK), F32)],
        compiler_params=_params("arbitrary", "arbitrary", "arbitrary"),
        name="gla",
    )(proj, proj, proj, tail, wg.astype(BF16), gate_b.reshape(2, 1, GLA_QK))


def _head_norm(t, g, heads):
    dh = t.shape[1] // heads
    outs = []
    for h in range(heads):
        th = t[:, h * dh:(h + 1) * dh]
        outs.append(th * lax.rsqrt(jnp.mean(th * th, axis=1, keepdims=True) + NORM_EPS))
    return jnp.concatenate(outs, axis=1) * g


def _ab_out_kernel(x_ref, gr_ref, og_ref, hl_ref, hm_ref, g_ref, w_ref, o_ref):
    k = gr_ref.shape[1]
    ya = jax.nn.gelu(gr_ref[...]) * (hl_ref[0] + hl_ref[1])
    yb = jax.nn.sigmoid(og_ref[...]) * _head_norm(hm_ref[0].astype(F32) + hm_ref[1].astype(F32), g_ref[...], MLSTM_HEADS)
    o_ref[...] = (x_ref[...]
                  + jnp.dot(ya.astype(BF16), w_ref[0:k, :], preferred_element_type=F32)
                  + jnp.dot(yb.astype(BF16), w_ref[k:, :], preferred_element_type=F32))


def _ab_out(x, proj, hl, hm, norm_g, w, tm=512):
    n, d = x.shape
    k = W_LRU
    tm = min(tm, n)
    return pl.pallas_call(
        _ab_out_kernel,
        grid=(n // tm,),
        in_specs=[
            pl.BlockSpec((tm, d), lambda i: (i, 0)),
            pl.BlockSpec((tm, k), lambda i: (i, 1)),
            pl.BlockSpec((tm, k), lambda i: (i, 5)),
            pl.BlockSpec((2, tm, k), lambda i: (0, i, 0)),
            pl.BlockSpec((2, tm, k), lambda i: (0, i, 0)),
            pl.BlockSpec((1, k), lambda i: (0, 0)),
            pl.BlockSpec((2 * k, d), lambda i: (0, 0)),
        ],
        out_specs=pl.BlockSpec((tm, d), lambda i: (i, 0)),
        out_shape=jax.ShapeDtypeStruct((n, d), F32),
        compiler_params=_params("arbitrary"),
        name="ab_out",
    )(x, proj, proj, hl, hm, norm_g.reshape(1, k), w)


def _cd_out_kernel(x_ref, r_ref, og_ref, g_ref, ys_ref, u_ref, d_ref, wg_ref, w_ref, o_ref):
    k = r_ref.shape[1]
    r = r_ref[...]
    yc = _head_norm(og_ref[0].astype(F32) + og_ref[1].astype(F32), g_ref[...], GLA_HEADS) * (r * jax.nn.sigmoid(r))
    y = jax.nn.gelu(ys_ref[...] + d_ref[...] * u_ref[...])
    yd = y * jax.nn.sigmoid(jnp.dot(y.astype(BF16), wg_ref[...], preferred_element_type=F32))
    o_ref[...] = (x_ref[...]
                  + jnp.dot(yc.astype(BF16), w_ref[0:k, :], preferred_element_type=F32)
                  + jnp.dot(yd.astype(BF16), w_ref[k:, :], preferred_element_type=F32))


def _cd_out(x, proj, og, norm_g, ys, s5_d, w_glu, w, tm=512):
    n, d = x.shape
    k = S5_W
    tm = min(tm, n)
    return pl.pallas_call(
        _cd_out_kernel,
        grid=(n // tm,),
        in_specs=[
            pl.BlockSpec((tm, d), lambda i: (i, 0)),
            pl.BlockSpec((tm, k), lambda i: (i, 2)),
            pl.BlockSpec((2, tm, k), lambda i: (0, i, 0)),
            pl.BlockSpec((1, k), lambda i: (0, 0)),
            pl.BlockSpec((tm, k), lambda i: (i, 0)),
            pl.BlockSpec((tm, k), lambda i: (i, 3)),
            pl.BlockSpec((1, k), lambda i: (0, 0)),
            pl.BlockSpec((k, k), lambda i: (0, 0)),
            pl.BlockSpec((2 * k, d), lambda i: (0, 0)),
        ],
        out_specs=pl.BlockSpec((tm, d), lambda i: (i, 0)),
        out_shape=jax.ShapeDtypeStruct((n, d), F32),
        compiler_params=_params("arbitrary"),
        name="cd_out",
    )(x, proj, og, norm_g.reshape(1, k), ys, proj, s5_d.reshape(1, k), w_glu, w)


def _split_small(w, n_main):
    tail = w[:, n_main:]
    tail = jnp.pad(tail, ((0, 0), (0, LANE - tail.shape[1])))
    return w[:, :n_main].astype(BF16), tail.astype(BF16)


def kernel(x, norm_ffn1, ffn1_w_gu, ffn1_w_down, norm_mix, norm_ffn2, ffn2_w_gu, ffn2_w_down,
           ab_w_in, lru_conv_w, lru_conv_b, lru_gate_w, lru_gate_b, lru_lambda, mlstm_gate_b,
           mlstm_norm, ab_w_out, cd_w_in, gla_w_gate2, gla_gate_b, gla_norm, s5_a_re, s5_a_im,
           s5_log_dt, s5_b_re, s5_b_im, s5_c_re, s5_c_im, s5_d, s5_w_glu, cd_w_out, final_norm):
    bsz, seq, d = x.shape
    n = bsz * seq
    depth = norm_ffn1.shape[0]
    xf = x.reshape(n, d)
    w1_gu, w1_down = ffn1_w_gu.astype(BF16), ffn1_w_down.astype(BF16)
    w2_gu, w2_down = ffn2_w_gu.astype(BF16), ffn2_w_down.astype(BF16)
    for l in range(depth):
        xf = _ffn(xf, norm_ffn1[l], w1_gu, w1_down, l)
        j = l // 2
        if l % 2 == 0:
            w_main, w_tail = _split_small(ab_w_in[j], 2 * W_LRU + 4 * W_MLSTM)
            proj, tail = _norm_proj(xf, norm_mix[l], w_main, w_tail)
            hl = _lru(proj, bsz, lru_conv_w[j], lru_conv_b[j], lru_gate_w[j], lru_gate_b[j], lru_lambda[j])
            hm = _mlstm(proj, tail, bsz, mlstm_gate_b[j])
            xf = _ab_out(xf, proj, hl, hm, mlstm_norm[j], ab_w_out[j].astype(BF16))
        else:
            w = cd_w_in[j]
            n_gla = 2 * GLA_QK + 2 * GLA_V
            w = jnp.concatenate([w[:, :n_gla], w[:, n_gla + 2 * GLA_RANK:], w[:, n_gla:n_gla + 2 * GLA_RANK]], axis=1)
            w_main, w_tail = _split_small(w, n_gla + S5_W)
            proj, tail = _norm_proj(xf, norm_mix[l], w_main, w_tail)
            og = _gla(proj, tail, bsz, gla_w_gate2[j], gla_gate_b[j])
            ops = _s5_operators(s5_a_re[j], s5_a_im[j], s5_log_dt[j], s5_b_re[j], s5_b_im[j],
                                s5_c_re[j], s5_c_im[j])
            ys = _s5(proj[:, n_gla:], bsz, ops)
            xf = _cd_out(xf, proj, og, gla_norm[j], ys, s5_d[j],
                         s5_w_glu[j].astype(BF16), cd_w_out[j].astype(BF16))
        xf = _ffn(xf, norm_ffn2[l], w2_gu, w2_down, l, final_g=final_norm if l == depth - 1 else None)
    return xf.reshape(bsz, seq, d)
```

```python
import functools

import jax
import jax.numpy as jnp
from jax import lax
from jax.experimental import pallas as pl
from jax.experimental.pallas import tpu as pltpu

NORM_EPS = 1e-6

W_LRU = 1024
LRU_HEADS = 8
LRU_BLOCK = W_LRU // LRU_HEADS
CONV_W = 4
CONV_PAD_L = 2
LRU_C = 8.0

W_MLSTM = 1024
MLSTM_HEADS = 4
MLSTM_DH = W_MLSTM // MLSTM_HEADS

GLA_HEADS = 4
GLA_DK = 128
GLA_DV = 256
GLA_QK = GLA_HEADS * GLA_DK
GLA_V = GLA_HEADS * GLA_DV
GLA_RANK = 16
GLA_TAU = 16.0

S5_W = 1024
S5_GROUP = 16
S5_GROUPS = S5_W // S5_GROUP
S5_P = 64
S5_CHUNK = 16
S5_GB = 8
S5_ROW_TILE = 256

LANE = 128
SUBLANE = 8
VMEM_LIMIT = 52 * 1024 * 1024

BF16 = jnp.bfloat16
F32 = jnp.float32


def _rms(x, g):
    ms = jnp.mean(x * x, axis=-1, keepdims=True)
    return x * lax.rsqrt(ms + NORM_EPS) * g


def _sigmoid(z):
    return 0.5 * jnp.tanh(0.5 * z) + 0.5


def _softplus(z):
    return jnp.maximum(z, 0.0) + jnp.log1p(jnp.exp(-jnp.abs(z)))


def _params(*sem):
    return pltpu.CompilerParams(dimension_semantics=sem, vmem_limit_bytes=VMEM_LIMIT)


def _ffn_kernel(x_ref, g_ref, wg_ref, wu_ref, wd_ref, *rest, final, normed):
    rest = list(rest)
    fg_ref = rest.pop(0) if final else None
    o_ref = rest.pop(0)
    h_ref = g_ref if normed else rest.pop(0)
    j = pl.program_id(1)

    @pl.when(j == 0)
    def _():
        x = x_ref[...]
        if not normed:
            h_ref[...] = _rms(x, g_ref[...]).astype(BF16)
        o_ref[...] = x

    h = h_ref[...]
    g = jnp.dot(h, wg_ref[...], preferred_element_type=F32)
    u = jnp.dot(h, wu_ref[...], preferred_element_type=F32)
    hg = 0.5 * g
    a = (hg * (1.0 + jnp.tanh(hg)) * (0.5 * u)).astype(BF16)
    o_ref[...] += jnp.dot(a, wd_ref[...], preferred_element_type=F32)

    if final:
        @pl.when(j == pl.num_programs(1) - 1)
        def _():
            o_ref[...] = _rms(o_ref[...], fg_ref[...])


def _ffn(x, g, w_gu, w_down, layer, final_g=None, tm=512, tf=512):
    n, d = x.shape
    f = w_down.shape[1]
    tm = min(tm, n)
    nj = f // tf
    final = final_g is not None
    normed = g.ndim == 2
    in_specs = [
        pl.BlockSpec((tm, d), lambda i, j: (i, 0)),
        pl.BlockSpec((tm, d), lambda i, j: (i, 0)) if normed else pl.BlockSpec((1, d), lambda i, j: (0, 0)),
        pl.BlockSpec((None, d, tf), lambda i, j: (layer, 0, j)),
        pl.BlockSpec((None, d, tf), lambda i, j: (layer, 0, j + nj)),
        pl.BlockSpec((None, tf, d), lambda i, j: (layer, j, 0)),
    ]
    args = [x, g if normed else g.reshape(1, d), w_gu, w_gu, w_down]
    if final:
        in_specs.append(pl.BlockSpec((1, d), lambda i, j: (0, 0)))
        args.append(final_g.reshape(1, d))
    return pl.pallas_call(
        functools.partial(_ffn_kernel, final=final, normed=normed),
        grid=(n // tm, nj),
        in_specs=in_specs,
        out_specs=pl.BlockSpec((tm, d), lambda i, j: (i, 0)),
        out_shape=jax.ShapeDtypeStruct((n, d), F32),
        scratch_shapes=[] if normed else [pltpu.VMEM((tm, d), BF16)],
        compiler_params=_params("arbitrary", "arbitrary"),
        name="ffn_final" if final else "ffn",
    )(*args)


def _norm_proj_kernel(x_ref, g_ref, w_ref, ws_ref, o_ref, os_ref, h_ref):
    @pl.when(pl.program_id(1) == 0)
    def _():
        h = _rms(x_ref[...], g_ref[...]).astype(BF16)
        h_ref[...] = h
        os_ref[...] = jnp.dot(h, ws_ref[...], preferred_element_type=F32)

    o_ref[...] = jnp.dot(h_ref[...], w_ref[...], preferred_element_type=F32)


def _norm_proj(x, g, w, w_small, tm=1024, tn=1024):
    n, d = x.shape
    m = w.shape[1]
    ms = w_small.shape[1]
    tm = min(tm, n)
    return pl.pallas_call(
        _norm_proj_kernel,
        grid=(n // tm, m // tn),
        in_specs=[
            pl.BlockSpec((tm, d), lambda i, j: (i, 0)),
            pl.BlockSpec((1, d), lambda i, j: (0, 0)),
            pl.BlockSpec((d, tn), lambda i, j: (0, j)),
            pl.BlockSpec((d, ms), lambda i, j: (0, 0)),
        ],
        out_specs=[pl.BlockSpec((tm, tn), lambda i, j: (i, j)),
                   pl.BlockSpec((tm, ms), lambda i, j: (i, 0))],
        out_shape=[jax.ShapeDtypeStruct((n, m), F32), jax.ShapeDtypeStruct((n, ms), F32)],
        scratch_shapes=[pltpu.VMEM((tm, d), BF16)],
        compiler_params=_params("arbitrary", "arbitrary"),
        name="norm_proj",
    )(x, g.reshape(1, d), w, w_small)


def _lru_kernel(xp_ref, x_ref, xn_ref, cw_ref, cb_ref, gw_ref, gb_ref, lam_ref, o_ref,
                xs_ref, a_ref, b_ref, h_ref, *, tt):
    d = pl.program_id(0)
    t = pl.program_id(2)
    nt = pl.num_programs(2)
    tb = jnp.where(d == 0, t, nt - 1 - t)
    w = x_ref.shape[1]

    xs_ref[0:SUBLANE, :] = jnp.where(tb > 0, xp_ref[...], 0.0)
    xs_ref[SUBLANE:SUBLANE + tt, :] = x_ref[...]
    xs_ref[SUBLANE + tt:2 * SUBLANE + tt, :] = jnp.where(tb < nt - 1, xn_ref[...], 0.0)
    cw = cw_ref[...]
    xc = cb_ref[...]
    for k in range(CONV_W):
        off = SUBLANE + k - CONV_PAD_L
        xc = xc + cw[k:k + 1, :] * xs_ref[off:off + tt, :]

    xcb = xc.astype(BF16)
    gb = gb_ref[0]
    sp = _softplus(-lam_ref[0])
    for hd in range(w // LRU_BLOCK):
        sl = slice(hd * LRU_BLOCK, (hd + 1) * LRU_BLOCK)
        z = jnp.dot(xcb[:, sl], gw_ref[0, hd], preferred_element_type=F32)
        r = _sigmoid(z[:, :LRU_BLOCK] + gb[0:1, sl])
        i = _sigmoid(z[:, LRU_BLOCK:] + gb[1:2, sl])
        log_a = (-LRU_C * sp[:, sl]) * r
        a = jnp.exp(log_a)
        a_ref[:, sl] = a
        b_ref[:, sl] = jnp.sqrt(-jnp.tanh(log_a) * (a * a + 1.0)) * (i * xc[:, sl])

    @pl.when(t == 0)
    def _():
        h_ref[...] = jnp.zeros_like(h_ref)

    sub = lax.broadcasted_iota(jnp.int32, (SUBLANE, w), 0)
    ntile = tt // SUBLANE

    def scan(rev):
        def earlier(x, k, fill):
            if rev:
                return jnp.where(sub < SUBLANE - k, pltpu.roll(x, SUBLANE - k, 0), fill)
            return jnp.where(sub >= k, pltpu.roll(x, k, 0), fill)

        def body(jt, h):
            r0 = pl.multiple_of((ntile - 1 - jt if rev else jt) * SUBLANE, SUBLANE)
            a = a_ref[pl.ds(r0, SUBLANE), :]
            b = b_ref[pl.ds(r0, SUBLANE), :]
            k = 1
            while k < SUBLANE:
                a, b = a * earlier(a, k, 1.0), b + a * earlier(b, k, 0.0)
                k *= 2
            tile = a * h + b
            o_ref[0, pl.ds(r0, SUBLANE), :] = tile
            last = 0 if rev else SUBLANE - 1
            return tile[last:last + 1, :]

        h_ref[...] = lax.fori_loop(0, ntile, body, h_ref[...])

    pl.when(d == 0)(functools.partial(scan, False))
    pl.when(d == 1)(functools.partial(scan, True))


def _lru(proj, bsz, conv_w, conv_b, gate_w, gate_b, lam, tt=256):
    n = proj.shape[0]
    seq = n // bsz
    tt = min(tt, seq)
    nt = seq // tt
    w = W_LRU
    r8 = tt // SUBLANE
    gw = jnp.concatenate([gate_w[:, 0], gate_w[:, 1]], axis=-1).astype(BF16)

    def tb(d, t):
        return t + d * (nt - 1 - 2 * t)

    return pl.pallas_call(
        functools.partial(_lru_kernel, tt=tt),
        grid=(2, bsz, nt),
        in_specs=[
            pl.BlockSpec((SUBLANE, w), lambda d, b, t: (jnp.maximum((b * nt + tb(d, t)) * r8 - 1, 0), 0)),
            pl.BlockSpec((tt, w), lambda d, b, t: (b * nt + tb(d, t), 0)),
            pl.BlockSpec((SUBLANE, w), lambda d, b, t: (jnp.minimum((b * nt + tb(d, t) + 1) * r8, n // SUBLANE - 1), 0)),
            pl.BlockSpec((CONV_W, w), lambda d, b, t: (0, 0)),
            pl.BlockSpec((1, w), lambda d, b, t: (0, 0)),
            pl.BlockSpec((1, LRU_HEADS, LRU_BLOCK, 2 * LRU_BLOCK), lambda d, b, t: (d, 0, 0, 0)),
            pl.BlockSpec((1, 2, w), lambda d, b, t: (d, 0, 0)),
            pl.BlockSpec((1, 1, w), lambda d, b, t: (d, 0, 0)),
        ],
        out_specs=pl.BlockSpec((1, tt, w), lambda d, b, t: (d, b * nt + tb(d, t), 0)),
        out_shape=jax.ShapeDtypeStruct((2, n, w), F32),
        scratch_shapes=[pltpu.VMEM((tt + 2 * SUBLANE, w), F32), pltpu.VMEM((tt, w), F32),
                        pltpu.VMEM((tt, w), F32), pltpu.VMEM((1, w), F32)],
        compiler_params=_params("arbitrary", "arbitrary", "arbitrary"),
        name="rg_lru",
    )(proj, proj, proj, conv_w, conv_b.reshape(1, w), gw, gate_b, lam.reshape(2, 1, w))


def _s5_discretise(a_re, a_im, log_dt, b_re, b_im):
    dt = jnp.exp(log_dt)[:, None]
    mag = jnp.exp(dt * a_re)
    lr = mag * jnp.cos(dt * a_im)
    li = mag * jnp.sin(dt * a_im)
    den = a_re * a_re + a_im * a_im
    nr = lr - 1.0
    cr = (nr * a_re + li * a_im) / den
    ci = (li * a_re - nr * a_im) / den
    bbr = cr[..., None] * b_re - ci[..., None] * b_im
    bbi = cr[..., None] * b_im + ci[..., None] * b_re
    return lr, li, bbr, bbi


def _s5_operators(a_re, a_im, log_dt, b_re, b_im, c_re, c_im):
    L = S5_CHUNK
    G, P, C = S5_GROUPS, S5_P, S5_GROUP
    kfs, mins, mouts, lams = [], [], [], []
    for d in range(2):
        _, _, bbr, bbi = _s5_discretise(a_re[d], a_im[d], log_dt[d], b_re, b_im)
        bbr, bbi = bbr.transpose(0, 2, 1), bbi.transpose(0, 2, 1)
        jj = jnp.arange(L + 1, dtype=F32)[:, None, None, None]
        dt = jnp.exp(log_dt[d])[None, :, None, None]
        mag = jnp.exp(jj * dt * a_re[d][None, :, None, :])
        pr = mag * jnp.cos(jj * dt * a_im[d][None, :, None, :])
        pi = mag * jnp.sin(jj * dt * a_im[d][None, :, None, :])
        cl = jnp.concatenate([c_re[None] * pr - c_im[None] * pi, -(c_re[None] * pi + c_im[None] * pr)], axis=-1)
        bl = jnp.concatenate([pr * bbr[None] - pi * bbi[None], pr * bbi[None] + pi * bbr[None]], axis=-1)
        kfs.append(jnp.sum(cl[:L, :, None, :, :] * bl[0][None, :, :, None, :], axis=-1))
        if d == 0:
            e_in = jnp.arange(L - 1, -1, -1)
            e_out = jnp.arange(1, L + 1)
        else:
            e_in = jnp.arange(L)
            e_out = jnp.arange(L, 0, -1)
        mins.append(bl[e_in])
        mouts.append(cl[e_out])
        lams.append(jnp.stack([pr[L, :, 0], pi[L, :, 0]], axis=1))
    kf, kb = kfs
    kfull = jnp.concatenate([kb[:0:-1], (kf[0] + kb[0])[None], kf[1:]], axis=0)
    gb = S5_GB
    nb_ = G // gb
    k2 = kfull.reshape(2 * L - 1, nb_, gb * C, C).transpose(1, 0, 2, 3)

    def rows_of_block(parts):
        a = jnp.concatenate(parts, axis=-1).reshape(L, nb_, gb * C, 4 * P)
        return a.transpose(1, 0, 2, 3).reshape(nb_, L * gb * C, 4 * P)

    lam_l = jnp.concatenate(lams, axis=1)
    lam_blk = lam_l.reshape(nb_, gb, 4, P).transpose(0, 2, 1, 3).reshape(nb_, 1, 4 * gb * P)
    return k2, rows_of_block(mins), rows_of_block(mouts), lam_blk


def _s5_state_kernel(u_ref, amin_ref, lam_ref, hin_ref, h_ref, min_ref, *, nb):
    nl, rows, _ = u_ref.shape
    _s5_spread(amin_ref[0], min_ref, False)
    rb = rows // nb
    hw = h_ref.shape[1] // 4
    rt = min(rows, S5_ROW_TILE)
    lo = lax.broadcasted_iota(jnp.int32, (rt, LANE), 1) < S5_P
    ng = hw // S5_P

    def pairs():
        for d in range(2):
            for k in range(ng // 2):
                yield ((d * ng + 2 * k) * LANE, (d * ng + 2 * k + 1) * LANE,
                       2 * d * hw + k * LANE, (2 * d + 1) * hw + k * LANE)

    for r0 in range(0, rows, rt):
        u = jnp.concatenate([u_ref[s, r0:r0 + rt, :] for s in range(nl)], axis=1)
        hl = jnp.dot(u, min_ref[...], preferred_element_type=F32)
        for ca, cb, cre, cim in pairs():
            a, b = hl[:, ca:ca + LANE], hl[:, cb:cb + LANE]
            h_ref[r0:r0 + rt, cre:cre + LANE] = jnp.where(lo, a, pltpu.roll(b, S5_P, 1))
            h_ref[r0:r0 + rt, cim:cim + LANE] = jnp.where(lo, pltpu.roll(a, S5_P, 1), b)
    lam = lam_ref[0]
    lrf, lif = lam[:, 0:hw], lam[:, hw:2 * hw]
    lrb, lib = lam[:, 2 * hw:3 * hw], lam[:, 3 * hw:4 * hw]
    sub = lax.broadcasted_iota(jnp.int32, (SUBLANE, hw), 0)

    def body(it, carry):
        new = []
        for b in range(nb):
            fr, fi, br, bi = carry[4 * b:4 * b + 4]
            f0 = pl.multiple_of(b * rb + it * SUBLANE, SUBLANE)
            b0 = pl.multiple_of(b * rb + rb - SUBLANE - it * SUBLANE, SUBLANE)
            tiles = [jnp.zeros((SUBLANE, hw), F32)] * 4
            lf = h_ref[pl.ds(f0, SUBLANE), 0:2 * hw]
            lb = h_ref[pl.ds(b0, SUBLANE), 2 * hw:4 * hw]
            for r in range(SUBLANE):
                q = SUBLANE - 1 - r
                tiles = [jnp.where(sub == r, fr, tiles[0]), jnp.where(sub == r, fi, tiles[1]),
                         jnp.where(sub == q, br, tiles[2]), jnp.where(sub == q, bi, tiles[3])]
                fr, fi, br, bi = (lrf * fr - lif * fi + lf[r:r + 1, 0:hw],
                                  lrf * fi + lif * fr + lf[r:r + 1, hw:2 * hw],
                                  lrb * br - lib * bi + lb[q:q + 1, 0:hw],
                                  lrb * bi + lib * br + lb[q:q + 1, hw:2 * hw])
            h_ref[pl.ds(f0, SUBLANE), 0:hw] = tiles[0]
            h_ref[pl.ds(f0, SUBLANE), hw:2 * hw] = tiles[1]
            h_ref[pl.ds(b0, SUBLANE), 2 * hw:3 * hw] = tiles[2]
            h_ref[pl.ds(b0, SUBLANE), 3 * hw:4 * hw] = tiles[3]
            new += [fr, fi, br, bi]
        return tuple(new)

    zero = jnp.zeros((1, hw), F32)
    lax.fori_loop(0, rb // SUBLANE, body, (zero,) * (4 * nb))
    for r0 in range(0, rows, rt):
        for ca, cb, cre, cim in pairs():
            re, im = h_ref[r0:r0 + rt, cre:cre + LANE], h_ref[r0:r0 + rt, cim:cim + LANE]
            hin_ref[0, r0:r0 + rt, ca:ca + LANE] = jnp.where(lo, re, pltpu.roll(im, S5_P, 1)).astype(BF16)
            hin_ref[0, r0:r0 + rt, cb:cb + LANE] = jnp.where(lo, pltpu.roll(re, S5_P, 1), im).astype(BF16)


def _s5_spread(a, o_ref, transpose):
    rows = a.shape[0]
    row_g = (lax.broadcasted_iota(jnp.int32, (rows, LANE), 0) // S5_GROUP) % S5_GB
    for d in range(2):
        ad = a[:, d * LANE:(d + 1) * LANE]
        for g in range(S5_GB):
            tile = jnp.where(row_g == g, ad, 0.0)
            col = (d * S5_GB + g) * LANE
            if transpose:
                o_ref[col:col + LANE, :] = tile.T.astype(BF16)
            else:
                o_ref[:, col:col + LANE] = tile.astype(BF16)


def _s5_out_kernel(u_ref, hin_ref, k2_ref, amout_ref, y_ref, t_ref, mout_ref):
    nl = u_ref.shape[0]

    @pl.when(pl.program_id(1) == 0)
    def _():
        _s5_spread(amout_ref[0], mout_ref, True)
        row_g = lax.broadcasted_iota(jnp.int32, (LANE, LANE), 0) // S5_GROUP
        col_g = lax.broadcasted_iota(jnp.int32, (LANE, LANE), 1) // S5_GROUP
        spread = (lax.broadcasted_iota(jnp.int32, (S5_GROUP, LANE), 1) % S5_GROUP
                  == lax.broadcasted_iota(jnp.int32, (S5_GROUP, LANE), 0)).astype(BF16)
        for j in range(2 * nl - 1):
            rep = jnp.dot(k2_ref[0, j].astype(BF16), spread, preferred_element_type=F32)
            tile = jnp.where(row_g == col_g, rep, 0.0).astype(BF16)
            for s in range(nl):
                t = s + j - (nl - 1)
                if 0 <= t < nl:
                    t_ref[s * LANE:(s + 1) * LANE, t * LANE:(t + 1) * LANE] = tile

    u = jnp.concatenate([u_ref[s] for s in range(nl)], axis=1)
    y = (jnp.dot(u, t_ref[...], preferred_element_type=F32)
         + jnp.dot(hin_ref[0], mout_ref[...], preferred_element_type=F32))
    for t in range(nl):
        y_ref[t] = y[:, t * LANE:(t + 1) * LANE]


def _s5(u, bsz, ops, tr=256):
    k2, a_min, a_mout, lam_blk = ops
    n, w = u.shape
    L = S5_CHUNK
    rows = n // L
    nblk = w // LANE
    kw = L * LANE
    sw = lam_blk.shape[2]
    tr = min(tr, rows)
    u3 = u.reshape(rows, L, w).transpose(1, 0, 2).astype(BF16)
    hin = pl.pallas_call(
        functools.partial(_s5_state_kernel, nb=bsz),
        grid=(nblk,),
        in_specs=[
            pl.BlockSpec((L, rows, LANE), lambda b: (0, 0, b)),
            pl.BlockSpec((1,) + a_min.shape[1:], lambda b: (b, 0, 0)),
            pl.BlockSpec((1, 1, sw), lambda b: (b, 0, 0)),
        ],
        out_specs=pl.BlockSpec((1, rows, sw), lambda b: (b, 0, 0)),
        out_shape=jax.ShapeDtypeStruct((nblk, rows, sw), BF16),
        scratch_shapes=[pltpu.VMEM((rows, sw), F32), pltpu.VMEM((kw, sw), BF16)],
        compiler_params=_params("arbitrary"),
        name="s5_state",
    )(u3, a_min, lam_blk)
    y3 = pl.pallas_call(
        _s5_out_kernel,
        grid=(nblk, rows // tr),
        in_specs=[
            pl.BlockSpec((L, tr, LANE), lambda b, i: (0, i, b)),
            pl.BlockSpec((1, tr, sw), lambda b, i: (b, i, 0)),
            pl.BlockSpec((1,) + k2.shape[1:], lambda b, i: (b, 0, 0, 0)),
            pl.BlockSpec((1,) + a_mout.shape[1:], lambda b, i: (b, 0, 0)),
        ],
        out_specs=pl.BlockSpec((L, tr, LANE), lambda b, i: (0, i, b)),
        out_shape=jax.ShapeDtypeStruct((L, rows, w), F32),
        scratch_shapes=[pltpu.VMEM((kw, kw), BF16), pltpu.VMEM((sw, kw), BF16)],
        compiler_params=_params("arbitrary", "arbitrary"),
        name="s5_out",
    )(u3, hin, k2, a_mout)
    return y3.transpose(1, 0, 2).reshape(n, w)


def _log_sigmoid(z):
    return -_softplus(-z)


def _tri(length, d):
    ti = lax.broadcasted_iota(jnp.int32, (length, length), 0)
    si = lax.broadcasted_iota(jnp.int32, (length, length), 1)
    return (ti - si) * (1 - 2 * d) >= 0


def _masked_sums(mask, x):
    m = jnp.where(mask, 1.0, 0.0).astype(BF16)
    hi = x.astype(BF16)
    rest = x - hi.astype(F32)
    mid = rest.astype(BF16)
    lo = (rest - mid.astype(F32)).astype(BF16)
    return (jnp.dot(m, hi, preferred_element_type=F32) + jnp.dot(m, mid, preferred_element_type=F32)
            + jnp.dot(m, lo, preferred_element_type=F32))


def _mlstm_kernel(q_ref, k_ref, v_ref, g_ref, gb_ref, o_ref, c_ref, n_ref, m_ref):
    d = pl.program_id(0)
    length = q_ref.shape[0]
    nh, dh = MLSTM_HEADS, MLSTM_DH

    @pl.when(pl.program_id(2) == 0)
    def _():
        c_ref[...] = jnp.zeros_like(c_ref)
        n_ref[...] = jnp.zeros_like(n_ref)
        m_ref[...] = jnp.zeros_like(m_ref)

    causal = _tri(length, d)
    gp = g_ref[...] + gb_ref[...]
    gp = jnp.where(d == 0, gp, pltpu.roll(gp, LANE - 2 * nh, 1))
    lf = _log_sigmoid(gp)
    cum = _masked_sums(causal, lf)
    tot = jnp.sum(lf, axis=0, keepdims=True)
    gp_t = gp.T
    cum_t = cum.T
    nt = (((1,), (1,)), ((), ()))
    tn = (((0,), (0,)), ((), ()))

    for h in range(nh):
        sl = slice(h * dh, (h + 1) * dh)
        qh = (q_ref[:, sl] * (dh ** -0.5)).astype(BF16)
        kf = k_ref[:, sl]
        kh = kf.astype(BF16)
        vh = v_ref[:, sl].astype(BF16)
        ig_c, cum_c = gp[:, h:h + 1], cum[:, nh + h:nh + h + 1]
        ig_r, cum_r = gp_t[h:h + 1, :], cum_t[nh + h:nh + h + 1, :]
        tot_h = tot[:, nh + h:nh + h + 1]
        m_st = m_ref[h:h + 1, 0:1]
        c_st = c_ref[h]
        dmat = jnp.where(causal, cum_c - cum_r + ig_r, -jnp.inf)
        m_inter = cum_c + m_st
        m_t = jnp.maximum(jnp.max(dmat, axis=1, keepdims=True), m_inter)
        w_inter = jnp.exp(m_inter - m_t)
        s = lax.dot_general(qh, kh, nt, preferred_element_type=F32) * jnp.exp(dmat - m_t)
        num = (jnp.dot(s.astype(BF16), vh, preferred_element_type=F32)
               + w_inter * jnp.dot(qh, c_st.astype(BF16), preferred_element_type=F32))
        den = (jnp.sum(s, axis=1, keepdims=True)
               + w_inter * jnp.sum(qh.astype(F32) * n_ref[h:h + 1, :], axis=1, keepdims=True))
        o_ref[0, :, sl] = (num / jnp.maximum(jnp.abs(den), jnp.exp(-m_t))).astype(o_ref.dtype)
        dec = tot_h - cum_c + ig_c
        m_new = jnp.maximum(tot_h + m_st, jnp.max(dec, axis=0, keepdims=True))
        wc = jnp.exp(tot_h + m_st - m_new)
        kw = jnp.exp(dec - m_new) * kf
        c_ref[h] = wc * c_st + lax.dot_general(kw.astype(BF16), vh, tn, preferred_element_type=F32)
        n_ref[h:h + 1, :] = wc * n_ref[h:h + 1, :] + jnp.sum(kw, axis=0, keepdims=True)
        m_ref[h:h + 1, :] = jnp.broadcast_to(m_new, (1, LANE))


def _mlstm(proj, tail, bsz, gate_b, length=256):
    n = proj.shape[0]
    seq = n // bsz
    length = min(length, seq)
    nc = seq // length
    w = W_MLSTM
    q_blk = 2 * W_LRU // w
    gb = jnp.pad(gate_b.reshape(1, -1), ((0, 0), (0, LANE - gate_b.size)))

    def row(d, b, c):
        return b * nc + c + d * (nc - 1 - 2 * c)

    return pl.pallas_call(
        _mlstm_kernel,
        grid=(2, bsz, nc),
        in_specs=[
            pl.BlockSpec((length, w), lambda d, b, c: (row(d, b, c), q_blk)),
            pl.BlockSpec((length, w), lambda d, b, c: (row(d, b, c), q_blk + 1)),
            pl.BlockSpec((length, w), lambda d, b, c: (row(d, b, c), q_blk + 2)),
            pl.BlockSpec((length, LANE), lambda d, b, c: (row(d, b, c), 0)),
            pl.BlockSpec((1, LANE), lambda d, b, c: (0, 0)),
        ],
        out_specs=pl.BlockSpec((1, length, w), lambda d, b, c: (d, row(d, b, c), 0)),
        out_shape=jax.ShapeDtypeStruct((2, n, w), BF16),
        scratch_shapes=[pltpu.VMEM((MLSTM_HEADS, MLSTM_DH, MLSTM_DH), F32),
                        pltpu.VMEM((SUBLANE, MLSTM_DH), F32), pltpu.VMEM((SUBLANE, LANE), F32)],
        compiler_params=_params("arbitrary", "arbitrary", "arbitrary"),
        name="mlstm",
    )(proj, proj, proj, tail, gb)


GLA_SUB = 16
GLA_SAFE_DECAY = 40.0


def _gla_chunk_dense(q_ref, k_ref, v_ref, cum, o_ref, s_ref, rev):
    length = q_ref.shape[0]
    nt = (((1,), (1,)), ((), ()))
    tn = (((0,), (0,)), ((), ()))
    causal = _tri(length, int(rev))
    last = slice(0, 1) if rev else slice(length - 1, length)
    for h in range(GLA_HEADS):
        kl = slice(h * GLA_DK, (h + 1) * GLA_DK)
        vl = slice(h * GLA_DV, (h + 1) * GLA_DV)
        ch = cum[:, kl]
        kk = k_ref[:, kl]
        vv = v_ref[:, vl].astype(BF16)
        st = s_ref[h]
        tot = ch[last]
        qe = (q_ref[:, kl] * (GLA_DK ** -0.5) * jnp.exp(ch)).astype(BF16)
        ke = (kk * jnp.exp(-ch)).astype(BF16)
        att = jnp.where(causal, lax.dot_general(qe, ke, nt, preferred_element_type=F32), 0.0)
        o_ref[0, :, vl] = (jnp.dot(att.astype(BF16), vv, preferred_element_type=F32)
                           + lax.dot_general(qe, st.astype(BF16), nt, preferred_element_type=F32)).astype(o_ref.dtype)
        kd = (kk * jnp.exp(tot - ch)).astype(BF16)
        s_ref[h] = jnp.exp(tot) * st + lax.dot_general(vv, kd, tn, preferred_element_type=F32)


def _gla_chunk(q_ref, k_ref, v_ref, cum, o_ref, s_ref, rev):
    length = q_ref.shape[0]
    c = GLA_SUB
    nt = (((1,), (1,)), ((), ()))
    tn = (((0,), (0,)), ((), ()))

    def rows(a, b):
        return slice(length - b, length - a) if rev else slice(a, b)

    def row(a):
        i = length - 1 - a if rev else a
        return slice(i, i + 1)

    ti = lax.broadcasted_iota(jnp.int32, (c, 1), 0)
    for h in range(GLA_HEADS):
        kl = slice(h * GLA_DK, (h + 1) * GLA_DK)
        vl = slice(h * GLA_DV, (h + 1) * GLA_DV)
        ch = cum[:, kl]
        qs = q_ref[:, kl] * (GLA_DK ** -0.5)
        kk = k_ref[:, kl]
        vf = v_ref[:, vl]
        vv = vf.astype(BF16)
        st = s_ref[h]
        tot = ch[row(length - 1)]
        o = lax.dot_general((qs * jnp.exp(ch)).astype(BF16), st.astype(BF16), nt, preferred_element_type=F32)
        ob = [o[rows(p * c, (p + 1) * c)] for p in range(length // c)]
        m = length // 2
        while m >= c:
            for start in range(0, length, 2 * m):
                fst, sec = rows(start, start + m), rows(start + m, start + 2 * m)
                r = ch[row(start + m)]
                qh = (qs[sec] * jnp.exp(ch[sec] - r)).astype(BF16)
                kh = (kk[fst] * jnp.exp(r - ch[fst])).astype(BF16)
                att = lax.dot_general(qh, kh, nt, preferred_element_type=F32)
                contrib = jnp.dot(att.astype(BF16), vv[fst], preferred_element_type=F32)
                for p in range((start + m) // c, (start + 2 * m) // c):
                    lo = rows(p * c, (p + 1) * c).start - sec.start
                    ob[p] = ob[p] + contrib[lo:lo + c]
            m //= 2
        for p in range(length // c):
            blk = rows(p * c, (p + 1) * c)
            cb, qb, kb, vb = ch[blk], qs[blk], kk[blk], vf[blk]
            acc = ob[p]
            for s in range(c):
                sees = (ti <= s) if rev else (ti >= s)
                e = jnp.exp(jnp.where(sees, cb - cb[s:s + 1], -jnp.inf))
                a = jnp.sum(qb * kb[s:s + 1] * e, axis=1, keepdims=True)
                acc = acc + a * vb[s:s + 1]
            o_ref[0, blk, vl] = acc.astype(o_ref.dtype)
        kd = (kk * jnp.exp(tot - ch)).astype(BF16)
        s_ref[h] = jnp.exp(tot) * st + lax.dot_general(vv, kd, tn, preferred_element_type=F32)


def _gla_kernel(q_ref, k_ref, v_ref, low_ref, wg_ref, bg_ref, o_ref, s_ref):
    d = pl.program_id(0)
    length = q_ref.shape[0]

    @pl.when(pl.program_id(2) == 0)
    def _():
        s_ref[...] = jnp.zeros_like(s_ref)

    gate_pre = jnp.dot(low_ref[...].astype(BF16), wg_ref[0], preferred_element_type=F32) + bg_ref[0]
    la = _log_sigmoid(gate_pre) * (1.0 / GLA_TAU)
    cum = _masked_sums(_tri(length, d), la)
    safe = jnp.sum(la, axis=0, keepdims=True).min() >= -GLA_SAFE_DECAY
    for rev in (False, True):
        pl.when((d == int(rev)) & safe)(
            functools.partial(_gla_chunk_dense, q_ref, k_ref, v_ref, cum, o_ref, s_ref, rev))
        pl.when((d == int(rev)) & jnp.logical_not(safe))(
            functools.partial(_gla_chunk, q_ref, k_ref, v_ref, cum, o_ref, s_ref, rev))


def _gla(proj, tail, bsz, w_gate2, gate_b, length=128):
    n = proj.shape[0]
    seq = n // bsz
    length = min(length, seq)
    nc = seq // length
    wg = jnp.zeros((2, LANE, GLA_QK), F32)
    for d in range(2):
        wg = wg.at[d, d * GLA_RANK:(d + 1) * GLA_RANK].set(w_gate2[d])

    def row(d, b, c):
        return b * nc + c + d * (nc - 1 - 2 * c)

    return pl.pallas_call(
        _gla_kernel,
        grid=(2, bsz, nc),
        in_specs=[
            pl.BlockSpec((length, GLA_QK), lambda d, b, c: (row(d, b, c), 0)),
            pl.BlockSpec((length, GLA_QK), lambda d, b, c: (row(d, b, c), 1)),
            pl.BlockSpec((length, GLA_V), lambda d, b, c: (row(d, b, c), 2 * GLA_QK // GLA_V)),
            pl.BlockSpec((length, LANE), lambda d, b, c: (row(d, b, c), 0)),
            pl.BlockSpec((1, LANE, GLA_QK), lambda d, b, c: (d, 0, 0)),
            pl.BlockSpec((1, 1, GLA_QK), lambda d, b, c: (d, 0, 0)),
        ],
        out_specs=pl.BlockSpec((1, length, GLA_V), lambda d, b, c: (d, row(d, b, c), 0)),
        out_shape=jax.ShapeDtypeStruct((2, n, GLA_V), BF16),
        scratch_shapes=[pltpu.VMEM((GLA_HEADS, GLA_DV, GLA_DK), F32)],
        compiler_params=_params("arbitrary", "arbitrary", "arbitrary"),
        name="gla",
    )(proj, proj, proj, tail, wg.astype(BF16), gate_b.reshape(2, 1, GLA_QK))


def _head_norm(t, g, heads):
    dh = t.shape[1] // heads
    outs = []
    for h in range(heads):
        th = t[:, h * dh:(h + 1) * dh]
        outs.append(th * lax.rsqrt(jnp.mean(th * th, axis=1, keepdims=True) + NORM_EPS))
    return jnp.concatenate(outs, axis=1) * g


def _ab_out_kernel(x_ref, gr_ref, og_ref, hl_ref, hm_ref, g_ref, w_ref, ng_ref, o_ref, hn_ref):
    k = gr_ref.shape[1]
    ya = jax.nn.gelu(gr_ref[...]) * (hl_ref[0] + hl_ref[1])
    yb = jax.nn.sigmoid(og_ref[...]) * _head_norm(hm_ref[0].astype(F32) + hm_ref[1].astype(F32), g_ref[...], MLSTM_HEADS)
    o = (x_ref[...]
         + jnp.dot(ya.astype(BF16), w_ref[0:k, :], preferred_element_type=F32)
         + jnp.dot(yb.astype(BF16), w_ref[k:, :], preferred_element_type=F32))
    o_ref[...] = o
    hn_ref[...] = _rms(o, ng_ref[...]).astype(BF16)


def _ab_out(x, proj, hl, hm, norm_g, w, next_g, tm=512):
    n, d = x.shape
    k = W_LRU
    tm = min(tm, n)
    return pl.pallas_call(
        _ab_out_kernel,
        grid=(n // tm,),
        in_specs=[
            pl.BlockSpec((tm, d), lambda i: (i, 0)),
            pl.BlockSpec((tm, k), lambda i: (i, 1)),
            pl.BlockSpec((tm, k), lambda i: (i, 5)),
            pl.BlockSpec((2, tm, k), lambda i: (0, i, 0)),
            pl.BlockSpec((2, tm, k), lambda i: (0, i, 0)),
            pl.BlockSpec((1, k), lambda i: (0, 0)),
            pl.BlockSpec((2 * k, d), lambda i: (0, 0)),
            pl.BlockSpec((1, d), lambda i: (0, 0)),
        ],
        out_specs=[pl.BlockSpec((tm, d), lambda i: (i, 0)), pl.BlockSpec((tm, d), lambda i: (i, 0))],
        out_shape=[jax.ShapeDtypeStruct((n, d), F32), jax.ShapeDtypeStruct((n, d), BF16)],
        compiler_params=_params("arbitrary"),
        name="ab_out",
    )(x, proj, proj, hl, hm, norm_g.reshape(1, k), w, next_g.reshape(1, d))


def _cd_out_kernel(x_ref, r_ref, og_ref, g_ref, ys_ref, u_ref, d_ref, wg_ref, w_ref, ng_ref, o_ref, hn_ref):
    k = r_ref.shape[1]
    r = r_ref[...]
    yc = _head_norm(og_ref[0].astype(F32) + og_ref[1].astype(F32), g_ref[...], GLA_HEADS) * (r * jax.nn.sigmoid(r))
    y = jax.nn.gelu(ys_ref[...] + d_ref[...] * u_ref[...])
    yd = y * jax.nn.sigmoid(jnp.dot(y.astype(BF16), wg_ref[...], preferred_element_type=F32))
    o = (x_ref[...]
         + jnp.dot(yc.astype(BF16), w_ref[0:k, :], preferred_element_type=F32)
         + jnp.dot(yd.astype(BF16), w_ref[k:, :], preferred_element_type=F32))
    o_ref[...] = o
    hn_ref[...] = _rms(o, ng_ref[...]).astype(BF16)


def _cd_out(x, proj, og, norm_g, ys, s5_d, w_glu, w, next_g, tm=512):
    n, d = x.shape
    k = S5_W
    tm = min(tm, n)
    return pl.pallas_call(
        _cd_out_kernel,
        grid=(n // tm,),
        in_specs=[
            pl.BlockSpec((tm, d), lambda i: (i, 0)),
            pl.BlockSpec((tm, k), lambda i: (i, 2)),
            pl.BlockSpec((2, tm, k), lambda i: (0, i, 0)),
            pl.BlockSpec((1, k), lambda i: (0, 0)),
            pl.BlockSpec((tm, k), lambda i: (i, 0)),
            pl.BlockSpec((tm, k), lambda i: (i, 3)),
            pl.BlockSpec((1, k), lambda i: (0, 0)),
            pl.BlockSpec((k, k), lambda i: (0, 0)),
            pl.BlockSpec((2 * k, d), lambda i: (0, 0)),
            pl.BlockSpec((1, d), lambda i: (0, 0)),
        ],
        out_specs=[pl.BlockSpec((tm, d), lambda i: (i, 0)), pl.BlockSpec((tm, d), lambda i: (i, 0))],
        out_shape=[jax.ShapeDtypeStruct((n, d), F32), jax.ShapeDtypeStruct((n, d), BF16)],
        compiler_params=_params("arbitrary"),
        name="cd_out",
    )(x, proj, og, norm_g.reshape(1, k), ys, proj, s5_d.reshape(1, k), w_glu, w, next_g.reshape(1, d))


def _split_small(w, n_main):
    tail = w[:, n_main:]
    tail = jnp.pad(tail, ((0, 0), (0, LANE - tail.shape[1])))
    return w[:, :n_main].astype(BF16), tail.astype(BF16)


def kernel(x, norm_ffn1, ffn1_w_gu, ffn1_w_down, norm_mix, norm_ffn2, ffn2_w_gu, ffn2_w_down,
           ab_w_in, lru_conv_w, lru_conv_b, lru_gate_w, lru_gate_b, lru_lambda, mlstm_gate_b,
           mlstm_norm, ab_w_out, cd_w_in, gla_w_gate2, gla_gate_b, gla_norm, s5_a_re, s5_a_im,
           s5_log_dt, s5_b_re, s5_b_im, s5_c_re, s5_c_im, s5_d, s5_w_glu, cd_w_out, final_norm):
    bsz, seq, d = x.shape
    n = bsz * seq
    depth = norm_ffn1.shape[0]
    xf = x.reshape(n, d)
    w1_gu, w1_down = ffn1_w_gu.astype(BF16), ffn1_w_down.astype(BF16)
    w2_gu, w2_down = ffn2_w_gu.astype(BF16), ffn2_w_down.astype(BF16)
    for l in range(depth):
        xf = _ffn(xf, norm_ffn1[l], w1_gu, w1_down, l)
        j = l // 2
        if l % 2 == 0:
            w_main, w_tail = _split_small(ab_w_in[j], 2 * W_LRU + 4 * W_MLSTM)
            proj, tail = _norm_proj(xf, norm_mix[l], w_main, w_tail)
            hl = _lru(proj, bsz, lru_conv_w[j], lru_conv_b[j], lru_gate_w[j], lru_gate_b[j], lru_lambda[j])
            hm = _mlstm(proj, tail, bsz, mlstm_gate_b[j])
            xf, hn = _ab_out(xf, proj, hl, hm, mlstm_norm[j], ab_w_out[j].astype(BF16), norm_ffn2[l])
        else:
            w = cd_w_in[j]
            n_gla = 2 * GLA_QK + 2 * GLA_V
            w = jnp.concatenate([w[:, :n_gla], w[:, n_gla + 2 * GLA_RANK:], w[:, n_gla:n_gla + 2 * GLA_RANK]], axis=1)
            w_main, w_tail = _split_small(w, n_gla + S5_W)
            proj, tail = _norm_proj(xf, norm_mix[l], w_main, w_tail)
            og = _gla(proj, tail, bsz, gla_w_gate2[j], gla_gate_b[j])
            ops = _s5_operators(s5_a_re[j], s5_a_im[j], s5_log_dt[j], s5_b_re[j], s5_b_im[j],
                                s5_c_re[j], s5_c_im[j])
            ys = _s5(proj[:, n_gla:], bsz, ops)
            xf, hn = _cd_out(xf, proj, og, gla_norm[j], ys, s5_d[j],
                             s5_w_glu[j].astype(BF16), cd_w_out[j].astype(BF16), norm_ffn2[l])
        xf = _ffn(xf, hn, w2_gu, w2_down, l, final_g=final_norm if l == depth - 1 else None)
    return xf.reshape(bsz, seq, d)
```

```python
import functools

import jax
import jax.numpy as jnp
from jax import lax
from jax.experimental import pallas as pl
from jax.experimental.pallas import tpu as pltpu

NORM_EPS = 1e-6

W_LRU = 1024
LRU_HEADS = 8
LRU_BLOCK = W_LRU // LRU_HEADS
CONV_W = 4
CONV_PAD_L = 2
LRU_C = 8.0

W_MLSTM = 1024
MLSTM_HEADS = 4
MLSTM_DH = W_MLSTM // MLSTM_HEADS

GLA_HEADS = 4
GLA_DK = 128
GLA_DV = 256
GLA_QK = GLA_HEADS * GLA_DK
GLA_V = GLA_HEADS * GLA_DV
GLA_RANK = 16
GLA_TAU = 16.0

S5_W = 1024
S5_GROUP = 16
S5_GROUPS = S5_W // S5_GROUP
S5_P = 64
S5_CHUNK = 16
S5_GB = 8
S5_ROW_TILE = 256

LANE = 128
SUBLANE = 8
VMEM_LIMIT = 52 * 1024 * 1024

BF16 = jnp.bfloat16
F32 = jnp.float32


def _rms(x, g):
    ms = jnp.mean(x * x, axis=-1, keepdims=True)
    return x * lax.rsqrt(ms + NORM_EPS) * g


def _sigmoid(z):
    return 0.5 * jnp.tanh(0.5 * z) + 0.5


def _softplus(z):
    return jnp.maximum(z, 0.0) + jnp.log1p(jnp.exp(-jnp.abs(z)))


def _params(*sem):
    return pltpu.CompilerParams(dimension_semantics=sem, vmem_limit_bytes=VMEM_LIMIT)


def _ffn_kernel(x_ref, g_ref, wg_ref, wu_ref, wd_ref, *rest, final):
    if final:
        fg_ref, o_ref, h_ref = rest
    else:
        o_ref, h_ref = rest
    j = pl.program_id(1)

    @pl.when(j == 0)
    def _():
        x = x_ref[...]
        h_ref[...] = _rms(x, g_ref[...]).astype(BF16)
        o_ref[...] = x

    h = h_ref[...]
    g = jnp.dot(h, wg_ref[...], preferred_element_type=F32)
    u = jnp.dot(h, wu_ref[...], preferred_element_type=F32)
    hg = 0.5 * g
    a = (hg * (1.0 + jnp.tanh(hg)) * (0.5 * u)).astype(BF16)
    o_ref[...] += jnp.dot(a, wd_ref[...], preferred_element_type=F32)

    if final:
        @pl.when(j == pl.num_programs(1) - 1)
        def _():
            o_ref[...] = _rms(o_ref[...], fg_ref[...])


def _ffn(x, g, w_gu, w_down, layer, final_g=None, tm=512, tf=512):
    n, d = x.shape
    f = w_down.shape[1]
    tm = min(tm, n)
    nj = f // tf
    final = final_g is not None
    in_specs = [
        pl.BlockSpec((tm, d), lambda i, j: (i, 0)),
        pl.BlockSpec((1, d), lambda i, j: (0, 0)),
        pl.BlockSpec((None, d, tf), lambda i, j: (layer, 0, j)),
        pl.BlockSpec((None, d, tf), lambda i, j: (layer, 0, j + nj)),
        pl.BlockSpec((None, tf, d), lambda i, j: (layer, j, 0)),
    ]
    args = [x, g.reshape(1, d), w_gu, w_gu, w_down]
    if final:
        in_specs.append(pl.BlockSpec((1, d), lambda i, j: (0, 0)))
        args.append(final_g.reshape(1, d))
    return pl.pallas_call(
        functools.partial(_ffn_kernel, final=final),
        grid=(n // tm, nj),
        in_specs=in_specs,
        out_specs=pl.BlockSpec((tm, d), lambda i, j: (i, 0)),
        out_shape=jax.ShapeDtypeStruct((n, d), F32),
        scratch_shapes=[pltpu.VMEM((tm, d), BF16)],
        compiler_params=_params("arbitrary", "arbitrary"),
        name="ffn_final" if final else "ffn",
    )(*args)


def _norm_proj_kernel(x_ref, g_ref, w_ref, ws_ref, o_ref, os_ref, h_ref):
    @pl.when(pl.program_id(1) == 0)
    def _():
        h = _rms(x_ref[...], g_ref[...]).astype(BF16)
        h_ref[...] = h
        os_ref[...] = jnp.dot(h, ws_ref[...], preferred_element_type=F32)

    o_ref[...] = jnp.dot(h_ref[...], w_ref[...], preferred_element_type=F32)


def _norm_proj(x, g, w, w_small, tm=1024, tn=1024):
    n, d = x.shape
    m = w.shape[1]
    ms = w_small.shape[1]
    tm = min(tm, n)
    return pl.pallas_call(
        _norm_proj_kernel,
        grid=(n // tm, m // tn),
        in_specs=[
            pl.BlockSpec((tm, d), lambda i, j: (i, 0)),
            pl.BlockSpec((1, d), lambda i, j: (0, 0)),
            pl.BlockSpec((d, tn), lambda i, j: (0, j)),
            pl.BlockSpec((d, ms), lambda i, j: (0, 0)),
        ],
        out_specs=[pl.BlockSpec((tm, tn), lambda i, j: (i, j)),
                   pl.BlockSpec((tm, ms), lambda i, j: (i, 0))],
        out_shape=[jax.ShapeDtypeStruct((n, m), F32), jax.ShapeDtypeStruct((n, ms), F32)],
        scratch_shapes=[pltpu.VMEM((tm, d), BF16)],
        compiler_params=_params("arbitrary", "arbitrary"),
        name="norm_proj",
    )(x, g.reshape(1, d), w, w_small)


def _lru_kernel(xp_ref, x_ref, xn_ref, cw_ref, cb_ref, gw_ref, gb_ref, lam_ref, o_ref,
                xs_ref, a_ref, b_ref, h_ref, *, tt):
    d = pl.program_id(0)
    t = pl.program_id(2)
    nt = pl.num_programs(2)
    tb = jnp.where(d == 0, t, nt - 1 - t)
    w = x_ref.shape[1]

    xs_ref[0:SUBLANE, :] = jnp.where(tb > 0, xp_ref[...], 0.0)
    xs_ref[SUBLANE:SUBLANE + tt, :] = x_ref[...]
    xs_ref[SUBLANE + tt:2 * SUBLANE + tt, :] = jnp.where(tb < nt - 1, xn_ref[...], 0.0)
    cw = cw_ref[...]
    xc = cb_ref[...]
    for k in range(CONV_W):
        off = SUBLANE + k - CONV_PAD_L
        xc = xc + cw[k:k + 1, :] * xs_ref[off:off + tt, :]

    xcb = xc.astype(BF16)
    gb = gb_ref[0]
    sp = _softplus(-lam_ref[0])
    for hd in range(w // LRU_BLOCK):
        sl = slice(hd * LRU_BLOCK, (hd + 1) * LRU_BLOCK)
        z = jnp.dot(xcb[:, sl], gw_ref[0, hd], preferred_element_type=F32)
        r = _sigmoid(z[:, :LRU_BLOCK] + gb[0:1, sl])
        i = _sigmoid(z[:, LRU_BLOCK:] + gb[1:2, sl])
        log_a = (-LRU_C * sp[:, sl]) * r
        a = jnp.exp(log_a)
        a_ref[:, sl] = a
        b_ref[:, sl] = jnp.sqrt(-jnp.tanh(log_a) * (a * a + 1.0)) * (i * xc[:, sl])

    @pl.when(t == 0)
    def _():
        h_ref[...] = jnp.zeros_like(h_ref)

    sub = lax.broadcasted_iota(jnp.int32, (SUBLANE, w), 0)
    ntile = tt // SUBLANE

    def scan(rev):
        def earlier(x, k, fill):
            if rev:
                return jnp.where(sub < SUBLANE - k, pltpu.roll(x, SUBLANE - k, 0), fill)
            return jnp.where(sub >= k, pltpu.roll(x, k, 0), fill)

        def body(jt, h):
            r0 = pl.multiple_of((ntile - 1 - jt if rev else jt) * SUBLANE, SUBLANE)
            a = a_ref[pl.ds(r0, SUBLANE), :]
            b = b_ref[pl.ds(r0, SUBLANE), :]
            k = 1
            while k < SUBLANE:
                a, b = a * earlier(a, k, 1.0), b + a * earlier(b, k, 0.0)
                k *= 2
            tile = a * h + b
            o_ref[0, pl.ds(r0, SUBLANE), :] = tile
            last = 0 if rev else SUBLANE - 1
            return tile[last:last + 1, :]

        h_ref[...] = lax.fori_loop(0, ntile, body, h_ref[...])

    pl.when(d == 0)(functools.partial(scan, False))
    pl.when(d == 1)(functools.partial(scan, True))


def _lru(proj, bsz, conv_w, conv_b, gate_w, gate_b, lam, tt=512):
    n = proj.shape[0]
    seq = n // bsz
    tt = min(tt, seq)
    nt = seq // tt
    w = W_LRU
    r8 = tt // SUBLANE
    gw = jnp.concatenate([gate_w[:, 0], gate_w[:, 1]], axis=-1).astype(BF16)

    def tb(d, t):
        return t + d * (nt - 1 - 2 * t)

    return pl.pallas_call(
        functools.partial(_lru_kernel, tt=tt),
        grid=(2, bsz, nt),
        in_specs=[
            pl.BlockSpec((SUBLANE, w), lambda d, b, t: (jnp.maximum((b * nt + tb(d, t)) * r8 - 1, 0), 0)),
            pl.BlockSpec((tt, w), lambda d, b, t: (b * nt + tb(d, t), 0)),
            pl.BlockSpec((SUBLANE, w), lambda d, b, t: (jnp.minimum((b * nt + tb(d, t) + 1) * r8, n // SUBLANE - 1), 0)),
            pl.BlockSpec((CONV_W, w), lambda d, b, t: (0, 0)),
            pl.BlockSpec((1, w), lambda d, b, t: (0, 0)),
            pl.BlockSpec((1, LRU_HEADS, LRU_BLOCK, 2 * LRU_BLOCK), lambda d, b, t: (d, 0, 0, 0)),
            pl.BlockSpec((1, 2, w), lambda d, b, t: (d, 0, 0)),
            pl.BlockSpec((1, 1, w), lambda d, b, t: (d, 0, 0)),
        ],
        out_specs=pl.BlockSpec((1, tt, w), lambda d, b, t: (d, b * nt + tb(d, t), 0)),
        out_shape=jax.ShapeDtypeStruct((2, n, w), F32),
        scratch_shapes=[pltpu.VMEM((tt + 2 * SUBLANE, w), F32), pltpu.VMEM((tt, w), F32),
                        pltpu.VMEM((tt, w), F32), pltpu.VMEM((1, w), F32)],
        compiler_params=_params("arbitrary", "arbitrary", "arbitrary"),
        name="rg_lru",
    )(proj, proj, proj, conv_w, conv_b.reshape(1, w), gw, gate_b, lam.reshape(2, 1, w))


def _s5_discretise(a_re, a_im, log_dt, b_re, b_im):
    dt = jnp.exp(log_dt)[:, None]
    mag = jnp.exp(dt * a_re)
    lr = mag * jnp.cos(dt * a_im)
    li = mag * jnp.sin(dt * a_im)
    den = a_re * a_re + a_im * a_im
    nr = lr - 1.0
    cr = (nr * a_re + li * a_im) / den
    ci = (li * a_re - nr * a_im) / den
    bbr = cr[..., None] * b_re - ci[..., None] * b_im
    bbi = cr[..., None] * b_im + ci[..., None] * b_re
    return lr, li, bbr, bbi


def _s5_operators(a_re, a_im, log_dt, b_re, b_im, c_re, c_im):
    L = S5_CHUNK
    G, P, C = S5_GROUPS, S5_P, S5_GROUP
    kfs, mins, mouts, lams = [], [], [], []
    for d in range(2):
        _, _, bbr, bbi = _s5_discretise(a_re[d], a_im[d], log_dt[d], b_re, b_im)
        bbr, bbi = bbr.transpose(0, 2, 1), bbi.transpose(0, 2, 1)
        jj = jnp.arange(L + 1, dtype=F32)[:, None, None, None]
        dt = jnp.exp(log_dt[d])[None, :, None, None]
        mag = jnp.exp(jj * dt * a_re[d][None, :, None, :])
        pr = mag * jnp.cos(jj * dt * a_im[d][None, :, None, :])
        pi = mag * jnp.sin(jj * dt * a_im[d][None, :, None, :])
        cl = jnp.concatenate([c_re[None] * pr - c_im[None] * pi, -(c_re[None] * pi + c_im[None] * pr)], axis=-1)
        bl = jnp.concatenate([pr * bbr[None] - pi * bbi[None], pr * bbi[None] + pi * bbr[None]], axis=-1)
        kfs.append(jnp.sum(cl[:L, :, None, :, :] * bl[0][None, :, :, None, :], axis=-1))
        if d == 0:
            e_in = jnp.arange(L - 1, -1, -1)
            e_out = jnp.arange(1, L + 1)
        else:
            e_in = jnp.arange(L)
            e_out = jnp.arange(L, 0, -1)
        mins.append(bl[e_in])
        mouts.append(cl[e_out])
        lams.append(jnp.stack([pr[L, :, 0], pi[L, :, 0]], axis=1))
    kf, kb = kfs
    kfull = jnp.concatenate([kb[:0:-1], (kf[0] + kb[0])[None], kf[1:]], axis=0)
    gb = S5_GB
    nb_ = G // gb
    k2 = kfull.reshape(2 * L - 1, nb_, gb * C, C).transpose(1, 0, 2, 3)

    def rows_of_block(parts):
        a = jnp.concatenate(parts, axis=-1).reshape(L, nb_, gb * C, 4 * P)
        return a.transpose(1, 0, 2, 3).reshape(nb_, L * gb * C, 4 * P)

    lam_l = jnp.concatenate(lams, axis=1)
    lam_blk = lam_l.reshape(nb_, gb, 4, P).transpose(0, 2, 1, 3).reshape(nb_, 1, 4 * gb * P)
    return k2, rows_of_block(mins), rows_of_block(mouts), lam_blk


def _s5_state_kernel(u_ref, amin_ref, lam_ref, hin_ref, h_ref, min_ref, *, nb):
    nl, rows, _ = u_ref.shape
    _s5_spread(amin_ref[0], min_ref, False)
    rb = rows // nb
    hw = h_ref.shape[1] // 4
    rt = min(rows, S5_ROW_TILE)
    lo = lax.broadcasted_iota(jnp.int32, (rt, LANE), 1) < S5_P
    ng = hw // S5_P

    def pairs():
        for d in range(2):
            for k in range(ng // 2):
                yield ((d * ng + 2 * k) * LANE, (d * ng + 2 * k + 1) * LANE,
                       2 * d * hw + k * LANE, (2 * d + 1) * hw + k * LANE)

    for r0 in range(0, rows, rt):
        u = jnp.concatenate([u_ref[s, r0:r0 + rt, :] for s in range(nl)], axis=1)
        hl = jnp.dot(u, min_ref[...], preferred_element_type=F32)
        for ca, cb, cre, cim in pairs():
            a, b = hl[:, ca:ca + LANE], hl[:, cb:cb + LANE]
            h_ref[r0:r0 + rt, cre:cre + LANE] = jnp.where(lo, a, pltpu.roll(b, S5_P, 1))
            h_ref[r0:r0 + rt, cim:cim + LANE] = jnp.where(lo, pltpu.roll(a, S5_P, 1), b)
    lam = lam_ref[0]
    lrf, lif = lam[:, 0:hw], lam[:, hw:2 * hw]
    lrb, lib = lam[:, 2 * hw:3 * hw], lam[:, 3 * hw:4 * hw]
    sub = lax.broadcasted_iota(jnp.int32, (SUBLANE, hw), 0)

    def body(it, carry):
        new = []
        for b in range(nb):
            fr, fi, br, bi = carry[4 * b:4 * b + 4]
            f0 = pl.multiple_of(b * rb + it * SUBLANE, SUBLANE)
            b0 = pl.multiple_of(b * rb + rb - SUBLANE - it * SUBLANE, SUBLANE)
            tiles = [jnp.zeros((SUBLANE, hw), F32)] * 4
            lf = h_ref[pl.ds(f0, SUBLANE), 0:2 * hw]
            lb = h_ref[pl.ds(b0, SUBLANE), 2 * hw:4 * hw]
            for r in range(SUBLANE):
                q = SUBLANE - 1 - r
                tiles = [jnp.where(sub == r, fr, tiles[0]), jnp.where(sub == r, fi, tiles[1]),
                         jnp.where(sub == q, br, tiles[2]), jnp.where(sub == q, bi, tiles[3])]
                fr, fi, br, bi = (lrf * fr - lif * fi + lf[r:r + 1, 0:hw],
                                  lrf * fi + lif * fr + lf[r:r + 1, hw:2 * hw],
                                  lrb * br - lib * bi + lb[q:q + 1, 0:hw],
                                  lrb * bi + lib * br + lb[q:q + 1, hw:2 * hw])
            h_ref[pl.ds(f0, SUBLANE), 0:hw] = tiles[0]
            h_ref[pl.ds(f0, SUBLANE), hw:2 * hw] = tiles[1]
            h_ref[pl.ds(b0, SUBLANE), 2 * hw:3 * hw] = tiles[2]
            h_ref[pl.ds(b0, SUBLANE), 3 * hw:4 * hw] = tiles[3]
            new += [fr, fi, br, bi]
        return tuple(new)

    zero = jnp.zeros((1, hw), F32)
    lax.fori_loop(0, rb // SUBLANE, body, (zero,) * (4 * nb))
    for r0 in range(0, rows, rt):
        for ca, cb, cre, cim in pairs():
            re, im = h_ref[r0:r0 + rt, cre:cre + LANE], h_ref[r0:r0 + rt, cim:cim + LANE]
            hin_ref[0, r0:r0 + rt, ca:ca + LANE] = jnp.where(lo, re, pltpu.roll(im, S5_P, 1)).astype(BF16)
            hin_ref[0, r0:r0 + rt, cb:cb + LANE] = jnp.where(lo, pltpu.roll(re, S5_P, 1), im).astype(BF16)


def _s5_spread(a, o_ref, transpose):
    rows = a.shape[0]
    row_g = (lax.broadcasted_iota(jnp.int32, (rows, LANE), 0) // S5_GROUP) % S5_GB
    for d in range(2):
        ad = a[:, d * LANE:(d + 1) * LANE]
        for g in range(S5_GB):
            tile = jnp.where(row_g == g, ad, 0.0)
            col = (d * S5_GB + g) * LANE
            if transpose:
                o_ref[col:col + LANE, :] = tile.T.astype(BF16)
            else:
                o_ref[:, col:col + LANE] = tile.astype(BF16)


def _s5_out_kernel(u_ref, hin_ref, k2_ref, amout_ref, y_ref, t_ref, mout_ref):
    nl = u_ref.shape[0]

    @pl.when(pl.program_id(1) == 0)
    def _():
        _s5_spread(amout_ref[0], mout_ref, True)
        row_g = lax.broadcasted_iota(jnp.int32, (LANE, LANE), 0) // S5_GROUP
        col_g = lax.broadcasted_iota(jnp.int32, (LANE, LANE), 1) // S5_GROUP
        spread = (lax.broadcasted_iota(jnp.int32, (S5_GROUP, LANE), 1) % S5_GROUP
                  == lax.broadcasted_iota(jnp.int32, (S5_GROUP, LANE), 0)).astype(BF16)
        for j in range(2 * nl - 1):
            rep = jnp.dot(k2_ref[0, j].astype(BF16), spread, preferred_element_type=F32)
            tile = jnp.where(row_g == col_g, rep, 0.0).astype(BF16)
            for s in range(nl):
                t = s + j - (nl - 1)
                if 0 <= t < nl:
                    t_ref[s * LANE:(s + 1) * LANE, t * LANE:(t + 1) * LANE] = tile

    u = jnp.concatenate([u_ref[s] for s in range(nl)], axis=1)
    y = (jnp.dot(u, t_ref[...], preferred_element_type=F32)
         + jnp.dot(hin_ref[0], mout_ref[...], preferred_element_type=F32))
    for t in range(nl):
        y_ref[t] = y[:, t * LANE:(t + 1) * LANE]


def _s5(u, bsz, ops, tr=256):
    k2, a_min, a_mout, lam_blk = ops
    n, w = u.shape
    L = S5_CHUNK
    rows = n // L
    nblk = w // LANE
    kw = L * LANE
    sw = lam_blk.shape[2]
    tr = min(tr, rows)
    u3 = u.reshape(rows, L, w).transpose(1, 0, 2).astype(BF16)
    hin = pl.pallas_call(
        functools.partial(_s5_state_kernel, nb=bsz),
        grid=(nblk,),
        in_specs=[
            pl.BlockSpec((L, rows, LANE), lambda b: (0, 0, b)),
            pl.BlockSpec((1,) + a_min.shape[1:], lambda b: (b, 0, 0)),
            pl.BlockSpec((1, 1, sw), lambda b: (b, 0, 0)),
        ],
        out_specs=pl.BlockSpec((1, rows, sw), lambda b: (b, 0, 0)),
        out_shape=jax.ShapeDtypeStruct((nblk, rows, sw), BF16),
        scratch_shapes=[pltpu.VMEM((rows, sw), F32), pltpu.VMEM((kw, sw), BF16)],
        compiler_params=_params("arbitrary"),
        name="s5_state",
    )(u3, a_min, lam_blk)
    y3 = pl.pallas_call(
        _s5_out_kernel,
        grid=(nblk, rows // tr),
        in_specs=[
            pl.BlockSpec((L, tr, LANE), lambda b, i: (0, i, b)),
            pl.BlockSpec((1, tr, sw), lambda b, i: (b, i, 0)),
            pl.BlockSpec((1,) + k2.shape[1:], lambda b, i: (b, 0, 0, 0)),
            pl.BlockSpec((1,) + a_mout.shape[1:], lambda b, i: (b, 0, 0)),
        ],
        out_specs=pl.BlockSpec((L, tr, LANE), lambda b, i: (0, i, b)),
        out_shape=jax.ShapeDtypeStruct((L, rows, w), F32),
        scratch_shapes=[pltpu.VMEM((kw, kw), BF16), pltpu.VMEM((sw, kw), BF16)],
        compiler_params=_params("arbitrary", "arbitrary"),
        name="s5_out",
    )(u3, hin, k2, a_mout)
    return y3.transpose(1, 0, 2).reshape(n, w)


def _log_sigmoid(z):
    return -_softplus(-z)


def _tri(length, d):
    ti = lax.broadcasted_iota(jnp.int32, (length, length), 0)
    si = lax.broadcasted_iota(jnp.int32, (length, length), 1)
    return (ti - si) * (1 - 2 * d) >= 0


def _masked_sums(mask, x):
    m = jnp.where(mask, 1.0, 0.0).astype(BF16)
    hi = x.astype(BF16)
    rest = x - hi.astype(F32)
    mid = rest.astype(BF16)
    lo = (rest - mid.astype(F32)).astype(BF16)
    return (jnp.dot(m, hi, preferred_element_type=F32) + jnp.dot(m, mid, preferred_element_type=F32)
            + jnp.dot(m, lo, preferred_element_type=F32))


def _mlstm_kernel(q_ref, k_ref, v_ref, g_ref, gb_ref, o_ref, c_ref, n_ref, m_ref):
    d = pl.program_id(0)
    length = q_ref.shape[0]
    nh, dh = MLSTM_HEADS, MLSTM_DH

    @pl.when(pl.program_id(2) == 0)
    def _():
        c_ref[...] = jnp.zeros_like(c_ref)
        n_ref[...] = jnp.zeros_like(n_ref)
        m_ref[...] = jnp.zeros_like(m_ref)

    causal = _tri(length, d)
    gp = g_ref[...] + gb_ref[...]
    gp = jnp.where(d == 0, gp, pltpu.roll(gp, LANE - 2 * nh, 1))
    lf = _log_sigmoid(gp)
    cum = _masked_sums(causal, lf)
    tot = jnp.sum(lf, axis=0, keepdims=True)
    gp_t = gp.T
    cum_t = cum.T
    nt = (((1,), (1,)), ((), ()))
    tn = (((0,), (0,)), ((), ()))

    for h in range(nh):
        sl = slice(h * dh, (h + 1) * dh)
        qh = (q_ref[:, sl] * (dh ** -0.5)).astype(BF16)
        kf = k_ref[:, sl]
        kh = kf.astype(BF16)
        vh = v_ref[:, sl].astype(BF16)
        ig_c, cum_c = gp[:, h:h + 1], cum[:, nh + h:nh + h + 1]
        ig_r, cum_r = gp_t[h:h + 1, :], cum_t[nh + h:nh + h + 1, :]
        tot_h = tot[:, nh + h:nh + h + 1]
        m_st = m_ref[h:h + 1, 0:1]
        c_st = c_ref[h]
        dmat = jnp.where(causal, cum_c - cum_r + ig_r, -jnp.inf)
        m_inter = cum_c + m_st
        m_t = jnp.maximum(jnp.max(dmat, axis=1, keepdims=True), m_inter)
        w_inter = jnp.exp(m_inter - m_t)
        s = lax.dot_general(qh, kh, nt, preferred_element_type=F32) * jnp.exp(dmat - m_t)
        num = (jnp.dot(s.astype(BF16), vh, preferred_element_type=F32)
               + w_inter * jnp.dot(qh, c_st.astype(BF16), preferred_element_type=F32))
        den = (jnp.sum(s, axis=1, keepdims=True)
               + w_inter * jnp.sum(qh.astype(F32) * n_ref[h:h + 1, :], axis=1, keepdims=True))
        o_ref[0, :, sl] = (num / jnp.maximum(jnp.abs(den), jnp.exp(-m_t))).astype(o_ref.dtype)
        dec = tot_h - cum_c + ig_c
        m_new = jnp.maximum(tot_h + m_st, jnp.max(dec, axis=0, keepdims=True))
        wc = jnp.exp(tot_h + m_st - m_new)
        kw = jnp.exp(dec - m_new) * kf
        c_ref[h] = wc * c_st + lax.dot_general(kw.astype(BF16), vh, tn, preferred_element_type=F32)
        n_ref[h:h + 1, :] = wc * n_ref[h:h + 1, :] + jnp.sum(kw, axis=0, keepdims=True)
        m_ref[h:h + 1, :] = jnp.broadcast_to(m_new, (1, LANE))


def _mlstm(proj, tail, bsz, gate_b, length=256):
    n = proj.shape[0]
    seq = n // bsz
    length = min(length, seq)
    nc = seq // length
    w = W_MLSTM
    q_blk = 2 * W_LRU // w
    gb = jnp.pad(gate_b.reshape(1, -1), ((0, 0), (0, LANE - gate_b.size)))

    def row(d, b, c):
        return b * nc + c + d * (nc - 1 - 2 * c)

    return pl.pallas_call(
        _mlstm_kernel,
        grid=(2, bsz, nc),
        in_specs=[
            pl.BlockSpec((length, w), lambda d, b, c: (row(d, b, c), q_blk)),
            pl.BlockSpec((length, w), lambda d, b, c: (row(d, b, c), q_blk + 1)),
            pl.BlockSpec((length, w), lambda d, b, c: (row(d, b, c), q_blk + 2)),
            pl.BlockSpec((length, LANE), lambda d, b, c: (row(d, b, c), 0)),
            pl.BlockSpec((1, LANE), lambda d, b, c: (0, 0)),
        ],
        out_specs=pl.BlockSpec((1, length, w), lambda d, b, c: (d, row(d, b, c), 0)),
        out_shape=jax.ShapeDtypeStruct((2, n, w), BF16),
        scratch_shapes=[pltpu.VMEM((MLSTM_HEADS, MLSTM_DH, MLSTM_DH), F32),
                        pltpu.VMEM((SUBLANE, MLSTM_DH), F32), pltpu.VMEM((SUBLANE, LANE), F32)],
        compiler_params=_params("arbitrary", "arbitrary", "arbitrary"),
        name="mlstm",
    )(proj, proj, proj, tail, gb)


GLA_SUB = 16
GLA_SAFE_DECAY = 40.0


def _gla_chunk_dense(q_ref, k_ref, v_ref, cum, o_ref, s_ref, rev):
    length = q_ref.shape[0]
    nt = (((1,), (1,)), ((), ()))
    tn = (((0,), (0,)), ((), ()))
    causal = _tri(length, int(rev))
    last = slice(0, 1) if rev else slice(length - 1, length)
    for h in range(GLA_HEADS):
        kl = slice(h * GLA_DK, (h + 1) * GLA_DK)
        vl = slice(h * GLA_DV, (h + 1) * GLA_DV)
        ch = cum[:, kl]
        kk = k_ref[:, kl]
        vv = v_ref[:, vl].astype(BF16)
        st = s_ref[h]
        tot = ch[last]
        qe = (q_ref[:, kl] * (GLA_DK ** -0.5) * jnp.exp(ch)).astype(BF16)
        ke = (kk * jnp.exp(-ch)).astype(BF16)
        att = jnp.where(causal, lax.dot_general(qe, ke, nt, preferred_element_type=F32), 0.0)
        o_ref[0, :, vl] = (jnp.dot(att.astype(BF16), vv, preferred_element_type=F32)
                           + lax.dot_general(qe, st.astype(BF16), nt, preferred_element_type=F32)).astype(o_ref.dtype)
        kd = (kk * jnp.exp(tot - ch)).astype(BF16)
        s_ref[h] = jnp.exp(tot) * st + lax.dot_general(vv, kd, tn, preferred_element_type=F32)


def _gla_chunk(q_ref, k_ref, v_ref, cum, o_ref, s_ref, rev):
    length = q_ref.shape[0]
    c = GLA_SUB
    nt = (((1,), (1,)), ((), ()))
    tn = (((0,), (0,)), ((), ()))

    def rows(a, b):
        return slice(length - b, length - a) if rev else slice(a, b)

    def row(a):
        i = length - 1 - a if rev else a
        return slice(i, i + 1)

    ti = lax.broadcasted_iota(jnp.int32, (c, 1), 0)
    for h in range(GLA_HEADS):
        kl = slice(h * GLA_DK, (h + 1) * GLA_DK)
        vl = slice(h * GLA_DV, (h + 1) * GLA_DV)
        ch = cum[:, kl]
        qs = q_ref[:, kl] * (GLA_DK ** -0.5)
        kk = k_ref[:, kl]
        vf = v_ref[:, vl]
        vv = vf.astype(BF16)
        st = s_ref[h]
        tot = ch[row(length - 1)]
        o = lax.dot_general((qs * jnp.exp(ch)).astype(BF16), st.astype(BF16), nt, preferred_element_type=F32)
        ob = [o[rows(p * c, (p + 1) * c)] for p in range(length // c)]
        m = length // 2
        while m >= c:
            for start in range(0, length, 2 * m):
                fst, sec = rows(start, start + m), rows(start + m, start + 2 * m)
                r = ch[row(start + m)]
                qh = (qs[sec] * jnp.exp(ch[sec] - r)).astype(BF16)
                kh = (kk[fst] * jnp.exp(r - ch[fst])).astype(BF16)
                att = lax.dot_general(qh, kh, nt, preferred_element_type=F32)
                contrib = jnp.dot(att.astype(BF16), vv[fst], preferred_element_type=F32)
                for p in range((start + m) // c, (start + 2 * m) // c):
                    lo = rows(p * c, (p + 1) * c).start - sec.start
                    ob[p] = ob[p] + contrib[lo:lo + c]
            m //= 2
        for p in range(length // c):
            blk = rows(p * c, (p + 1) * c)
            cb, qb, kb, vb = ch[blk], qs[blk], kk[blk], vf[blk]
            acc = ob[p]
            for s in range(c):
                sees = (ti <= s) if rev else (ti >= s)
                e = jnp.exp(jnp.where(sees, cb - cb[s:s + 1], -jnp.inf))
                a = jnp.sum(qb * kb[s:s + 1] * e, axis=1, keepdims=True)
                acc = acc + a * vb[s:s + 1]
            o_ref[0, blk, vl] = acc.astype(o_ref.dtype)
        kd = (kk * jnp.exp(tot - ch)).astype(BF16)
        s_ref[h] = jnp.exp(tot) * st + lax.dot_general(vv, kd, tn, preferred_element_type=F32)


def _gla_kernel(q_ref, k_ref, v_ref, low_ref, wg_ref, bg_ref, o_ref, s_ref):
    d = pl.program_id(0)
    length = q_ref.shape[0]

    @pl.when(pl.program_id(2) == 0)
    def _():
        s_ref[...] = jnp.zeros_like(s_ref)

    gate_pre = jnp.dot(low_ref[...].astype(BF16), wg_ref[0], preferred_element_type=F32) + bg_ref[0]
    la = _log_sigmoid(gate_pre) * (1.0 / GLA_TAU)
    cum = _masked_sums(_tri(length, d), la)
    safe = jnp.sum(la, axis=0, keepdims=True).min() >= -GLA_SAFE_DECAY
    for rev in (False, True):
        pl.when((d == int(rev)) & safe)(
            functools.partial(_gla_chunk_dense, q_ref, k_ref, v_ref, cum, o_ref, s_ref, rev))
        pl.when((d == int(rev)) & jnp.logical_not(safe))(
            functools.partial(_gla_chunk, q_ref, k_ref, v_ref, cum, o_ref, s_ref, rev))


def _gla(proj, tail, bsz, w_gate2, gate_b, length=128):
    n = proj.shape[0]
    seq = n // bsz
    length = min(length, seq)
    nc = seq // length
    wg = jnp.zeros((2, LANE, GLA_QK), F32)
    for d in range(2):
        wg = wg.at[d, d * GLA_RANK:(d + 1) * GLA_RANK].set(w_gate2[d])

    def row(d, b, c):
        return b * nc + c + d * (nc - 1 - 2 * c)

    return pl.pallas_call(
        _gla_kernel,
        grid=(2, bsz, nc),
        in_specs=[
            pl.BlockSpec((length, GLA_QK), lambda d, b, c: (row(d, b, c), 0)),
            pl.BlockSpec((length, GLA_QK), lambda d, b, c: (row(d, b, c), 1)),
            pl.BlockSpec((length, GLA_V), lambda d, b, c: (row(d, b, c), 2 * GLA_QK // GLA_V)),
            pl.BlockSpec((length, LANE), lambda d, b, c: (row(d, b, c), 0)),
            pl.BlockSpec((1, LANE, GLA_QK), lambda d, b, c: (d, 0, 0)),
            pl.BlockSpec((1, 1, GLA_QK), lambda d, b, c: (d, 0, 0)),
        ],
        out_specs=pl.BlockSpec((1, length, GLA_V), lambda d, b, c: (d, row(d, b, c), 0)),
        out_shape=jax.ShapeDtypeStruct((2, n, GLA_V), BF16),
        scratch_shapes=[pltpu.VMEM((GLA_HEADS, GLA_DV, GLA_DK), F32)],
        compiler_params=_params("arbitrary", "arbitrary", "arbitrary"),
        name="gla",
    )(proj, proj, proj, tail, wg.astype(BF16), gate_b.reshape(2, 1, GLA_QK))


def _head_norm(t, g, heads):
    dh = t.shape[1] // heads
    outs = []
    for h in range(heads):
        th = t[:, h * dh:(h + 1) * dh]
        outs.append(th * lax.rsqrt(jnp.mean(th * th, axis=1, keepdims=True) + NORM_EPS))
    return jnp.concatenate(outs, axis=1) * g


def _ab_out_kernel(x_ref, gr_ref, og_ref, hl_ref, hm_ref, g_ref, w_ref, o_ref):
    k = gr_ref.shape[1]
    ya = jax.nn.gelu(gr_ref[...]) * (hl_ref[0] + hl_ref[1])
    yb = jax.nn.sigmoid(og_ref[...]) * _head_norm(hm_ref[0].astype(F32) + hm_ref[1].astype(F32), g_ref[...], MLSTM_HEADS)
    o_ref[...] = (x_ref[...]
                  + jnp.dot(ya.astype(BF16), w_ref[0:k, :], preferred_element_type=F32)
                  + jnp.dot(yb.astype(BF16), w_ref[k:, :], preferred_element_type=F32))


def _ab_out(x, proj, hl, hm, norm_g, w, tm=512):
    n, d = x.shape
    k = W_LRU
    tm = min(tm, n)
    return pl.pallas_call(
        _ab_out_kernel,
        grid=(n // tm,),
        in_specs=[
            pl.BlockSpec((tm, d), lambda i: (i, 0)),
            pl.BlockSpec((tm, k), lambda i: (i, 1)),
            pl.BlockSpec((tm, k), lambda i: (i, 5)),
            pl.BlockSpec((2, tm, k), lambda i: (0, i, 0)),
            pl.BlockSpec((2, tm, k), lambda i: (0, i, 0)),
            pl.BlockSpec((1, k), lambda i: (0, 0)),
            pl.BlockSpec((2 * k, d), lambda i: (0, 0)),
        ],
        out_specs=pl.BlockSpec((tm, d), lambda i: (i, 0)),
        out_shape=jax.ShapeDtypeStruct((n, d), F32),
        compiler_params=_params("arbitrary"),
        name="ab_out",
    )(x, proj, proj, hl, hm, norm_g.reshape(1, k), w)


def _cd_out_kernel(x_ref, r_ref, og_ref, g_ref, ys_ref, u_ref, d_ref, wg_ref, w_ref, o_ref):
    k = r_ref.shape[1]
    r = r_ref[...]
    yc = _head_norm(og_ref[0].astype(F32) + og_ref[1].astype(F32), g_ref[...], GLA_HEADS) * (r * jax.nn.sigmoid(r))
    y = jax.nn.gelu(ys_ref[...] + d_ref[...] * u_ref[...])
    yd = y * jax.nn.sigmoid(jnp.dot(y.astype(BF16), wg_ref[...], preferred_element_type=F32))
    o_ref[...] = (x_ref[...]
                  + jnp.dot(yc.astype(BF16), w_ref[0:k, :], preferred_element_type=F32)
                  + jnp.dot(yd.astype(BF16), w_ref[k:, :], preferred_element_type=F32))


def _cd_out(x, proj, og, norm_g, ys, s5_d, w_glu, w, tm=512):
    n, d = x.shape
    k = S5_W
    tm = min(tm, n)
    return pl.pallas_call(
        _cd_out_kernel,
        grid=(n // tm,),
        in_specs=[
            pl.BlockSpec((tm, d), lambda i: (i, 0)),
            pl.BlockSpec((tm, k), lambda i: (i, 2)),
            pl.BlockSpec((2, tm, k), lambda i: (0, i, 0)),
            pl.BlockSpec((1, k), lambda i: (0, 0)),
            pl.BlockSpec((tm, k), lambda i: (i, 0)),
            pl.BlockSpec((tm, k), lambda i: (i, 3)),
            pl.BlockSpec((1, k), lambda i: (0, 0)),
            pl.BlockSpec((k, k), lambda i: (0, 0)),
            pl.BlockSpec((2 * k, d), lambda i: (0, 0)),
        ],
        out_specs=pl.BlockSpec((tm, d), lambda i: (i, 0)),
        out_shape=jax.ShapeDtypeStruct((n, d), F32),
        compiler_params=_params("arbitrary"),
        name="cd_out",
    )(x, proj, og, norm_g.reshape(1, k), ys, proj, s5_d.reshape(1, k), w_glu, w)


def _split_small(w, n_main):
    tail = w[:, n_main:]
    tail = jnp.pad(tail, ((0, 0), (0, LANE - tail.shape[1])))
    return w[:, :n_main].astype(BF16), tail.astype(BF16)


def kernel(x, norm_ffn1, ffn1_w_gu, ffn1_w_down, norm_mix, norm_ffn2, ffn2_w_gu, ffn2_w_down,
           ab_w_in, lru_conv_w, lru_conv_b, lru_gate_w, lru_gate_b, lru_lambda, mlstm_gate_b,
           mlstm_norm, ab_w_out, cd_w_in, gla_w_gate2, gla_gate_b, gla_norm, s5_a_re, s5_a_im,
           s5_log_dt, s5_b_re, s5_b_im, s5_c_re, s5_c_im, s5_d, s5_w_glu, cd_w_out, final_norm):
    bsz, seq, d = x.shape
    n = bsz * seq
    depth = norm_ffn1.shape[0]
    xf = x.reshape(n, d)
    w1_gu, w1_down = ffn1_w_gu.astype(BF16), ffn1_w_down.astype(BF16)
    w2_gu, w2_down = ffn2_w_gu.astype(BF16), ffn2_w_down.astype(BF16)
    for l in range(depth):
        xf = _ffn(xf, norm_ffn1[l], w1_gu, w1_down, l)
        j = l // 2
        if l % 2 == 0:
            w_main, w_tail = _split_small(ab_w_in[j], 2 * W_LRU + 4 * W_MLSTM)
            proj, tail = _norm_proj(xf, norm_mix[l], w_main, w_tail)
            hl = _lru(proj, bsz, lru_conv_w[j], lru_conv_b[j], lru_gate_w[j], lru_gate_b[j], lru_lambda[j])
            hm = _mlstm(proj, tail, bsz, mlstm_gate_b[j])
            xf = _ab_out(xf, proj, hl, hm, mlstm_norm[j], ab_w_out[j].astype(BF16))
        else:
            w = cd_w_in[j]
            n_gla = 2 * GLA_QK + 2 * GLA_V
            w = jnp.concatenate([w[:, :n_gla], w[:, n_gla + 2 * GLA_RANK:], w[:, n_gla:n_gla + 2 * GLA_RANK]], axis=1)
            w_main, w_tail = _split_small(w, n_gla + S5_W)
            proj, tail = _norm_proj(xf, norm_mix[l], w_main, w_tail)
            og = _gla(proj, tail, bsz, gla_w_gate2[j], gla_gate_b[j])
            ops = _s5_operators(s5_a_re[j], s5_a_im[j], s5_log_dt[j], s5_b_re[j], s5_b_im[j],
                                s5_c_re[j], s5_c_im[j])
            ys = _s5(proj[:, n_gla:], bsz, ops)
            xf = _cd_out(xf, proj, og, gla_norm[j], ys, s5_d[j],
                         s5_w_glu[j].astype(BF16), cd_w_out[j].astype(BF16))
        xf = _ffn(xf, norm_ffn2[l], w2_gu, w2_down, l, final_g=final_norm if l == depth - 1 else None)
    return xf.reshape(bsz, seq, d)
```
